```python
import jax, jax.numpy as jnp
from jax import lax
import numpy as np

D_MODEL = 1024
BATCH = 16
SEQ = 2048
DEPTH = 4

CHUNK = 64
Q_BLOCK = 128
N_MEM = 256
FOX_HEADS = 8
FOX_HEAD_DIM = 64
MLA_HEADS = 8
MLA_NOPE_DIM = 64
MLA_ROPE_DIM = 32
MLA_V_DIM = 64
MLA_Q_RANK = 384
MLA_KV_RANK = 256
ROPE_BASE = 10000.0
MEM_HEADS = 4
MEM_HEAD_DIM = 128
N_BRANCHES = 3
BRANCH_WIDTH = 512
D_FF = 4 * D_MODEL
DEEPNORM_ALPHA = (2 * DEPTH) ** 0.25
DEEPNORM_BETA = (8 * DEPTH) ** -0.25
LN_EPS = 1e-5
RMS_EPS = 1e-6
NEG_INF = -1e30

IN_SPLITS = (
    3 * FOX_HEADS * FOX_HEAD_DIM,
    FOX_HEADS,
    MLA_Q_RANK,
    MLA_KV_RANK,
    MLA_ROPE_DIM,
    MEM_HEADS * MEM_HEAD_DIM,
    N_BRANCHES * D_MODEL,
)
D_IN = sum(IN_SPLITS)
SPLIT_POINTS = tuple(int(p) for p in np.cumsum(IN_SPLITS)[:-1])

kernel_name = 'hybrid_fox_mla_memory_deepnorm_trunk'


def _layer_norm(x, g, b):
    xf = x.astype(jnp.float32)
    mu = jnp.mean(xf, axis=-1, keepdims=True)
    var = jnp.mean(jnp.square(xf - mu), axis=-1, keepdims=True)
    y = (xf - mu) * lax.rsqrt(var + LN_EPS)
    return (y * g.astype(jnp.float32) + b.astype(jnp.float32)).astype(x.dtype)


def _rms_norm(x, g):
    xf = x.astype(jnp.float32)
    y = xf * lax.rsqrt(jnp.mean(jnp.square(xf), axis=-1, keepdims=True) + RMS_EPS)
    return (y * g.astype(jnp.float32)).astype(x.dtype)


def _rope_tables(positions):
    inv_freq = ROPE_BASE ** (-jnp.arange(0, MLA_ROPE_DIM, 2, dtype=jnp.float32) / MLA_ROPE_DIM)
    ang = positions.astype(jnp.float32)[..., None] * inv_freq
    return jnp.cos(ang), jnp.sin(ang)


def _rope(x, cos, sin):
    half = x.shape[-1] // 2
    x1 = x[..., :half].astype(jnp.float32)
    x2 = x[..., half:].astype(jnp.float32)
    return jnp.concatenate([x1 * cos - x2 * sin, x2 * cos + x1 * sin], axis=-1).astype(x.dtype)


def _swept_attention(q, k, v, bias_fn):
    seq = q.shape[2]
    scale = q.shape[-1] ** -0.5
    outs = []
    for i in range(seq // Q_BLOCK):
        q0 = i * Q_BLOCK
        k_len = q0 + Q_BLOCK
        logits = jnp.einsum('bhqd,bhkd->bhqk', q[:, :, q0:k_len], k[:, :, :k_len]).astype(jnp.float32)
        logits = logits * scale + bias_fn(q0, k_len)
        p = jax.nn.softmax(logits, axis=-1).astype(v.dtype)
        outs.append(jnp.einsum('bhqk,bhkd->bhqd', p, v[:, :, :k_len]))
    return jnp.concatenate(outs, axis=2)


def _chunk_causal_bias(q0, k_len):
    t_chunk = (q0 + jnp.arange(Q_BLOCK)) // CHUNK
    s_chunk = jnp.arange(k_len) // CHUNK
    allowed = s_chunk[None, :] <= t_chunk[:, None]
    return jnp.where(allowed, jnp.float32(0.0), jnp.float32(NEG_INF))[None, None]


def _mixer(h, mem, cos, sin, w_in, b_forget, w_uq, g_cq, w_ukv, g_ckv, w_mem_kv, w_br, w_out):
    B, S, _ = h.shape
    proj = h @ w_in
    fox_qkv, f_logit, c_q, c_kv, k_rope, q_mem, gate_logit = jnp.split(proj, SPLIT_POINTS, axis=-1)

    qkv = fox_qkv.reshape(B, S, 3, FOX_HEADS, FOX_HEAD_DIM).transpose(2, 0, 3, 1, 4)
    log_f = jax.nn.log_sigmoid(f_logit.astype(jnp.float32) + b_forget.astype(jnp.float32))
    cum_f = jnp.cumsum(log_f, axis=1).transpose(0, 2, 1)

    def fox_bias(q0, k_len):
        t = q0 + jnp.arange(Q_BLOCK)
        s = jnp.arange(k_len)
        decay = cum_f[:, :, q0:q0 + Q_BLOCK, None] - cum_f[:, :, None, :k_len]
        return jnp.where(s[None, :] <= t[:, None], decay, jnp.float32(NEG_INF))

    o_a = _swept_attention(qkv[0], qkv[1], qkv[2], fox_bias)
    o_a = o_a.transpose(0, 2, 1, 3).reshape(B, S, BRANCH_WIDTH)

    q_b = (_rms_norm(c_q, g_cq) @ w_uq).reshape(B, S, MLA_HEADS, MLA_NOPE_DIM + MLA_ROPE_DIM)
    q_b = q_b.transpose(0, 2, 1, 3)
    q_nope, q_pe = q_b[..., :MLA_NOPE_DIM], q_b[..., MLA_NOPE_DIM:]
    q_pe = _rope(q_pe, cos[:, None], sin[:, None])
    kv_b = (_rms_norm(c_kv, g_ckv) @ w_ukv).reshape(B, S, MLA_HEADS, MLA_NOPE_DIM + MLA_V_DIM)
    kv_b = kv_b.transpose(0, 2, 1, 3)
    k_nope, v_b = kv_b[..., :MLA_NOPE_DIM], kv_b[..., MLA_NOPE_DIM:]
    k_pe = _rope(k_rope, cos, sin)[:, None]
    q_full = jnp.concatenate([q_nope, q_pe], axis=-1)
    k_full = jnp.concatenate([k_nope, jnp.broadcast_to(k_pe, (B, MLA_HEADS, S, MLA_ROPE_DIM))], axis=-1)
    o_b = _swept_attention(q_full, k_full, v_b, _chunk_causal_bias)
    o_b = o_b.transpose(0, 2, 1, 3).reshape(B, S, BRANCH_WIDTH)

    mkv = (mem @ w_mem_kv).reshape(B, mem.shape[1], 2, MEM_HEADS, MEM_HEAD_DIM).transpose(2, 0, 3, 1, 4)
    qm = q_mem.reshape(B, S, MEM_HEADS, MEM_HEAD_DIM).transpose(0, 2, 1, 3)
    logits_m = jnp.einsum('bhqd,bhmd->bhqm', qm, mkv[0]).astype(jnp.float32) * (MEM_HEAD_DIM ** -0.5)
    p_m = jax.nn.softmax(logits_m, axis=-1).astype(mkv.dtype)
    o_c = jnp.einsum('bhqm,bhmd->bhqd', p_m, mkv[1]).transpose(0, 2, 1, 3).reshape(B, S, BRANCH_WIDTH)

    branches = jnp.stack([o_a, o_b, o_c], axis=2)
    branch_proj = jnp.einsum('bsnc,ncd->bsnd', branches, w_br)
    gates = jax.nn.sigmoid(gate_logit.reshape(B, S, N_BRANCHES, D_MODEL))
    merged = jnp.sum(gates * branch_proj, axis=2)
    return merged @ w_out


def _fwd_setup_inputs(seed: int = 0) -> dict:
    key = jax.random.key(seed)
    ks = jax.random.split(key, 24)
    L = DEPTH

    def w(k, shape, fan_in, scale=1.0):
        return jax.random.normal(k, shape, jnp.float32) * (fan_in ** -0.5) * scale

    def gain(k, shape):
        return 1.0 + 0.02 * jax.random.normal(k, shape, jnp.float32)

    def bias(k, shape):
        return 0.02 * jax.random.normal(k, shape, jnp.float32)

    x = jax.random.normal(ks[0], (BATCH, SEQ, D_MODEL), jnp.float32)
    mem = jax.random.normal(ks[1], (BATCH, N_MEM, D_MODEL), jnp.float32)
    offsets = jax.random.randint(ks[2], (BATCH, 1), 0, 8192, dtype=jnp.int32)
    positions = offsets + jnp.arange(SEQ, dtype=jnp.int32)[None, :]
    return {
        'x': x,
        'mem': mem,
        'positions': positions,
        'ln_in_g': gain(ks[3], (D_MODEL,)),
        'ln_in_b': bias(ks[4], (D_MODEL,)),
        'w_in': w(ks[5], (L, D_MODEL, D_IN), D_MODEL),
        'b_forget': jax.random.uniform(ks[6], (L, FOX_HEADS), jnp.float32, 1.0, 6.0),
        'w_uq': w(ks[7], (L, MLA_Q_RANK, MLA_HEADS * (MLA_NOPE_DIM + MLA_ROPE_DIM)), MLA_Q_RANK),
        'g_cq': gain(ks[8], (L, MLA_Q_RANK)),
        'w_ukv': w(ks[9], (L, MLA_KV_RANK, MLA_HEADS * (MLA_NOPE_DIM + MLA_V_DIM)), MLA_KV_RANK),
        'g_ckv': gain(ks[10], (L, MLA_KV_RANK)),
        'w_mem_kv': w(ks[11], (L, D_MODEL, 2 * MEM_HEADS * MEM_HEAD_DIM), D_MODEL),
        'w_br': w(ks[12], (L, N_BRANCHES, BRANCH_WIDTH, D_MODEL), BRANCH_WIDTH, DEEPNORM_BETA),
        'w_out': w(ks[13], (L, D_MODEL, D_MODEL), D_MODEL, DEEPNORM_BETA),
        'ln1_g': gain(ks[14], (L, D_MODEL)),
        'ln1_b': bias(ks[15], (L, D_MODEL)),
        'w_ff1': w(ks[16], (L, D_MODEL, D_FF), D_MODEL),
        'w_ff2': w(ks[17], (L, D_FF, D_MODEL), D_FF, DEEPNORM_BETA),
        'ln2_g': gain(ks[18], (L, D_MODEL)),
        'ln2_b': bias(ks[19], (L, D_MODEL)),
    }


def _fwd_reference(x, mem, positions, ln_in_g, ln_in_b, w_in, b_forget, w_uq, g_cq, w_ukv, g_ckv,
              w_mem_kv, w_br, w_out, ln1_g, ln1_b, w_ff1, w_ff2, ln2_g, ln2_b):
    cos, sin = _rope_tables(positions)
    h = _layer_norm(x, ln_in_g, ln_in_b)
    for l in range(DEPTH):
        y = _mixer(h, mem, cos, sin, w_in[l], b_forget[l], w_uq[l], g_cq[l], w_ukv[l], g_ckv[l],
                   w_mem_kv[l], w_br[l], w_out[l])
        h = _layer_norm(DEEPNORM_ALPHA * h + y, ln1_g[l], ln1_b[l])
        ff = jnp.square(jax.nn.relu(h @ w_ff1[l])) @ w_ff2[l]
        h = _layer_norm(DEEPNORM_ALPHA * h + ff, ln2_g[l], ln2_b[l])
    return h


import jax as _jax
import jax.numpy as _jnp

TWIN_FORMAT = 'train_step'
FWD_PARAMS = ['x', 'mem', 'positions', 'ln_in_g', 'ln_in_b', 'w_in', 'b_forget', 'w_uq', 'g_cq', 'w_ukv', 'g_ckv', 'w_mem_kv', 'w_br', 'w_out', 'ln1_g', 'ln1_b', 'w_ff1', 'w_ff2', 'ln2_g', 'ln2_b']
TWIN_WEIGHTS = ['ln_in_g', 'ln_in_b', 'w_in', 'b_forget', 'w_uq', 'g_cq', 'w_ukv', 'g_ckv', 'w_mem_kv', 'w_br', 'w_out', 'ln1_g', 'ln1_b', 'w_ff1', 'w_ff2', 'ln2_g', 'ln2_b']
TWIN_DIFF_INPUT = 'x'
TWIN_INPUTS = ['x', 'mem', 'positions', 'ln_in_g', 'ln_in_b', 'w_in', 'b_forget', 'w_uq', 'g_cq', 'w_ukv', 'g_ckv', 'w_mem_kv', 'w_br', 'w_out', 'ln1_g', 'ln1_b', 'w_ff1', 'w_ff2', 'ln2_g', 'ln2_b', 'loss_target', 'm_ln_in_g', 'm_ln_in_b', 'm_w_in', 'm_b_forget', 'm_w_uq', 'm_g_cq', 'm_w_ukv', 'm_g_ckv', 'm_w_mem_kv', 'm_w_br', 'm_w_out', 'm_ln1_g', 'm_ln1_b', 'm_w_ff1', 'm_w_ff2', 'm_ln2_g', 'm_ln2_b', 'v_ln_in_g', 'v_ln_in_b', 'v_w_in', 'v_b_forget', 'v_w_uq', 'v_g_cq', 'v_w_ukv', 'v_g_ckv', 'v_w_mem_kv', 'v_w_br', 'v_w_out', 'v_ln1_g', 'v_ln1_b', 'v_w_ff1', 'v_w_ff2', 'v_ln2_g', 'v_ln2_b']
TWIN_OUTPUTS = ['loss', 'grad_x', 'grad_ln_in_g', 'grad_ln_in_b', 'grad_w_in', 'grad_b_forget', 'grad_w_uq', 'grad_g_cq', 'grad_w_ukv', 'grad_g_ckv', 'grad_w_mem_kv', 'grad_w_br', 'grad_w_out', 'grad_ln1_g', 'grad_ln1_b', 'grad_w_ff1', 'grad_w_ff2', 'grad_ln2_g', 'grad_ln2_b', 'delta_ln_in_g', 'delta_ln_in_b', 'delta_w_in', 'delta_b_forget', 'delta_w_uq', 'delta_g_cq', 'delta_w_ukv', 'delta_g_ckv', 'delta_w_mem_kv', 'delta_w_br', 'delta_w_out', 'delta_ln1_g', 'delta_ln1_b', 'delta_w_ff1', 'delta_w_ff2', 'delta_ln2_g', 'delta_ln2_b', 'new_m_ln_in_g', 'new_m_ln_in_b', 'new_m_w_in', 'new_m_b_forget', 'new_m_w_uq', 'new_m_g_cq', 'new_m_w_ukv', 'new_m_g_ckv', 'new_m_w_mem_kv', 'new_m_w_br', 'new_m_w_out', 'new_m_ln1_g', 'new_m_ln1_b', 'new_m_w_ff1', 'new_m_w_ff2', 'new_m_ln2_g', 'new_m_ln2_b', 'new_v_ln_in_g', 'new_v_ln_in_b', 'new_v_w_in', 'new_v_b_forget', 'new_v_w_uq', 'new_v_g_cq', 'new_v_w_ukv', 'new_v_g_ckv', 'new_v_w_mem_kv', 'new_v_w_br', 'new_v_w_out', 'new_v_ln1_g', 'new_v_ln1_b', 'new_v_w_ff1', 'new_v_w_ff2', 'new_v_ln2_g', 'new_v_ln2_b']
TWIN_LEAF_KINDS = {'loss': 'loss', 'grad_x': 'grad_x', 'grad_ln_in_g': 'grad_w', 'grad_ln_in_b': 'grad_w', 'grad_w_in': 'grad_w', 'grad_b_forget': 'grad_w', 'grad_w_uq': 'grad_w', 'grad_g_cq': 'grad_w', 'grad_w_ukv': 'grad_w', 'grad_g_ckv': 'grad_w', 'grad_w_mem_kv': 'grad_w', 'grad_w_br': 'grad_w', 'grad_w_out': 'grad_w', 'grad_ln1_g': 'grad_w', 'grad_ln1_b': 'grad_w', 'grad_w_ff1': 'grad_w', 'grad_w_ff2': 'grad_w', 'grad_ln2_g': 'grad_w', 'grad_ln2_b': 'grad_w', 'delta_ln_in_g': 'delta_w', 'delta_ln_in_b': 'delta_w', 'delta_w_in': 'delta_w', 'delta_b_forget': 'delta_w', 'delta_w_uq': 'delta_w', 'delta_g_cq': 'delta_w', 'delta_w_ukv': 'delta_w', 'delta_g_ckv': 'delta_w', 'delta_w_mem_kv': 'delta_w', 'delta_w_br': 'delta_w', 'delta_w_out': 'delta_w', 'delta_ln1_g': 'delta_w', 'delta_ln1_b': 'delta_w', 'delta_w_ff1': 'delta_w', 'delta_w_ff2': 'delta_w', 'delta_ln2_g': 'delta_w', 'delta_ln2_b': 'delta_w', 'new_m_ln_in_g': 'new_m', 'new_m_ln_in_b': 'new_m', 'new_m_w_in': 'new_m', 'new_m_b_forget': 'new_m', 'new_m_w_uq': 'new_m', 'new_m_g_cq': 'new_m', 'new_m_w_ukv': 'new_m', 'new_m_g_ckv': 'new_m', 'new_m_w_mem_kv': 'new_m', 'new_m_w_br': 'new_m', 'new_m_w_out': 'new_m', 'new_m_ln1_g': 'new_m', 'new_m_ln1_b': 'new_m', 'new_m_w_ff1': 'new_m', 'new_m_w_ff2': 'new_m', 'new_m_ln2_g': 'new_m', 'new_m_ln2_b': 'new_m', 'new_v_ln_in_g': 'new_v', 'new_v_ln_in_b': 'new_v', 'new_v_w_in': 'new_v', 'new_v_b_forget': 'new_v', 'new_v_w_uq': 'new_v', 'new_v_g_cq': 'new_v', 'new_v_w_ukv': 'new_v', 'new_v_g_ckv': 'new_v', 'new_v_w_mem_kv': 'new_v', 'new_v_w_br': 'new_v', 'new_v_w_out': 'new_v', 'new_v_ln1_g': 'new_v', 'new_v_ln1_b': 'new_v', 'new_v_w_ff1': 'new_v', 'new_v_w_ff2': 'new_v', 'new_v_ln2_g': 'new_v', 'new_v_ln2_b': 'new_v'}


def _forward(args):
    return _fwd_reference(*[args[k] for k in FWD_PARAMS])


def _output_shape():
    out = _jax.eval_shape(lambda: _forward(_fwd_setup_inputs(0)))
    return out.shape, out.dtype

N_MICROBATCH = 1
ADAM_LR = 0.001
ADAM_B1 = 0.9
ADAM_B2 = 0.999
ADAM_EPS = 1e-08
ADAM_WD = 0.01
ADAM_STEP = 10
PER_EXAMPLE_BATCH_AXIS = {'x': 0, 'mem': 0, 'positions': 0, 'loss_target': 0}
SHARED_INPUTS = []
_WEIGHT_DTYPES = {'ln_in_g': _jnp.float32, 'ln_in_b': _jnp.float32, 'w_in': _jnp.float32, 'b_forget': _jnp.float32, 'w_uq': _jnp.float32, 'g_cq': _jnp.float32, 'w_ukv': _jnp.float32, 'g_ckv': _jnp.float32, 'w_mem_kv': _jnp.float32, 'w_br': _jnp.float32, 'w_out': _jnp.float32, 'ln1_g': _jnp.float32, 'ln1_b': _jnp.float32, 'w_ff1': _jnp.float32, 'w_ff2': _jnp.float32, 'ln2_g': _jnp.float32, 'ln2_b': _jnp.float32}
MOMENT_SCALE = {'ln_in_g': 7.573688e-01, 'ln_in_b': 4.815938e-01, 'w_in': 2.743500e-03, 'b_forget': 2.411872e-02, 'w_uq': 1.752998e-03, 'g_cq': 2.459544e-03, 'w_ukv': 3.043308e-03, 'g_ckv': 6.402722e-03, 'w_mem_kv': 1.671234e-03, 'w_br': 7.378953e-03, 'w_out': 1.255758e-02, 'ln1_g': 8.916814e-01, 'ln1_b': 4.991950e-01, 'w_ff1': 3.120007e-02, 'w_ff2': 1.720255e-01, 'ln2_g': 1.609933e+01, 'ln2_b': 3.669060e+00}


def _to_microbatches(a, axis):
    t = _jnp.moveaxis(a, axis, 0)
    t = t.reshape((N_MICROBATCH, t.shape[0] // N_MICROBATCH) + t.shape[1:])
    return _jnp.moveaxis(t, 1, axis + 1)


def setup_inputs(seed: int = 0) -> dict:
    inp = _fwd_setup_inputs(seed)
    key = _jax.random.fold_in(_jax.random.key(seed), 7919)
    shape, _ = _output_shape()
    out = dict(inp)
    out["loss_target"] = _jax.random.normal(_jax.random.fold_in(key, 0), shape, _jnp.float32)
    for i, name in enumerate(TWIN_WEIGHTS):
        w = inp[name].astype(_jnp.float32)
        if MOMENT_SCALE is None:
            s = _jnp.sqrt(_jnp.mean(_jnp.square(w)) + 1e-30)
        else:
            s = MOMENT_SCALE[name]
        km, kv = _jax.random.split(_jax.random.fold_in(key, i + 1))
        out[name] = w
        out["m_" + name] = s * _jax.random.normal(km, w.shape, _jnp.float32)
        out["v_" + name] = (s * s) * _jax.random.uniform(kv, w.shape, _jnp.float32, 0.5, 1.5)
    if N_MICROBATCH > 1:
        for name, axis in PER_EXAMPLE_BATCH_AXIS.items():
            out[name] = _to_microbatches(out[name], axis)
    return {'x': out['x'], 'mem': out['mem'], 'positions': out['positions'], 'ln_in_g': out['ln_in_g'], 'ln_in_b': out['ln_in_b'], 'w_in': out['w_in'], 'b_forget': out['b_forget'], 'w_uq': out['w_uq'], 'g_cq': out['g_cq'], 'w_ukv': out['w_ukv'], 'g_ckv': out['g_ckv'], 'w_mem_kv': out['w_mem_kv'], 'w_br': out['w_br'], 'w_out': out['w_out'], 'ln1_g': out['ln1_g'], 'ln1_b': out['ln1_b'], 'w_ff1': out['w_ff1'], 'w_ff2': out['w_ff2'], 'ln2_g': out['ln2_g'], 'ln2_b': out['ln2_b'], 'loss_target': out['loss_target'], 'm_ln_in_g': out['m_ln_in_g'], 'm_ln_in_b': out['m_ln_in_b'], 'm_w_in': out['m_w_in'], 'm_b_forget': out['m_b_forget'], 'm_w_uq': out['m_w_uq'], 'm_g_cq': out['m_g_cq'], 'm_w_ukv': out['m_w_ukv'], 'm_g_ckv': out['m_g_ckv'], 'm_w_mem_kv': out['m_w_mem_kv'], 'm_w_br': out['m_w_br'], 'm_w_out': out['m_w_out'], 'm_ln1_g': out['m_ln1_g'], 'm_ln1_b': out['m_ln1_b'], 'm_w_ff1': out['m_w_ff1'], 'm_w_ff2': out['m_w_ff2'], 'm_ln2_g': out['m_ln2_g'], 'm_ln2_b': out['m_ln2_b'], 'v_ln_in_g': out['v_ln_in_g'], 'v_ln_in_b': out['v_ln_in_b'], 'v_w_in': out['v_w_in'], 'v_b_forget': out['v_b_forget'], 'v_w_uq': out['v_w_uq'], 'v_g_cq': out['v_g_cq'], 'v_w_ukv': out['v_w_ukv'], 'v_g_ckv': out['v_g_ckv'], 'v_w_mem_kv': out['v_w_mem_kv'], 'v_w_br': out['v_w_br'], 'v_w_out': out['v_w_out'], 'v_ln1_g': out['v_ln1_g'], 'v_ln1_b': out['v_ln1_b'], 'v_w_ff1': out['v_w_ff1'], 'v_w_ff2': out['v_w_ff2'], 'v_ln2_g': out['v_ln2_g'], 'v_ln2_b': out['v_ln2_b']}


def _loss(weights, diff, rest, loss_target):
    with _jax.named_scope("forward"):
        args = {**rest, TWIN_DIFF_INPUT: diff, **{k: w.astype(_WEIGHT_DTYPES[k]) for k, w in weights.items()}}
        y = _forward(args)
    with _jax.named_scope("loss_head"):
        err = _jnp.square(y.astype(_jnp.float32) - loss_target)
        return 0.5 * _jnp.sum(_jnp.mean(err, axis=-1)) if err.ndim else 0.5 * err


def _adamw(w, g, m, v):
    m = ADAM_B1 * m + (1.0 - ADAM_B1) * g
    v = ADAM_B2 * v + (1.0 - ADAM_B2) * _jnp.square(g)
    m_hat = m / (1.0 - ADAM_B1 ** ADAM_STEP)
    v_hat = v / (1.0 - ADAM_B2 ** ADAM_STEP)
    delta = -ADAM_LR * (m_hat / (_jnp.sqrt(v_hat) + ADAM_EPS) + ADAM_WD * w)
    return delta, m, v


def reference(x, mem, positions, ln_in_g, ln_in_b, w_in, b_forget, w_uq, g_cq, w_ukv, g_ckv, w_mem_kv, w_br, w_out, ln1_g, ln1_b, w_ff1, w_ff2, ln2_g, ln2_b, loss_target, m_ln_in_g, m_ln_in_b, m_w_in, m_b_forget, m_w_uq, m_g_cq, m_w_ukv, m_g_ckv, m_w_mem_kv, m_w_br, m_w_out, m_ln1_g, m_ln1_b, m_w_ff1, m_w_ff2, m_ln2_g, m_ln2_b, v_ln_in_g, v_ln_in_b, v_w_in, v_b_forget, v_w_uq, v_g_cq, v_w_ukv, v_g_ckv, v_w_mem_kv, v_w_br, v_w_out, v_ln1_g, v_ln1_b, v_w_ff1, v_w_ff2, v_ln2_g, v_ln2_b):
    given = dict(x=x, mem=mem, positions=positions, ln_in_g=ln_in_g, ln_in_b=ln_in_b, w_in=w_in, b_forget=b_forget, w_uq=w_uq, g_cq=g_cq, w_ukv=w_ukv, g_ckv=g_ckv, w_mem_kv=w_mem_kv, w_br=w_br, w_out=w_out, ln1_g=ln1_g, ln1_b=ln1_b, w_ff1=w_ff1, w_ff2=w_ff2, ln2_g=ln2_g, ln2_b=ln2_b, loss_target=loss_target, m_ln_in_g=m_ln_in_g, m_ln_in_b=m_ln_in_b, m_w_in=m_w_in, m_b_forget=m_b_forget, m_w_uq=m_w_uq, m_g_cq=m_g_cq, m_w_ukv=m_w_ukv, m_g_ckv=m_g_ckv, m_w_mem_kv=m_w_mem_kv, m_w_br=m_w_br, m_w_out=m_w_out, m_ln1_g=m_ln1_g, m_ln1_b=m_ln1_b, m_w_ff1=m_w_ff1, m_w_ff2=m_w_ff2, m_ln2_g=m_ln2_g, m_ln2_b=m_ln2_b, v_ln_in_g=v_ln_in_g, v_ln_in_b=v_ln_in_b, v_w_in=v_w_in, v_b_forget=v_b_forget, v_w_uq=v_w_uq, v_g_cq=v_g_cq, v_w_ukv=v_w_ukv, v_g_ckv=v_g_ckv, v_w_mem_kv=v_w_mem_kv, v_w_br=v_w_br, v_w_out=v_w_out, v_ln1_g=v_ln1_g, v_ln1_b=v_ln1_b, v_w_ff1=v_w_ff1, v_w_ff2=v_w_ff2, v_ln2_g=v_ln2_g, v_ln2_b=v_ln2_b)
    weights = {n: given[n] for n in TWIN_WEIGHTS}
    shared = {n: given[n] for n in SHARED_INPUTS}
    per_example = {n: given[n] for n in ['x', 'mem', 'positions']}
    grad_fn = _jax.value_and_grad(_loss, argnums=(0, 1))

    def one_microbatch(ex, loss_target):
        ex = dict(ex)
        diff = ex.pop(TWIN_DIFF_INPUT)
        return grad_fn(weights, diff, {**shared, **ex}, loss_target)

    if N_MICROBATCH == 1:
        loss, (grad_w, grad_x) = one_microbatch(per_example, given["loss_target"])
    else:
        def body(carry, xs):
            loss_sum, grad_sum = carry
            l_k, (gw_k, gx_k) = one_microbatch(xs[0], xs[1])
            with _jax.named_scope("update"):
                return (loss_sum + l_k, _jax.tree.map(_jnp.add, grad_sum, gw_k)), gx_k

        init = (_jnp.zeros((), _jnp.float32), _jax.tree.map(_jnp.zeros_like, weights))
        (loss, grad_w), grad_x = _jax.lax.scan(body, init, (per_example, given["loss_target"]))
    with _jax.named_scope("update"):
        delta_w, new_m, new_v = {}, {}, {}
        for n in TWIN_WEIGHTS:
            delta_w[n], new_m[n], new_v[n] = _adamw(weights[n], grad_w[n], given["m_" + n], given["v_" + n])
    return (loss, grad_x, *[grad_w[n] for n in TWIN_WEIGHTS], *[delta_w[n] for n in TWIN_WEIGHTS],
            *[new_m[n] for n in TWIN_WEIGHTS], *[new_v[n] for n in TWIN_WEIGHTS])
```

```python
import functools

import numpy as np
import jax
import jax.numpy as jnp
from jax import lax
from jax.experimental import pallas as pl
from jax.experimental.pallas import tpu as pltpu

f32, bf16 = jnp.float32, jnp.bfloat16
SDS = jax.ShapeDtypeStruct

N_DEV = 8
MESH = pl.DeviceIdType.MESH
LANES = 128
VMEM_LIMIT = 56 * 1024 * 1024

FOX_HEADS, FOX_HD = 8, 64
MLA_HEADS, NOPE, ROPE, MLA_VD = 8, 64, 32, 64
Q_RANK, KV_RANK = 384, 256
ROPE_BASE = 10000.0
MEM_HEADS, MEM_HD = 4, 128
BW = 512
CHUNK = 64
LN_EPS, RMS_EPS, NEG_INF = 1e-5, 1e-6, -1e30
ADAM_LR, ADAM_B1, ADAM_B2, ADAM_EPS, ADAM_WD, ADAM_STEP = 0.001, 0.9, 0.999, 1e-08, 0.01, 10

ATT_TILE = 256

C_Q, C_K, C_V, C_F, C_KPE, C_CKV, C_QM, C_PAD, C_CQ, C_GATE = 0, 512, 1024, 1536, 1664, 1792, 2048, 2560, 2688, 3072
O_F, O_CQ, O_CKV, O_KR, O_QM, O_GATE = 1536, 1544, 1928, 2184, 2216, 2728


def _pick(dim, pref):
    if dim <= pref:
        return dim
    for c in (1024, 768, 512, 384, 256, 128, 64, 32, 16, 8):
        if c <= pref and dim % c == 0:
            return c
    return dim


def _round_up(n, m):
    return -(-n // m) * m


def _params(sem=None):
    return pltpu.CompilerParams(dimension_semantics=sem, vmem_limit_bytes=VMEM_LIMIT)


_DIMS = {"nn": (((1,), (0,)), ((), ())), "nt": (((1,), (1,)), ((), ())), "tn": (((0,), (0,)), ((), ()))}


def _mm(a, b, *, mode="nn", out_dtype=f32, epi="none", extra=None, alpha=1.0, tm=512, tn=512, tk=1024, name):
    if mode == "nn":
        (M, K), (_, N) = a.shape, b.shape
    elif mode == "nt":
        (M, K), (N, _) = a.shape, b.shape
    else:
        (K, M), (_, N) = a.shape, b.shape
    tm, tn, tk = _pick(M, tm), _pick(N, tn), _pick(K, tk)
    nk = K // tk
    a_spec = pl.BlockSpec((tk, tm), lambda i, j, k: (k, i)) if mode == "tn" else pl.BlockSpec((tm, tk), lambda i, j, k: (i, k))
    b_spec = pl.BlockSpec((tn, tk), lambda i, j, k: (j, k)) if mode == "nt" else pl.BlockSpec((tk, tn), lambda i, j, k: (k, j))
    o_spec = pl.BlockSpec((tm, tn), lambda i, j, k: (i, j))
    n_out = 2 if epi == "relu2" else 1
    dims = _DIMS[mode]

    def body(*refs):
        a_ref, b_ref = refs[0], refs[1]
        e_ref = refs[2] if extra is not None else None
        outs = refs[-1 - n_out:-1]
        acc = refs[-1]
        k = pl.program_id(2)

        @pl.when(k == 0)
        def _():
            acc[...] = jnp.zeros_like(acc)

        acc[...] += lax.dot_general(a_ref[...].astype(bf16), b_ref[...].astype(bf16), dims, preferred_element_type=f32)

        @pl.when(k == nk - 1)
        def _():
            r = acc[...]
            if epi == "none":
                outs[0][...] = r.astype(out_dtype)
            elif epi == "relu2":
                p = jnp.maximum(r, 0.0)
                outs[0][...] = (p * p).astype(bf16)
                outs[1][...] = (2.0 * p).astype(bf16)
            elif epi == "mul":
                outs[0][...] = (r * e_ref[...].astype(f32)).astype(out_dtype)
            else:
                outs[0][...] = (r + alpha * e_ref[...].astype(f32)).astype(out_dtype)

    ins, in_specs = [a, b], [a_spec, b_spec]
    if extra is not None:
        ins.append(extra)
        in_specs.append(o_spec)
    if epi == "relu2":
        out_shape, out_specs = [SDS((M, N), bf16), SDS((M, N), bf16)], [o_spec, o_spec]
    else:
        out_shape, out_specs = SDS((M, N), out_dtype), o_spec
    return pl.pallas_call(
        body, grid=(M // tm, N // tn, nk), in_specs=in_specs, out_specs=out_specs, out_shape=out_shape,
        scratch_shapes=[pltpu.VMEM((tm, tn), f32)], name=name,
        compiler_params=_params(("parallel", "parallel", "arbitrary")))(*ins)


def _lane_permute(src, cmap, out_dtype, name):
    R, _ = src.shape
    cmap = np.asarray(cmap, np.int64)
    nb = cmap.shape[0] // LANES
    srcs = [sorted({int(c) // LANES for c in cmap[b * LANES:(b + 1) * LANES] if c >= 0}) or [0] for b in range(nb)]
    smax = max(len(s) for s in srcs)
    nsrc = np.array([len(s) for s in srcs], np.int32)
    blk = np.array([s + [s[-1]] * (smax - len(s)) for s in srcs], np.int32).reshape(-1)
    exact = src.dtype == f32
    tr = _pick(R, 1024)

    def body(blk_ref, ns_ref, src_ref, cm_ref, o_ref, acc):
        j, s = pl.program_id(1), pl.program_id(2)

        @pl.when(s == 0)
        def _():
            acc[...] = jnp.zeros_like(acc)

        @pl.when(s < ns_ref[j])
        def _():
            rows = lax.broadcasted_iota(jnp.int32, (LANES, LANES), 0) + blk_ref[j * smax + s] * LANES
            sel = rows == cm_ref[...]
            if exact:
                acc[...] += jnp.dot(src_ref[...], sel.astype(f32), precision=lax.Precision.HIGHEST, preferred_element_type=f32)
            else:
                acc[...] += jnp.dot(src_ref[...], sel.astype(src_ref.dtype), preferred_element_type=f32)

        @pl.when(s == smax - 1)
        def _():
            o_ref[...] = acc[...].astype(out_dtype)

    grid_spec = pltpu.PrefetchScalarGridSpec(
        num_scalar_prefetch=2, grid=(R // tr, nb, smax),
        in_specs=[pl.BlockSpec((tr, LANES), lambda i, j, s, blk_ref, ns_ref: (i, blk_ref[j * smax + s])),
                  pl.BlockSpec((1, LANES), lambda i, j, s, blk_ref, ns_ref: (0, j))],
        out_specs=pl.BlockSpec((tr, LANES), lambda i, j, s, blk_ref, ns_ref: (i, j)),
        scratch_shapes=[pltpu.VMEM((tr, LANES), f32)])
    return pl.pallas_call(
        body, grid_spec=grid_spec, out_shape=SDS((R, nb * LANES), out_dtype), name=name,
        compiler_params=_params(("parallel", "parallel", "arbitrary")))(
            jnp.asarray(blk), jnp.asarray(nsrc), src, jnp.asarray(cmap.astype(np.int32)).reshape(1, -1))


def _ln_fwd(h, y, g, b, alpha, name):
    T, D = y.shape
    tr = _pick(T, 256)
    row = pl.BlockSpec((tr, D), lambda i: (i, 0))
    vec = pl.BlockSpec((1, D), lambda i: (0, 0))
    col = pl.BlockSpec((tr, 1), lambda i: (i, 0))

    def body(*refs):
        if h is None:
            y_ref, g_ref, b_ref, o_ref, xh_ref, rs_ref = refs
            u = y_ref[...]
        else:
            h_ref, y_ref, g_ref, b_ref, o_ref, xh_ref, rs_ref = refs
            u = alpha * h_ref[...] + y_ref[...]
        mu = jnp.mean(u, axis=-1, keepdims=True)
        d = u - mu
        var = jnp.mean(d * d, axis=-1, keepdims=True)
        rstd = lax.rsqrt(var + LN_EPS)
        xh = d * rstd
        xh_ref[...] = xh
        rs_ref[...] = rstd
        o_ref[...] = xh * g_ref[...] + b_ref[...]

    ins = ([] if h is None else [h]) + [y, g.reshape(1, D), b.reshape(1, D)]
    specs = ([] if h is None else [row]) + [row, vec, vec]
    return pl.pallas_call(
        body, grid=(T // tr,), in_specs=specs, out_specs=[row, row, col],
        out_shape=[SDS((T, D), f32), SDS((T, D), f32), SDS((T, 1), f32)], name=name,
        compiler_params=_params(("parallel",)))(*ins)


def _ln_bwd(dy, xh, rstd, g, name):
    T, D = dy.shape
    tr = _pick(T, 256)
    row = pl.BlockSpec((tr, D), lambda i: (i, 0))
    vec = pl.BlockSpec((1, D), lambda i: (0, 0))
    col = pl.BlockSpec((tr, 1), lambda i: (i, 0))

    def body(dy_ref, xh_ref, rs_ref, g_ref, du_ref, dg_ref, db_ref):
        @pl.when(pl.program_id(0) == 0)
        def _():
            dg_ref[...] = jnp.zeros_like(dg_ref)
            db_ref[...] = jnp.zeros_like(db_ref)

        dyv, xhv = dy_ref[...], xh_ref[...]
        dxh = dyv * g_ref[...]
        m1 = jnp.mean(dxh, axis=-1, keepdims=True)
        m2 = jnp.mean(dxh * xhv, axis=-1, keepdims=True)
        du_ref[...] = rs_ref[...] * (dxh - m1 - xhv * m2)
        dg_ref[...] += jnp.sum(dyv * xhv, axis=0, keepdims=True)
        db_ref[...] += jnp.sum(dyv, axis=0, keepdims=True)

    return pl.pallas_call(
        body, grid=(T // tr,), in_specs=[row, row, col, vec], out_specs=[row, vec, vec],
        out_shape=[SDS((T, D), f32), SDS((1, D), f32), SDS((1, D), f32)], name=name,
        compiler_params=_params(("arbitrary",)))(dy, xh, rstd, g.reshape(1, D))


def _rms_fwd(proj, col_off, R, g, name):
    T = proj.shape[0]
    tr = _pick(T, 512)
    cb = col_off // R

    def body(c_ref, g_ref, o_ref):
        c = c_ref[...]
        r = lax.rsqrt(jnp.mean(c * c, axis=-1, keepdims=True) + RMS_EPS)
        o_ref[...] = (c * r * g_ref[...]).astype(bf16)

    return pl.pallas_call(
        body, grid=(T // tr,), in_specs=[pl.BlockSpec((tr, R), lambda i: (i, cb)), pl.BlockSpec((1, R), lambda i: (0, 0))],
        out_specs=pl.BlockSpec((tr, R), lambda i: (i, 0)), out_shape=SDS((T, R), bf16), name=name,
        compiler_params=_params(("parallel",)))(proj, g.reshape(1, R))


def _rms_bwd(dy, proj, col_off, R, g, name):
    T = proj.shape[0]
    tr = _pick(T, 512)
    cb = col_off // R

    def body(dy_ref, c_ref, g_ref, dc_ref, dg_ref):
        @pl.when(pl.program_id(0) == 0)
        def _():
            dg_ref[...] = jnp.zeros_like(dg_ref)

        c, dyv = c_ref[...], dy_ref[...]
        r = lax.rsqrt(jnp.mean(c * c, axis=-1, keepdims=True) + RMS_EPS)
        t = dyv * g_ref[...]
        mt = jnp.mean(t * c, axis=-1, keepdims=True)
        dc_ref[...] = (r * t - c * (r * r * r) * mt).astype(bf16)
        dg_ref[...] += jnp.sum(dyv * c * r, axis=0, keepdims=True)

    return pl.pallas_call(
        body, grid=(T // tr,),
        in_specs=[pl.BlockSpec((tr, R), lambda i: (i, 0)), pl.BlockSpec((tr, R), lambda i: (i, cb)), pl.BlockSpec((1, R), lambda i: (0, 0))],
        out_specs=[pl.BlockSpec((tr, R), lambda i: (i, 0)), pl.BlockSpec((1, R), lambda i: (0, 0))],
        out_shape=[SDS((T, R), bf16), SDS((1, R), f32)], name=name,
        compiler_params=_params(("arbitrary",)))(dy, proj, g.reshape(1, R))


def _tri(n, lower):
    r = lax.broadcasted_iota(jnp.int32, (n, n), 0)
    c = lax.broadcasted_iota(jnp.int32, (n, n), 1)
    return ((r >= c) if lower else (c >= r)).astype(f32)


def _gate_fwd(proj, bpad, B, S, name):
    ch = _pick(S, 256)
    nch = S // ch
    cb = C_F // LANES

    def body(f_ref, b_ref, o_ref):
        tri = _tri(ch, True)
        carry = jnp.zeros((1, LANES), f32)
        for c in range(nch):
            z = f_ref[c * ch:(c + 1) * ch, :] + b_ref[...]
            lf = jnp.minimum(z, 0.0) - jnp.log1p(jnp.exp(-jnp.abs(z)))
            cs = jnp.dot(tri, lf, precision=lax.Precision.HIGHEST, preferred_element_type=f32) + carry
            o_ref[c * ch:(c + 1) * ch, :] = cs
            carry = cs[ch - 1:ch, :]

    return pl.pallas_call(
        body, grid=(B,), in_specs=[pl.BlockSpec((S, LANES), lambda b: (b, cb)), pl.BlockSpec((1, LANES), lambda b: (0, 0))],
        out_specs=pl.BlockSpec((S, LANES), lambda b: (b, 0)), out_shape=SDS((B * S, LANES), f32), name=name,
        compiler_params=_params(("parallel",)))(proj, bpad)


def _gate_bwd(dcum, proj, bpad, B, S, name):
    ch = _pick(S, 256)
    nch = S // ch
    cb = C_F // LANES

    def body(d_ref, f_ref, b_ref, dz_ref, db_ref):
        @pl.when(pl.program_id(0) == 0)
        def _():
            db_ref[...] = jnp.zeros_like(db_ref)

        tri = _tri(ch, False)
        carry = jnp.zeros((1, LANES), f32)
        dbs = jnp.zeros((1, LANES), f32)
        for c in reversed(range(nch)):
            d = d_ref[c * ch:(c + 1) * ch, :]
            dlf = jnp.dot(tri, d, precision=lax.Precision.HIGHEST, preferred_element_type=f32) + carry
            carry = carry + jnp.sum(d, axis=0, keepdims=True)
            z = f_ref[c * ch:(c + 1) * ch, :] + b_ref[...]
            dz = dlf / (1.0 + jnp.exp(z))
            dz_ref[c * ch:(c + 1) * ch, :] = dz.astype(bf16)
            dbs = dbs + jnp.sum(dz, axis=0, keepdims=True)
        db_ref[...] += dbs

    return pl.pallas_call(
        body, grid=(B,),
        in_specs=[pl.BlockSpec((S, LANES), lambda b: (b, 0)), pl.BlockSpec((S, LANES), lambda b: (b, cb)), pl.BlockSpec((1, LANES), lambda b: (0, 0))],
        out_specs=[pl.BlockSpec((S, LANES), lambda b: (b, 0)), pl.BlockSpec((1, LANES), lambda b: (0, 0))],
        out_shape=[SDS((B * S, LANES), bf16), SDS((1, LANES), f32)], name=name,
        compiler_params=_params(("arbitrary",)))(dcum, proj, bpad)


def _rot(v, c, sa, sb):
    return v * c + pltpu.roll(v, LANES - 16, 1) * sa + pltpu.roll(v, 16, 1) * sb


def _rot_t(v, c, sa, sb):
    return v * c + pltpu.roll(v * sa, 16, 1) + pltpu.roll(v * sb, LANES - 16, 1)


def _rope_q(x, tabs, *, bwd, name):
    T = x.shape[0]
    W = MLA_HEADS * LANES
    tr = _pick(T, 512)
    fn = _rot_t if bwd else _rot

    def body(x_ref, c_ref, sa_ref, sb_ref, o_ref):
        c, sa, sb = c_ref[...], sa_ref[...], sb_ref[...]
        for g in range(MLA_HEADS):
            o_ref[:, g * LANES:(g + 1) * LANES] = fn(x_ref[:, g * LANES:(g + 1) * LANES], c, sa, sb).astype(bf16)

    tab = pl.BlockSpec((tr, LANES), lambda i: (i, 0))
    wide = pl.BlockSpec((tr, W), lambda i: (i, 0))
    return pl.pallas_call(body, grid=(T // tr,), in_specs=[wide, tab, tab, tab], out_specs=wide, out_shape=SDS((T, W), bf16),
                          name=name, compiler_params=_params(("parallel",)))(x, *tabs)


def _rope_k(kv, tabs, proj, name):
    T = kv.shape[0]
    W = MLA_HEADS * LANES
    tr = _pick(T, 512)
    kcb = C_KPE // LANES

    def body(x_ref, c_ref, sa_ref, sb_ref, kpe_ref, k_ref, v_ref):
        c, sa, sb, kpe = c_ref[...], sa_ref[...], sb_ref[...], kpe_ref[...]
        low = lax.broadcasted_iota(jnp.int32, (tr, LANES), 1) < NOPE
        halves = []
        for g in range(MLA_HEADS):
            x = x_ref[:, g * LANES:(g + 1) * LANES]
            k_ref[:, g * LANES:(g + 1) * LANES] = _rot(jnp.where(low, x, 0.0) + kpe, c, sa, sb).astype(bf16)
            halves.append(x[:, NOPE:])
            if g % 2 == 1:
                v_ref[:, (g // 2) * LANES:(g // 2 + 1) * LANES] = jnp.concatenate(halves, axis=1).astype(bf16)
                halves = []

    tab = pl.BlockSpec((tr, LANES), lambda i: (i, 0))
    wide = pl.BlockSpec((tr, W), lambda i: (i, 0))
    return pl.pallas_call(
        body, grid=(T // tr,), in_specs=[wide, tab, tab, tab, pl.BlockSpec((tr, LANES), lambda i: (i, kcb))],
        out_specs=[wide, pl.BlockSpec((tr, W // 2), lambda i: (i, 0))], out_shape=[SDS((T, W), bf16), SDS((T, W // 2), bf16)],
        name=name, compiler_params=_params(("parallel",)))(kv, *tabs, proj)


def _rope_k_bwd(dk, dv, tabs, name):
    T = dk.shape[0]
    W = MLA_HEADS * LANES
    tr = _pick(T, 512)

    def body(dk_ref, dv_ref, c_ref, sa_ref, sb_ref, o_ref, s_ref):
        c, sa, sb = c_ref[...], sa_ref[...], sb_ref[...]
        tot = jnp.zeros((tr, LANES), f32)
        for g in range(MLA_HEADS):
            d = _rot_t(dk_ref[:, g * LANES:(g + 1) * LANES], c, sa, sb)
            tot = tot + d
            o_ref[:, g * LANES:(g + 1) * LANES] = jnp.concatenate(
                [d[:, :NOPE].astype(bf16), dv_ref[:, g * MLA_VD:(g + 1) * MLA_VD]], axis=1)
        s_ref[...] = tot.astype(bf16)

    tab = pl.BlockSpec((tr, LANES), lambda i: (i, 0))
    wide = pl.BlockSpec((tr, W), lambda i: (i, 0))
    return pl.pallas_call(
        body, grid=(T // tr,), in_specs=[wide, pl.BlockSpec((tr, W // 2), lambda i: (i, 0)), tab, tab, tab],
        out_specs=[wide, tab], out_shape=[SDS((T, W), bf16), SDS((T, LANES), bf16)], name=name,
        compiler_params=_params(("parallel",)))(dk, dv, *tabs)


def _diag_mask(mode, t):
    r = lax.broadcasted_iota(jnp.int32, (t, t), 0)
    c = lax.broadcasted_iota(jnp.int32, (t, t), 1)
    if mode == "chunk":
        return (c // CHUNK) <= (r // CHUNK)
    return c <= r


def _attn_fwd(q_arr, q_cb, k_arr, k_cb, v_arr, v_cb, *, B, Sq, Sk, nblk, hb, dqk, dv, mode, scale, cum=None, name):
    wq, wv = hb * dqk, hb * dv
    tq = _pick(Sq, ATT_TILE)
    tk = tq if mode != "none" else _pick(Sk, ATT_TILE)
    fox = mode == "fox"

    def body(*refs):
        q_ref, k_ref, v_ref = refs[:3]
        cc_ref, cr_ref = (refs[3], refs[4]) if fox else (None, None)
        o_ref, lse_ref = refs[-2], refs[-1]
        for hh in range(hb):
            ql, vl = slice(hh * dqk, (hh + 1) * dqk), slice(hh * dv, (hh + 1) * dv)

            def qstep(i, _, hh=hh, ql=ql, vl=vl):
                rows = pl.ds(pl.multiple_of(i * tq, tq), tq)
                q = q_ref[rows, ql].astype(bf16)
                cq = cc_ref[0, rows, hh:hh + 1] if fox else None

                def tile(j, carry, diag):
                    m, l, acc = carry
                    cols = pl.ds(pl.multiple_of(j * tk, tk), tk)
                    k = k_ref[cols, ql].astype(bf16)
                    v = v_ref[cols, vl].astype(bf16)
                    s = lax.dot_general(q, k, _DIMS["nt"], preferred_element_type=f32) * scale
                    if fox:
                        s = s + (cq - cr_ref[0, hh:hh + 1, cols])
                    if diag:
                        s = jnp.where(_diag_mask(mode, tq), s, NEG_INF)
                    m_new = jnp.maximum(m, jnp.max(s, axis=-1, keepdims=True))
                    a = jnp.exp(m - m_new)
                    p = jnp.exp(s - m_new)
                    l = a * l + jnp.sum(p, axis=-1, keepdims=True)
                    acc = a * acc + lax.dot_general(p.astype(bf16), v, _DIMS["nn"], preferred_element_type=f32)
                    return m_new, l, acc

                carry = (jnp.full((tq, 1), -jnp.inf, f32), jnp.zeros((tq, 1), f32), jnp.zeros((tq, dv), f32))
                if mode == "none":
                    carry = lax.fori_loop(0, Sk // tk, functools.partial(tile, diag=False), carry)
                else:
                    carry = lax.fori_loop(0, i, functools.partial(tile, diag=False), carry)
                    carry = tile(i, carry, True)
                m, l, acc = carry
                o_ref[rows, vl] = acc / l
                lse_ref[0, rows, hh:hh + 1] = m + jnp.log(l)
                return 0

            lax.fori_loop(0, Sq // tq, qstep, 0)

    in_specs = [pl.BlockSpec((Sq, wq), lambda b, p: (b, q_cb + p)), pl.BlockSpec((Sk, wq), lambda b, p: (b, k_cb + p)),
                pl.BlockSpec((Sk, wv), lambda b, p: (b, v_cb + p))]
    ins = [q_arr, k_arr, v_arr]
    if fox:
        in_specs += [pl.BlockSpec((1, Sq, hb), lambda b, p: (b * nblk + p, 0, 0)), pl.BlockSpec((1, hb, Sk), lambda b, p: (b * nblk + p, 0, 0))]
        ins += list(cum)
    return pl.pallas_call(
        body, grid=(B, nblk), in_specs=in_specs,
        out_specs=[pl.BlockSpec((Sq, wv), lambda b, p: (b, p)), pl.BlockSpec((1, Sq, hb), lambda b, p: (b * nblk + p, 0, 0))],
        out_shape=[SDS((B * Sq, nblk * wv), f32), SDS((B * nblk, Sq, hb), f32)], name=name,
        compiler_params=_params(("parallel", "parallel")))(*ins)


def _attn_bwd(q_arr, q_cb, k_arr, k_cb, v_arr, v_cb, o, do, lse, *, B, Sq, Sk, nblk, hb, dqk, dv, mode, scale, cum=None,
              dq_dtype=bf16, name):
    wq, wv = hb * dqk, hb * dv
    tq = _pick(Sq, ATT_TILE)
    tk = tq if mode != "none" else _pick(Sk, ATT_TILE)
    fox = mode == "fox"
    n_in = 8 if fox else 6
    n_out = 5 if fox else 3

    def body(*refs):
        q_ref, k_ref, v_ref, o_ref, do_ref, lse_ref = refs[:6]
        cc_ref, cr_ref = (refs[6], refs[7]) if fox else (None, None)
        dq_ref, dk_ref, dv_ref = refs[n_in:n_in + 3]
        dcq_ref, dck_ref = (refs[n_in + 3], refs[n_in + 4]) if fox else (None, None)
        scr = refs[n_in + n_out:]
        dk_acc, dv_acc = scr[0], scr[1]
        dck_acc = scr[2] if fox else None
        for hh in range(hb):
            ql, vl = slice(hh * dqk, (hh + 1) * dqk), slice(hh * dv, (hh + 1) * dv)
            dk_acc[...] = jnp.zeros_like(dk_acc)
            dv_acc[...] = jnp.zeros_like(dv_acc)
            if fox:
                dck_acc[...] = jnp.zeros_like(dck_acc)

            def qstep(i, _, hh=hh, ql=ql, vl=vl):
                rows = pl.ds(pl.multiple_of(i * tq, tq), tq)
                q = q_ref[rows, ql].astype(bf16)
                dof = do_ref[rows, vl]
                dob = dof.astype(bf16)
                dd = jnp.sum(dof * o_ref[rows, vl], axis=-1, keepdims=True)
                lsev = lse_ref[0, rows, hh:hh + 1]
                cq = cc_ref[0, rows, hh:hh + 1] if fox else None

                def tile(j, carry, diag):
                    dq_acc, dcq = carry
                    cols = pl.ds(pl.multiple_of(j * tk, tk), tk)
                    k = k_ref[cols, ql].astype(bf16)
                    v = v_ref[cols, vl].astype(bf16)
                    s = lax.dot_general(q, k, _DIMS["nt"], preferred_element_type=f32) * scale
                    if fox:
                        s = s + (cq - cr_ref[0, hh:hh + 1, cols])
                    if diag:
                        s = jnp.where(_diag_mask(mode, tq), s, NEG_INF)
                    p = jnp.exp(s - lsev)
                    dv_acc[cols, :] += lax.dot_general(p.astype(bf16), dob, _DIMS["tn"], preferred_element_type=f32)
                    dp = lax.dot_general(dob, v, _DIMS["nt"], preferred_element_type=f32)
                    ds = p * (dp - dd)
                    if fox:
                        dcq = dcq + jnp.sum(ds, axis=-1, keepdims=True)
                        dck_acc[0:1, cols] -= jnp.sum(ds, axis=0, keepdims=True)
                    dsb = (ds * scale).astype(bf16)
                    dq_acc = dq_acc + lax.dot_general(dsb, k, _DIMS["nn"], preferred_element_type=f32)
                    dk_acc[cols, :] += lax.dot_general(dsb, q, _DIMS["tn"], preferred_element_type=f32)
                    return dq_acc, dcq

                carry = (jnp.zeros((tq, dqk), f32), jnp.zeros((tq, 1), f32))
                if mode == "none":
                    carry = lax.fori_loop(0, Sk // tk, functools.partial(tile, diag=False), carry)
                else:
                    carry = lax.fori_loop(0, i, functools.partial(tile, diag=False), carry)
                    carry = tile(i, carry, True)
                dq_ref[rows, ql] = carry[0].astype(dq_dtype)
                if fox:
                    dcq_ref[0, rows, hh:hh + 1] = carry[1]
                return 0

            lax.fori_loop(0, Sq // tq, qstep, 0)
            dk_ref[:, ql] = dk_acc[...].astype(dq_dtype)
            dv_ref[:, vl] = dv_acc[...].astype(bf16)
            if fox:
                dck_ref[0, hh:hh + 1, :] = dck_acc[...]

    qs = pl.BlockSpec((Sq, wq), lambda b, p: (b, q_cb + p))
    ks = pl.BlockSpec((Sk, wq), lambda b, p: (b, k_cb + p))
    vs = pl.BlockSpec((Sk, wv), lambda b, p: (b, v_cb + p))
    os_ = pl.BlockSpec((Sq, wv), lambda b, p: (b, p))
    st = pl.BlockSpec((1, Sq, hb), lambda b, p: (b * nblk + p, 0, 0))
    rw = pl.BlockSpec((1, hb, Sk), lambda b, p: (b * nblk + p, 0, 0))
    in_specs = [qs, ks, vs, os_, os_, st] + ([st, rw] if fox else [])
    ins = [q_arr, k_arr, v_arr, o, do, lse] + (list(cum) if fox else [])
    out_specs = [pl.BlockSpec((Sq, wq), lambda b, p: (b, p)), pl.BlockSpec((Sk, wq), lambda b, p: (b, p)),
                 pl.BlockSpec((Sk, wv), lambda b, p: (b, p))] + ([st, rw] if fox else [])
    out_shape = [SDS((B * Sq, nblk * wq), dq_dtype), SDS((B * Sk, nblk * wq), dq_dtype), SDS((B * Sk, nblk * wv), bf16)]
    scratch = [pltpu.VMEM((Sk, dqk), f32), pltpu.VMEM((Sk, dv), f32)]
    if fox:
        out_shape += [SDS((B * nblk, Sq, hb), f32), SDS((B * nblk, hb, Sk), f32)]
        scratch.append(pltpu.VMEM((1, Sk), f32))
    return pl.pallas_call(
        body, grid=(B, nblk), in_specs=in_specs, out_specs=out_specs, out_shape=out_shape, scratch_shapes=scratch, name=name,
        compiler_params=_params(("parallel", "parallel")))(*ins)


def _merge_fwd(oa, ob, oc, w_br, proj, D, name):
    T = oa.shape[0]
    tr = _pick(T, 256)
    g0 = C_GATE // D
    o_spec = pl.BlockSpec((tr, BW), lambda i: (i, 0))
    w_spec = pl.BlockSpec((3, BW, D), lambda i: (0, 0, 0))

    def body(oa_ref, ob_ref, oc_ref, w_ref, ga_ref, gb_ref, gc_ref, m_ref):
        tot = jnp.zeros((tr, D), f32)
        for n, (o_ref, g_ref) in enumerate(((oa_ref, ga_ref), (ob_ref, gb_ref), (oc_ref, gc_ref))):
            bp = lax.dot_general(o_ref[...].astype(bf16), w_ref[n], _DIMS["nn"], preferred_element_type=f32)
            tot = tot + jax.nn.sigmoid(g_ref[...]) * bp
        m_ref[...] = tot.astype(bf16)

    gspecs = [pl.BlockSpec((tr, D), lambda i, n=n: (i, g0 + n)) for n in range(3)]
    return pl.pallas_call(
        body, grid=(T // tr,), in_specs=[o_spec, o_spec, o_spec, w_spec, *gspecs],
        out_specs=pl.BlockSpec((tr, D), lambda i: (i, 0)), out_shape=SDS((T, D), bf16), name=name,
        compiler_params=_params(("parallel",)))(oa, ob, oc, w_br, proj, proj, proj)


def _merge_bwd(dm, oa, ob, oc, w_br, proj, D, name):
    T = oa.shape[0]
    tr = _pick(T, 256)
    g0 = C_GATE // D
    o_spec = pl.BlockSpec((tr, BW), lambda i: (i, 0))
    d_spec = pl.BlockSpec((tr, D), lambda i: (i, 0))
    w_spec = pl.BlockSpec((3, BW, D), lambda i: (0, 0, 0))

    def body(dm_ref, oa_ref, ob_ref, oc_ref, w_ref, ga_ref, gb_ref, gc_ref, *outs):
        dmv = dm_ref[...]
        for n, (o_ref, g_ref) in enumerate(((oa_ref, ga_ref), (ob_ref, gb_ref), (oc_ref, gc_ref))):
            bp = lax.dot_general(o_ref[...].astype(bf16), w_ref[n], _DIMS["nn"], preferred_element_type=f32)
            g = jax.nn.sigmoid(g_ref[...])
            dbp = (dmv * g).astype(bf16)
            outs[n][...] = (dmv * bp * g * (1.0 - g)).astype(bf16)
            outs[3 + n][...] = lax.dot_general(dbp, w_ref[n], _DIMS["nt"], preferred_element_type=f32)
            outs[6 + n][...] = dbp

    gspecs = [pl.BlockSpec((tr, D), lambda i, n=n: (i, g0 + n)) for n in range(3)]
    return pl.pallas_call(
        body, grid=(T // tr,), in_specs=[d_spec, o_spec, o_spec, o_spec, w_spec, *gspecs],
        out_specs=[d_spec] * 3 + [o_spec] * 3 + [d_spec] * 3,
        out_shape=[SDS((T, D), bf16)] * 3 + [SDS((T, BW), f32)] * 3 + [SDS((T, D), bf16)] * 3, name=name,
        compiler_params=_params(("parallel",)))(dm, oa, ob, oc, w_br, proj, proj, proj)


def _loss(h, target, name):
    T, D = h.shape
    tr = _pick(T, 256)
    nt = T // tr
    row = pl.BlockSpec((tr, D), lambda i: (i, 0))

    def body(h_ref, t_ref, l_ref, d_ref, acc):
        i = pl.program_id(0)

        @pl.when(i == 0)
        def _():
            acc[...] = jnp.zeros_like(acc)

        e = h_ref[...] - t_ref[...]
        d_ref[...] = e * (1.0 / D)
        acc[...] += jnp.sum(e * e, axis=0, keepdims=True)

        @pl.when(i == nt - 1)
        def _():
            l_ref[...] = jnp.sum(acc[...], axis=-1, keepdims=True) * (0.5 / D)

    return pl.pallas_call(
        body, grid=(nt,), in_specs=[row, row], out_specs=[pl.BlockSpec((1, 1), lambda i: (0, 0)), row],
        out_shape=[SDS((1, 1), f32), SDS((T, D), f32)], scratch_shapes=[pltpu.VMEM((1, D), f32)], name=name,
        compiler_params=_params(("arbitrary",)))(h, target)


def _adamw(w, m, v, parts, name):
    shape = w.shape
    C = shape[-1]
    R = int(np.prod(shape[:-1])) if len(shape) > 1 else 1
    Cp = parts.shape[-1]
    tr = _pick(R, 256)
    spec = pl.BlockSpec((tr, C), lambda i: (i, 0))

    def body(w_ref, m_ref, v_ref, p_ref, g_ref, d_ref, mo_ref, vo_ref):
        gv = p_ref[0]
        for q in range(1, N_DEV):
            gv = gv + p_ref[q]
        gv = gv[:, :C]
        mn = ADAM_B1 * m_ref[...] + (1.0 - ADAM_B1) * gv
        vn = ADAM_B2 * v_ref[...] + (1.0 - ADAM_B2) * (gv * gv)
        m_hat = mn / (1.0 - ADAM_B1 ** ADAM_STEP)
        v_hat = vn / (1.0 - ADAM_B2 ** ADAM_STEP)
        g_ref[...] = gv
        d_ref[...] = -ADAM_LR * (m_hat / (jnp.sqrt(v_hat) + ADAM_EPS) + ADAM_WD * w_ref[...])
        mo_ref[...] = mn
        vo_ref[...] = vn

    outs = pl.pallas_call(
        body, grid=(R // tr,), in_specs=[spec] * 3 + [pl.BlockSpec((N_DEV, tr, Cp), lambda i: (0, i, 0))], out_specs=[spec] * 4,
        out_shape=[SDS((R, C), f32)] * 4, name=name,
        compiler_params=_params(("parallel",)))(*[a.reshape(R, C) for a in (w, m, v)], parts.reshape(N_DEV, R, Cp))
    return [o.reshape(shape) for o in outs]


def _coords():
    return lax.axis_index("x"), lax.axis_index("y"), lax.axis_index("c")


def _window(ref, axis, idx, n):
    start = pl.multiple_of(idx * n, n)
    if axis == 1:
        return ref.at[:, pl.ds(start, n), :]
    return ref.at[:, :, pl.ds(start, n)]


def _all_gather(shards, axes, name):
    nt = len(shards)
    sizes = [s.shape[a] for s, a in zip(shards, axes)]
    out_shape = [SDS(tuple(d * N_DEV if i == a else d for i, d in enumerate(s.shape)), s.dtype) for s, a in zip(shards, axes)]

    def body(*refs):
        x_refs, out_refs = refs[:nt], refs[nt:2 * nt]
        send_sems, recv_sems, local_sems = refs[2 * nt:]
        x, y, c = _coords()
        me, sibling = (x, y, c), (x, y, 1 - c)
        chips = [(1 - x, y), (x, 1 - y), (1 - x, 1 - y)]

        def win(t, px, py, pc):
            return _window(out_refs[t], axes[t], 4 * px + 2 * py + pc, sizes[t])

        def copy(t, k, block, to, src=None):
            return pltpu.make_async_remote_copy(
                src_ref=win(t, *block) if src is None else src, dst_ref=win(t, *block),
                send_sem=send_sems.at[7 * t + k], recv_sem=recv_sems.at[7 * t + k], device_id=to, device_id_type=MESH)

        mine = [pltpu.make_async_copy(x_refs[t], win(t, *me), local_sems.at[t]) for t in range(nt)]
        for cp in mine:
            cp.start()
        sent = []
        for t in range(nt):
            sent.append(copy(t, 0, me, sibling, src=x_refs[t]))
            sent += [copy(t, 1 + j, me, (*chip, c), src=x_refs[t]) for j, chip in enumerate(chips)]
        for cp in sent:
            cp.start()
        for j, chip in enumerate(chips):
            for t in range(nt):
                copy(t, 1 + j, (*chip, c), me).wait_recv()
                fwd = copy(t, 4 + j, (*chip, c), sibling)
                fwd.start()
                sent.append(fwd)
        for t in range(nt):
            copy(t, 0, sibling, me).wait_recv()
            for j, chip in enumerate(chips):
                copy(t, 4 + j, (*chip, 1 - c), me).wait_recv()
        for cp in sent:
            cp.wait_send()
        for cp in mine:
            cp.wait()

    any_spec = pl.BlockSpec(memory_space=pl.ANY)
    return pl.pallas_call(
        body, out_shape=out_shape, in_specs=[any_spec] * nt, out_specs=[any_spec] * nt,
        scratch_shapes=[pltpu.SemaphoreType.DMA((7 * nt,)), pltpu.SemaphoreType.DMA((7 * nt,)), pltpu.SemaphoreType.DMA((nt,))],
        name=name)(*shards)


def _exchange(grads, axes, slots, layer, name):
    nt = len(grads)
    sizes = [s.shape[2 + a] for s, a in zip(slots, axes)]

    def body(*refs):
        src_refs, out_refs = refs[:nt], refs[2 * nt:3 * nt]
        send_sems, recv_sems, local_sems = refs[3 * nt:]
        x, y, c = _coords()
        my = 4 * x + 2 * y + c
        mine = [pltpu.make_async_copy(_window(src_refs[t], axes[t], my, sizes[t]), out_refs[t].at[my, layer], local_sems.at[t])
                for t in range(nt)]
        for cp in mine:
            cp.start()
        copies = []
        for k in range(1, N_DEV):
            px, py, pc = lax.rem(x + (k >> 2), 2), lax.rem(y + ((k >> 1) & 1), 2), lax.rem(c + (k & 1), 2)
            for t in range(nt):
                cp = pltpu.make_async_remote_copy(
                    src_ref=_window(src_refs[t], axes[t], 4 * px + 2 * py + pc, sizes[t]), dst_ref=out_refs[t].at[my, layer],
                    send_sem=send_sems.at[7 * t + k - 1], recv_sem=recv_sems.at[7 * t + k - 1],
                    device_id=(px, py, pc), device_id_type=MESH)
                cp.start()
                copies.append(cp)
        for cp in copies:
            cp.wait_recv()
        for cp in copies:
            cp.wait_send()
        for cp in mine:
            cp.wait()

    any_spec = pl.BlockSpec(memory_space=pl.ANY)
    return pl.pallas_call(
        body, out_shape=[SDS(s.shape, s.dtype) for s in slots], in_specs=[any_spec] * (2 * nt), out_specs=[any_spec] * nt,
        input_output_aliases={nt + t: t for t in range(nt)},
        scratch_shapes=[pltpu.SemaphoreType.DMA((7 * nt,)), pltpu.SemaphoreType.DMA((7 * nt,)), pltpu.SemaphoreType.DMA((nt,))],
        name=name)(*grads, *slots)


_BIG = (("w_in", 2), ("w_uq", 2), ("w_ukv", 2), ("w_mem_kv", 1), ("w_br", 2), ("w_out", 1), ("w_ff1", 2), ("w_ff2", 1))


def _in_col_maps(D):
    d_in = O_GATE + 3 * D
    sh = d_in // N_DEV
    shp = _round_up(sh, LANES)
    perm_of_orig = np.empty(d_in, np.int64)
    for a, b, dst in ((0, O_F, C_Q), (O_F, O_CQ, C_F), (O_CQ, O_CKV, C_CQ), (O_CKV, O_KR, C_CKV), (O_KR, O_QM, C_KPE + NOPE),
                      (O_QM, O_GATE, C_QM), (O_GATE, d_in, C_GATE)):
        perm_of_orig[a:b] = dst + np.arange(b - a)
    orig = np.arange(d_in)
    gath_of_orig = (orig // sh) * shp + orig % sh
    fwd = -np.ones(C_GATE + 3 * D, np.int64)
    fwd[perm_of_orig] = gath_of_orig
    bwd = -np.ones(N_DEV * shp, np.int64)
    bwd[gath_of_orig] = perm_of_orig
    return fwd, bwd


def _lane_pad(a):
    c = a.shape[-1]
    return jnp.pad(a, [(0, 0)] * (a.ndim - 1) + [(0, _round_up(c, LANES) - c)])


def _as3(a):
    return a.reshape((-1,) + a.shape[-2:]) if a.ndim != 3 else a


def _rope_tables(positions):
    inv_freq = ROPE_BASE ** (-jnp.arange(0, ROPE, 2, dtype=f32) / ROPE)
    ang = positions.astype(f32).reshape(-1, 1) * inv_freq
    cos, sin = jnp.cos(ang), jnp.sin(ang)
    T = ang.shape[0]
    one, zero = jnp.ones((T, NOPE), f32), jnp.zeros((T, 16), f32)
    c = jnp.concatenate([one, cos, cos, jnp.ones((T, 32), f32)], axis=1)
    sa = jnp.concatenate([jnp.zeros((T, NOPE), f32), -sin, zero, jnp.zeros((T, 32), f32)], axis=1)
    sb = jnp.concatenate([jnp.zeros((T, NOPE), f32), zero, sin, jnp.zeros((T, 32), f32)], axis=1)
    return c, sa, sb


def kernel(x, mem, positions, ln_in_g, ln_in_b, w_in, b_forget, w_uq, g_cq, w_ukv, g_ckv, w_mem_kv, w_br, w_out, ln1_g, ln1_b, w_ff1, w_ff2, ln2_g, ln2_b, loss_target, m_ln_in_g, m_ln_in_b, m_w_in, m_b_forget, m_w_uq, m_g_cq, m_w_ukv, m_g_ckv, m_w_mem_kv, m_w_br, m_w_out, m_ln1_g, m_ln1_b, m_w_ff1, m_w_ff2, m_ln2_g, m_ln2_b, v_ln_in_g, v_ln_in_b, v_w_in, v_b_forget, v_w_uq, v_g_cq, v_w_ukv, v_g_ckv, v_w_mem_kv, v_w_br, v_w_out, v_ln1_g, v_ln1_b, v_w_ff1, v_w_ff2, v_ln2_g, v_ln2_b):
    B, S, D = x.shape
    T = B * S
    L = w_in.shape[0]
    NM = mem.shape[1]
    alpha = (2 * L) ** 0.25
    weights = dict(ln_in_g=ln_in_g, ln_in_b=ln_in_b, w_in=w_in, b_forget=b_forget, w_uq=w_uq, g_cq=g_cq, w_ukv=w_ukv, g_ckv=g_ckv,
                   w_mem_kv=w_mem_kv, w_br=w_br, w_out=w_out, ln1_g=ln1_g, ln1_b=ln1_b, w_ff1=w_ff1, w_ff2=w_ff2, ln2_g=ln2_g, ln2_b=ln2_b)
    mom_m = dict(ln_in_g=m_ln_in_g, ln_in_b=m_ln_in_b, w_in=m_w_in, b_forget=m_b_forget, w_uq=m_w_uq, g_cq=m_g_cq, w_ukv=m_w_ukv,
                 g_ckv=m_g_ckv, w_mem_kv=m_w_mem_kv, w_br=m_w_br, w_out=m_w_out, ln1_g=m_ln1_g, ln1_b=m_ln1_b, w_ff1=m_w_ff1,
                 w_ff2=m_w_ff2, ln2_g=m_ln2_g, ln2_b=m_ln2_b)
    mom_v = dict(ln_in_g=v_ln_in_g, ln_in_b=v_ln_in_b, w_in=v_w_in, b_forget=v_b_forget, w_uq=v_w_uq, g_cq=v_g_cq, w_ukv=v_w_ukv,
                 g_ckv=v_g_ckv, w_mem_kv=v_w_mem_kv, w_br=v_w_br, w_out=v_w_out, ln1_g=v_ln1_g, ln1_b=v_ln1_b, w_ff1=v_w_ff1,
                 w_ff2=v_w_ff2, ln2_g=v_ln2_g, ln2_b=v_ln2_b)
    order = list(weights)
    big_axes = [ax for _, ax in _BIG]
    fwd_map, bwd_map = _in_col_maps(D)
    shard3 = {n: _lane_pad(weights[n]) for n, _ in _BIG}

    def gather_layer(l):
        full = _all_gather([_as3(shard3[n][l]).astype(bf16) for n, _ in _BIG], big_axes, "weights_all_gather")
        W = {n: (f if n == "w_br" else f[0]) for (n, _), f in zip(_BIG, full)}
        W["w_in"] = _lane_permute(W["w_in"], fwd_map, bf16, "w_in_reorder")
        return W

    tabs = _rope_tables(positions)
    x2d, mem2d, tgt2d = x.reshape(T, D), mem.reshape(B * NM, D), loss_target.reshape(T, D)
    fox = dict(B=B, Sq=S, Sk=S, nblk=FOX_HEADS // 2, hb=2, dqk=FOX_HD, dv=FOX_HD, mode="fox", scale=FOX_HD ** -0.5)
    mla = dict(B=B, Sq=S, Sk=S, nblk=MLA_HEADS // 2, hb=2, dqk=LANES, dv=MLA_VD, mode="chunk", scale=(NOPE + ROPE) ** -0.5)
    memat = dict(B=B, Sq=S, Sk=NM, nblk=MEM_HEADS, hb=1, dqk=MEM_HD, dv=MEM_HD, mode="none", scale=MEM_HD ** -0.5)
    nb = FOX_HEADS // 2

    h, xh0, rstd0 = _ln_fwd(None, x2d, ln_in_g, ln_in_b, alpha, "ln_in")
    saved = []
    for l in range(L):
        W = gather_layer(l)
        bpad = jnp.pad(b_forget[l].reshape(1, FOX_HEADS), ((0, 0), (0, LANES - FOX_HEADS)))
        proj = _mm(h, W["w_in"], name="proj")
        cum = _gate_fwd(proj, bpad, B, S, "gate_fwd")
        cum8 = cum[:, :FOX_HEADS].reshape(B, S, nb, 2)
        cum_c = cum8.transpose(0, 2, 1, 3).reshape(B * nb, S, 2)
        cum_r = cum8.transpose(0, 2, 3, 1).reshape(B * nb, 2, S)
        o_a, lse_a = _attn_fwd(proj, C_Q // LANES, proj, C_K // LANES, proj, C_V // LANES, cum=(cum_c, cum_r), name="fox_fwd", **fox)
        cqn = _rms_fwd(proj, C_CQ, Q_RANK, g_cq[l], "rms_q")
        ckvn = _rms_fwd(proj, C_CKV, KV_RANK, g_ckv[l], "rms_kv")
        qraw = _mm(cqn, W["w_uq"], name="q_up")
        kv = _mm(ckvn, W["w_ukv"], name="kv_up")
        qf = _rope_q(qraw, tabs, bwd=False, name="rope_q")
        kf, v_b = _rope_k(kv, tabs, proj, "rope_k")
        o_b, lse_b = _attn_fwd(qf, 0, kf, 0, v_b, 0, name="mla_fwd", **mla)
        mkv = _mm(mem2d, W["w_mem_kv"], name="mem_kv")
        o_c, lse_c = _attn_fwd(proj, C_QM // MEM_HD, mkv, 0, mkv, MEM_HEADS, name="mem_fwd", **memat)
        merged = _merge_fwd(o_a, o_b, o_c, W["w_br"], proj, D, "merge_fwd")
        y = _mm(merged, W["w_out"], name="out_proj")
        h1, xh1, rstd1 = _ln_fwd(h, y, ln1_g[l], ln1_b[l], alpha, "ln1")
        a, r = _mm(h1, W["w_ff1"], epi="relu2", name="ff1")
        ff = _mm(a, W["w_ff2"], name="ff2")
        h2, xh2, rstd2 = _ln_fwd(h1, ff, ln2_g[l], ln2_b[l], alpha, "ln2")
        saved.append(dict(W=W, bpad=bpad, h=h, proj=proj, cum=(cum_c, cum_r), o_a=o_a, lse_a=lse_a, cqn=cqn, ckvn=ckvn, qf=qf, kf=kf,
                          v_b=v_b, o_b=o_b, lse_b=lse_b, mkv=mkv, o_c=o_c, lse_c=lse_c, merged=merged, h1=h1, xh1=xh1,
                          rstd1=rstd1, a=a, r=r, xh2=xh2, rstd2=rstd2))
        h = h2

    loss_local, dh = _loss(h, tgt2d, "loss")
    loss = lax.psum(loss_local[0, 0], ("x", "y", "c"))

    slots = [lax.empty((N_DEV, L) + _as3(shard3[n][0]).shape, f32) for n, _ in _BIG]
    small = {n: [None] * L for n in ("b_forget", "g_cq", "g_ckv", "ln1_g", "ln1_b", "ln2_g", "ln2_b")}
    for l in reversed(range(L)):
        sv = saved[l]
        W = sv["W"]
        du2, dg2, db2 = _ln_bwd(dh, sv["xh2"], sv["rstd2"], ln2_g[l], "ln2_bwd")
        dz = _mm(du2, W["w_ff2"], mode="nt", epi="mul", extra=sv["r"], out_dtype=bf16, name="ff2_dx")
        dw_ff2 = _mm(sv["a"], du2, mode="tn", name="ff2_dw")
        dh1 = _mm(dz, W["w_ff1"], mode="nt", epi="add", extra=du2, alpha=alpha, name="ff1_dx")
        dw_ff1 = _mm(sv["h1"], dz, mode="tn", name="ff1_dw")
        du1, dg1, db1 = _ln_bwd(dh1, sv["xh1"], sv["rstd1"], ln1_g[l], "ln1_bwd")
        dmerged = _mm(du1, W["w_out"], mode="nt", name="out_dx")
        dw_out = _mm(sv["merged"], du1, mode="tn", name="out_dw")
        mb = _merge_bwd(dmerged, sv["o_a"], sv["o_b"], sv["o_c"], W["w_br"], sv["proj"], D, "merge_bwd")
        dgl, do, dbp = mb[0:3], mb[3:6], mb[6:9]
        dw_br = jnp.stack([_mm(o_n, dbp_n, mode="tn", name="br_dw") for o_n, dbp_n in zip((sv["o_a"], sv["o_b"], sv["o_c"]), dbp)])
        proj = sv["proj"]
        dq_a, dk_a, dv_a, dcq, dck = _attn_bwd(proj, C_Q // LANES, proj, C_K // LANES, proj, C_V // LANES, sv["o_a"], do[0], sv["lse_a"],
                                               cum=sv["cum"], name="fox_bwd", **fox)
        dcum = (dcq.reshape(B, nb, S, 2).transpose(0, 2, 1, 3) + dck.reshape(B, nb, 2, S).transpose(0, 3, 1, 2)).reshape(T, FOX_HEADS)
        dcum = jnp.pad(dcum, ((0, 0), (0, LANES - FOX_HEADS)))
        dzf, dbf = _gate_bwd(dcum, proj, sv["bpad"], B, S, "gate_bwd")
        dqf, dkf, dv_b = _attn_bwd(sv["qf"], 0, sv["kf"], 0, sv["v_b"], 0, sv["o_b"], do[1], sv["lse_b"], dq_dtype=f32,
                                   name="mla_bwd", **mla)
        dqraw = _rope_q(dqf, tabs, bwd=True, name="rope_q_bwd")
        dkv, dkpe = _rope_k_bwd(dkf, dv_b, tabs, "rope_k_bwd")
        dw_uq = _mm(sv["cqn"], dqraw, mode="tn", name="q_up_dw")
        dcqn = _mm(dqraw, W["w_uq"], mode="nt", name="q_up_dx")
        dcq_, dg_cq = _rms_bwd(dcqn, proj, C_CQ, Q_RANK, g_cq[l], "rms_q_bwd")
        dw_ukv = _mm(sv["ckvn"], dkv, mode="tn", name="kv_up_dw")
        dckvn = _mm(dkv, W["w_ukv"], mode="nt", name="kv_up_dx")
        dckv, dg_ckv = _rms_bwd(dckvn, proj, C_CKV, KV_RANK, g_ckv[l], "rms_kv_bwd")
        dqm, dmk, dmv = _attn_bwd(proj, C_QM // MEM_HD, sv["mkv"], 0, sv["mkv"], MEM_HEADS, sv["o_c"], do[2], sv["lse_c"],
                                  name="mem_bwd", **memat)
        dw_mem = _mm(mem2d, jnp.concatenate([dmk, dmv], axis=1), mode="tn", name="mem_kv_dw")
        dproj = jnp.concatenate([dq_a, dk_a, dv_a, dzf, dkpe, dckv, dqm, jnp.zeros((T, C_CQ - C_PAD), bf16), dcq_, *dgl], axis=1)
        dw_in = _lane_permute(_mm(sv["h"], dproj, mode="tn", name="proj_dw"), bwd_map, f32, "w_in_grad_reorder")
        dh = _mm(dproj, W["w_in"], mode="nt", epi="add", extra=du1, alpha=alpha, name="proj_dx")
        grads_l = dict(w_in=dw_in, w_uq=dw_uq, w_ukv=dw_ukv, w_mem_kv=dw_mem, w_br=dw_br, w_out=dw_out, w_ff1=dw_ff1, w_ff2=dw_ff2)
        slots = _exchange([_as3(grads_l[n]) for n, _ in _BIG], big_axes, slots, l, "grads_exchange")
        for n, val in (("b_forget", dbf[:, :FOX_HEADS]), ("g_cq", dg_cq), ("g_ckv", dg_ckv), ("ln1_g", dg1), ("ln1_b", db1),
                       ("ln2_g", dg2), ("ln2_b", db2)):
            small[n][l] = val.reshape(-1)
    grad_x, dg_in, db_in = _ln_bwd(dh, xh0, rstd0, ln_in_g, "ln_in_bwd")

    small_list = [("ln_in_g", dg_in.reshape(-1)), ("ln_in_b", db_in.reshape(-1))] + [(n, jnp.stack(v).reshape(-1)) for n, v in small.items()]
    sflat = jnp.concatenate([v for _, v in small_list])
    n_small = sflat.shape[0]
    rs = _round_up(-(-n_small // LANES), 8)
    spacked = jnp.pad(sflat, (0, rs * LANES - n_small)).reshape(1, rs, LANES)
    sparts = _all_gather([spacked], [1], "small_all_gather")[0].reshape(N_DEV, rs * LANES)

    grads, deltas, new_m, new_v = {}, {}, {}, {}
    off = 0
    for n, vv in small_list:
        parts = sparts[:, off:off + vv.shape[0]].reshape((N_DEV,) + weights[n].shape)
        off += vv.shape[0]
        grads[n], deltas[n], new_m[n], new_v[n] = _adamw(weights[n], mom_m[n], mom_v[n], parts, "adamw_" + n)
    for (n, _), sl in zip(_BIG, slots):
        parts = sl.reshape((N_DEV,) + weights[n].shape[:-1] + (sl.shape[-1],))
        grads[n], deltas[n], new_m[n], new_v[n] = _adamw(weights[n], mom_m[n], mom_v[n], parts, "adamw_" + n)
    return (loss, grad_x.reshape(B, S, D), *[grads[n] for n in order], *[deltas[n] for n in order], *[new_m[n] for n in order],
            *[new_v[n] for n in order])
```

```python
import functools

import numpy as np
import jax
import jax.numpy as jnp
from jax import lax
from jax.experimental import pallas as pl
from jax.experimental.pallas import tpu as pltpu

f32, bf16 = jnp.float32, jnp.bfloat16
SDS = jax.ShapeDtypeStruct

N_DEV = 8
MESH = pl.DeviceIdType.MESH
LANES = 128
VMEM_LIMIT = 56 * 1024 * 1024

FOX_HEADS, FOX_HD = 8, 64
MLA_HEADS, NOPE, ROPE, MLA_VD = 8, 64, 32, 64
Q_RANK, KV_RANK = 384, 256
ROPE_BASE = 10000.0
MEM_HEADS, MEM_HD = 4, 128
BW = 512
CHUNK = 64
LN_EPS, RMS_EPS, NEG_INF = 1e-5, 1e-6, -1e30
ADAM_LR, ADAM_B1, ADAM_B2, ADAM_EPS, ADAM_WD, ADAM_STEP = 0.001, 0.9, 0.999, 1e-08, 0.01, 10

ATT_TQ, ATT_TK = 512, 256

C_Q, C_K, C_V, C_F, C_KPE, C_CKV, C_QM, C_PAD, C_CQ, C_GATE = 0, 512, 1024, 1536, 1664, 1792, 2048, 2560, 2688, 3072
O_F, O_CQ, O_CKV, O_KR, O_QM, O_GATE = 1536, 1544, 1928, 2184, 2216, 2728


def _pick(dim, pref):
    if dim <= pref:
        return dim
    for c in (1024, 768, 512, 384, 256, 128, 64, 32, 16, 8):
        if c <= pref and dim % c == 0:
            return c
    return dim


def _round_up(n, m):
    return -(-n // m) * m


def _params(sem=None):
    return pltpu.CompilerParams(dimension_semantics=sem, vmem_limit_bytes=VMEM_LIMIT)


_DIMS = {"nn": (((1,), (0,)), ((), ())), "nt": (((1,), (1,)), ((), ())), "tn": (((0,), (0,)), ((), ()))}


def _mm(a, b, *, mode="nn", out_dtype=f32, epi="none", extra=None, alpha=1.0, tm=512, tn=512, tk=1024, name):
    if mode == "nn":
        (M, K), (_, N) = a.shape, b.shape
    elif mode == "nt":
        (M, K), (N, _) = a.shape, b.shape
    else:
        (K, M), (_, N) = a.shape, b.shape
    tm, tn, tk = _pick(M, tm), _pick(N, tn), _pick(K, tk)
    nk = K // tk
    a_spec = pl.BlockSpec((tk, tm), lambda i, j, k: (k, i)) if mode == "tn" else pl.BlockSpec((tm, tk), lambda i, j, k: (i, k))
    b_spec = pl.BlockSpec((tn, tk), lambda i, j, k: (j, k)) if mode == "nt" else pl.BlockSpec((tk, tn), lambda i, j, k: (k, j))
    o_spec = pl.BlockSpec((tm, tn), lambda i, j, k: (i, j))
    n_out = 2 if epi == "relu2" else 1
    dims = _DIMS[mode]

    def body(*refs):
        a_ref, b_ref = refs[0], refs[1]
        e_ref = refs[2] if extra is not None else None
        outs = refs[-1 - n_out:-1]
        acc = refs[-1]
        k = pl.program_id(2)

        @pl.when(k == 0)
        def _():
            acc[...] = jnp.zeros_like(acc)

        acc[...] += lax.dot_general(a_ref[...].astype(bf16), b_ref[...].astype(bf16), dims, preferred_element_type=f32)

        @pl.when(k == nk - 1)
        def _():
            r = acc[...]
            if epi == "none":
                outs[0][...] = r.astype(out_dtype)
            elif epi == "relu2":
                p = jnp.maximum(r, 0.0)
                outs[0][...] = (p * p).astype(bf16)
                outs[1][...] = (2.0 * p).astype(bf16)
            elif epi == "mul":
                outs[0][...] = (r * e_ref[...].astype(f32)).astype(out_dtype)
            else:
                outs[0][...] = (r + alpha * e_ref[...].astype(f32)).astype(out_dtype)

    ins, in_specs = [a, b], [a_spec, b_spec]
    if extra is not None:
        ins.append(extra)
        in_specs.append(o_spec)
    if epi == "relu2":
        out_shape, out_specs = [SDS((M, N), bf16), SDS((M, N), bf16)], [o_spec, o_spec]
    else:
        out_shape, out_specs = SDS((M, N), out_dtype), o_spec
    return pl.pallas_call(
        body, grid=(M // tm, N // tn, nk), in_specs=in_specs, out_specs=out_specs, out_shape=out_shape,
        scratch_shapes=[pltpu.VMEM((tm, tn), f32)], name=name,
        compiler_params=_params(("parallel", "parallel", "arbitrary")))(*ins)


def _lane_permute(src, cmap, out_dtype, name):
    R, _ = src.shape
    cmap = np.asarray(cmap, np.int64)
    nb = cmap.shape[0] // LANES
    srcs = [sorted({int(c) // LANES for c in cmap[b * LANES:(b + 1) * LANES] if c >= 0}) or [0] for b in range(nb)]
    smax = max(len(s) for s in srcs)
    nsrc = np.array([len(s) for s in srcs], np.int32)
    blk = np.array([s + [s[-1]] * (smax - len(s)) for s in srcs], np.int32).reshape(-1)
    exact = src.dtype == f32
    tr = _pick(R, 1024)

    def body(blk_ref, ns_ref, src_ref, cm_ref, o_ref, acc):
        j, s = pl.program_id(1), pl.program_id(2)

        @pl.when(s == 0)
        def _():
            acc[...] = jnp.zeros_like(acc)

        @pl.when(s < ns_ref[j])
        def _():
            rows = lax.broadcasted_iota(jnp.int32, (LANES, LANES), 0) + blk_ref[j * smax + s] * LANES
            sel = rows == cm_ref[...]
            if exact:
                acc[...] += jnp.dot(src_ref[...], sel.astype(f32), precision=lax.Precision.HIGHEST, preferred_element_type=f32)
            else:
                acc[...] += jnp.dot(src_ref[...], sel.astype(src_ref.dtype), preferred_element_type=f32)

        @pl.when(s == smax - 1)
        def _():
            o_ref[...] = acc[...].astype(out_dtype)

    grid_spec = pltpu.PrefetchScalarGridSpec(
        num_scalar_prefetch=2, grid=(R // tr, nb, smax),
        in_specs=[pl.BlockSpec((tr, LANES), lambda i, j, s, blk_ref, ns_ref: (i, blk_ref[j * smax + s])),
                  pl.BlockSpec((1, LANES), lambda i, j, s, blk_ref, ns_ref: (0, j))],
        out_specs=pl.BlockSpec((tr, LANES), lambda i, j, s, blk_ref, ns_ref: (i, j)),
        scratch_shapes=[pltpu.VMEM((tr, LANES), f32)])
    return pl.pallas_call(
        body, grid_spec=grid_spec, out_shape=SDS((R, nb * LANES), out_dtype), name=name,
        compiler_params=_params(("parallel", "parallel", "arbitrary")))(
            jnp.asarray(blk), jnp.asarray(nsrc), src, jnp.asarray(cmap.astype(np.int32)).reshape(1, -1))


def _ln_fwd(h, y, g, b, alpha, name):
    T, D = y.shape
    tr = _pick(T, 256)
    row = pl.BlockSpec((tr, D), lambda i: (i, 0))
    vec = pl.BlockSpec((1, D), lambda i: (0, 0))
    col = pl.BlockSpec((tr, 1), lambda i: (i, 0))

    def body(*refs):
        if h is None:
            y_ref, g_ref, b_ref, o_ref, xh_ref, rs_ref = refs
            u = y_ref[...]
        else:
            h_ref, y_ref, g_ref, b_ref, o_ref, xh_ref, rs_ref = refs
            u = alpha * h_ref[...] + y_ref[...]
        mu = jnp.mean(u, axis=-1, keepdims=True)
        d = u - mu
        var = jnp.mean(d * d, axis=-1, keepdims=True)
        rstd = lax.rsqrt(var + LN_EPS)
        xh = d * rstd
        xh_ref[...] = xh
        rs_ref[...] = rstd
        o_ref[...] = xh * g_ref[...] + b_ref[...]

    ins = ([] if h is None else [h]) + [y, g.reshape(1, D), b.reshape(1, D)]
    specs = ([] if h is None else [row]) + [row, vec, vec]
    return pl.pallas_call(
        body, grid=(T // tr,), in_specs=specs, out_specs=[row, row, col],
        out_shape=[SDS((T, D), f32), SDS((T, D), f32), SDS((T, 1), f32)], name=name,
        compiler_params=_params(("parallel",)))(*ins)


def _ln_bwd(dy, xh, rstd, g, name):
    T, D = dy.shape
    tr = _pick(T, 256)
    row = pl.BlockSpec((tr, D), lambda i: (i, 0))
    vec = pl.BlockSpec((1, D), lambda i: (0, 0))
    col = pl.BlockSpec((tr, 1), lambda i: (i, 0))

    def body(dy_ref, xh_ref, rs_ref, g_ref, du_ref, dg_ref, db_ref):
        @pl.when(pl.program_id(0) == 0)
        def _():
            dg_ref[...] = jnp.zeros_like(dg_ref)
            db_ref[...] = jnp.zeros_like(db_ref)

        dyv, xhv = dy_ref[...], xh_ref[...]
        dxh = dyv * g_ref[...]
        m1 = jnp.mean(dxh, axis=-1, keepdims=True)
        m2 = jnp.mean(dxh * xhv, axis=-1, keepdims=True)
        du_ref[...] = rs_ref[...] * (dxh - m1 - xhv * m2)
        dg_ref[...] += jnp.sum(dyv * xhv, axis=0, keepdims=True)
        db_ref[...] += jnp.sum(dyv, axis=0, keepdims=True)

    return pl.pallas_call(
        body, grid=(T // tr,), in_specs=[row, row, col, vec], out_specs=[row, vec, vec],
        out_shape=[SDS((T, D), f32), SDS((1, D), f32), SDS((1, D), f32)], name=name,
        compiler_params=_params(("arbitrary",)))(dy, xh, rstd, g.reshape(1, D))


def _rms_fwd(proj, col_off, R, g, name):
    T = proj.shape[0]
    tr = _pick(T, 512)
    cb = col_off // R

    def body(c_ref, g_ref, o_ref):
        c = c_ref[...]
        r = lax.rsqrt(jnp.mean(c * c, axis=-1, keepdims=True) + RMS_EPS)
        o_ref[...] = (c * r * g_ref[...]).astype(bf16)

    return pl.pallas_call(
        body, grid=(T // tr,), in_specs=[pl.BlockSpec((tr, R), lambda i: (i, cb)), pl.BlockSpec((1, R), lambda i: (0, 0))],
        out_specs=pl.BlockSpec((tr, R), lambda i: (i, 0)), out_shape=SDS((T, R), bf16), name=name,
        compiler_params=_params(("parallel",)))(proj, g.reshape(1, R))


def _rms_bwd(dy, proj, col_off, R, g, name):
    T = proj.shape[0]
    tr = _pick(T, 512)
    cb = col_off // R

    def body(dy_ref, c_ref, g_ref, dc_ref, dg_ref):
        @pl.when(pl.program_id(0) == 0)
        def _():
            dg_ref[...] = jnp.zeros_like(dg_ref)

        c, dyv = c_ref[...], dy_ref[...]
        r = lax.rsqrt(jnp.mean(c * c, axis=-1, keepdims=True) + RMS_EPS)
        t = dyv * g_ref[...]
        mt = jnp.mean(t * c, axis=-1, keepdims=True)
        dc_ref[...] = (r * t - c * (r * r * r) * mt).astype(bf16)
        dg_ref[...] += jnp.sum(dyv * c * r, axis=0, keepdims=True)

    return pl.pallas_call(
        body, grid=(T // tr,),
        in_specs=[pl.BlockSpec((tr, R), lambda i: (i, 0)), pl.BlockSpec((tr, R), lambda i: (i, cb)), pl.BlockSpec((1, R), lambda i: (0, 0))],
        out_specs=[pl.BlockSpec((tr, R), lambda i: (i, 0)), pl.BlockSpec((1, R), lambda i: (0, 0))],
        out_shape=[SDS((T, R), bf16), SDS((1, R), f32)], name=name,
        compiler_params=_params(("arbitrary",)))(dy, proj, g.reshape(1, R))


def _tri(n, lower):
    r = lax.broadcasted_iota(jnp.int32, (n, n), 0)
    c = lax.broadcasted_iota(jnp.int32, (n, n), 1)
    return ((r >= c) if lower else (c >= r)).astype(f32)


def _gate_fwd(proj, bpad, B, S, name):
    ch = _pick(S, 256)
    nch = S // ch
    cb = C_F // LANES

    def body(f_ref, b_ref, o_ref):
        tri = _tri(ch, True)
        carry = jnp.zeros((1, LANES), f32)
        for c in range(nch):
            z = f_ref[c * ch:(c + 1) * ch, :] + b_ref[...]
            lf = jnp.minimum(z, 0.0) - jnp.log1p(jnp.exp(-jnp.abs(z)))
            cs = jnp.dot(tri, lf, precision=lax.Precision.HIGHEST, preferred_element_type=f32) + carry
            o_ref[c * ch:(c + 1) * ch, :] = cs
            carry = cs[ch - 1:ch, :]

    return pl.pallas_call(
        body, grid=(B,), in_specs=[pl.BlockSpec((S, LANES), lambda b: (b, cb)), pl.BlockSpec((1, LANES), lambda b: (0, 0))],
        out_specs=pl.BlockSpec((S, LANES), lambda b: (b, 0)), out_shape=SDS((B * S, LANES), f32), name=name,
        compiler_params=_params(("parallel",)))(proj, bpad)


def _gate_bwd(dcum, proj, bpad, B, S, name):
    ch = _pick(S, 256)
    nch = S // ch
    cb = C_F // LANES

    def body(d_ref, f_ref, b_ref, dz_ref, db_ref):
        @pl.when(pl.program_id(0) == 0)
        def _():
            db_ref[...] = jnp.zeros_like(db_ref)

        tri = _tri(ch, False)
        carry = jnp.zeros((1, LANES), f32)
        dbs = jnp.zeros((1, LANES), f32)
        for c in reversed(range(nch)):
            d = d_ref[c * ch:(c + 1) * ch, :]
            dlf = jnp.dot(tri, d, precision=lax.Precision.HIGHEST, preferred_element_type=f32) + carry
            carry = carry + jnp.sum(d, axis=0, keepdims=True)
            z = f_ref[c * ch:(c + 1) * ch, :] + b_ref[...]
            dz = dlf / (1.0 + jnp.exp(z))
            dz_ref[c * ch:(c + 1) * ch, :] = dz.astype(bf16)
            dbs = dbs + jnp.sum(dz, axis=0, keepdims=True)
        db_ref[...] += dbs

    return pl.pallas_call(
        body, grid=(B,),
        in_specs=[pl.BlockSpec((S, LANES), lambda b: (b, 0)), pl.BlockSpec((S, LANES), lambda b: (b, cb)), pl.BlockSpec((1, LANES), lambda b: (0, 0))],
        out_specs=[pl.BlockSpec((S, LANES), lambda b: (b, 0)), pl.BlockSpec((1, LANES), lambda b: (0, 0))],
        out_shape=[SDS((B * S, LANES), bf16), SDS((1, LANES), f32)], name=name,
        compiler_params=_params(("arbitrary",)))(dcum, proj, bpad)


def _rot(v, c, sa, sb):
    return v * c + pltpu.roll(v, LANES - 16, 1) * sa + pltpu.roll(v, 16, 1) * sb


def _rot_t(v, c, sa, sb):
    return v * c + pltpu.roll(v * sa, 16, 1) + pltpu.roll(v * sb, LANES - 16, 1)


def _rope_q(x, tabs, *, bwd, name):
    T = x.shape[0]
    W = MLA_HEADS * LANES
    tr = _pick(T, 512)
    fn = _rot_t if bwd else _rot

    def body(x_ref, c_ref, sa_ref, sb_ref, o_ref):
        c, sa, sb = c_ref[...], sa_ref[...], sb_ref[...]
        for g in range(MLA_HEADS):
            o_ref[:, g * LANES:(g + 1) * LANES] = fn(x_ref[:, g * LANES:(g + 1) * LANES], c, sa, sb).astype(bf16)

    tab = pl.BlockSpec((tr, LANES), lambda i: (i, 0))
    wide = pl.BlockSpec((tr, W), lambda i: (i, 0))
    return pl.pallas_call(body, grid=(T // tr,), in_specs=[wide, tab, tab, tab], out_specs=wide, out_shape=SDS((T, W), bf16),
                          name=name, compiler_params=_params(("parallel",)))(x, *tabs)


def _rope_k(kv, tabs, proj, name):
    T = kv.shape[0]
    W = MLA_HEADS * LANES
    tr = _pick(T, 512)
    kcb = C_KPE // LANES

    def body(x_ref, c_ref, sa_ref, sb_ref, kpe_ref, k_ref, v_ref):
        c, sa, sb, kpe = c_ref[...], sa_ref[...], sb_ref[...], kpe_ref[...]
        low = lax.broadcasted_iota(jnp.int32, (tr, LANES), 1) < NOPE
        halves = []
        for g in range(MLA_HEADS):
            x = x_ref[:, g * LANES:(g + 1) * LANES]
            k_ref[:, g * LANES:(g + 1) * LANES] = _rot(jnp.where(low, x, 0.0) + kpe, c, sa, sb).astype(bf16)
            halves.append(x[:, NOPE:])
            if g % 2 == 1:
                v_ref[:, (g // 2) * LANES:(g // 2 + 1) * LANES] = jnp.concatenate(halves, axis=1).astype(bf16)
                halves = []

    tab = pl.BlockSpec((tr, LANES), lambda i: (i, 0))
    wide = pl.BlockSpec((tr, W), lambda i: (i, 0))
    return pl.pallas_call(
        body, grid=(T // tr,), in_specs=[wide, tab, tab, tab, pl.BlockSpec((tr, LANES), lambda i: (i, kcb))],
        out_specs=[wide, pl.BlockSpec((tr, W // 2), lambda i: (i, 0))], out_shape=[SDS((T, W), bf16), SDS((T, W // 2), bf16)],
        name=name, compiler_params=_params(("parallel",)))(kv, *tabs, proj)


def _rope_k_bwd(dk, dv, tabs, name):
    T = dk.shape[0]
    W = MLA_HEADS * LANES
    tr = _pick(T, 512)

    def body(dk_ref, dv_ref, c_ref, sa_ref, sb_ref, o_ref, s_ref):
        c, sa, sb = c_ref[...], sa_ref[...], sb_ref[...]
        tot = jnp.zeros((tr, LANES), f32)
        for g in range(MLA_HEADS):
            d = _rot_t(dk_ref[:, g * LANES:(g + 1) * LANES], c, sa, sb)
            tot = tot + d
            o_ref[:, g * LANES:(g + 1) * LANES] = jnp.concatenate(
                [d[:, :NOPE].astype(bf16), dv_ref[:, g * MLA_VD:(g + 1) * MLA_VD]], axis=1)
        s_ref[...] = tot.astype(bf16)

    tab = pl.BlockSpec((tr, LANES), lambda i: (i, 0))
    wide = pl.BlockSpec((tr, W), lambda i: (i, 0))
    return pl.pallas_call(
        body, grid=(T // tr,), in_specs=[wide, pl.BlockSpec((tr, W // 2), lambda i: (i, 0)), tab, tab, tab],
        out_specs=[wide, tab], out_shape=[SDS((T, W), bf16), SDS((T, LANES), bf16)], name=name,
        compiler_params=_params(("parallel",)))(dk, dv, *tabs)


def _tile_mask(mode, i, j, tq, tk):
    r = i * tq + lax.broadcasted_iota(jnp.int32, (tq, tk), 0)
    c = j * tk + lax.broadcasted_iota(jnp.int32, (tq, tk), 1)
    if mode == "chunk":
        return (c // CHUNK) <= (r // CHUNK)
    return c <= r


def _attn_tiles(Sq, Sk):
    tk = _pick(Sk, ATT_TK)
    tq = _pick(Sq, ATT_TQ)
    assert (tk % tq == 0 or tq % tk == 0) and tq % CHUNK == 0 and tk % CHUNK == 0
    return tq, tk


def _stage_heads(dst, src_ref, hb, w, mark=None, val=1.0):
    S = src_ref.shape[0]
    ch = _pick(S, 256)
    assert w <= LANES and (mark is None or w <= mark < LANES)

    def step(c, _):
        rows = pl.ds(pl.multiple_of(c * ch, ch), ch)
        for hh in range(hb):
            x = src_ref[rows, hh * w:(hh + 1) * w].astype(bf16)
            if w < LANES:
                pad = jnp.zeros((ch, LANES - w), bf16)
                if mark is not None:
                    lane = lax.broadcasted_iota(jnp.int32, (ch, LANES - w), 1)
                    pad = jnp.where(lane == mark - w, val, 0.0).astype(bf16)
                x = jnp.concatenate([x, pad], axis=1)
            dst[hh, rows, :] = x
        return 0

    lax.fori_loop(0, S // ch, step, 0)


def _attn_fwd(q_arr, q_cb, k_arr, k_cb, v_arr, v_cb, *, B, Sq, Sk, nblk, hb, dqk, dv, mode, scale, cum_r=None, name):
    wq, wv = hb * dqk, hb * dv
    tq, tk = _attn_tiles(Sq, Sk)
    fox = mode == "fox"
    n_in = 4 if fox else 3
    ones = dv < LANES

    def body(*refs):
        q_ref, k_ref, v_ref = refs[:3]
        cr_ref = refs[3] if fox else None
        o_ref, lse_ref, qh, kh, vh = refs[n_in:]
        _stage_heads(qh, q_ref, hb, dqk)
        _stage_heads(kh, k_ref, hb, dqk)
        _stage_heads(vh, v_ref, hb, dv, dv if ones else None)

        def qstep(i, _):
            rows = pl.ds(pl.multiple_of(i * tq, tq), tq)
            qs = [qh[hh, rows, :] for hh in range(hb)]

            def tile(j, carry, diag):
                cols = pl.ds(pl.multiple_of(j * tk, tk), tk)
                out = []
                for hh in range(hb):
                    m, l, acc = carry[hh]
                    s = lax.dot_general(qs[hh], kh[hh, cols, :], _DIMS["nt"], preferred_element_type=f32) * scale
                    if fox:
                        s = s - cr_ref[0, hh:hh + 1, cols]
                    if diag:
                        s = jnp.where(_tile_mask(mode, i, j, tq, tk), s, NEG_INF)
                    m_new = jnp.maximum(m, jnp.max(s, axis=-1, keepdims=True))
                    a = jnp.exp(m - m_new)
                    p = jnp.exp(s - m_new)
                    if not ones:
                        l = a * l + jnp.sum(p, axis=-1, keepdims=True)
                    acc = a * acc + lax.dot_general(p.astype(bf16), vh[hh, cols, :], _DIMS["nn"], preferred_element_type=f32)
                    out.append((m_new, l, acc))
                return tuple(out)

            carry = tuple((jnp.full((tq, 1), -jnp.inf, f32), jnp.zeros((tq, 1), f32), jnp.zeros((tq, LANES), f32)) for _ in range(hb))
            if mode == "none":
                carry = lax.fori_loop(0, Sk // tk, functools.partial(tile, diag=False), carry)
            else:
                nfull = (i * tq) // tk
                carry = lax.fori_loop(0, nfull, functools.partial(tile, diag=False), carry)
                for d in range(max(1, tq // tk)):
                    carry = tile(nfull + d, carry, True)
            for hh in range(hb):
                m, l, acc = carry[hh]
                if ones:
                    l = acc[:, dv:dv + 1]
                o_ref[rows, hh * dv:(hh + 1) * dv] = acc[:, :dv] / l
                lse_ref[0, rows, hh:hh + 1] = m + jnp.log(l)
            return 0

        lax.fori_loop(0, Sq // tq, qstep, 0)

    in_specs = [pl.BlockSpec((Sq, wq), lambda b, p: (b, q_cb + p)), pl.BlockSpec((Sk, wq), lambda b, p: (b, k_cb + p)),
                pl.BlockSpec((Sk, wv), lambda b, p: (b, v_cb + p))]
    ins = [q_arr, k_arr, v_arr]
    if fox:
        in_specs.append(pl.BlockSpec((1, hb, Sk), lambda b, p: (b * nblk + p, 0, 0)))
        ins.append(cum_r)
    return pl.pallas_call(
        body, grid=(B, nblk), in_specs=in_specs,
        out_specs=[pl.BlockSpec((Sq, wv), lambda b, p: (b, p)), pl.BlockSpec((1, Sq, hb), lambda b, p: (b * nblk + p, 0, 0))],
        out_shape=[SDS((B * Sq, nblk * wv), f32), SDS((B * nblk, Sq, hb), f32)],
        scratch_shapes=[pltpu.VMEM((hb, Sq, LANES), bf16), pltpu.VMEM((hb, Sk, LANES), bf16), pltpu.VMEM((hb, Sk, LANES), bf16)],
        name=name, compiler_params=_params(("parallel", "parallel")))(*ins)


def _attn_bwd(q_arr, q_cb, k_arr, k_cb, v_arr, v_cb, o, do, lse, *, B, Sq, Sk, nblk, hb, dqk, dv, mode, scale, cum_r=None,
              dq_dtype=bf16, name):
    wq, wv = hb * dqk, hb * dv
    tq, tk = _attn_tiles(Sq, Sk)
    fox = mode == "fox"
    n_in = 7 if fox else 6
    n_out = 4 if fox else 3
    if fox:
        assert dqk + 2 <= LANES and np.log2(scale) == np.round(np.log2(scale))

    def body(*refs):
        q_ref, k_ref, v_ref, o_ref, do_ref, lse_ref = refs[:6]
        cr_ref = refs[6] if fox else None
        dq_ref, dk_ref, dv_ref = refs[n_in:n_in + 3]
        dcum_ref = refs[n_in + 3] if fox else None
        qh, kh, vh, doh, dk_acc, dv_acc = refs[n_in + n_out:n_in + n_out + 6]
        dcq_acc = refs[n_in + n_out + 6] if fox else None
        _stage_heads(qh, q_ref, hb, dqk, dqk + 1 if fox else None, 1.0 / scale)
        _stage_heads(kh, k_ref, hb, dqk, dqk if fox else None, 1.0 / scale)
        _stage_heads(vh, v_ref, hb, dv)
        _stage_heads(doh, do_ref, hb, dv)
        dk_acc[...] = jnp.zeros_like(dk_acc)
        dv_acc[...] = jnp.zeros_like(dv_acc)

        def qstep(i, _):
            rows = pl.ds(pl.multiple_of(i * tq, tq), tq)
            qs = [qh[hh, rows, :] for hh in range(hb)]
            dos = [doh[hh, rows, :] for hh in range(hb)]
            dds = [jnp.sum(do_ref[rows, hh * dv:(hh + 1) * dv] * o_ref[rows, hh * dv:(hh + 1) * dv], axis=-1, keepdims=True)
                   for hh in range(hb)]
            lses = [lse_ref[0, rows, hh:hh + 1] for hh in range(hb)]

            def tile(j, carry, diag):
                cols = pl.ds(pl.multiple_of(j * tk, tk), tk)
                out = []
                for hh in range(hb):
                    k = kh[hh, cols, :]
                    s = lax.dot_general(qs[hh], k, _DIMS["nt"], preferred_element_type=f32) * scale
                    if fox:
                        s = s - cr_ref[0, hh:hh + 1, cols]
                    if diag:
                        s = jnp.where(_tile_mask(mode, i, j, tq, tk), s, NEG_INF)
                    p = jnp.exp(s - lses[hh])
                    dv_acc[hh, cols, :] += lax.dot_general(p.astype(bf16), dos[hh], _DIMS["tn"], preferred_element_type=f32)
                    dp = lax.dot_general(dos[hh], vh[hh, cols, :], _DIMS["nt"], preferred_element_type=f32)
                    ds = p * (dp - dds[hh])
                    dsb = (ds * scale).astype(bf16)
                    out.append(carry[hh] + lax.dot_general(dsb, k, _DIMS["nn"], preferred_element_type=f32))
                    dk_acc[hh, cols, :] += lax.dot_general(dsb, qs[hh], _DIMS["tn"], preferred_element_type=f32)
                return tuple(out)

            carry = tuple(jnp.zeros((tq, LANES), f32) for _ in range(hb))
            if mode == "none":
                carry = lax.fori_loop(0, Sk // tk, functools.partial(tile, diag=False), carry)
            else:
                nfull = (i * tq) // tk
                carry = lax.fori_loop(0, nfull, functools.partial(tile, diag=False), carry)
                for d in range(max(1, tq // tk)):
                    carry = tile(nfull + d, carry, True)
            for hh in range(hb):
                dq_ref[rows, hh * dqk:(hh + 1) * dqk] = carry[hh][:, :dqk].astype(dq_dtype)
                if fox:
                    dcq_acc[rows, hh:hh + 1] = carry[hh][:, dqk:dqk + 1]
            return 0

        lax.fori_loop(0, Sq // tq, qstep, 0)
        for hh in range(hb):
            dk_ref[:, hh * dqk:(hh + 1) * dqk] = dk_acc[hh, :, :dqk].astype(dq_dtype)
            dv_ref[:, hh * dv:(hh + 1) * dv] = dv_acc[hh, :, :dv].astype(bf16)
            if fox:
                dcum_ref[0, :, hh:hh + 1] = dcq_acc[:, hh:hh + 1] - dk_acc[hh, :, dqk + 1:dqk + 2]

    qs_ = pl.BlockSpec((Sq, wq), lambda b, p: (b, q_cb + p))
    ks_ = pl.BlockSpec((Sk, wq), lambda b, p: (b, k_cb + p))
    vs_ = pl.BlockSpec((Sk, wv), lambda b, p: (b, v_cb + p))
    os_ = pl.BlockSpec((Sq, wv), lambda b, p: (b, p))
    st = pl.BlockSpec((1, Sq, hb), lambda b, p: (b * nblk + p, 0, 0))
    rw = pl.BlockSpec((1, hb, Sk), lambda b, p: (b * nblk + p, 0, 0))
    in_specs = [qs_, ks_, vs_, os_, os_, st] + ([rw] if fox else [])
    ins = [q_arr, k_arr, v_arr, o, do, lse] + ([cum_r] if fox else [])
    out_specs = [pl.BlockSpec((Sq, wq), lambda b, p: (b, p)), pl.BlockSpec((Sk, wq), lambda b, p: (b, p)),
                 pl.BlockSpec((Sk, wv), lambda b, p: (b, p))] + ([st] if fox else [])
    out_shape = [SDS((B * Sq, nblk * wq), dq_dtype), SDS((B * Sk, nblk * wq), dq_dtype), SDS((B * Sk, nblk * wv), bf16)]
    scratch = [pltpu.VMEM((hb, Sq, LANES), bf16), pltpu.VMEM((hb, Sk, LANES), bf16), pltpu.VMEM((hb, Sk, LANES), bf16),
               pltpu.VMEM((hb, Sq, LANES), bf16), pltpu.VMEM((hb, Sk, LANES), f32), pltpu.VMEM((hb, Sk, LANES), f32)]
    if fox:
        assert Sq == Sk
        out_shape.append(SDS((B * nblk, Sq, hb), f32))
        scratch.append(pltpu.VMEM((Sq, hb), f32))
    return pl.pallas_call(
        body, grid=(B, nblk), in_specs=in_specs, out_specs=out_specs, out_shape=out_shape, scratch_shapes=scratch, name=name,
        compiler_params=_params(("parallel", "parallel")))(*ins)


def _merge_fwd(oa, ob, oc, w_br, proj, D, name):
    T = oa.shape[0]
    tr = _pick(T, 256)
    g0 = C_GATE // D
    o_spec = pl.BlockSpec((tr, BW), lambda i: (i, 0))
    w_spec = pl.BlockSpec((3, BW, D), lambda i: (0, 0, 0))

    def body(oa_ref, ob_ref, oc_ref, w_ref, ga_ref, gb_ref, gc_ref, m_ref):
        tot = jnp.zeros((tr, D), f32)
        for n, (o_ref, g_ref) in enumerate(((oa_ref, ga_ref), (ob_ref, gb_ref), (oc_ref, gc_ref))):
            bp = lax.dot_general(o_ref[...].astype(bf16), w_ref[n], _DIMS["nn"], preferred_element_type=f32)
            tot = tot + jax.nn.sigmoid(g_ref[...]) * bp
        m_ref[...] = tot.astype(bf16)

    gspecs = [pl.BlockSpec((tr, D), lambda i, n=n: (i, g0 + n)) for n in range(3)]
    return pl.pallas_call(
        body, grid=(T // tr,), in_specs=[o_spec, o_spec, o_spec, w_spec, *gspecs],
        out_specs=pl.BlockSpec((tr, D), lambda i: (i, 0)), out_shape=SDS((T, D), bf16), name=name,
        compiler_params=_params(("parallel",)))(oa, ob, oc, w_br, proj, proj, proj)


def _merge_bwd(dm, oa, ob, oc, w_br, proj, D, name):
    T = oa.shape[0]
    tr = _pick(T, 256)
    g0 = C_GATE // D
    o_spec = pl.BlockSpec((tr, BW), lambda i: (i, 0))
    d_spec = pl.BlockSpec((tr, D), lambda i: (i, 0))
    w_spec = pl.BlockSpec((3, BW, D), lambda i: (0, 0, 0))

    def body(dm_ref, oa_ref, ob_ref, oc_ref, w_ref, ga_ref, gb_ref, gc_ref, *outs):
        dmv = dm_ref[...]
        for n, (o_ref, g_ref) in enumerate(((oa_ref, ga_ref), (ob_ref, gb_ref), (oc_ref, gc_ref))):
            bp = lax.dot_general(o_ref[...].astype(bf16), w_ref[n], _DIMS["nn"], preferred_element_type=f32)
            g = jax.nn.sigmoid(g_ref[...])
            dbp = (dmv * g).astype(bf16)
            outs[n][...] = (dmv * bp * g * (1.0 - g)).astype(bf16)
            outs[3 + n][...] = lax.dot_general(dbp, w_ref[n], _DIMS["nt"], preferred_element_type=f32)
            outs[6 + n][...] = dbp

    gspecs = [pl.BlockSpec((tr, D), lambda i, n=n: (i, g0 + n)) for n in range(3)]
    return pl.pallas_call(
        body, grid=(T // tr,), in_specs=[d_spec, o_spec, o_spec, o_spec, w_spec, *gspecs],
        out_specs=[d_spec] * 3 + [o_spec] * 3 + [d_spec] * 3,
        out_shape=[SDS((T, D), bf16)] * 3 + [SDS((T, BW), f32)] * 3 + [SDS((T, D), bf16)] * 3, name=name,
        compiler_params=_params(("parallel",)))(dm, oa, ob, oc, w_br, proj, proj, proj)


def _loss(h, target, name):
    T, D = h.shape
    tr = _pick(T, 256)
    nt = T // tr
    row = pl.BlockSpec((tr, D), lambda i: (i, 0))

    def body(h_ref, t_ref, l_ref, d_ref, acc):
        i = pl.program_id(0)

        @pl.when(i == 0)
        def _():
            acc[...] = jnp.zeros_like(acc)

        e = h_ref[...] - t_ref[...]
        d_ref[...] = e * (1.0 / D)
        acc[...] += jnp.sum(e * e, axis=0, keepdims=True)

        @pl.when(i == nt - 1)
        def _():
            l_ref[...] = jnp.sum(acc[...], axis=-1, keepdims=True) * (0.5 / D)

    return pl.pallas_call(
        body, grid=(nt,), in_specs=[row, row], out_specs=[pl.BlockSpec((1, 1), lambda i: (0, 0)), row],
        out_shape=[SDS((1, 1), f32), SDS((T, D), f32)], scratch_shapes=[pltpu.VMEM((1, D), f32)], name=name,
        compiler_params=_params(("arbitrary",)))(h, target)


def _adamw(w, m, v, parts, name):
    shape = w.shape
    C = shape[-1]
    R = int(np.prod(shape[:-1])) if len(shape) > 1 else 1
    Cp = parts.shape[-1]
    tr = _pick(R, 256)
    spec = pl.BlockSpec((tr, C), lambda i: (i, 0))

    def body(w_ref, m_ref, v_ref, p_ref, g_ref, d_ref, mo_ref, vo_ref):
        gv = p_ref[0]
        for q in range(1, N_DEV):
            gv = gv + p_ref[q]
        gv = gv[:, :C]
        mn = ADAM_B1 * m_ref[...] + (1.0 - ADAM_B1) * gv
        vn = ADAM_B2 * v_ref[...] + (1.0 - ADAM_B2) * (gv * gv)
        m_hat = mn / (1.0 - ADAM_B1 ** ADAM_STEP)
        v_hat = vn / (1.0 - ADAM_B2 ** ADAM_STEP)
        g_ref[...] = gv
        d_ref[...] = -ADAM_LR * (m_hat / (jnp.sqrt(v_hat) + ADAM_EPS) + ADAM_WD * w_ref[...])
        mo_ref[...] = mn
        vo_ref[...] = vn

    outs = pl.pallas_call(
        body, grid=(R // tr,), in_specs=[spec] * 3 + [pl.BlockSpec((N_DEV, tr, Cp), lambda i: (0, i, 0))], out_specs=[spec] * 4,
        out_shape=[SDS((R, C), f32)] * 4, name=name,
        compiler_params=_params(("parallel",)))(*[a.reshape(R, C) for a in (w, m, v)], parts.reshape(N_DEV, R, Cp))
    return [o.reshape(shape) for o in outs]


def _coords():
    return lax.axis_index("x"), lax.axis_index("y"), lax.axis_index("c")


def _window(ref, axis, idx, n):
    start = pl.multiple_of(idx * n, n)
    if axis == 1:
        return ref.at[:, pl.ds(start, n), :]
    return ref.at[:, :, pl.ds(start, n)]


def _all_gather(shards, axes, name):
    nt = len(shards)
    sizes = [s.shape[a] for s, a in zip(shards, axes)]
    out_shape = [SDS(tuple(d * N_DEV if i == a else d for i, d in enumerate(s.shape)), s.dtype) for s, a in zip(shards, axes)]

    def body(*refs):
        x_refs, out_refs = refs[:nt], refs[nt:2 * nt]
        send_sems, recv_sems, local_sems = refs[2 * nt:]
        x, y, c = _coords()
        me, sibling = (x, y, c), (x, y, 1 - c)
        chips = [(1 - x, y), (x, 1 - y), (1 - x, 1 - y)]

        def win(t, px, py, pc):
            return _window(out_refs[t], axes[t], 4 * px + 2 * py + pc, sizes[t])

        def copy(t, k, block, to, src=None):
            return pltpu.make_async_remote_copy(
                src_ref=win(t, *block) if src is None else src, dst_ref=win(t, *block),
                send_sem=send_sems.at[7 * t + k], recv_sem=recv_sems.at[7 * t + k], device_id=to, device_id_type=MESH)

        mine = [pltpu.make_async_copy(x_refs[t], win(t, *me), local_sems.at[t]) for t in range(nt)]
        for cp in mine:
            cp.start()
        sent = []
        for t in range(nt):
            sent.append(copy(t, 0, me, sibling, src=x_refs[t]))
            sent += [copy(t, 1 + j, me, (*chip, c), src=x_refs[t]) for j, chip in enumerate(chips)]
        for cp in sent:
            cp.start()
        for j, chip in enumerate(chips):
            for t in range(nt):
                copy(t, 1 + j, (*chip, c), me).wait_recv()
                fwd = copy(t, 4 + j, (*chip, c), sibling)
                fwd.start()
                sent.append(fwd)
        for t in range(nt):
            copy(t, 0, sibling, me).wait_recv()
            for j, chip in enumerate(chips):
                copy(t, 4 + j, (*chip, 1 - c), me).wait_recv()
        for cp in sent:
            cp.wait_send()
        for cp in mine:
            cp.wait()

    any_spec = pl.BlockSpec(memory_space=pl.ANY)
    return pl.pallas_call(
        body, out_shape=out_shape, in_specs=[any_spec] * nt, out_specs=[any_spec] * nt,
        scratch_shapes=[pltpu.SemaphoreType.DMA((7 * nt,)), pltpu.SemaphoreType.DMA((7 * nt,)), pltpu.SemaphoreType.DMA((nt,))],
        name=name)(*shards)


def _exchange(grads, axes, slots, layer, name):
    nt = len(grads)
    sizes = [s.shape[2 + a] for s, a in zip(slots, axes)]

    def body(*refs):
        src_refs, out_refs = refs[:nt], refs[2 * nt:3 * nt]
        send_sems, recv_sems, local_sems = refs[3 * nt:]
        x, y, c = _coords()
        my = 4 * x + 2 * y + c
        mine = [pltpu.make_async_copy(_window(src_refs[t], axes[t], my, sizes[t]), out_refs[t].at[my, layer], local_sems.at[t])
                for t in range(nt)]
        for cp in mine:
            cp.start()
        copies = []
        for k in range(1, N_DEV):
            px, py, pc = lax.rem(x + (k >> 2), 2), lax.rem(y + ((k >> 1) & 1), 2), lax.rem(c + (k & 1), 2)
            for t in range(nt):
                cp = pltpu.make_async_remote_copy(
                    src_ref=_window(src_refs[t], axes[t], 4 * px + 2 * py + pc, sizes[t]), dst_ref=out_refs[t].at[my, layer],
                    send_sem=send_sems.at[7 * t + k - 1], recv_sem=recv_sems.at[7 * t + k - 1],
                    device_id=(px, py, pc), device_id_type=MESH)
                cp.start()
                copies.append(cp)
        for cp in copies:
            cp.wait_recv()
        for cp in copies:
            cp.wait_send()
        for cp in mine:
            cp.wait()

    any_spec = pl.BlockSpec(memory_space=pl.ANY)
    return pl.pallas_call(
        body, out_shape=[SDS(s.shape, s.dtype) for s in slots], in_specs=[any_spec] * (2 * nt), out_specs=[any_spec] * nt,
        input_output_aliases={nt + t: t for t in range(nt)},
        scratch_shapes=[pltpu.SemaphoreType.DMA((7 * nt,)), pltpu.SemaphoreType.DMA((7 * nt,)), pltpu.SemaphoreType.DMA((nt,))],
        name=name)(*grads, *slots)


_BIG = (("w_in", 2), ("w_uq", 2), ("w_ukv", 2), ("w_mem_kv", 1), ("w_br", 2), ("w_out", 1), ("w_ff1", 2), ("w_ff2", 1))


def _in_col_maps(D):
    d_in = O_GATE + 3 * D
    sh = d_in // N_DEV
    shp = _round_up(sh, LANES)
    perm_of_orig = np.empty(d_in, np.int64)
    for a, b, dst in ((0, O_F, C_Q), (O_F, O_CQ, C_F), (O_CQ, O_CKV, C_CQ), (O_CKV, O_KR, C_CKV), (O_KR, O_QM, C_KPE + NOPE),
                      (O_QM, O_GATE, C_QM), (O_GATE, d_in, C_GATE)):
        perm_of_orig[a:b] = dst + np.arange(b - a)
    orig = np.arange(d_in)
    gath_of_orig = (orig // sh) * shp + orig % sh
    fwd = -np.ones(C_GATE + 3 * D, np.int64)
    fwd[perm_of_orig] = gath_of_orig
    bwd = -np.ones(N_DEV * shp, np.int64)
    bwd[gath_of_orig] = perm_of_orig
    return fwd, bwd


def _lane_pad(a):
    c = a.shape[-1]
    return jnp.pad(a, [(0, 0)] * (a.ndim - 1) + [(0, _round_up(c, LANES) - c)])


def _as3(a):
    return a.reshape((-1,) + a.shape[-2:]) if a.ndim != 3 else a


def _rope_tables(positions):
    inv_freq = ROPE_BASE ** (-jnp.arange(0, ROPE, 2, dtype=f32) / ROPE)
    ang = positions.astype(f32).reshape(-1, 1) * inv_freq
    cos, sin = jnp.cos(ang), jnp.sin(ang)
    T = ang.shape[0]
    one, zero = jnp.ones((T, NOPE), f32), jnp.zeros((T, 16), f32)
    c = jnp.concatenate([one, cos, cos, jnp.ones((T, 32), f32)], axis=1)
    sa = jnp.concatenate([jnp.zeros((T, NOPE), f32), -sin, zero, jnp.zeros((T, 32), f32)], axis=1)
    sb = jnp.concatenate([jnp.zeros((T, NOPE), f32), zero, sin, jnp.zeros((T, 32), f32)], axis=1)
    return c, sa, sb


def kernel(x, mem, positions, ln_in_g, ln_in_b, w_in, b_forget, w_uq, g_cq, w_ukv, g_ckv, w_mem_kv, w_br, w_out, ln1_g, ln1_b, w_ff1, w_ff2, ln2_g, ln2_b, loss_target, m_ln_in_g, m_ln_in_b, m_w_in, m_b_forget, m_w_uq, m_g_cq, m_w_ukv, m_g_ckv, m_w_mem_kv, m_w_br, m_w_out, m_ln1_g, m_ln1_b, m_w_ff1, m_w_ff2, m_ln2_g, m_ln2_b, v_ln_in_g, v_ln_in_b, v_w_in, v_b_forget, v_w_uq, v_g_cq, v_w_ukv, v_g_ckv, v_w_mem_kv, v_w_br, v_w_out, v_ln1_g, v_ln1_b, v_w_ff1, v_w_ff2, v_ln2_g, v_ln2_b):
    B, S, D = x.shape
    T = B * S
    L = w_in.shape[0]
    NM = mem.shape[1]
    alpha = (2 * L) ** 0.25
    weights = dict(ln_in_g=ln_in_g, ln_in_b=ln_in_b, w_in=w_in, b_forget=b_forget, w_uq=w_uq, g_cq=g_cq, w_ukv=w_ukv, g_ckv=g_ckv,
                   w_mem_kv=w_mem_kv, w_br=w_br, w_out=w_out, ln1_g=ln1_g, ln1_b=ln1_b, w_ff1=w_ff1, w_ff2=w_ff2, ln2_g=ln2_g, ln2_b=ln2_b)
    mom_m = dict(ln_in_g=m_ln_in_g, ln_in_b=m_ln_in_b, w_in=m_w_in, b_forget=m_b_forget, w_uq=m_w_uq, g_cq=m_g_cq, w_ukv=m_w_ukv,
                 g_ckv=m_g_ckv, w_mem_kv=m_w_mem_kv, w_br=m_w_br, w_out=m_w_out, ln1_g=m_ln1_g, ln1_b=m_ln1_b, w_ff1=m_w_ff1,
                 w_ff2=m_w_ff2, ln2_g=m_ln2_g, ln2_b=m_ln2_b)
    mom_v = dict(ln_in_g=v_ln_in_g, ln_in_b=v_ln_in_b, w_in=v_w_in, b_forget=v_b_forget, w_uq=v_w_uq, g_cq=v_g_cq, w_ukv=v_w_ukv,
                 g_ckv=v_g_ckv, w_mem_kv=v_w_mem_kv, w_br=v_w_br, w_out=v_w_out, ln1_g=v_ln1_g, ln1_b=v_ln1_b, w_ff1=v_w_ff1,
                 w_ff2=v_w_ff2, ln2_g=v_ln2_g, ln2_b=v_ln2_b)
    order = list(weights)
    big_axes = [ax for _, ax in _BIG]
    fwd_map, bwd_map = _in_col_maps(D)
    shard3 = {n: _lane_pad(weights[n]) for n, _ in _BIG}

    def gather_layer(l):
        full = _all_gather([_as3(shard3[n][l]).astype(bf16) for n, _ in _BIG], big_axes, "weights_all_gather")
        W = {n: (f if n == "w_br" else f[0]) for (n, _), f in zip(_BIG, full)}
        W["w_in"] = _lane_permute(W["w_in"], fwd_map, bf16, "w_in_reorder")
        return W

    tabs = _rope_tables(positions)
    x2d, mem2d, tgt2d = x.reshape(T, D), mem.reshape(B * NM, D), loss_target.reshape(T, D)
    fox = dict(B=B, Sq=S, Sk=S, nblk=FOX_HEADS // 2, hb=2, dqk=FOX_HD, dv=FOX_HD, mode="fox", scale=FOX_HD ** -0.5)
    mla = dict(B=B, Sq=S, Sk=S, nblk=MLA_HEADS // 2, hb=2, dqk=LANES, dv=MLA_VD, mode="chunk", scale=(NOPE + ROPE) ** -0.5)
    memat = dict(B=B, Sq=S, Sk=NM, nblk=MEM_HEADS, hb=1, dqk=MEM_HD, dv=MEM_HD, mode="none", scale=MEM_HD ** -0.5)
    nb = FOX_HEADS // 2

    h, xh0, rstd0 = _ln_fwd(None, x2d, ln_in_g, ln_in_b, alpha, "ln_in")
    saved = []
    for l in range(L):
        W = gather_layer(l)
        bpad = jnp.pad(b_forget[l].reshape(1, FOX_HEADS), ((0, 0), (0, LANES - FOX_HEADS)))
        proj = _mm(h, W["w_in"], name="proj")
        cum = _gate_fwd(proj, bpad, B, S, "gate_fwd")
        cum_r = cum[:, :FOX_HEADS].reshape(B, S, nb, 2).transpose(0, 2, 3, 1).reshape(B * nb, 2, S)
        o_a, lse_a = _attn_fwd(proj, C_Q // LANES, proj, C_K // LANES, proj, C_V // LANES, cum_r=cum_r, name="fox_fwd", **fox)
        cqn = _rms_fwd(proj, C_CQ, Q_RANK, g_cq[l], "rms_q")
        ckvn = _rms_fwd(proj, C_CKV, KV_RANK, g_ckv[l], "rms_kv")
        qraw = _mm(cqn, W["w_uq"], name="q_up")
        kv = _mm(ckvn, W["w_ukv"], name="kv_up")
        qf = _rope_q(qraw, tabs, bwd=False, name="rope_q")
        kf, v_b = _rope_k(kv, tabs, proj, "rope_k")
        o_b, lse_b = _attn_fwd(qf, 0, kf, 0, v_b, 0, name="mla_fwd", **mla)
        mkv = _mm(mem2d, W["w_mem_kv"], name="mem_kv")
        o_c, lse_c = _attn_fwd(proj, C_QM // MEM_HD, mkv, 0, mkv, MEM_HEADS, name="mem_fwd", **memat)
        merged = _merge_fwd(o_a, o_b, o_c, W["w_br"], proj, D, "merge_fwd")
        y = _mm(merged, W["w_out"], name="out_proj")
        h1, xh1, rstd1 = _ln_fwd(h, y, ln1_g[l], ln1_b[l], alpha, "ln1")
        a, r = _mm(h1, W["w_ff1"], epi="relu2", name="ff1")
        ff = _mm(a, W["w_ff2"], name="ff2")
        h2, xh2, rstd2 = _ln_fwd(h1, ff, ln2_g[l], ln2_b[l], alpha, "ln2")
        saved.append(dict(W=W, bpad=bpad, h=h, proj=proj, cum_r=cum_r, o_a=o_a, lse_a=lse_a, cqn=cqn, ckvn=ckvn, qf=qf, kf=kf,
                          v_b=v_b, o_b=o_b, lse_b=lse_b, mkv=mkv, o_c=o_c, lse_c=lse_c, merged=merged, h1=h1, xh1=xh1,
                          rstd1=rstd1, a=a, r=r, xh2=xh2, rstd2=rstd2))
        h = h2

    loss_local, dh = _loss(h, tgt2d, "loss")
    loss = lax.psum(loss_local[0, 0], ("x", "y", "c"))

    slots = [lax.empty((N_DEV, L) + _as3(shard3[n][0]).shape, f32) for n, _ in _BIG]
    small = {n: [None] * L for n in ("b_forget", "g_cq", "g_ckv", "ln1_g", "ln1_b", "ln2_g", "ln2_b")}
    for l in reversed(range(L)):
        sv = saved[l]
        W = sv["W"]
        du2, dg2, db2 = _ln_bwd(dh, sv["xh2"], sv["rstd2"], ln2_g[l], "ln2_bwd")
        dz = _mm(du2, W["w_ff2"], mode="nt", epi="mul", extra=sv["r"], out_dtype=bf16, name="ff2_dx")
        dw_ff2 = _mm(sv["a"], du2, mode="tn", name="ff2_dw")
        dh1 = _mm(dz, W["w_ff1"], mode="nt", epi="add", extra=du2, alpha=alpha, name="ff1_dx")
        dw_ff1 = _mm(sv["h1"], dz, mode="tn", name="ff1_dw")
        du1, dg1, db1 = _ln_bwd(dh1, sv["xh1"], sv["rstd1"], ln1_g[l], "ln1_bwd")
        dmerged = _mm(du1, W["w_out"], mode="nt", name="out_dx")
        dw_out = _mm(sv["merged"], du1, mode="tn", name="out_dw")
        mb = _merge_bwd(dmerged, sv["o_a"], sv["o_b"], sv["o_c"], W["w_br"], sv["proj"], D, "merge_bwd")
        dgl, do, dbp = mb[0:3], mb[3:6], mb[6:9]
        dw_br = jnp.stack([_mm(o_n, dbp_n, mode="tn", name="br_dw") for o_n, dbp_n in zip((sv["o_a"], sv["o_b"], sv["o_c"]), dbp)])
        proj = sv["proj"]
        dq_a, dk_a, dv_a, dcum = _attn_bwd(proj, C_Q // LANES, proj, C_K // LANES, proj, C_V // LANES, sv["o_a"], do[0], sv["lse_a"],
                                           cum_r=sv["cum_r"], name="fox_bwd", **fox)
        dcum = dcum.reshape(B, nb, S, 2).transpose(0, 2, 1, 3).reshape(T, FOX_HEADS)
        dcum = jnp.pad(dcum, ((0, 0), (0, LANES - FOX_HEADS)))
        dzf, dbf = _gate_bwd(dcum, proj, sv["bpad"], B, S, "gate_bwd")
        dqf, dkf, dv_b = _attn_bwd(sv["qf"], 0, sv["kf"], 0, sv["v_b"], 0, sv["o_b"], do[1], sv["lse_b"], dq_dtype=f32,
                                   name="mla_bwd", **mla)
        dqraw = _rope_q(dqf, tabs, bwd=True, name="rope_q_bwd")
        dkv, dkpe = _rope_k_bwd(dkf, dv_b, tabs, "rope_k_bwd")
        dw_uq = _mm(sv["cqn"], dqraw, mode="tn", name="q_up_dw")
        dcqn = _mm(dqraw, W["w_uq"], mode="nt", name="q_up_dx")
        dcq_, dg_cq = _rms_bwd(dcqn, proj, C_CQ, Q_RANK, g_cq[l], "rms_q_bwd")
        dw_ukv = _mm(sv["ckvn"], dkv, mode="tn", name="kv_up_dw")
        dckvn = _mm(dkv, W["w_ukv"], mode="nt", name="kv_up_dx")
        dckv, dg_ckv = _rms_bwd(dckvn, proj, C_CKV, KV_RANK, g_ckv[l], "rms_kv_bwd")
        dqm, dmk, dmv = _attn_bwd(proj, C_QM // MEM_HD, sv["mkv"], 0, sv["mkv"], MEM_HEADS, sv["o_c"], do[2], sv["lse_c"],
                                  name="mem_bwd", **memat)
        dw_mem = _mm(mem2d, jnp.concatenate([dmk, dmv], axis=1), mode="tn", name="mem_kv_dw")
        dproj = jnp.concatenate([dq_a, dk_a, dv_a, dzf, dkpe, dckv, dqm, jnp.zeros((T, C_CQ - C_PAD), bf16), dcq_, *dgl], axis=1)
        dw_in = _lane_permute(_mm(sv["h"], dproj, mode="tn", name="proj_dw"), bwd_map, f32, "w_in_grad_reorder")
        dh = _mm(dproj, W["w_in"], mode="nt", epi="add", extra=du1, alpha=alpha, name="proj_dx")
        grads_l = dict(w_in=dw_in, w_uq=dw_uq, w_ukv=dw_ukv, w_mem_kv=dw_mem, w_br=dw_br, w_out=dw_out, w_ff1=dw_ff1, w_ff2=dw_ff2)
        slots = _exchange([_as3(grads_l[n]) for n, _ in _BIG], big_axes, slots, l, "grads_exchange")
        for n, val in (("b_forget", dbf[:, :FOX_HEADS]), ("g_cq", dg_cq), ("g_ckv", dg_ckv), ("ln1_g", dg1), ("ln1_b", db1),
                       ("ln2_g", dg2), ("ln2_b", db2)):
            small[n][l] = val.reshape(-1)
    grad_x, dg_in, db_in = _ln_bwd(dh, xh0, rstd0, ln_in_g, "ln_in_bwd")

    small_list = [("ln_in_g", dg_in.reshape(-1)), ("ln_in_b", db_in.reshape(-1))] + [(n, jnp.stack(v).reshape(-1)) for n, v in small.items()]
    sflat = jnp.concatenate([v for _, v in small_list])
    n_small = sflat.shape[0]
    rs = _round_up(-(-n_small // LANES), 8)
    spacked = jnp.pad(sflat, (0, rs * LANES - n_small)).reshape(1, rs, LANES)
    sparts = _all_gather([spacked], [1], "small_all_gather")[0].reshape(N_DEV, rs * LANES)

    grads, deltas, new_m, new_v = {}, {}, {}, {}
    off = 0
    for n, vv in small_list:
        parts = sparts[:, off:off + vv.shape[0]].reshape((N_DEV,) + weights[n].shape)
        off += vv.shape[0]
        grads[n], deltas[n], new_m[n], new_v[n] = _adamw(weights[n], mom_m[n], mom_v[n], parts, "adamw_" + n)
    for (n, _), sl in zip(_BIG, slots):
        parts = sl.reshape((N_DEV,) + weights[n].shape[:-1] + (sl.shape[-1],))
        grads[n], deltas[n], new_m[n], new_v[n] = _adamw(weights[n], mom_m[n], mom_v[n], parts, "adamw_" + n)
    return (loss, grad_x.reshape(B, S, D), *[grads[n] for n in order], *[deltas[n] for n in order], *[new_m[n] for n in order],
            *[new_v[n] for n in order])
```

```python
import functools

import numpy as np
import jax
import jax.numpy as jnp
from jax import lax
from jax.experimental import pallas as pl
from jax.experimental.pallas import tpu as pltpu

f32, bf16 = jnp.float32, jnp.bfloat16
SDS = jax.ShapeDtypeStruct

N_DEV = 8
MESH = pl.DeviceIdType.MESH
LANES = 128
VMEM_LIMIT = 56 * 1024 * 1024

FOX_HEADS, FOX_HD = 8, 64
MLA_HEADS, NOPE, ROPE, MLA_VD = 8, 64, 32, 64
Q_RANK, KV_RANK = 384, 256
ROPE_BASE = 10000.0
MEM_HEADS, MEM_HD = 4, 128
BW = 512
CHUNK = 64
LN_EPS, RMS_EPS, NEG_INF = 1e-5, 1e-6, -1e30
ADAM_LR, ADAM_B1, ADAM_B2, ADAM_EPS, ADAM_WD, ADAM_STEP = 0.001, 0.9, 0.999, 1e-08, 0.01, 10

ATT_TQ, ATT_TK = 512, 256

C_Q, C_K, C_V, C_F, C_KPE, C_CKV, C_QM, C_PAD, C_CQ, C_GATE = 0, 512, 1024, 1536, 1664, 1792, 2048, 2560, 2688, 3072
O_F, O_CQ, O_CKV, O_KR, O_QM, O_GATE = 1536, 1544, 1928, 2184, 2216, 2728


def _pick(dim, pref):
    if dim <= pref:
        return dim
    for c in (1024, 768, 512, 384, 256, 128, 64, 32, 16, 8):
        if c <= pref and dim % c == 0:
            return c
    return dim


def _round_up(n, m):
    return -(-n // m) * m


def _params(sem=None):
    return pltpu.CompilerParams(dimension_semantics=sem, vmem_limit_bytes=VMEM_LIMIT)


_DIMS = {"nn": (((1,), (0,)), ((), ())), "nt": (((1,), (1,)), ((), ())), "tn": (((0,), (0,)), ((), ()))}


def _mm(a, b, *, mode="nn", out_dtype=f32, epi="none", extra=None, alpha=1.0, tm=1024, tn=1024, tk=None, name):
    if mode == "nn":
        (M, K), (_, N) = a.shape, b.shape
    elif mode == "nt":
        (M, K), (N, _) = a.shape, b.shape
    else:
        (K, M), (_, N) = a.shape, b.shape
    tm, tn, tk = _pick(M, tm), _pick(N, tn), _pick(K, tk or (512 if mode == "tn" else 1024))
    nk = K // tk
    a_spec = pl.BlockSpec((tk, tm), lambda i, j, k: (k, i)) if mode == "tn" else pl.BlockSpec((tm, tk), lambda i, j, k: (i, k))
    b_spec = pl.BlockSpec((tn, tk), lambda i, j, k: (j, k)) if mode == "nt" else pl.BlockSpec((tk, tn), lambda i, j, k: (k, j))
    o_spec = pl.BlockSpec((tm, tn), lambda i, j, k: (i, j))
    n_out = 2 if epi == "relu2" else 1
    dims = _DIMS[mode]

    def body(*refs):
        a_ref, b_ref = refs[0], refs[1]
        e_ref = refs[2] if extra is not None else None
        outs = refs[-1 - n_out:-1]
        acc = refs[-1]
        k = pl.program_id(2)

        @pl.when(k == 0)
        def _():
            acc[...] = jnp.zeros_like(acc)

        acc[...] += lax.dot_general(a_ref[...].astype(bf16), b_ref[...].astype(bf16), dims, preferred_element_type=f32)

        @pl.when(k == nk - 1)
        def _():
            r = acc[...]
            if epi == "none":
                outs[0][...] = r.astype(out_dtype)
            elif epi == "relu2":
                p = jnp.maximum(r, 0.0)
                outs[0][...] = (p * p).astype(bf16)
                outs[1][...] = (2.0 * p).astype(bf16)
            elif epi == "mul":
                outs[0][...] = (r * e_ref[...].astype(f32)).astype(out_dtype)
            else:
                outs[0][...] = (r + alpha * e_ref[...].astype(f32)).astype(out_dtype)

    ins, in_specs = [a, b], [a_spec, b_spec]
    if extra is not None:
        ins.append(extra)
        in_specs.append(o_spec)
    if epi == "relu2":
        out_shape, out_specs = [SDS((M, N), bf16), SDS((M, N), bf16)], [o_spec, o_spec]
    else:
        out_shape, out_specs = SDS((M, N), out_dtype), o_spec
    return pl.pallas_call(
        body, grid=(M // tm, N // tn, nk), in_specs=in_specs, out_specs=out_specs, out_shape=out_shape,
        scratch_shapes=[pltpu.VMEM((tm, tn), f32)], name=name,
        compiler_params=_params(("parallel", "parallel", "arbitrary")))(*ins)


def _lane_permute(src, cmap, out_dtype, name):
    R, _ = src.shape
    cmap = np.asarray(cmap, np.int64)
    nb = cmap.shape[0] // LANES
    srcs = [sorted({int(c) // LANES for c in cmap[b * LANES:(b + 1) * LANES] if c >= 0}) or [0] for b in range(nb)]
    smax = max(len(s) for s in srcs)
    nsrc = np.array([len(s) for s in srcs], np.int32)
    blk = np.array([s + [s[-1]] * (smax - len(s)) for s in srcs], np.int32).reshape(-1)
    exact = src.dtype == f32
    tr = _pick(R, 1024)

    def body(blk_ref, ns_ref, src_ref, cm_ref, o_ref, acc):
        j, s = pl.program_id(1), pl.program_id(2)

        @pl.when(s == 0)
        def _():
            acc[...] = jnp.zeros_like(acc)

        @pl.when(s < ns_ref[j])
        def _():
            rows = lax.broadcasted_iota(jnp.int32, (LANES, LANES), 0) + blk_ref[j * smax + s] * LANES
            sel = rows == cm_ref[...]
            if exact:
                acc[...] += jnp.dot(src_ref[...], sel.astype(f32), precision=lax.Precision.HIGHEST, preferred_element_type=f32)
            else:
                acc[...] += jnp.dot(src_ref[...], sel.astype(src_ref.dtype), preferred_element_type=f32)

        @pl.when(s == smax - 1)
        def _():
            o_ref[...] = acc[...].astype(out_dtype)

    grid_spec = pltpu.PrefetchScalarGridSpec(
        num_scalar_prefetch=2, grid=(R // tr, nb, smax),
        in_specs=[pl.BlockSpec((tr, LANES), lambda i, j, s, blk_ref, ns_ref: (i, blk_ref[j * smax + s])),
                  pl.BlockSpec((1, LANES), lambda i, j, s, blk_ref, ns_ref: (0, j))],
        out_specs=pl.BlockSpec((tr, LANES), lambda i, j, s, blk_ref, ns_ref: (i, j)),
        scratch_shapes=[pltpu.VMEM((tr, LANES), f32)])
    return pl.pallas_call(
        body, grid_spec=grid_spec, out_shape=SDS((R, nb * LANES), out_dtype), name=name,
        compiler_params=_params(("parallel", "parallel", "arbitrary")))(
            jnp.asarray(blk), jnp.asarray(nsrc), src, jnp.asarray(cmap.astype(np.int32)).reshape(1, -1))


def _ln_fwd(h, y, g, b, alpha, name):
    T, D = y.shape
    tr = _pick(T, 256)
    row = pl.BlockSpec((tr, D), lambda i: (i, 0))
    vec = pl.BlockSpec((1, D), lambda i: (0, 0))
    col = pl.BlockSpec((tr, 1), lambda i: (i, 0))

    def body(*refs):
        if h is None:
            y_ref, g_ref, b_ref, o_ref, ob_ref, xh_ref, rs_ref = refs
            u = y_ref[...]
        else:
            h_ref, y_ref, g_ref, b_ref, o_ref, ob_ref, xh_ref, rs_ref = refs
            u = alpha * h_ref[...] + y_ref[...]
        mu = jnp.mean(u, axis=-1, keepdims=True)
        d = u - mu
        var = jnp.mean(d * d, axis=-1, keepdims=True)
        rstd = lax.rsqrt(var + LN_EPS)
        xh = d * rstd
        xh_ref[...] = xh
        rs_ref[...] = rstd
        o = xh * g_ref[...] + b_ref[...]
        o_ref[...] = o
        ob_ref[...] = o.astype(bf16)

    ins = ([] if h is None else [h]) + [y, g.reshape(1, D), b.reshape(1, D)]
    specs = ([] if h is None else [row]) + [row, vec, vec]
    return pl.pallas_call(
        body, grid=(T // tr,), in_specs=specs, out_specs=[row, row, row, col],
        out_shape=[SDS((T, D), f32), SDS((T, D), bf16), SDS((T, D), f32), SDS((T, 1), f32)], name=name,
        compiler_params=_params(("parallel",)))(*ins)


def _ln_bwd(dy, xh, rstd, g, name):
    T, D = dy.shape
    tr = _pick(T, 256)
    row = pl.BlockSpec((tr, D), lambda i: (i, 0))
    vec = pl.BlockSpec((1, D), lambda i: (0, 0))
    col = pl.BlockSpec((tr, 1), lambda i: (i, 0))

    def body(dy_ref, xh_ref, rs_ref, g_ref, du_ref, dg_ref, db_ref):
        @pl.when(pl.program_id(0) == 0)
        def _():
            dg_ref[...] = jnp.zeros_like(dg_ref)
            db_ref[...] = jnp.zeros_like(db_ref)

        dyv, xhv = dy_ref[...], xh_ref[...]
        dxh = dyv * g_ref[...]
        m1 = jnp.mean(dxh, axis=-1, keepdims=True)
        m2 = jnp.mean(dxh * xhv, axis=-1, keepdims=True)
        du_ref[...] = rs_ref[...] * (dxh - m1 - xhv * m2)
        dg_ref[...] += jnp.sum(dyv * xhv, axis=0, keepdims=True)
        db_ref[...] += jnp.sum(dyv, axis=0, keepdims=True)

    return pl.pallas_call(
        body, grid=(T // tr,), in_specs=[row, row, col, vec], out_specs=[row, vec, vec],
        out_shape=[SDS((T, D), f32), SDS((1, D), f32), SDS((1, D), f32)], name=name,
        compiler_params=_params(("arbitrary",)))(dy, xh, rstd, g.reshape(1, D))


def _rms_fwd(proj, col_off, R, g, name):
    T = proj.shape[0]
    tr = _pick(T, 512)
    cb = col_off // R

    def body(c_ref, g_ref, o_ref):
        c = c_ref[...]
        r = lax.rsqrt(jnp.mean(c * c, axis=-1, keepdims=True) + RMS_EPS)
        o_ref[...] = (c * r * g_ref[...]).astype(bf16)

    return pl.pallas_call(
        body, grid=(T // tr,), in_specs=[pl.BlockSpec((tr, R), lambda i: (i, cb)), pl.BlockSpec((1, R), lambda i: (0, 0))],
        out_specs=pl.BlockSpec((tr, R), lambda i: (i, 0)), out_shape=SDS((T, R), bf16), name=name,
        compiler_params=_params(("parallel",)))(proj, g.reshape(1, R))


def _rms_bwd(dy, proj, col_off, R, g, name):
    T = proj.shape[0]
    tr = _pick(T, 512)
    cb = col_off // R

    def body(dy_ref, c_ref, g_ref, dc_ref, dg_ref):
        @pl.when(pl.program_id(0) == 0)
        def _():
            dg_ref[...] = jnp.zeros_like(dg_ref)

        c, dyv = c_ref[...], dy_ref[...]
        r = lax.rsqrt(jnp.mean(c * c, axis=-1, keepdims=True) + RMS_EPS)
        t = dyv * g_ref[...]
        mt = jnp.mean(t * c, axis=-1, keepdims=True)
        dc_ref[...] = (r * t - c * (r * r * r) * mt).astype(bf16)
        dg_ref[...] += jnp.sum(dyv * c * r, axis=0, keepdims=True)

    return pl.pallas_call(
        body, grid=(T // tr,),
        in_specs=[pl.BlockSpec((tr, R), lambda i: (i, 0)), pl.BlockSpec((tr, R), lambda i: (i, cb)), pl.BlockSpec((1, R), lambda i: (0, 0))],
        out_specs=[pl.BlockSpec((tr, R), lambda i: (i, 0)), pl.BlockSpec((1, R), lambda i: (0, 0))],
        out_shape=[SDS((T, R), bf16), SDS((1, R), f32)], name=name,
        compiler_params=_params(("arbitrary",)))(dy, proj, g.reshape(1, R))


def _tri(n, lower):
    r = lax.broadcasted_iota(jnp.int32, (n, n), 0)
    c = lax.broadcasted_iota(jnp.int32, (n, n), 1)
    return ((r >= c) if lower else (c >= r)).astype(f32)


def _gate_fwd(proj, bpad, B, S, name):
    ch = _pick(S, 256)
    nch = S // ch
    cb = C_F // LANES

    def body(f_ref, b_ref, o_ref):
        tri = _tri(ch, True)
        carry = jnp.zeros((1, LANES), f32)
        for c in range(nch):
            z = f_ref[c * ch:(c + 1) * ch, :] + b_ref[...]
            lf = jnp.minimum(z, 0.0) - jnp.log1p(jnp.exp(-jnp.abs(z)))
            cs = jnp.dot(tri, lf, precision=lax.Precision.HIGHEST, preferred_element_type=f32) + carry
            o_ref[c * ch:(c + 1) * ch, :] = cs
            carry = cs[ch - 1:ch, :]

    return pl.pallas_call(
        body, grid=(B,), in_specs=[pl.BlockSpec((S, LANES), lambda b: (b, cb)), pl.BlockSpec((1, LANES), lambda b: (0, 0))],
        out_specs=pl.BlockSpec((S, LANES), lambda b: (b, 0)), out_shape=SDS((B * S, LANES), f32), name=name,
        compiler_params=_params(("parallel",)))(proj, bpad)


def _gate_bwd(dcum, proj, bpad, B, S, name):
    ch = _pick(S, 256)
    nch = S // ch
    cb = C_F // LANES

    def body(d_ref, f_ref, b_ref, dz_ref, db_ref):
        @pl.when(pl.program_id(0) == 0)
        def _():
            db_ref[...] = jnp.zeros_like(db_ref)

        tri = _tri(ch, False)
        carry = jnp.zeros((1, LANES), f32)
        dbs = jnp.zeros((1, LANES), f32)
        for c in reversed(range(nch)):
            d = d_ref[c * ch:(c + 1) * ch, :]
            dlf = jnp.dot(tri, d, precision=lax.Precision.HIGHEST, preferred_element_type=f32) + carry
            carry = carry + jnp.sum(d, axis=0, keepdims=True)
            z = f_ref[c * ch:(c + 1) * ch, :] + b_ref[...]
            dz = dlf / (1.0 + jnp.exp(z))
            dz_ref[c * ch:(c + 1) * ch, :] = dz.astype(bf16)
            dbs = dbs + jnp.sum(dz, axis=0, keepdims=True)
        db_ref[...] += dbs

    return pl.pallas_call(
        body, grid=(B,),
        in_specs=[pl.BlockSpec((S, LANES), lambda b: (b, 0)), pl.BlockSpec((S, LANES), lambda b: (b, cb)), pl.BlockSpec((1, LANES), lambda b: (0, 0))],
        out_specs=[pl.BlockSpec((S, LANES), lambda b: (b, 0)), pl.BlockSpec((1, LANES), lambda b: (0, 0))],
        out_shape=[SDS((B * S, LANES), bf16), SDS((1, LANES), f32)], name=name,
        compiler_params=_params(("arbitrary",)))(dcum, proj, bpad)


def _rot(v, c, sa, sb):
    return v * c + pltpu.roll(v, LANES - 16, 1) * sa + pltpu.roll(v, 16, 1) * sb


def _rot_t(v, c, sa, sb):
    return v * c + pltpu.roll(v * sa, 16, 1) + pltpu.roll(v * sb, LANES - 16, 1)


def _rope_q(x, tabs, *, bwd, name):
    T = x.shape[0]
    W = MLA_HEADS * LANES
    tr = _pick(T, 512)
    fn = _rot_t if bwd else _rot

    def body(x_ref, c_ref, sa_ref, sb_ref, o_ref):
        c, sa, sb = c_ref[...], sa_ref[...], sb_ref[...]
        for g in range(MLA_HEADS):
            o_ref[:, g * LANES:(g + 1) * LANES] = fn(x_ref[:, g * LANES:(g + 1) * LANES], c, sa, sb).astype(bf16)

    tab = pl.BlockSpec((tr, LANES), lambda i: (i, 0))
    wide = pl.BlockSpec((tr, W), lambda i: (i, 0))
    return pl.pallas_call(body, grid=(T // tr,), in_specs=[wide, tab, tab, tab], out_specs=wide, out_shape=SDS((T, W), bf16),
                          name=name, compiler_params=_params(("parallel",)))(x, *tabs)


def _rope_k(kv, tabs, proj, name):
    T = kv.shape[0]
    W = MLA_HEADS * LANES
    tr = _pick(T, 512)
    kcb = C_KPE // LANES

    def body(x_ref, c_ref, sa_ref, sb_ref, kpe_ref, k_ref, v_ref):
        c, sa, sb, kpe = c_ref[...], sa_ref[...], sb_ref[...], kpe_ref[...]
        low = lax.broadcasted_iota(jnp.int32, (tr, LANES), 1) < NOPE
        halves = []
        for g in range(MLA_HEADS):
            x = x_ref[:, g * LANES:(g + 1) * LANES]
            k_ref[:, g * LANES:(g + 1) * LANES] = _rot(jnp.where(low, x, 0.0) + kpe, c, sa, sb).astype(bf16)
            halves.append(x[:, NOPE:])
            if g % 2 == 1:
                v_ref[:, (g // 2) * LANES:(g // 2 + 1) * LANES] = jnp.concatenate(halves, axis=1).astype(bf16)
                halves = []

    tab = pl.BlockSpec((tr, LANES), lambda i: (i, 0))
    wide = pl.BlockSpec((tr, W), lambda i: (i, 0))
    return pl.pallas_call(
        body, grid=(T // tr,), in_specs=[wide, tab, tab, tab, pl.BlockSpec((tr, LANES), lambda i: (i, kcb))],
        out_specs=[wide, pl.BlockSpec((tr, W // 2), lambda i: (i, 0))], out_shape=[SDS((T, W), bf16), SDS((T, W // 2), bf16)],
        name=name, compiler_params=_params(("parallel",)))(kv, *tabs, proj)


def _rope_k_bwd(dk, dv, tabs, name):
    T = dk.shape[0]
    W = MLA_HEADS * LANES
    tr = _pick(T, 512)

    def body(dk_ref, dv_ref, c_ref, sa_ref, sb_ref, o_ref, s_ref):
        c, sa, sb = c_ref[...], sa_ref[...], sb_ref[...]
        tot = jnp.zeros((tr, LANES), f32)
        for g in range(MLA_HEADS):
            d = _rot_t(dk_ref[:, g * LANES:(g + 1) * LANES], c, sa, sb)
            tot = tot + d
            o_ref[:, g * LANES:(g + 1) * LANES] = jnp.concatenate(
                [d[:, :NOPE].astype(bf16), dv_ref[:, g * MLA_VD:(g + 1) * MLA_VD]], axis=1)
        s_ref[...] = tot.astype(bf16)

    tab = pl.BlockSpec((tr, LANES), lambda i: (i, 0))
    wide = pl.BlockSpec((tr, W), lambda i: (i, 0))
    return pl.pallas_call(
        body, grid=(T // tr,), in_specs=[wide, pl.BlockSpec((tr, W // 2), lambda i: (i, 0)), tab, tab, tab],
        out_specs=[wide, tab], out_shape=[SDS((T, W), bf16), SDS((T, LANES), bf16)], name=name,
        compiler_params=_params(("parallel",)))(dk, dv, *tabs)


def _tile_mask(mode, i, j, tq, tk):
    r = i * tq + lax.broadcasted_iota(jnp.int32, (tq, tk), 0)
    c = j * tk + lax.broadcasted_iota(jnp.int32, (tq, tk), 1)
    if mode == "chunk":
        return (c // CHUNK) <= (r // CHUNK)
    return c <= r


def _attn_tiles(Sq, Sk):
    tk = _pick(Sk, ATT_TK)
    tq = _pick(Sq, ATT_TQ)
    assert (tk % tq == 0 or tq % tk == 0) and tq % CHUNK == 0 and tk % CHUNK == 0
    return tq, tk


def _stage_heads(dst, src_ref, hb, w, mark=None, val=1.0):
    S = src_ref.shape[0]
    ch = _pick(S, 256)
    assert w <= LANES and (mark is None or w <= mark < LANES)

    def step(c, _):
        rows = pl.ds(pl.multiple_of(c * ch, ch), ch)
        for hh in range(hb):
            x = src_ref[rows, hh * w:(hh + 1) * w].astype(bf16)
            if w < LANES:
                pad = jnp.zeros((ch, LANES - w), bf16)
                if mark is not None:
                    lane = lax.broadcasted_iota(jnp.int32, (ch, LANES - w), 1)
                    pad = jnp.where(lane == mark - w, val, 0.0).astype(bf16)
                x = jnp.concatenate([x, pad], axis=1)
            dst[hh, rows, :] = x
        return 0

    lax.fori_loop(0, S // ch, step, 0)


def _attn_fwd(q_arr, q_cb, k_arr, k_cb, v_arr, v_cb, *, B, Sq, Sk, nblk, hb, dqk, dv, mode, scale, cum_r=None, name):
    wq, wv = hb * dqk, hb * dv
    tq, tk = _attn_tiles(Sq, Sk)
    fox = mode == "fox"
    n_in = 4 if fox else 3
    ones = dv < LANES

    def body(*refs):
        q_ref, k_ref, v_ref = refs[:3]
        cr_ref = refs[3] if fox else None
        o_ref, lse_ref, qh, kh, vh = refs[n_in:]
        _stage_heads(qh, q_ref, hb, dqk)
        _stage_heads(kh, k_ref, hb, dqk)
        _stage_heads(vh, v_ref, hb, dv, dv if ones else None)

        def qstep(i, _):
            rows = pl.ds(pl.multiple_of(i * tq, tq), tq)
            qs = [qh[hh, rows, :] for hh in range(hb)]

            def tile(j, carry, diag):
                cols = pl.ds(pl.multiple_of(j * tk, tk), tk)
                out = []
                for hh in range(hb):
                    m, l, acc = carry[hh]
                    s = lax.dot_general(qs[hh], kh[hh, cols, :], _DIMS["nt"], preferred_element_type=f32) * scale
                    if fox:
                        s = s - cr_ref[0, hh:hh + 1, cols]
                    if diag:
                        s = jnp.where(_tile_mask(mode, i, j, tq, tk), s, NEG_INF)
                    m_new = jnp.maximum(m, jnp.max(s, axis=-1, keepdims=True))
                    a = jnp.exp(m - m_new)
                    p = jnp.exp(s - m_new)
                    if not ones:
                        l = a * l + jnp.sum(p, axis=-1, keepdims=True)
                    acc = a * acc + lax.dot_general(p.astype(bf16), vh[hh, cols, :], _DIMS["nn"], preferred_element_type=f32)
                    out.append((m_new, l, acc))
                return tuple(out)

            carry = tuple((jnp.full((tq, 1), -jnp.inf, f32), jnp.zeros((tq, 1), f32), jnp.zeros((tq, LANES), f32)) for _ in range(hb))
            if mode == "none":
                carry = lax.fori_loop(0, Sk // tk, functools.partial(tile, diag=False), carry)
            else:
                nfull = (i * tq) // tk
                carry = lax.fori_loop(0, nfull, functools.partial(tile, diag=False), carry)
                for d in range(max(1, tq // tk)):
                    carry = tile(nfull + d, carry, True)
            for hh in range(hb):
                m, l, acc = carry[hh]
                if ones:
                    l = acc[:, dv:dv + 1]
                o_ref[rows, hh * dv:(hh + 1) * dv] = acc[:, :dv] / l
                lse_ref[0, rows, hh:hh + 1] = m + jnp.log(l)
            return 0

        lax.fori_loop(0, Sq // tq, qstep, 0)

    in_specs = [pl.BlockSpec((Sq, wq), lambda b, p: (b, q_cb + p)), pl.BlockSpec((Sk, wq), lambda b, p: (b, k_cb + p)),
                pl.BlockSpec((Sk, wv), lambda b, p: (b, v_cb + p))]
    ins = [q_arr, k_arr, v_arr]
    if fox:
        in_specs.append(pl.BlockSpec((1, hb, Sk), lambda b, p: (b * nblk + p, 0, 0)))
        ins.append(cum_r)
    return pl.pallas_call(
        body, grid=(B, nblk), in_specs=in_specs,
        out_specs=[pl.BlockSpec((Sq, wv), lambda b, p: (b, p)), pl.BlockSpec((1, Sq, hb), lambda b, p: (b * nblk + p, 0, 0))],
        out_shape=[SDS((B * Sq, nblk * wv), f32), SDS((B * nblk, Sq, hb), f32)],
        scratch_shapes=[pltpu.VMEM((hb, Sq, LANES), bf16), pltpu.VMEM((hb, Sk, LANES), bf16), pltpu.VMEM((hb, Sk, LANES), bf16)],
        name=name, compiler_params=_params(("parallel", "parallel")))(*ins)


def _attn_bwd(q_arr, q_cb, k_arr, k_cb, v_arr, v_cb, o, do, lse, *, B, Sq, Sk, nblk, hb, dqk, dv, mode, scale, cum_r=None,
              dq_dtype=bf16, name):
    wq, wv = hb * dqk, hb * dv
    tq, tk = _attn_tiles(Sq, Sk)
    fox = mode == "fox"
    n_in = 7 if fox else 6
    n_out = 4 if fox else 3
    if fox:
        assert dqk + 2 <= LANES and np.log2(scale) == np.round(np.log2(scale))

    def body(*refs):
        q_ref, k_ref, v_ref, o_ref, do_ref, lse_ref = refs[:6]
        cr_ref = refs[6] if fox else None
        dq_ref, dk_ref, dv_ref = refs[n_in:n_in + 3]
        dcum_ref = refs[n_in + 3] if fox else None
        qh, kh, vh, doh, dk_acc, dv_acc = refs[n_in + n_out:n_in + n_out + 6]
        dcq_acc = refs[n_in + n_out + 6] if fox else None
        _stage_heads(qh, q_ref, hb, dqk, dqk + 1 if fox else None, 1.0 / scale)
        _stage_heads(kh, k_ref, hb, dqk, dqk if fox else None, 1.0 / scale)
        _stage_heads(vh, v_ref, hb, dv)
        _stage_heads(doh, do_ref, hb, dv)
        dk_acc[...] = jnp.zeros_like(dk_acc)
        dv_acc[...] = jnp.zeros_like(dv_acc)

        def qstep(i, _):
            rows = pl.ds(pl.multiple_of(i * tq, tq), tq)
            qs = [qh[hh, rows, :] for hh in range(hb)]
            dos = [doh[hh, rows, :] for hh in range(hb)]
            dds = [jnp.sum(do_ref[rows, hh * dv:(hh + 1) * dv] * o_ref[rows, hh * dv:(hh + 1) * dv], axis=-1, keepdims=True)
                   for hh in range(hb)]
            lses = [lse_ref[0, rows, hh:hh + 1] for hh in range(hb)]

            def tile(j, carry, diag):
                cols = pl.ds(pl.multiple_of(j * tk, tk), tk)
                out = []
                for hh in range(hb):
                    k = kh[hh, cols, :]
                    s = lax.dot_general(qs[hh], k, _DIMS["nt"], preferred_element_type=f32) * scale
                    if fox:
                        s = s - cr_ref[0, hh:hh + 1, cols]
                    if diag:
                        s = jnp.where(_tile_mask(mode, i, j, tq, tk), s, NEG_INF)
                    p = jnp.exp(s - lses[hh])
                    dv_acc[hh, cols, :] += lax.dot_general(p.astype(bf16), dos[hh], _DIMS["tn"], preferred_element_type=f32)
                    dp = lax.dot_general(dos[hh], vh[hh, cols, :], _DIMS["nt"], preferred_element_type=f32)
                    ds = p * (dp - dds[hh])
                    dsb = (ds * scale).astype(bf16)
                    out.append(carry[hh] + lax.dot_general(dsb, k, _DIMS["nn"], preferred_element_type=f32))
                    dk_acc[hh, cols, :] += lax.dot_general(dsb, qs[hh], _DIMS["tn"], preferred_element_type=f32)
                return tuple(out)

            carry = tuple(jnp.zeros((tq, LANES), f32) for _ in range(hb))
            if mode == "none":
                carry = lax.fori_loop(0, Sk // tk, functools.partial(tile, diag=False), carry)
            else:
                nfull = (i * tq) // tk
                carry = lax.fori_loop(0, nfull, functools.partial(tile, diag=False), carry)
                for d in range(max(1, tq // tk)):
                    carry = tile(nfull + d, carry, True)
            for hh in range(hb):
                dq_ref[rows, hh * dqk:(hh + 1) * dqk] = carry[hh][:, :dqk].astype(dq_dtype)
                if fox:
                    dcq_acc[rows, hh:hh + 1] = carry[hh][:, dqk:dqk + 1]
            return 0

        lax.fori_loop(0, Sq // tq, qstep, 0)
        for hh in range(hb):
            dk_ref[:, hh * dqk:(hh + 1) * dqk] = dk_acc[hh, :, :dqk].astype(dq_dtype)
            dv_ref[:, hh * dv:(hh + 1) * dv] = dv_acc[hh, :, :dv].astype(bf16)
            if fox:
                dcum_ref[0, :, hh:hh + 1] = dcq_acc[:, hh:hh + 1] - dk_acc[hh, :, dqk + 1:dqk + 2]

    qs_ = pl.BlockSpec((Sq, wq), lambda b, p: (b, q_cb + p))
    ks_ = pl.BlockSpec((Sk, wq), lambda b, p: (b, k_cb + p))
    vs_ = pl.BlockSpec((Sk, wv), lambda b, p: (b, v_cb + p))
    os_ = pl.BlockSpec((Sq, wv), lambda b, p: (b, p))
    st = pl.BlockSpec((1, Sq, hb), lambda b, p: (b * nblk + p, 0, 0))
    rw = pl.BlockSpec((1, hb, Sk), lambda b, p: (b * nblk + p, 0, 0))
    in_specs = [qs_, ks_, vs_, os_, os_, st] + ([rw] if fox else [])
    ins = [q_arr, k_arr, v_arr, o, do, lse] + ([cum_r] if fox else [])
    out_specs = [pl.BlockSpec((Sq, wq), lambda b, p: (b, p)), pl.BlockSpec((Sk, wq), lambda b, p: (b, p)),
                 pl.BlockSpec((Sk, wv), lambda b, p: (b, p))] + ([st] if fox else [])
    out_shape = [SDS((B * Sq, nblk * wq), dq_dtype), SDS((B * Sk, nblk * wq), dq_dtype), SDS((B * Sk, nblk * wv), bf16)]
    scratch = [pltpu.VMEM((hb, Sq, LANES), bf16), pltpu.VMEM((hb, Sk, LANES), bf16), pltpu.VMEM((hb, Sk, LANES), bf16),
               pltpu.VMEM((hb, Sq, LANES), bf16), pltpu.VMEM((hb, Sk, LANES), f32), pltpu.VMEM((hb, Sk, LANES), f32)]
    if fox:
        assert Sq == Sk
        out_shape.append(SDS((B * nblk, Sq, hb), f32))
        scratch.append(pltpu.VMEM((Sq, hb), f32))
    return pl.pallas_call(
        body, grid=(B, nblk), in_specs=in_specs, out_specs=out_specs, out_shape=out_shape, scratch_shapes=scratch, name=name,
        compiler_params=_params(("parallel", "parallel")))(*ins)


def _merge_fwd(oa, ob, oc, w_br, proj, D, name):
    T = oa.shape[0]
    tr = _pick(T, 256)
    g0 = C_GATE // D
    o_spec = pl.BlockSpec((tr, BW), lambda i: (i, 0))
    w_spec = pl.BlockSpec((3, BW, D), lambda i: (0, 0, 0))

    def body(oa_ref, ob_ref, oc_ref, w_ref, ga_ref, gb_ref, gc_ref, m_ref):
        tot = jnp.zeros((tr, D), f32)
        for n, (o_ref, g_ref) in enumerate(((oa_ref, ga_ref), (ob_ref, gb_ref), (oc_ref, gc_ref))):
            bp = lax.dot_general(o_ref[...].astype(bf16), w_ref[n], _DIMS["nn"], preferred_element_type=f32)
            tot = tot + jax.nn.sigmoid(g_ref[...]) * bp
        m_ref[...] = tot.astype(bf16)

    gspecs = [pl.BlockSpec((tr, D), lambda i, n=n: (i, g0 + n)) for n in range(3)]
    return pl.pallas_call(
        body, grid=(T // tr,), in_specs=[o_spec, o_spec, o_spec, w_spec, *gspecs],
        out_specs=pl.BlockSpec((tr, D), lambda i: (i, 0)), out_shape=SDS((T, D), bf16), name=name,
        compiler_params=_params(("parallel",)))(oa, ob, oc, w_br, proj, proj, proj)


def _merge_bwd(dm, oa, ob, oc, w_br, proj, D, name):
    T = oa.shape[0]
    tr = _pick(T, 256)
    g0 = C_GATE // D
    o_spec = pl.BlockSpec((tr, BW), lambda i: (i, 0))
    d_spec = pl.BlockSpec((tr, D), lambda i: (i, 0))
    w_spec = pl.BlockSpec((3, BW, D), lambda i: (0, 0, 0))

    def body(dm_ref, oa_ref, ob_ref, oc_ref, w_ref, ga_ref, gb_ref, gc_ref, *outs):
        dmv = dm_ref[...]
        for n, (o_ref, g_ref) in enumerate(((oa_ref, ga_ref), (ob_ref, gb_ref), (oc_ref, gc_ref))):
            bp = lax.dot_general(o_ref[...].astype(bf16), w_ref[n], _DIMS["nn"], preferred_element_type=f32)
            g = jax.nn.sigmoid(g_ref[...])
            dbp = (dmv * g).astype(bf16)
            outs[n][...] = (dmv * bp * g * (1.0 - g)).astype(bf16)
            outs[3 + n][...] = lax.dot_general(dbp, w_ref[n], _DIMS["nt"], preferred_element_type=f32)
            outs[6 + n][...] = dbp

    gspecs = [pl.BlockSpec((tr, D), lambda i, n=n: (i, g0 + n)) for n in range(3)]
    return pl.pallas_call(
        body, grid=(T // tr,), in_specs=[d_spec, o_spec, o_spec, o_spec, w_spec, *gspecs],
        out_specs=[d_spec] * 3 + [o_spec] * 3 + [d_spec] * 3,
        out_shape=[SDS((T, D), bf16)] * 3 + [SDS((T, BW), f32)] * 3 + [SDS((T, D), bf16)] * 3, name=name,
        compiler_params=_params(("parallel",)))(dm, oa, ob, oc, w_br, proj, proj, proj)


def _loss(h, target, name):
    T, D = h.shape
    tr = _pick(T, 256)
    nt = T // tr
    row = pl.BlockSpec((tr, D), lambda i: (i, 0))

    def body(h_ref, t_ref, l_ref, d_ref, acc):
        i = pl.program_id(0)

        @pl.when(i == 0)
        def _():
            acc[...] = jnp.zeros_like(acc)

        e = h_ref[...] - t_ref[...]
        d_ref[...] = e * (1.0 / D)
        acc[...] += jnp.sum(e * e, axis=0, keepdims=True)

        @pl.when(i == nt - 1)
        def _():
            l_ref[...] = jnp.sum(acc[...], axis=-1, keepdims=True) * (0.5 / D)

    return pl.pallas_call(
        body, grid=(nt,), in_specs=[row, row], out_specs=[pl.BlockSpec((1, 1), lambda i: (0, 0)), row],
        out_shape=[SDS((1, 1), f32), SDS((T, D), f32)], scratch_shapes=[pltpu.VMEM((1, D), f32)], name=name,
        compiler_params=_params(("arbitrary",)))(h, target)


def _adamw(w, m, v, parts, name):
    shape = w.shape
    C = shape[-1]
    R = int(np.prod(shape[:-1])) if len(shape) > 1 else 1
    n_parts, Cp = parts.shape[0], parts.shape[-1]
    tr = _pick(R, 256)
    spec = pl.BlockSpec((tr, C), lambda i: (i, 0))

    def body(w_ref, m_ref, v_ref, p_ref, g_ref, d_ref, mo_ref, vo_ref):
        gv = p_ref[0].astype(f32)
        for q in range(1, n_parts):
            gv = gv + p_ref[q].astype(f32)
        gv = gv[:, :C]
        mn = ADAM_B1 * m_ref[...] + (1.0 - ADAM_B1) * gv
        vn = ADAM_B2 * v_ref[...] + (1.0 - ADAM_B2) * (gv * gv)
        m_hat = mn / (1.0 - ADAM_B1 ** ADAM_STEP)
        v_hat = vn / (1.0 - ADAM_B2 ** ADAM_STEP)
        g_ref[...] = gv
        d_ref[...] = -ADAM_LR * (m_hat / (jnp.sqrt(v_hat) + ADAM_EPS) + ADAM_WD * w_ref[...])
        mo_ref[...] = mn
        vo_ref[...] = vn

    outs = pl.pallas_call(
        body, grid=(R // tr,), in_specs=[spec] * 3 + [pl.BlockSpec((n_parts, tr, Cp), lambda i: (0, i, 0))], out_specs=[spec] * 4,
        out_shape=[SDS((R, C), f32)] * 4, name=name,
        compiler_params=_params(("parallel",)))(*[a.reshape(R, C) for a in (w, m, v)], parts.reshape(n_parts, R, Cp))
    return [o.reshape(shape) for o in outs]


def _coords():
    return lax.axis_index("x"), lax.axis_index("y"), lax.axis_index("c")


def _window(ref, axis, idx, n):
    start = pl.multiple_of(idx * n, n)
    if axis == 1:
        return ref.at[:, pl.ds(start, n), :]
    return ref.at[:, :, pl.ds(start, n)]


def _all_gather(shards, axes, name):
    nt = len(shards)
    sizes = [s.shape[a] for s, a in zip(shards, axes)]
    out_shape = [SDS(tuple(d * N_DEV if i == a else d for i, d in enumerate(s.shape)), s.dtype) for s, a in zip(shards, axes)]

    def body(*refs):
        x_refs, out_refs = refs[:nt], refs[nt:2 * nt]
        send_sems, recv_sems, local_sems = refs[2 * nt:]
        x, y, c = _coords()
        me, sibling = (x, y, c), (x, y, 1 - c)
        chips = [(1 - x, y), (x, 1 - y), (1 - x, 1 - y)]

        def win(t, px, py, pc):
            return _window(out_refs[t], axes[t], 4 * px + 2 * py + pc, sizes[t])

        def copy(t, k, block, to, src=None):
            return pltpu.make_async_remote_copy(
                src_ref=win(t, *block) if src is None else src, dst_ref=win(t, *block),
                send_sem=send_sems.at[7 * t + k], recv_sem=recv_sems.at[7 * t + k], device_id=to, device_id_type=MESH)

        mine = [pltpu.make_async_copy(x_refs[t], win(t, *me), local_sems.at[t]) for t in range(nt)]
        for cp in mine:
            cp.start()
        sent = []
        for t in range(nt):
            sent.append(copy(t, 0, me, sibling, src=x_refs[t]))
            sent += [copy(t, 1 + j, me, (*chip, c), src=x_refs[t]) for j, chip in enumerate(chips)]
        for cp in sent:
            cp.start()
        for j, chip in enumerate(chips):
            for t in range(nt):
                copy(t, 1 + j, (*chip, c), me).wait_recv()
                fwd = copy(t, 4 + j, (*chip, c), sibling)
                fwd.start()
                sent.append(fwd)
        for t in range(nt):
            copy(t, 0, sibling, me).wait_recv()
            for j, chip in enumerate(chips):
                copy(t, 4 + j, (*chip, 1 - c), me).wait_recv()
        for cp in sent:
            cp.wait_send()
        for cp in mine:
            cp.wait()

    any_spec = pl.BlockSpec(memory_space=pl.ANY)
    return pl.pallas_call(
        body, out_shape=out_shape, in_specs=[any_spec] * nt, out_specs=[any_spec] * nt,
        scratch_shapes=[pltpu.SemaphoreType.DMA((7 * nt,)), pltpu.SemaphoreType.DMA((7 * nt,)), pltpu.SemaphoreType.DMA((nt,))],
        name=name)(*shards)


N_CHIP = N_DEV // 2


def _pair_exchange(grads, axes, shards, name):
    nt = len(grads)
    sizes = [s[a] for s, a in zip(shards, axes)]

    def body(*refs):
        src_refs, out_refs = refs[:nt], refs[nt:2 * nt]
        send_sems, recv_sems = refs[2 * nt:]
        x, y, c = _coords()
        copies = []
        for t in range(nt):
            for q in range(N_CHIP):
                cp = pltpu.make_async_remote_copy(
                    src_ref=_window(src_refs[t], axes[t], 2 * q + (1 - c), sizes[t]), dst_ref=out_refs[t].at[q],
                    send_sem=send_sems.at[N_CHIP * t + q], recv_sem=recv_sems.at[N_CHIP * t + q],
                    device_id=(x, y, 1 - c), device_id_type=MESH)
                cp.start()
                copies.append(cp)
        for cp in copies:
            cp.wait_recv()
        for cp in copies:
            cp.wait_send()

    any_spec = pl.BlockSpec(memory_space=pl.ANY)
    return pl.pallas_call(
        body, out_shape=[SDS((N_CHIP,) + tuple(s), f32) for s in shards], in_specs=[any_spec] * nt, out_specs=[any_spec] * nt,
        scratch_shapes=[pltpu.SemaphoreType.DMA((N_CHIP * nt,)), pltpu.SemaphoreType.DMA((N_CHIP * nt,))], name=name)(*grads)


def _pair_sum(grad, buf, axis, core, name):
    _, G, r, c = buf.shape
    tr = _pick(r, 512)
    nr = r // tr

    def body(core_ref, g_ref, b_ref, o_ref):
        o_ref[0, 0] = (g_ref[0] + b_ref[0, 0]).astype(bf16)

    if axis == 2:
        g_spec = pl.BlockSpec((1, tr, c), lambda q, g, i, core_ref: (g, i, 2 * q + core_ref[0]))
    else:
        g_spec = pl.BlockSpec((1, tr, c), lambda q, g, i, core_ref: (g, (2 * q + core_ref[0]) * nr + i, 0))
    blk = pl.BlockSpec((1, 1, tr, c), lambda q, g, i, core_ref: (q, g, i, 0))
    grid_spec = pltpu.PrefetchScalarGridSpec(num_scalar_prefetch=1, grid=(N_CHIP, G, nr), in_specs=[g_spec, blk], out_specs=blk)
    return pl.pallas_call(body, grid_spec=grid_spec, out_shape=SDS(buf.shape, bf16), name=name,
                          compiler_params=_params(("parallel", "parallel", "parallel")))(core, grad, buf)


def _chip_exchange(chips, slots, layer, name):
    nt = len(chips)

    def body(*refs):
        src_refs, out_refs = refs[:nt], refs[2 * nt:3 * nt]
        send_sems, recv_sems, local_sems = refs[3 * nt:]
        x, y, c = _coords()
        me = 2 * x + y
        mine = [pltpu.make_async_copy(src_refs[t].at[me], out_refs[t].at[me, layer], local_sems.at[t]) for t in range(nt)]
        for cp in mine:
            cp.start()
        copies = []
        for k in range(1, N_CHIP):
            px, py = lax.rem(x + (k >> 1), 2), lax.rem(y + (k & 1), 2)
            for t in range(nt):
                cp = pltpu.make_async_remote_copy(
                    src_ref=src_refs[t].at[2 * px + py], dst_ref=out_refs[t].at[me, layer],
                    send_sem=send_sems.at[3 * t + k - 1], recv_sem=recv_sems.at[3 * t + k - 1],
                    device_id=(px, py, c), device_id_type=MESH)
                cp.start()
                copies.append(cp)
        for cp in copies:
            cp.wait_recv()
        for cp in copies:
            cp.wait_send()
        for cp in mine:
            cp.wait()

    any_spec = pl.BlockSpec(memory_space=pl.ANY)
    return pl.pallas_call(
        body, out_shape=[SDS(s.shape, s.dtype) for s in slots], in_specs=[any_spec] * (2 * nt), out_specs=[any_spec] * nt,
        input_output_aliases={nt + t: t for t in range(nt)},
        scratch_shapes=[pltpu.SemaphoreType.DMA((3 * nt,)), pltpu.SemaphoreType.DMA((3 * nt,)), pltpu.SemaphoreType.DMA((nt,))],
        name=name)(*chips, *slots)


_BIG = (("w_in", 2), ("w_uq", 2), ("w_ukv", 2), ("w_mem_kv", 1), ("w_br", 2), ("w_out", 1), ("w_ff1", 2), ("w_ff2", 1))


def _in_col_maps(D):
    d_in = O_GATE + 3 * D
    sh = d_in // N_DEV
    shp = _round_up(sh, LANES)
    perm_of_orig = np.empty(d_in, np.int64)
    for a, b, dst in ((0, O_F, C_Q), (O_F, O_CQ, C_F), (O_CQ, O_CKV, C_CQ), (O_CKV, O_KR, C_CKV), (O_KR, O_QM, C_KPE + NOPE),
                      (O_QM, O_GATE, C_QM), (O_GATE, d_in, C_GATE)):
        perm_of_orig[a:b] = dst + np.arange(b - a)
    orig = np.arange(d_in)
    gath_of_orig = (orig // sh) * shp + orig % sh
    fwd = -np.ones(C_GATE + 3 * D, np.int64)
    fwd[perm_of_orig] = gath_of_orig
    bwd = -np.ones(N_DEV * shp, np.int64)
    bwd[gath_of_orig] = perm_of_orig
    return fwd, bwd


def _lane_pad(a):
    c = a.shape[-1]
    return jnp.pad(a, [(0, 0)] * (a.ndim - 1) + [(0, _round_up(c, LANES) - c)])


def _as3(a):
    return a.reshape((-1,) + a.shape[-2:]) if a.ndim != 3 else a


def _rope_tables(positions):
    inv_freq = ROPE_BASE ** (-jnp.arange(0, ROPE, 2, dtype=f32) / ROPE)
    ang = positions.astype(f32).reshape(-1, 1) * inv_freq
    cos, sin = jnp.cos(ang), jnp.sin(ang)
    T = ang.shape[0]
    one, zero = jnp.ones((T, NOPE), f32), jnp.zeros((T, 16), f32)
    c = jnp.concatenate([one, cos, cos, jnp.ones((T, 32), f32)], axis=1)
    sa = jnp.concatenate([jnp.zeros((T, NOPE), f32), -sin, zero, jnp.zeros((T, 32), f32)], axis=1)
    sb = jnp.concatenate([jnp.zeros((T, NOPE), f32), zero, sin, jnp.zeros((T, 32), f32)], axis=1)
    return c, sa, sb


def kernel(x, mem, positions, ln_in_g, ln_in_b, w_in, b_forget, w_uq, g_cq, w_ukv, g_ckv, w_mem_kv, w_br, w_out, ln1_g, ln1_b, w_ff1, w_ff2, ln2_g, ln2_b, loss_target, m_ln_in_g, m_ln_in_b, m_w_in, m_b_forget, m_w_uq, m_g_cq, m_w_ukv, m_g_ckv, m_w_mem_kv, m_w_br, m_w_out, m_ln1_g, m_ln1_b, m_w_ff1, m_w_ff2, m_ln2_g, m_ln2_b, v_ln_in_g, v_ln_in_b, v_w_in, v_b_forget, v_w_uq, v_g_cq, v_w_ukv, v_g_ckv, v_w_mem_kv, v_w_br, v_w_out, v_ln1_g, v_ln1_b, v_w_ff1, v_w_ff2, v_ln2_g, v_ln2_b):
    B, S, D = x.shape
    T = B * S
    L = w_in.shape[0]
    NM = mem.shape[1]
    alpha = (2 * L) ** 0.25
    weights = dict(ln_in_g=ln_in_g, ln_in_b=ln_in_b, w_in=w_in, b_forget=b_forget, w_uq=w_uq, g_cq=g_cq, w_ukv=w_ukv, g_ckv=g_ckv,
                   w_mem_kv=w_mem_kv, w_br=w_br, w_out=w_out, ln1_g=ln1_g, ln1_b=ln1_b, w_ff1=w_ff1, w_ff2=w_ff2, ln2_g=ln2_g, ln2_b=ln2_b)
    mom_m = dict(ln_in_g=m_ln_in_g, ln_in_b=m_ln_in_b, w_in=m_w_in, b_forget=m_b_forget, w_uq=m_w_uq, g_cq=m_g_cq, w_ukv=m_w_ukv,
                 g_ckv=m_g_ckv, w_mem_kv=m_w_mem_kv, w_br=m_w_br, w_out=m_w_out, ln1_g=m_ln1_g, ln1_b=m_ln1_b, w_ff1=m_w_ff1,
                 w_ff2=m_w_ff2, ln2_g=m_ln2_g, ln2_b=m_ln2_b)
    mom_v = dict(ln_in_g=v_ln_in_g, ln_in_b=v_ln_in_b, w_in=v_w_in, b_forget=v_b_forget, w_uq=v_w_uq, g_cq=v_g_cq, w_ukv=v_w_ukv,
                 g_ckv=v_g_ckv, w_mem_kv=v_w_mem_kv, w_br=v_w_br, w_out=v_w_out, ln1_g=v_ln1_g, ln1_b=v_ln1_b, w_ff1=v_w_ff1,
                 w_ff2=v_w_ff2, ln2_g=v_ln2_g, ln2_b=v_ln2_b)
    order = list(weights)
    big_axes = [ax for _, ax in _BIG]
    fwd_map, bwd_map = _in_col_maps(D)
    shard3 = {n: _lane_pad(weights[n]) for n, _ in _BIG}

    def gather_layer(l):
        full = _all_gather([_as3(shard3[n][l]).astype(bf16) for n, _ in _BIG], big_axes, "weights_all_gather")
        W = {n: (f if n == "w_br" else f[0]) for (n, _), f in zip(_BIG, full)}
        W["w_in"] = _lane_permute(W["w_in"], fwd_map, bf16, "w_in_reorder")
        return W

    tabs = _rope_tables(positions)
    x2d, mem2d, tgt2d = x.reshape(T, D), mem.reshape(B * NM, D), loss_target.reshape(T, D)
    fox = dict(B=B, Sq=S, Sk=S, nblk=FOX_HEADS // 2, hb=2, dqk=FOX_HD, dv=FOX_HD, mode="fox", scale=FOX_HD ** -0.5)
    mla = dict(B=B, Sq=S, Sk=S, nblk=MLA_HEADS // 2, hb=2, dqk=LANES, dv=MLA_VD, mode="chunk", scale=(NOPE + ROPE) ** -0.5)
    memat = dict(B=B, Sq=S, Sk=NM, nblk=MEM_HEADS, hb=1, dqk=MEM_HD, dv=MEM_HD, mode="none", scale=MEM_HD ** -0.5)
    nb = FOX_HEADS // 2

    h, hb, xh0, rstd0 = _ln_fwd(None, x2d, ln_in_g, ln_in_b, alpha, "ln_in")
    saved = []
    for l in range(L):
        W = gather_layer(l)
        bpad = jnp.pad(b_forget[l].reshape(1, FOX_HEADS), ((0, 0), (0, LANES - FOX_HEADS)))
        proj = _mm(hb, W["w_in"], name="proj")
        cum = _gate_fwd(proj, bpad, B, S, "gate_fwd")
        cum_r = cum[:, :FOX_HEADS].reshape(B, S, nb, 2).transpose(0, 2, 3, 1).reshape(B * nb, 2, S)
        o_a, lse_a = _attn_fwd(proj, C_Q // LANES, proj, C_K // LANES, proj, C_V // LANES, cum_r=cum_r, name="fox_fwd", **fox)
        cqn = _rms_fwd(proj, C_CQ, Q_RANK, g_cq[l], "rms_q")
        ckvn = _rms_fwd(proj, C_CKV, KV_RANK, g_ckv[l], "rms_kv")
        qraw = _mm(cqn, W["w_uq"], name="q_up")
        kv = _mm(ckvn, W["w_ukv"], name="kv_up")
        qf = _rope_q(qraw, tabs, bwd=False, name="rope_q")
        kf, v_b = _rope_k(kv, tabs, proj, "rope_k")
        o_b, lse_b = _attn_fwd(qf, 0, kf, 0, v_b, 0, name="mla_fwd", **mla)
        mkv = _mm(mem2d, W["w_mem_kv"], name="mem_kv")
        o_c, lse_c = _attn_fwd(proj, C_QM // MEM_HD, mkv, 0, mkv, MEM_HEADS, name="mem_fwd", **memat)
        merged = _merge_fwd(o_a, o_b, o_c, W["w_br"], proj, D, "merge_fwd")
        y = _mm(merged, W["w_out"], name="out_proj")
        h1, h1b, xh1, rstd1 = _ln_fwd(h, y, ln1_g[l], ln1_b[l], alpha, "ln1")
        a, r = _mm(h1b, W["w_ff1"], epi="relu2", name="ff1")
        ff = _mm(a, W["w_ff2"], name="ff2")
        h2, h2b, xh2, rstd2 = _ln_fwd(h1, ff, ln2_g[l], ln2_b[l], alpha, "ln2")
        saved.append(dict(W=W, bpad=bpad, h=hb, proj=proj, cum_r=cum_r, o_a=o_a, lse_a=lse_a, cqn=cqn, ckvn=ckvn, qf=qf, kf=kf,
                          v_b=v_b, o_b=o_b, lse_b=lse_b, mkv=mkv, o_c=o_c, lse_c=lse_c, merged=merged, h1=h1b, xh1=xh1,
                          rstd1=rstd1, a=a, r=r, xh2=xh2, rstd2=rstd2))
        h, hb = h2, h2b

    loss_local, dh = _loss(h, tgt2d, "loss")
    loss = lax.psum(loss_local[0, 0], ("x", "y", "c"))

    shard_shapes = [_as3(shard3[n][0]).shape for n, _ in _BIG]
    slots = [lax.empty((N_CHIP, L) + s, bf16) for s in shard_shapes]
    core = lax.axis_index("c").astype(jnp.int32).reshape(1)
    small = {n: [None] * L for n in ("b_forget", "g_cq", "g_ckv", "ln1_g", "ln1_b", "ln2_g", "ln2_b")}
    for l in reversed(range(L)):
        sv = saved[l]
        W = sv["W"]
        du2, dg2, db2 = _ln_bwd(dh, sv["xh2"], sv["rstd2"], ln2_g[l], "ln2_bwd")
        dz = _mm(du2, W["w_ff2"], mode="nt", epi="mul", extra=sv["r"], out_dtype=bf16, name="ff2_dx")
        dw_ff2 = _mm(sv["a"], du2, mode="tn", name="ff2_dw")
        dh1 = _mm(dz, W["w_ff1"], mode="nt", epi="add", extra=du2, alpha=alpha, name="ff1_dx")
        dw_ff1 = _mm(sv["h1"], dz, mode="tn", name="ff1_dw")
        du1, dg1, db1 = _ln_bwd(dh1, sv["xh1"], sv["rstd1"], ln1_g[l], "ln1_bwd")
        dmerged = _mm(du1, W["w_out"], mode="nt", name="out_dx")
        dw_out = _mm(sv["merged"], du1, mode="tn", name="out_dw")
        mb = _merge_bwd(dmerged, sv["o_a"], sv["o_b"], sv["o_c"], W["w_br"], sv["proj"], D, "merge_bwd")
        dgl, do, dbp = mb[0:3], mb[3:6], mb[6:9]
        dw_br = jnp.stack([_mm(o_n, dbp_n, mode="tn", name="br_dw") for o_n, dbp_n in zip((sv["o_a"], sv["o_b"], sv["o_c"]), dbp)])
        proj = sv["proj"]
        dq_a, dk_a, dv_a, dcum = _attn_bwd(proj, C_Q // LANES, proj, C_K // LANES, proj, C_V // LANES, sv["o_a"], do[0], sv["lse_a"],
                                           cum_r=sv["cum_r"], name="fox_bwd", **fox)
        dcum = dcum.reshape(B, nb, S, 2).transpose(0, 2, 1, 3).reshape(T, FOX_HEADS)
        dcum = jnp.pad(dcum, ((0, 0), (0, LANES - FOX_HEADS)))
        dzf, dbf = _gate_bwd(dcum, proj, sv["bpad"], B, S, "gate_bwd")
        dqf, dkf, dv_b = _attn_bwd(sv["qf"], 0, sv["kf"], 0, sv["v_b"], 0, sv["o_b"], do[1], sv["lse_b"], dq_dtype=f32,
                                   name="mla_bwd", **mla)
        dqraw = _rope_q(dqf, tabs, bwd=True, name="rope_q_bwd")
        dkv, dkpe = _rope_k_bwd(dkf, dv_b, tabs, "rope_k_bwd")
        dw_uq = _mm(sv["cqn"], dqraw, mode="tn", name="q_up_dw")
        dcqn = _mm(dqraw, W["w_uq"], mode="nt", name="q_up_dx")
        dcq_, dg_cq = _rms_bwd(dcqn, proj, C_CQ, Q_RANK, g_cq[l], "rms_q_bwd")
        dw_ukv = _mm(sv["ckvn"], dkv, mode="tn", name="kv_up_dw")
        dckvn = _mm(dkv, W["w_ukv"], mode="nt", name="kv_up_dx")
        dckv, dg_ckv = _rms_bwd(dckvn, proj, C_CKV, KV_RANK, g_ckv[l], "rms_kv_bwd")
        dqm, dmk, dmv = _attn_bwd(proj, C_QM // MEM_HD, sv["mkv"], 0, sv["mkv"], MEM_HEADS, sv["o_c"], do[2], sv["lse_c"],
                                  name="mem_bwd", **memat)
        dw_mem = _mm(mem2d, jnp.concatenate([dmk, dmv], axis=1), mode="tn", name="mem_kv_dw")
        dproj = jnp.concatenate([dq_a, dk_a, dv_a, dzf, dkpe, dckv, dqm, jnp.zeros((T, C_CQ - C_PAD), bf16), dcq_, *dgl], axis=1)
        dw_in = _lane_permute(_mm(sv["h"], dproj, mode="tn", name="proj_dw"), bwd_map, f32, "w_in_grad_reorder")
        dh = _mm(dproj, W["w_in"], mode="nt", epi="add", extra=du1, alpha=alpha, name="proj_dx")
        grads_l = dict(w_in=dw_in, w_uq=dw_uq, w_ukv=dw_ukv, w_mem_kv=dw_mem, w_br=dw_br, w_out=dw_out, w_ff1=dw_ff1, w_ff2=dw_ff2)
        g3 = [_as3(grads_l[n]) for n, _ in _BIG]
        bufs = _pair_exchange(g3, big_axes, shard_shapes, "grads_pair_exchange")
        chips = [_pair_sum(g, b, ax, core, "grads_pair_sum") for g, b, ax in zip(g3, bufs, big_axes)]
        slots = _chip_exchange(chips, slots, l, "grads_chip_exchange")
        for n, val in (("b_forget", dbf[:, :FOX_HEADS]), ("g_cq", dg_cq), ("g_ckv", dg_ckv), ("ln1_g", dg1), ("ln1_b", db1),
                       ("ln2_g", dg2), ("ln2_b", db2)):
            small[n][l] = val.reshape(-1)
    grad_x, dg_in, db_in = _ln_bwd(dh, xh0, rstd0, ln_in_g, "ln_in_bwd")

    small_list = [("ln_in_g", dg_in.reshape(-1)), ("ln_in_b", db_in.reshape(-1))] + [(n, jnp.stack(v).reshape(-1)) for n, v in small.items()]
    sflat = jnp.concatenate([v for _, v in small_list])
    n_small = sflat.shape[0]
    rs = _round_up(-(-n_small // LANES), 8)
    spacked = jnp.pad(sflat, (0, rs * LANES - n_small)).reshape(1, rs, LANES)
    sparts = _all_gather([spacked], [1], "small_all_gather")[0].reshape(N_DEV, rs * LANES)

    grads, deltas, new_m, new_v = {}, {}, {}, {}
    off = 0
    for n, vv in small_list:
        parts = sparts[:, off:off + vv.shape[0]].reshape((N_DEV,) + weights[n].shape)
        off += vv.shape[0]
        grads[n], deltas[n], new_m[n], new_v[n] = _adamw(weights[n], mom_m[n], mom_v[n], parts, "adamw_" + n)
    for (n, _), sl in zip(_BIG, slots):
        parts = sl.reshape((N_CHIP,) + weights[n].shape[:-1] + (sl.shape[-1],))
        grads[n], deltas[n], new_m[n], new_v[n] = _adamw(weights[n], mom_m[n], mom_v[n], parts, "adamw_" + n)
    return (loss, grad_x.reshape(B, S, D), *[grads[n] for n in order], *[deltas[n] for n in order], *[new_m[n] for n in order],
            *[new_v[n] for n in order])
```

```python
import functools

import numpy as np
import jax
import jax.numpy as jnp
from jax import lax
from jax.experimental import pallas as pl
from jax.experimental.pallas import tpu as pltpu

f32, bf16 = jnp.float32, jnp.bfloat16
SDS = jax.ShapeDtypeStruct

N_DEV = 8
MESH = pl.DeviceIdType.MESH
LANES = 128
VMEM_LIMIT = 56 * 1024 * 1024

FOX_HEADS, FOX_HD = 8, 64
MLA_HEADS, NOPE, ROPE, MLA_VD = 8, 64, 32, 64
Q_RANK, KV_RANK = 384, 256
ROPE_BASE = 10000.0
MEM_HEADS, MEM_HD = 4, 128
BW = 512
CHUNK = 64
LN_EPS, RMS_EPS, NEG_INF = 1e-5, 1e-6, -1e30
ADAM_LR, ADAM_B1, ADAM_B2, ADAM_EPS, ADAM_WD, ADAM_STEP = 0.001, 0.9, 0.999, 1e-08, 0.01, 10

ATT_TQ, ATT_TK = 512, 256

C_Q, C_K, C_V, C_F, C_KPE, C_CKV, C_QM, C_PAD, C_CQ, C_GATE = 0, 512, 1024, 1536, 1664, 1792, 2048, 2560, 2688, 3072
O_F, O_CQ, O_CKV, O_KR, O_QM, O_GATE = 1536, 1544, 1928, 2184, 2216, 2728


def _pick(dim, pref):
    if dim <= pref:
        return dim
    for c in (1024, 768, 512, 384, 256, 128, 64, 32, 16, 8):
        if c <= pref and dim % c == 0:
            return c
    return dim


def _round_up(n, m):
    return -(-n // m) * m


def _params(sem=None):
    return pltpu.CompilerParams(dimension_semantics=sem, vmem_limit_bytes=VMEM_LIMIT)


_DIMS = {"nn": (((1,), (0,)), ((), ())), "nt": (((1,), (1,)), ((), ())), "tn": (((0,), (0,)), ((), ()))}


def _mm(a, b, *, mode="nn", out_dtype=f32, epi="none", extra=None, alpha=1.0, tm=1024, tn=1024, tk=None, name):
    if mode == "nn":
        (M, K), (_, N) = a.shape, b.shape
    elif mode == "nt":
        (M, K), (N, _) = a.shape, b.shape
    else:
        (K, M), (_, N) = a.shape, b.shape
    tm, tn, tk = _pick(M, tm), _pick(N, tn), _pick(K, tk or (512 if mode == "tn" else 1024))
    nk = K // tk
    a_spec = pl.BlockSpec((tk, tm), lambda i, j, k: (k, i)) if mode == "tn" else pl.BlockSpec((tm, tk), lambda i, j, k: (i, k))
    b_spec = pl.BlockSpec((tn, tk), lambda i, j, k: (j, k)) if mode == "nt" else pl.BlockSpec((tk, tn), lambda i, j, k: (k, j))
    o_spec = pl.BlockSpec((tm, tn), lambda i, j, k: (i, j))
    n_out = 2 if epi == "relu2" else 1
    dims = _DIMS[mode]

    def body(*refs):
        a_ref, b_ref = refs[0], refs[1]
        e_ref = refs[2] if extra is not None else None
        outs = refs[-1 - n_out:-1]
        acc = refs[-1]
        k = pl.program_id(2)

        @pl.when(k == 0)
        def _():
            acc[...] = jnp.zeros_like(acc)

        acc[...] += lax.dot_general(a_ref[...].astype(bf16), b_ref[...].astype(bf16), dims, preferred_element_type=f32)

        @pl.when(k == nk - 1)
        def _():
            r = acc[...]
            if epi == "none":
                outs[0][...] = r.astype(out_dtype)
            elif epi == "relu2":
                p = jnp.maximum(r, 0.0)
                outs[0][...] = (p * p).astype(bf16)
                outs[1][...] = (2.0 * p).astype(bf16)
            elif epi == "mul":
                outs[0][...] = (r * e_ref[...].astype(f32)).astype(out_dtype)
            else:
                outs[0][...] = (r + alpha * e_ref[...].astype(f32)).astype(out_dtype)

    ins, in_specs = [a, b], [a_spec, b_spec]
    if extra is not None:
        ins.append(extra)
        in_specs.append(o_spec)
    if epi == "relu2":
        out_shape, out_specs = [SDS((M, N), bf16), SDS((M, N), bf16)], [o_spec, o_spec]
    else:
        out_shape, out_specs = SDS((M, N), out_dtype), o_spec
    return pl.pallas_call(
        body, grid=(M // tm, N // tn, nk), in_specs=in_specs, out_specs=out_specs, out_shape=out_shape,
        scratch_shapes=[pltpu.VMEM((tm, tn), f32)], name=name,
        compiler_params=_params(("parallel", "parallel", "arbitrary")))(*ins)


def _lane_permute(src, cmap, out_dtype, name):
    R, Cs = src.shape
    cmap = np.asarray(cmap, np.int64)
    nb = cmap.shape[0] // LANES
    srcs = [sorted({int(c) // LANES for c in cmap[b * LANES:(b + 1) * LANES] if c >= 0}) for b in range(nb)]
    exact = src.dtype == f32
    tr = _pick(R, 256 if exact else 512)

    def body(src_ref, cm_ref, o_ref):
        rows = lax.broadcasted_iota(jnp.int32, (LANES, LANES), 0)
        for b in range(nb):
            tbl = cm_ref[:, b * LANES:(b + 1) * LANES]
            acc = jnp.zeros((tr, LANES), f32)
            for sb in srcs[b]:
                sel = (rows + sb * LANES) == tbl
                blk = src_ref[:, sb * LANES:(sb + 1) * LANES]
                if exact:
                    acc = acc + jnp.dot(blk, sel.astype(f32), precision=lax.Precision.HIGHEST, preferred_element_type=f32)
                else:
                    acc = acc + jnp.dot(blk, sel.astype(blk.dtype), preferred_element_type=f32)
            o_ref[:, b * LANES:(b + 1) * LANES] = acc.astype(out_dtype)

    return pl.pallas_call(
        body, grid=(R // tr,), in_specs=[pl.BlockSpec((tr, Cs), lambda i: (i, 0)), pl.BlockSpec((1, nb * LANES), lambda i: (0, 0))],
        out_specs=pl.BlockSpec((tr, nb * LANES), lambda i: (i, 0)), out_shape=SDS((R, nb * LANES), out_dtype), name=name,
        compiler_params=_params(("parallel",)))(src, jnp.asarray(cmap.astype(np.int32)).reshape(1, -1))


def _ln_fwd(h, y, g, b, alpha, name):
    T, D = y.shape
    tr = _pick(T, 256)
    row = pl.BlockSpec((tr, D), lambda i: (i, 0))
    vec = pl.BlockSpec((1, D), lambda i: (0, 0))
    col = pl.BlockSpec((tr, 1), lambda i: (i, 0))

    def body(*refs):
        if h is None:
            y_ref, g_ref, b_ref, o_ref, ob_ref, xh_ref, rs_ref = refs
            u = y_ref[...]
        else:
            h_ref, y_ref, g_ref, b_ref, o_ref, ob_ref, xh_ref, rs_ref = refs
            u = alpha * h_ref[...] + y_ref[...]
        mu = jnp.mean(u, axis=-1, keepdims=True)
        d = u - mu
        var = jnp.mean(d * d, axis=-1, keepdims=True)
        rstd = lax.rsqrt(var + LN_EPS)
        xh = d * rstd
        xh_ref[...] = xh
        rs_ref[...] = rstd
        o = xh * g_ref[...] + b_ref[...]
        o_ref[...] = o
        ob_ref[...] = o.astype(bf16)

    ins = ([] if h is None else [h]) + [y, g.reshape(1, D), b.reshape(1, D)]
    specs = ([] if h is None else [row]) + [row, vec, vec]
    return pl.pallas_call(
        body, grid=(T // tr,), in_specs=specs, out_specs=[row, row, row, col],
        out_shape=[SDS((T, D), f32), SDS((T, D), bf16), SDS((T, D), f32), SDS((T, 1), f32)], name=name,
        compiler_params=_params(("parallel",)))(*ins)


def _ln_bwd(dy, xh, rstd, g, name):
    T, D = dy.shape
    tr = _pick(T, 256)
    row = pl.BlockSpec((tr, D), lambda i: (i, 0))
    vec = pl.BlockSpec((1, D), lambda i: (0, 0))
    col = pl.BlockSpec((tr, 1), lambda i: (i, 0))

    def body(dy_ref, xh_ref, rs_ref, g_ref, du_ref, dg_ref, db_ref):
        @pl.when(pl.program_id(0) == 0)
        def _():
            dg_ref[...] = jnp.zeros_like(dg_ref)
            db_ref[...] = jnp.zeros_like(db_ref)

        dyv, xhv = dy_ref[...], xh_ref[...]
        dxh = dyv * g_ref[...]
        m1 = jnp.mean(dxh, axis=-1, keepdims=True)
        m2 = jnp.mean(dxh * xhv, axis=-1, keepdims=True)
        du_ref[...] = rs_ref[...] * (dxh - m1 - xhv * m2)
        dg_ref[...] += jnp.sum(dyv * xhv, axis=0, keepdims=True)
        db_ref[...] += jnp.sum(dyv, axis=0, keepdims=True)

    return pl.pallas_call(
        body, grid=(T // tr,), in_specs=[row, row, col, vec], out_specs=[row, vec, vec],
        out_shape=[SDS((T, D), f32), SDS((1, D), f32), SDS((1, D), f32)], name=name,
        compiler_params=_params(("arbitrary",)))(dy, xh, rstd, g.reshape(1, D))


def _rms_fwd(proj, col_off, R, g, name):
    T = proj.shape[0]
    tr = _pick(T, 512)
    cb = col_off // R

    def body(c_ref, g_ref, o_ref):
        c = c_ref[...]
        r = lax.rsqrt(jnp.mean(c * c, axis=-1, keepdims=True) + RMS_EPS)
        o_ref[...] = (c * r * g_ref[...]).astype(bf16)

    return pl.pallas_call(
        body, grid=(T // tr,), in_specs=[pl.BlockSpec((tr, R), lambda i: (i, cb)), pl.BlockSpec((1, R), lambda i: (0, 0))],
        out_specs=pl.BlockSpec((tr, R), lambda i: (i, 0)), out_shape=SDS((T, R), bf16), name=name,
        compiler_params=_params(("parallel",)))(proj, g.reshape(1, R))


def _rms_bwd(dy, proj, col_off, R, g, name):
    T = proj.shape[0]
    tr = _pick(T, 512)
    cb = col_off // R

    def body(dy_ref, c_ref, g_ref, dc_ref, dg_ref):
        @pl.when(pl.program_id(0) == 0)
        def _():
            dg_ref[...] = jnp.zeros_like(dg_ref)

        c, dyv = c_ref[...], dy_ref[...]
        r = lax.rsqrt(jnp.mean(c * c, axis=-1, keepdims=True) + RMS_EPS)
        t = dyv * g_ref[...]
        mt = jnp.mean(t * c, axis=-1, keepdims=True)
        dc_ref[...] = (r * t - c * (r * r * r) * mt).astype(bf16)
        dg_ref[...] += jnp.sum(dyv * c * r, axis=0, keepdims=True)

    return pl.pallas_call(
        body, grid=(T // tr,),
        in_specs=[pl.BlockSpec((tr, R), lambda i: (i, 0)), pl.BlockSpec((tr, R), lambda i: (i, cb)), pl.BlockSpec((1, R), lambda i: (0, 0))],
        out_specs=[pl.BlockSpec((tr, R), lambda i: (i, 0)), pl.BlockSpec((1, R), lambda i: (0, 0))],
        out_shape=[SDS((T, R), bf16), SDS((1, R), f32)], name=name,
        compiler_params=_params(("arbitrary",)))(dy, proj, g.reshape(1, R))


def _tri(n, lower):
    r = lax.broadcasted_iota(jnp.int32, (n, n), 0)
    c = lax.broadcasted_iota(jnp.int32, (n, n), 1)
    return ((r >= c) if lower else (c >= r)).astype(f32)


def _gate_fwd(proj, bpad, B, S, name):
    ch = _pick(S, 256)
    nch = S // ch
    cb = C_F // LANES

    def body(f_ref, b_ref, o_ref):
        tri = _tri(ch, True)
        carry = jnp.zeros((1, LANES), f32)
        for c in range(nch):
            z = f_ref[c * ch:(c + 1) * ch, :] + b_ref[...]
            lf = jnp.minimum(z, 0.0) - jnp.log1p(jnp.exp(-jnp.abs(z)))
            cs = jnp.dot(tri, lf, precision=lax.Precision.HIGHEST, preferred_element_type=f32) + carry
            o_ref[c * ch:(c + 1) * ch, :] = cs
            carry = cs[ch - 1:ch, :]

    return pl.pallas_call(
        body, grid=(B,), in_specs=[pl.BlockSpec((S, LANES), lambda b: (b, cb)), pl.BlockSpec((1, LANES), lambda b: (0, 0))],
        out_specs=pl.BlockSpec((S, LANES), lambda b: (b, 0)), out_shape=SDS((B * S, LANES), f32), name=name,
        compiler_params=_params(("parallel",)))(proj, bpad)


def _gate_bwd(dcum, proj, bpad, B, S, name):
    ch = _pick(S, 256)
    nch = S // ch
    cb = C_F // LANES

    def body(d_ref, f_ref, b_ref, dz_ref, db_ref):
        @pl.when(pl.program_id(0) == 0)
        def _():
            db_ref[...] = jnp.zeros_like(db_ref)

        tri = _tri(ch, False)
        carry = jnp.zeros((1, LANES), f32)
        dbs = jnp.zeros((1, LANES), f32)
        for c in reversed(range(nch)):
            d = d_ref[c * ch:(c + 1) * ch, :]
            dlf = jnp.dot(tri, d, precision=lax.Precision.HIGHEST, preferred_element_type=f32) + carry
            carry = carry + jnp.sum(d, axis=0, keepdims=True)
            z = f_ref[c * ch:(c + 1) * ch, :] + b_ref[...]
            dz = dlf / (1.0 + jnp.exp(z))
            dz_ref[c * ch:(c + 1) * ch, :] = dz.astype(bf16)
            dbs = dbs + jnp.sum(dz, axis=0, keepdims=True)
        db_ref[...] += dbs

    return pl.pallas_call(
        body, grid=(B,),
        in_specs=[pl.BlockSpec((S, LANES), lambda b: (b, 0)), pl.BlockSpec((S, LANES), lambda b: (b, cb)), pl.BlockSpec((1, LANES), lambda b: (0, 0))],
        out_specs=[pl.BlockSpec((S, LANES), lambda b: (b, 0)), pl.BlockSpec((1, LANES), lambda b: (0, 0))],
        out_shape=[SDS((B * S, LANES), bf16), SDS((1, LANES), f32)], name=name,
        compiler_params=_params(("arbitrary",)))(dcum, proj, bpad)


def _rot(v, c, sa, sb):
    return v * c + pltpu.roll(v, LANES - 16, 1) * sa + pltpu.roll(v, 16, 1) * sb


def _rot_t(v, c, sa, sb):
    return v * c + pltpu.roll(v * sa, 16, 1) + pltpu.roll(v * sb, LANES - 16, 1)


def _rope_q(x, tabs, *, bwd, name):
    T = x.shape[0]
    W = MLA_HEADS * LANES
    tr = _pick(T, 512)
    fn = _rot_t if bwd else _rot

    def body(x_ref, c_ref, sa_ref, sb_ref, o_ref):
        c, sa, sb = c_ref[...], sa_ref[...], sb_ref[...]
        for g in range(MLA_HEADS):
            o_ref[:, g * LANES:(g + 1) * LANES] = fn(x_ref[:, g * LANES:(g + 1) * LANES], c, sa, sb).astype(bf16)

    tab = pl.BlockSpec((tr, LANES), lambda i: (i, 0))
    wide = pl.BlockSpec((tr, W), lambda i: (i, 0))
    return pl.pallas_call(body, grid=(T // tr,), in_specs=[wide, tab, tab, tab], out_specs=wide, out_shape=SDS((T, W), bf16),
                          name=name, compiler_params=_params(("parallel",)))(x, *tabs)


def _rope_k(kv, tabs, proj, name):
    T = kv.shape[0]
    W = MLA_HEADS * LANES
    tr = _pick(T, 512)
    kcb = C_KPE // LANES

    def body(x_ref, c_ref, sa_ref, sb_ref, kpe_ref, k_ref, v_ref):
        c, sa, sb, kpe = c_ref[...], sa_ref[...], sb_ref[...], kpe_ref[...]
        low = lax.broadcasted_iota(jnp.int32, (tr, LANES), 1) < NOPE
        halves = []
        for g in range(MLA_HEADS):
            x = x_ref[:, g * LANES:(g + 1) * LANES]
            k_ref[:, g * LANES:(g + 1) * LANES] = _rot(jnp.where(low, x, 0.0) + kpe, c, sa, sb).astype(bf16)
            halves.append(x[:, NOPE:])
            if g % 2 == 1:
                v_ref[:, (g // 2) * LANES:(g // 2 + 1) * LANES] = jnp.concatenate(halves, axis=1).astype(bf16)
                halves = []

    tab = pl.BlockSpec((tr, LANES), lambda i: (i, 0))
    wide = pl.BlockSpec((tr, W), lambda i: (i, 0))
    return pl.pallas_call(
        body, grid=(T // tr,), in_specs=[wide, tab, tab, tab, pl.BlockSpec((tr, LANES), lambda i: (i, kcb))],
        out_specs=[wide, pl.BlockSpec((tr, W // 2), lambda i: (i, 0))], out_shape=[SDS((T, W), bf16), SDS((T, W // 2), bf16)],
        name=name, compiler_params=_params(("parallel",)))(kv, *tabs, proj)


def _rope_k_bwd(dk, dv, tabs, name):
    T = dk.shape[0]
    W = MLA_HEADS * LANES
    tr = _pick(T, 512)

    def body(dk_ref, dv_ref, c_ref, sa_ref, sb_ref, o_ref, s_ref):
        c, sa, sb = c_ref[...], sa_ref[...], sb_ref[...]
        tot = jnp.zeros((tr, LANES), f32)
        for g in range(MLA_HEADS):
            d = _rot_t(dk_ref[:, g * LANES:(g + 1) * LANES], c, sa, sb)
            tot = tot + d
            o_ref[:, g * LANES:(g + 1) * LANES] = jnp.concatenate(
                [d[:, :NOPE].astype(bf16), dv_ref[:, g * MLA_VD:(g + 1) * MLA_VD]], axis=1)
        s_ref[...] = tot.astype(bf16)

    tab = pl.BlockSpec((tr, LANES), lambda i: (i, 0))
    wide = pl.BlockSpec((tr, W), lambda i: (i, 0))
    return pl.pallas_call(
        body, grid=(T // tr,), in_specs=[wide, pl.BlockSpec((tr, W // 2), lambda i: (i, 0)), tab, tab, tab],
        out_specs=[wide, tab], out_shape=[SDS((T, W), bf16), SDS((T, LANES), bf16)], name=name,
        compiler_params=_params(("parallel",)))(dk, dv, *tabs)


def _tile_mask(mode, i, j, tq, tk):
    r = i * tq + lax.broadcasted_iota(jnp.int32, (tq, tk), 0)
    c = j * tk + lax.broadcasted_iota(jnp.int32, (tq, tk), 1)
    if mode == "chunk":
        return (c // CHUNK) <= (r // CHUNK)
    return c <= r


def _attn_tiles(Sq, Sk):
    tk = _pick(Sk, ATT_TK)
    tq = _pick(Sq, ATT_TQ)
    assert (tk % tq == 0 or tq % tk == 0) and tq % CHUNK == 0 and tk % CHUNK == 0
    return tq, tk


def _stage_heads(dst, src_ref, hb, w, mark=None, val=1.0):
    S = src_ref.shape[0]
    ch = _pick(S, 256)
    assert w <= LANES and (mark is None or w <= mark < LANES)

    def step(c, _):
        rows = pl.ds(pl.multiple_of(c * ch, ch), ch)
        for hh in range(hb):
            x = src_ref[rows, hh * w:(hh + 1) * w].astype(bf16)
            if w < LANES:
                pad = jnp.zeros((ch, LANES - w), bf16)
                if mark is not None:
                    lane = lax.broadcasted_iota(jnp.int32, (ch, LANES - w), 1)
                    pad = jnp.where(lane == mark - w, val, 0.0).astype(bf16)
                x = jnp.concatenate([x, pad], axis=1)
            dst[hh, rows, :] = x
        return 0

    lax.fori_loop(0, S // ch, step, 0)


def _carry(rider, refs, n_in, n_out, n_scr, grid):
    if rider is None:
        return refs
    ri, ro = len(rider.inputs), len(rider.out_shapes)
    own = refs[:n_in] + refs[n_in + ri:n_in + ri + n_out] + refs[n_in + ri + n_out + ro:n_in + ri + n_out + ro + n_scr]
    rider.refs = (refs[n_in:n_in + ri], refs[n_in + ri + n_out:n_in + ri + n_out + ro], refs[n_in + ri + n_out + ro + n_scr:])
    first = functools.reduce(jnp.logical_and, [pl.program_id(a) == 0 for a in range(len(grid))])

    @pl.when(first)
    def _():
        rider.start(*rider.refs)

    return own


def _carry_finish(rider, grid):
    if rider is None:
        return
    last = functools.reduce(jnp.logical_and, [pl.program_id(a) == n - 1 for a, n in enumerate(grid)])

    @pl.when(last)
    def _():
        rider.finish(*rider.refs)


def _carrier_call(body, rider, ins, in_specs, out_shape, out_specs, scratch, grid, name):
    if rider is None:
        return pl.pallas_call(body, grid=grid, in_specs=in_specs, out_specs=out_specs, out_shape=out_shape, scratch_shapes=scratch,
                              name=name, compiler_params=_params(("parallel",) * len(grid)))(*ins)
    any_spec = pl.BlockSpec(memory_space=pl.ANY)
    ri, ro = len(rider.inputs), len(rider.out_shapes)
    res = pl.pallas_call(
        body, grid=grid, in_specs=list(in_specs) + [any_spec] * ri, out_specs=list(out_specs) + [any_spec] * ro,
        out_shape=list(out_shape) + rider.out_shapes, scratch_shapes=list(scratch) + rider.scratch,
        input_output_aliases={len(ins) + i: len(out_shape) + o for i, o in rider.aliases.items()}, name=name,
        compiler_params=_params(("arbitrary",) * len(grid)))(*ins, *rider.inputs)
    return res[:len(out_shape)], res[len(out_shape):]


def _attn_fwd(q_arr, q_cb, k_arr, k_cb, v_arr, v_cb, *, B, Sq, Sk, nblk, hb, dqk, dv, mode, scale, cum_r=None, rider=None, name):
    wq, wv = hb * dqk, hb * dv
    tq, tk = _attn_tiles(Sq, Sk)
    fox = mode == "fox"
    n_in = 4 if fox else 3
    ones = dv < LANES

    def body(*refs):
        refs = _carry(rider, refs, n_in, 2, 3, (B, nblk))
        q_ref, k_ref, v_ref = refs[:3]
        cr_ref = refs[3] if fox else None
        o_ref, lse_ref, qh, kh, vh = refs[n_in:]
        _stage_heads(qh, q_ref, hb, dqk)
        _stage_heads(kh, k_ref, hb, dqk)
        _stage_heads(vh, v_ref, hb, dv, dv if ones else None)

        def qstep(i, _):
            rows = pl.ds(pl.multiple_of(i * tq, tq), tq)
            qs = [qh[hh, rows, :] for hh in range(hb)]

            def tile(j, carry, diag):
                cols = pl.ds(pl.multiple_of(j * tk, tk), tk)
                out = []
                for hh in range(hb):
                    m, l, acc = carry[hh]
                    s = lax.dot_general(qs[hh], kh[hh, cols, :], _DIMS["nt"], preferred_element_type=f32) * scale
                    if fox:
                        s = s - cr_ref[0, hh:hh + 1, cols]
                    if diag:
                        s = jnp.where(_tile_mask(mode, i, j, tq, tk), s, NEG_INF)
                    m_new = jnp.maximum(m, jnp.max(s, axis=-1, keepdims=True))
                    a = jnp.exp(m - m_new)
                    p = jnp.exp(s - m_new)
                    if not ones:
                        l = a * l + jnp.sum(p, axis=-1, keepdims=True)
                    acc = a * acc + lax.dot_general(p.astype(bf16), vh[hh, cols, :], _DIMS["nn"], preferred_element_type=f32)
                    out.append((m_new, l, acc))
                return tuple(out)

            carry = tuple((jnp.full((tq, 1), -jnp.inf, f32), jnp.zeros((tq, 1), f32), jnp.zeros((tq, LANES), f32)) for _ in range(hb))
            if mode == "none":
                carry = lax.fori_loop(0, Sk // tk, functools.partial(tile, diag=False), carry)
            else:
                nfull = (i * tq) // tk
                carry = lax.fori_loop(0, nfull, functools.partial(tile, diag=False), carry)
                for d in range(max(1, tq // tk)):
                    carry = tile(nfull + d, carry, True)
            for hh in range(hb):
                m, l, acc = carry[hh]
                if ones:
                    l = acc[:, dv:dv + 1]
                o_ref[rows, hh * dv:(hh + 1) * dv] = acc[:, :dv] / l
                lse_ref[0, rows, hh:hh + 1] = m + jnp.log(l)
            return 0

        lax.fori_loop(0, Sq // tq, qstep, 0)
        _carry_finish(rider, (B, nblk))

    in_specs = [pl.BlockSpec((Sq, wq), lambda b, p: (b, q_cb + p)), pl.BlockSpec((Sk, wq), lambda b, p: (b, k_cb + p)),
                pl.BlockSpec((Sk, wv), lambda b, p: (b, v_cb + p))]
    ins = [q_arr, k_arr, v_arr]
    if fox:
        in_specs.append(pl.BlockSpec((1, hb, Sk), lambda b, p: (b * nblk + p, 0, 0)))
        ins.append(cum_r)
    return _carrier_call(
        body, rider, ins, in_specs,
        [SDS((B * Sq, nblk * wv), f32), SDS((B * nblk, Sq, hb), f32)],
        [pl.BlockSpec((Sq, wv), lambda b, p: (b, p)), pl.BlockSpec((1, Sq, hb), lambda b, p: (b * nblk + p, 0, 0))],
        [pltpu.VMEM((hb, Sq, LANES), bf16), pltpu.VMEM((hb, Sk, LANES), bf16), pltpu.VMEM((hb, Sk, LANES), bf16)],
        (B, nblk), name)


def _attn_bwd(q_arr, q_cb, k_arr, k_cb, v_arr, v_cb, o, do, lse, *, B, Sq, Sk, nblk, hb, dqk, dv, mode, scale, cum_r=None,
              dq_dtype=bf16, rider=None, name):
    wq, wv = hb * dqk, hb * dv
    tq, tk = _attn_tiles(Sq, Sk)
    fox = mode == "fox"
    n_in = 7 if fox else 6
    n_out = 4 if fox else 3
    if fox:
        assert dqk + 2 <= LANES and np.log2(scale) == np.round(np.log2(scale))

    def body(*refs):
        refs = _carry(rider, refs, n_in, n_out, 7 if fox else 6, (B, nblk))
        q_ref, k_ref, v_ref, o_ref, do_ref, lse_ref = refs[:6]
        cr_ref = refs[6] if fox else None
        dq_ref, dk_ref, dv_ref = refs[n_in:n_in + 3]
        dcum_ref = refs[n_in + 3] if fox else None
        qh, kh, vh, doh, dk_acc, dv_acc = refs[n_in + n_out:n_in + n_out + 6]
        dcq_acc = refs[n_in + n_out + 6] if fox else None
        _stage_heads(qh, q_ref, hb, dqk, dqk + 1 if fox else None, 1.0 / scale)
        _stage_heads(kh, k_ref, hb, dqk, dqk if fox else None, 1.0 / scale)
        _stage_heads(vh, v_ref, hb, dv)
        _stage_heads(doh, do_ref, hb, dv)
        dk_acc[...] = jnp.zeros_like(dk_acc)
        dv_acc[...] = jnp.zeros_like(dv_acc)

        def qstep(i, _):
            rows = pl.ds(pl.multiple_of(i * tq, tq), tq)
            qs = [qh[hh, rows, :] for hh in range(hb)]
            dos = [doh[hh, rows, :] for hh in range(hb)]
            dds = [jnp.sum(do_ref[rows, hh * dv:(hh + 1) * dv] * o_ref[rows, hh * dv:(hh + 1) * dv], axis=-1, keepdims=True)
                   for hh in range(hb)]
            lses = [lse_ref[0, rows, hh:hh + 1] for hh in range(hb)]

            def tile(j, carry, diag):
                cols = pl.ds(pl.multiple_of(j * tk, tk), tk)
                out = []
                for hh in range(hb):
                    k = kh[hh, cols, :]
                    s = lax.dot_general(qs[hh], k, _DIMS["nt"], preferred_element_type=f32) * scale
                    if fox:
                        s = s - cr_ref[0, hh:hh + 1, cols]
                    if diag:
                        s = jnp.where(_tile_mask(mode, i, j, tq, tk), s, NEG_INF)
                    p = jnp.exp(s - lses[hh])
                    dv_acc[hh, cols, :] += lax.dot_general(p.astype(bf16), dos[hh], _DIMS["tn"], preferred_element_type=f32)
                    dp = lax.dot_general(dos[hh], vh[hh, cols, :], _DIMS["nt"], preferred_element_type=f32)
                    ds = p * (dp - dds[hh])
                    dsb = (ds * scale).astype(bf16)
                    out.append(carry[hh] + lax.dot_general(dsb, k, _DIMS["nn"], preferred_element_type=f32))
                    dk_acc[hh, cols, :] += lax.dot_general(dsb, qs[hh], _DIMS["tn"], preferred_element_type=f32)
                return tuple(out)

            carry = tuple(jnp.zeros((tq, LANES), f32) for _ in range(hb))
            if mode == "none":
                carry = lax.fori_loop(0, Sk // tk, functools.partial(tile, diag=False), carry)
            else:
                nfull = (i * tq) // tk
                carry = lax.fori_loop(0, nfull, functools.partial(tile, diag=False), carry)
                for d in range(max(1, tq // tk)):
                    carry = tile(nfull + d, carry, True)
            for hh in range(hb):
                dq_ref[rows, hh * dqk:(hh + 1) * dqk] = carry[hh][:, :dqk].astype(dq_dtype)
                if fox:
                    dcq_acc[rows, hh:hh + 1] = carry[hh][:, dqk:dqk + 1]
            return 0

        lax.fori_loop(0, Sq // tq, qstep, 0)
        for hh in range(hb):
            dk_ref[:, hh * dqk:(hh + 1) * dqk] = dk_acc[hh, :, :dqk].astype(dq_dtype)
            dv_ref[:, hh * dv:(hh + 1) * dv] = dv_acc[hh, :, :dv].astype(bf16)
            if fox:
                dcum_ref[0, :, hh:hh + 1] = dcq_acc[:, hh:hh + 1] - dk_acc[hh, :, dqk + 1:dqk + 2]
        _carry_finish(rider, (B, nblk))

    qs_ =pl.BlockSpec((Sq, wq), lambda b, p: (b, q_cb + p))
    ks_ = pl.BlockSpec((Sk, wq), lambda b, p: (b, k_cb + p))
    vs_ = pl.BlockSpec((Sk, wv), lambda b, p: (b, v_cb + p))
    os_ = pl.BlockSpec((Sq, wv), lambda b, p: (b, p))
    st = pl.BlockSpec((1, Sq, hb), lambda b, p: (b * nblk + p, 0, 0))
    rw = pl.BlockSpec((1, hb, Sk), lambda b, p: (b * nblk + p, 0, 0))
    in_specs = [qs_, ks_, vs_, os_, os_, st] + ([rw] if fox else [])
    ins = [q_arr, k_arr, v_arr, o, do, lse] + ([cum_r] if fox else [])
    out_specs = [pl.BlockSpec((Sq, wq), lambda b, p: (b, p)), pl.BlockSpec((Sk, wq), lambda b, p: (b, p)),
                 pl.BlockSpec((Sk, wv), lambda b, p: (b, p))] + ([st] if fox else [])
    out_shape = [SDS((B * Sq, nblk * wq), dq_dtype), SDS((B * Sk, nblk * wq), dq_dtype), SDS((B * Sk, nblk * wv), bf16)]
    scratch = [pltpu.VMEM((hb, Sq, LANES), bf16), pltpu.VMEM((hb, Sk, LANES), bf16), pltpu.VMEM((hb, Sk, LANES), bf16),
               pltpu.VMEM((hb, Sq, LANES), bf16), pltpu.VMEM((hb, Sk, LANES), f32), pltpu.VMEM((hb, Sk, LANES), f32)]
    if fox:
        assert Sq == Sk
        out_shape.append(SDS((B * nblk, Sq, hb), f32))
        scratch.append(pltpu.VMEM((Sq, hb), f32))
    return _carrier_call(body, rider, ins, in_specs, out_shape, out_specs, scratch, (B, nblk), name)


def _merge_fwd(oa, ob, oc, w_br, proj, D, name):
    T = oa.shape[0]
    tr = _pick(T, 256)
    g0 = C_GATE // D
    o_spec = pl.BlockSpec((tr, BW), lambda i: (i, 0))
    w_spec = pl.BlockSpec((3, BW, D), lambda i: (0, 0, 0))

    def body(oa_ref, ob_ref, oc_ref, w_ref, ga_ref, gb_ref, gc_ref, m_ref):
        tot = jnp.zeros((tr, D), f32)
        for n, (o_ref, g_ref) in enumerate(((oa_ref, ga_ref), (ob_ref, gb_ref), (oc_ref, gc_ref))):
            bp = lax.dot_general(o_ref[...].astype(bf16), w_ref[n], _DIMS["nn"], preferred_element_type=f32)
            tot = tot + jax.nn.sigmoid(g_ref[...]) * bp
        m_ref[...] = tot.astype(bf16)

    gspecs = [pl.BlockSpec((tr, D), lambda i, n=n: (i, g0 + n)) for n in range(3)]
    return pl.pallas_call(
        body, grid=(T // tr,), in_specs=[o_spec, o_spec, o_spec, w_spec, *gspecs],
        out_specs=pl.BlockSpec((tr, D), lambda i: (i, 0)), out_shape=SDS((T, D), bf16), name=name,
        compiler_params=_params(("parallel",)))(oa, ob, oc, w_br, proj, proj, proj)


def _merge_bwd(dm, oa, ob, oc, w_br, proj, D, name):
    T = oa.shape[0]
    tr = _pick(T, 256)
    g0 = C_GATE // D
    o_spec = pl.BlockSpec((tr, BW), lambda i: (i, 0))
    d_spec = pl.BlockSpec((tr, D), lambda i: (i, 0))
    w_spec = pl.BlockSpec((3, BW, D), lambda i: (0, 0, 0))

    def body(dm_ref, oa_ref, ob_ref, oc_ref, w_ref, ga_ref, gb_ref, gc_ref, *outs):
        dmv = dm_ref[...]
        for n, (o_ref, g_ref) in enumerate(((oa_ref, ga_ref), (ob_ref, gb_ref), (oc_ref, gc_ref))):
            bp = lax.dot_general(o_ref[...].astype(bf16), w_ref[n], _DIMS["nn"], preferred_element_type=f32)
            g = jax.nn.sigmoid(g_ref[...])
            dbp = (dmv * g).astype(bf16)
            outs[n][...] = (dmv * bp * g * (1.0 - g)).astype(bf16)
            outs[3 + n][...] = lax.dot_general(dbp, w_ref[n], _DIMS["nt"], preferred_element_type=f32)
            outs[6 + n][...] = dbp

    gspecs = [pl.BlockSpec((tr, D), lambda i, n=n: (i, g0 + n)) for n in range(3)]
    return pl.pallas_call(
        body, grid=(T // tr,), in_specs=[d_spec, o_spec, o_spec, o_spec, w_spec, *gspecs],
        out_specs=[d_spec] * 3 + [o_spec] * 3 + [d_spec] * 3,
        out_shape=[SDS((T, D), bf16)] * 3 + [SDS((T, BW), f32)] * 3 + [SDS((T, D), bf16)] * 3, name=name,
        compiler_params=_params(("parallel",)))(dm, oa, ob, oc, w_br, proj, proj, proj)


def _loss(h, target, name):
    T, D = h.shape
    tr = _pick(T, 256)
    nt = T // tr
    row = pl.BlockSpec((tr, D), lambda i: (i, 0))

    def body(h_ref, t_ref, l_ref, d_ref, acc):
        i = pl.program_id(0)

        @pl.when(i == 0)
        def _():
            acc[...] = jnp.zeros_like(acc)

        e = h_ref[...] - t_ref[...]
        d_ref[...] = e * (1.0 / D)
        acc[...] += jnp.sum(e * e, axis=0, keepdims=True)

        @pl.when(i == nt - 1)
        def _():
            l_ref[...] = jnp.sum(acc[...], axis=-1, keepdims=True) * (0.5 / D)

    return pl.pallas_call(
        body, grid=(nt,), in_specs=[row, row], out_specs=[pl.BlockSpec((1, 1), lambda i: (0, 0)), row],
        out_shape=[SDS((1, 1), f32), SDS((T, D), f32)], scratch_shapes=[pltpu.VMEM((1, D), f32)], name=name,
        compiler_params=_params(("arbitrary",)))(h, target)


def _adamw(w, m, v, parts, name):
    shape = w.shape
    C = shape[-1]
    R = int(np.prod(shape[:-1])) if len(shape) > 1 else 1
    n_parts, Cp = parts.shape[0], parts.shape[-1]
    tr = _pick(R, 256)
    spec = pl.BlockSpec((tr, C), lambda i: (i, 0))

    def body(w_ref, m_ref, v_ref, p_ref, g_ref, d_ref, mo_ref, vo_ref):
        gv = p_ref[0].astype(f32)
        for q in range(1, n_parts):
            gv = gv + p_ref[q].astype(f32)
        gv = gv[:, :C]
        mn = ADAM_B1 * m_ref[...] + (1.0 - ADAM_B1) * gv
        vn = ADAM_B2 * v_ref[...] + (1.0 - ADAM_B2) * (gv * gv)
        m_hat = mn / (1.0 - ADAM_B1 ** ADAM_STEP)
        v_hat = vn / (1.0 - ADAM_B2 ** ADAM_STEP)
        g_ref[...] = gv
        d_ref[...] = -ADAM_LR * (m_hat / (jnp.sqrt(v_hat) + ADAM_EPS) + ADAM_WD * w_ref[...])
        mo_ref[...] = mn
        vo_ref[...] = vn

    outs = pl.pallas_call(
        body, grid=(R // tr,), in_specs=[spec] * 3 + [pl.BlockSpec((n_parts, tr, Cp), lambda i: (0, i, 0))], out_specs=[spec] * 4,
        out_shape=[SDS((R, C), f32)] * 4, name=name,
        compiler_params=_params(("parallel",)))(*[a.reshape(R, C) for a in (w, m, v)], parts.reshape(n_parts, R, Cp))
    return [o.reshape(shape) for o in outs]


def _coords():
    return lax.axis_index("x"), lax.axis_index("y"), lax.axis_index("c")


def _window(ref, axis, idx, n):
    start = pl.multiple_of(idx * n, n)
    if axis == 1:
        return ref.at[:, pl.ds(start, n), :]
    return ref.at[:, :, pl.ds(start, n)]


class _Rider:
    def __init__(self, inputs, out_shapes, aliases, scratch, start, finish):
        self.inputs, self.out_shapes, self.aliases, self.scratch = list(inputs), list(out_shapes), dict(aliases), list(scratch)
        self.start, self.finish = start, finish


def _run_rider(r, name):
    ni, no = len(r.inputs), len(r.out_shapes)

    def body(*refs):
        ins, outs, scr = refs[:ni], refs[ni:ni + no], refs[ni + no:]
        r.start(ins, outs, scr)
        r.finish(ins, outs, scr)

    any_spec = pl.BlockSpec(memory_space=pl.ANY)
    return pl.pallas_call(body, out_shape=r.out_shapes, in_specs=[any_spec] * ni, out_specs=[any_spec] * no,
                          input_output_aliases=r.aliases, scratch_shapes=r.scratch, name=name)(*r.inputs)


def _gather_rider(shards, axes):
    nt = len(shards)
    sizes = [s.shape[a] for s, a in zip(shards, axes)]
    out_shapes = [SDS(tuple(d * N_DEV if i == a else d for i, d in enumerate(s.shape)), s.dtype) for s, a in zip(shards, axes)]

    def plan(x_refs, out_refs, scr):
        send_sems, recv_sems, local_sems = scr
        x, y, c = _coords()
        me, sibling = (x, y, c), (x, y, 1 - c)
        chips = [(1 - x, y), (x, 1 - y), (1 - x, 1 - y)]

        def win(t, px, py, pc):
            return _window(out_refs[t], axes[t], 4 * px + 2 * py + pc, sizes[t])

        def copy(t, k, block, to, src=None):
            return pltpu.make_async_remote_copy(
                src_ref=win(t, *block) if src is None else src, dst_ref=win(t, *block),
                send_sem=send_sems.at[7 * t + k], recv_sem=recv_sems.at[7 * t + k], device_id=to, device_id_type=MESH)

        mine = [pltpu.make_async_copy(x_refs[t], win(t, *me), local_sems.at[t]) for t in range(nt)]
        first = []
        for t in range(nt):
            first.append(copy(t, 0, me, sibling, src=x_refs[t]))
            first += [copy(t, 1 + j, me, (*chip, c), src=x_refs[t]) for j, chip in enumerate(chips)]
        return c, me, sibling, chips, copy, mine, first

    def start(x_refs, out_refs, scr):
        _, _, _, _, _, mine, first = plan(x_refs, out_refs, scr)
        for cp in mine + first:
            cp.start()

    def finish(x_refs, out_refs, scr):
        c, me, sibling, chips, copy, mine, first = plan(x_refs, out_refs, scr)
        passed = []
        for j, chip in enumerate(chips):
            for t in range(nt):
                copy(t, 1 + j, (*chip, c), me).wait_recv()
                fwd = copy(t, 4 + j, (*chip, c), sibling)
                fwd.start()
                passed.append(fwd)
        for t in range(nt):
            copy(t, 0, sibling, me).wait_recv()
            for j, chip in enumerate(chips):
                copy(t, 4 + j, (*chip, 1 - c), me).wait_recv()
        for cp in first + passed:
            cp.wait_send()
        for cp in mine:
            cp.wait()

    scratch = [pltpu.SemaphoreType.DMA((7 * nt,)), pltpu.SemaphoreType.DMA((7 * nt,)), pltpu.SemaphoreType.DMA((nt,))]
    return _Rider(shards, out_shapes, {}, scratch, start, finish)


N_CHIP = N_DEV // 2


def _pair_exchange(grads, axes, shards, name):
    nt = len(grads)
    sizes = [s[a] for s, a in zip(shards, axes)]

    def body(*refs):
        src_refs, out_refs = refs[:nt], refs[nt:2 * nt]
        send_sems, recv_sems = refs[2 * nt:]
        x, y, c = _coords()
        copies = []
        for t in range(nt):
            for q in range(N_CHIP):
                cp = pltpu.make_async_remote_copy(
                    src_ref=_window(src_refs[t], axes[t], 2 * q + (1 - c), sizes[t]), dst_ref=out_refs[t].at[q],
                    send_sem=send_sems.at[N_CHIP * t + q], recv_sem=recv_sems.at[N_CHIP * t + q],
                    device_id=(x, y, 1 - c), device_id_type=MESH)
                cp.start()
                copies.append(cp)
        for cp in copies:
            cp.wait_recv()
        for cp in copies:
            cp.wait_send()

    any_spec = pl.BlockSpec(memory_space=pl.ANY)
    return pl.pallas_call(
        body, out_shape=[SDS((N_CHIP,) + tuple(s), f32) for s in shards], in_specs=[any_spec] * nt, out_specs=[any_spec] * nt,
        scratch_shapes=[pltpu.SemaphoreType.DMA((N_CHIP * nt,)), pltpu.SemaphoreType.DMA((N_CHIP * nt,))], name=name)(*grads)


def _pair_sum(grad, buf, axis, core, name):
    _, G, r, c = buf.shape
    tr = _pick(r, 512)
    nr = r // tr

    def body(core_ref, g_ref, b_ref, o_ref):
        o_ref[0, 0] = (g_ref[0] + b_ref[0, 0]).astype(bf16)

    if axis == 2:
        g_spec = pl.BlockSpec((1, tr, c), lambda q, g, i, core_ref: (g, i, 2 * q + core_ref[0]))
    else:
        g_spec = pl.BlockSpec((1, tr, c), lambda q, g, i, core_ref: (g, (2 * q + core_ref[0]) * nr + i, 0))
    blk = pl.BlockSpec((1, 1, tr, c), lambda q, g, i, core_ref: (q, g, i, 0))
    grid_spec = pltpu.PrefetchScalarGridSpec(num_scalar_prefetch=1, grid=(N_CHIP, G, nr), in_specs=[g_spec, blk], out_specs=blk)
    return pl.pallas_call(body, grid_spec=grid_spec, out_shape=SDS(buf.shape, bf16), name=name,
                          compiler_params=_params(("parallel", "parallel", "parallel")))(core, grad, buf)


def _chip_exchange_rider(chips, slots, layer):
    nt = len(chips)

    def plan(ins, outs, scr):
        send_sems, recv_sems, local_sems = scr
        x, y, c = _coords()
        me = 2 * x + y
        mine = [pltpu.make_async_copy(ins[t].at[me], outs[t].at[me, layer], local_sems.at[t]) for t in range(nt)]
        copies = []
        for k in range(1, N_CHIP):
            px, py = lax.rem(x + (k >> 1), 2), lax.rem(y + (k & 1), 2)
            copies += [pltpu.make_async_remote_copy(
                src_ref=ins[t].at[2 * px + py], dst_ref=outs[t].at[me, layer],
                send_sem=send_sems.at[3 * t + k - 1], recv_sem=recv_sems.at[3 * t + k - 1],
                device_id=(px, py, c), device_id_type=MESH) for t in range(nt)]
        return mine, copies

    def start(ins, outs, scr):
        mine, copies = plan(ins, outs, scr)
        for cp in mine + copies:
            cp.start()

    def finish(ins, outs, scr):
        mine, copies = plan(ins, outs, scr)
        for cp in copies:
            cp.wait_recv()
        for cp in copies:
            cp.wait_send()
        for cp in mine:
            cp.wait()

    scratch = [pltpu.SemaphoreType.DMA((3 * nt,)), pltpu.SemaphoreType.DMA((3 * nt,)), pltpu.SemaphoreType.DMA((nt,))]
    return _Rider(list(chips) + list(slots), [SDS(s.shape, s.dtype) for s in slots], {nt + t: t for t in range(nt)}, scratch,
                  start, finish)


_BIG = (("w_in", 2), ("w_uq", 2), ("w_ukv", 2), ("w_mem_kv", 1), ("w_br", 2), ("w_out", 1), ("w_ff1", 2), ("w_ff2", 1))


def _in_col_maps(D):
    d_in = O_GATE + 3 * D
    sh = d_in // N_DEV
    shp = _round_up(sh, LANES)
    perm_of_orig = np.empty(d_in, np.int64)
    for a, b, dst in ((0, O_F, C_Q), (O_F, O_CQ, C_F), (O_CQ, O_CKV, C_CQ), (O_CKV, O_KR, C_CKV), (O_KR, O_QM, C_KPE + NOPE),
                      (O_QM, O_GATE, C_QM), (O_GATE, d_in, C_GATE)):
        perm_of_orig[a:b] = dst + np.arange(b - a)
    orig = np.arange(d_in)
    gath_of_orig = (orig // sh) * shp + orig % sh
    fwd = -np.ones(C_GATE + 3 * D, np.int64)
    fwd[perm_of_orig] = gath_of_orig
    bwd = -np.ones(N_DEV * shp, np.int64)
    bwd[gath_of_orig] = perm_of_orig
    return fwd, bwd


def _lane_pad(a):
    c = a.shape[-1]
    return jnp.pad(a, [(0, 0)] * (a.ndim - 1) + [(0, _round_up(c, LANES) - c)])


def _as3(a):
    return a.reshape((-1,) + a.shape[-2:]) if a.ndim != 3 else a


def _rope_tables(positions):
    inv_freq = ROPE_BASE ** (-jnp.arange(0, ROPE, 2, dtype=f32) / ROPE)
    ang = positions.astype(f32).reshape(-1, 1) * inv_freq
    cos, sin = jnp.cos(ang), jnp.sin(ang)
    T = ang.shape[0]
    one, zero = jnp.ones((T, NOPE), f32), jnp.zeros((T, 16), f32)
    c = jnp.concatenate([one, cos, cos, jnp.ones((T, 32), f32)], axis=1)
    sa = jnp.concatenate([jnp.zeros((T, NOPE), f32), -sin, zero, jnp.zeros((T, 32), f32)], axis=1)
    sb = jnp.concatenate([jnp.zeros((T, NOPE), f32), zero, sin, jnp.zeros((T, 32), f32)], axis=1)
    return c, sa, sb


def kernel(x, mem, positions, ln_in_g, ln_in_b, w_in, b_forget, w_uq, g_cq, w_ukv, g_ckv, w_mem_kv, w_br, w_out, ln1_g, ln1_b, w_ff1, w_ff2, ln2_g, ln2_b, loss_target, m_ln_in_g, m_ln_in_b, m_w_in, m_b_forget, m_w_uq, m_g_cq, m_w_ukv, m_g_ckv, m_w_mem_kv, m_w_br, m_w_out, m_ln1_g, m_ln1_b, m_w_ff1, m_w_ff2, m_ln2_g, m_ln2_b, v_ln_in_g, v_ln_in_b, v_w_in, v_b_forget, v_w_uq, v_g_cq, v_w_ukv, v_g_ckv, v_w_mem_kv, v_w_br, v_w_out, v_ln1_g, v_ln1_b, v_w_ff1, v_w_ff2, v_ln2_g, v_ln2_b):
    B, S, D = x.shape
    T = B * S
    L = w_in.shape[0]
    NM = mem.shape[1]
    alpha = (2 * L) ** 0.25
    weights = dict(ln_in_g=ln_in_g, ln_in_b=ln_in_b, w_in=w_in, b_forget=b_forget, w_uq=w_uq, g_cq=g_cq, w_ukv=w_ukv, g_ckv=g_ckv,
                   w_mem_kv=w_mem_kv, w_br=w_br, w_out=w_out, ln1_g=ln1_g, ln1_b=ln1_b, w_ff1=w_ff1, w_ff2=w_ff2, ln2_g=ln2_g, ln2_b=ln2_b)
    mom_m = dict(ln_in_g=m_ln_in_g, ln_in_b=m_ln_in_b, w_in=m_w_in, b_forget=m_b_forget, w_uq=m_w_uq, g_cq=m_g_cq, w_ukv=m_w_ukv,
                 g_ckv=m_g_ckv, w_mem_kv=m_w_mem_kv, w_br=m_w_br, w_out=m_w_out, ln1_g=m_ln1_g, ln1_b=m_ln1_b, w_ff1=m_w_ff1,
                 w_ff2=m_w_ff2, ln2_g=m_ln2_g, ln2_b=m_ln2_b)
    mom_v = dict(ln_in_g=v_ln_in_g, ln_in_b=v_ln_in_b, w_in=v_w_in, b_forget=v_b_forget, w_uq=v_w_uq, g_cq=v_g_cq, w_ukv=v_w_ukv,
                 g_ckv=v_g_ckv, w_mem_kv=v_w_mem_kv, w_br=v_w_br, w_out=v_w_out, ln1_g=v_ln1_g, ln1_b=v_ln1_b, w_ff1=v_w_ff1,
                 w_ff2=v_w_ff2, ln2_g=v_ln2_g, ln2_b=v_ln2_b)
    order = list(weights)
    big_axes = [ax for _, ax in _BIG]
    fwd_map, bwd_map = _in_col_maps(D)
    shard3 = {n: _lane_pad(weights[n]) for n, _ in _BIG}

    def gather_rider(l):
        return _gather_rider([_as3(shard3[n][l]).astype(bf16) for n, _ in _BIG], big_axes)

    def layer_weights(full):
        W = {n: (f if n == "w_br" else f[0]) for (n, _), f in zip(_BIG, full)}
        W["w_in"] = _lane_permute(W["w_in"], fwd_map, bf16, "w_in_reorder")
        return W

    tabs = _rope_tables(positions)
    x2d, mem2d, tgt2d = x.reshape(T, D), mem.reshape(B * NM, D), loss_target.reshape(T, D)
    fox = dict(B=B, Sq=S, Sk=S, nblk=FOX_HEADS // 2, hb=2, dqk=FOX_HD, dv=FOX_HD, mode="fox", scale=FOX_HD ** -0.5)
    mla = dict(B=B, Sq=S, Sk=S, nblk=MLA_HEADS // 2, hb=2, dqk=LANES, dv=MLA_VD, mode="chunk", scale=(NOPE + ROPE) ** -0.5)
    memat = dict(B=B, Sq=S, Sk=NM, nblk=MEM_HEADS, hb=1, dqk=MEM_HD, dv=MEM_HD, mode="none", scale=MEM_HD ** -0.5)
    nb = FOX_HEADS // 2

    h, hb, xh0, rstd0 = _ln_fwd(None, x2d, ln_in_g, ln_in_b, alpha, "ln_in")
    saved = []
    full = _run_rider(gather_rider(0), "weights_all_gather")
    for l in range(L):
        W = layer_weights(full)
        bpad = jnp.pad(b_forget[l].reshape(1, FOX_HEADS), ((0, 0), (0, LANES - FOX_HEADS)))
        proj = _mm(hb, W["w_in"], name="proj")
        cum = _gate_fwd(proj, bpad, B, S, "gate_fwd")
        cum_r = cum[:, :FOX_HEADS].reshape(B, S, nb, 2).transpose(0, 2, 3, 1).reshape(B * nb, 2, S)
        res = _attn_fwd(proj, C_Q // LANES, proj, C_K // LANES, proj, C_V // LANES, cum_r=cum_r, name="fox_fwd",
                        rider=gather_rider(l + 1) if l + 1 < L else None, **fox)
        (o_a, lse_a), full = res if l + 1 < L else (res, None)
        cqn = _rms_fwd(proj, C_CQ, Q_RANK, g_cq[l], "rms_q")
        ckvn = _rms_fwd(proj, C_CKV, KV_RANK, g_ckv[l], "rms_kv")
        qraw = _mm(cqn, W["w_uq"], name="q_up")
        kv = _mm(ckvn, W["w_ukv"], name="kv_up")
        qf = _rope_q(qraw, tabs, bwd=False, name="rope_q")
        kf, v_b = _rope_k(kv, tabs, proj, "rope_k")
        o_b, lse_b = _attn_fwd(qf, 0, kf, 0, v_b, 0, name="mla_fwd", **mla)
        mkv = _mm(mem2d, W["w_mem_kv"], name="mem_kv")
        o_c, lse_c = _attn_fwd(proj, C_QM // MEM_HD, mkv, 0, mkv, MEM_HEADS, name="mem_fwd", **memat)
        merged = _merge_fwd(o_a, o_b, o_c, W["w_br"], proj, D, "merge_fwd")
        y = _mm(merged, W["w_out"], name="out_proj")
        h1, h1b, xh1, rstd1 = _ln_fwd(h, y, ln1_g[l], ln1_b[l], alpha, "ln1")
        a, r = _mm(h1b, W["w_ff1"], epi="relu2", name="ff1")
        ff = _mm(a, W["w_ff2"], name="ff2")
        h2, h2b, xh2, rstd2 = _ln_fwd(h1, ff, ln2_g[l], ln2_b[l], alpha, "ln2")
        saved.append(dict(W=W, bpad=bpad, h=hb, proj=proj, cum_r=cum_r, o_a=o_a, lse_a=lse_a, cqn=cqn, ckvn=ckvn, qf=qf, kf=kf,
                          v_b=v_b, o_b=o_b, lse_b=lse_b, mkv=mkv, o_c=o_c, lse_c=lse_c, merged=merged, h1=h1b, xh1=xh1,
                          rstd1=rstd1, a=a, r=r, xh2=xh2, rstd2=rstd2))
        h, hb = h2, h2b

    loss_local, dh = _loss(h, tgt2d, "loss")
    loss = lax.psum(loss_local[0, 0], ("x", "y", "c"))

    shard_shapes = [_as3(shard3[n][0]).shape for n, _ in _BIG]
    slots = [lax.empty((N_CHIP, L) + s, bf16) for s in shard_shapes]
    core = lax.axis_index("c").astype(jnp.int32).reshape(1)
    small = {n: [None] * L for n in ("b_forget", "g_cq", "g_ckv", "ln1_g", "ln1_b", "ln2_g", "ln2_b")}
    pending = None
    for l in reversed(range(L)):
        sv = saved[l]
        W = sv["W"]
        du2, dg2, db2 = _ln_bwd(dh, sv["xh2"], sv["rstd2"], ln2_g[l], "ln2_bwd")
        dz = _mm(du2, W["w_ff2"], mode="nt", epi="mul", extra=sv["r"], out_dtype=bf16, name="ff2_dx")
        dw_ff2 = _mm(sv["a"], du2, mode="tn", name="ff2_dw")
        dh1 = _mm(dz, W["w_ff1"], mode="nt", epi="add", extra=du2, alpha=alpha, name="ff1_dx")
        dw_ff1 = _mm(sv["h1"], dz, mode="tn", name="ff1_dw")
        du1, dg1, db1 = _ln_bwd(dh1, sv["xh1"], sv["rstd1"], ln1_g[l], "ln1_bwd")
        dmerged = _mm(du1, W["w_out"], mode="nt", name="out_dx")
        dw_out = _mm(sv["merged"], du1, mode="tn", name="out_dw")
        mb = _merge_bwd(dmerged, sv["o_a"], sv["o_b"], sv["o_c"], W["w_br"], sv["proj"], D, "merge_bwd")
        dgl, do, dbp = mb[0:3], mb[3:6], mb[6:9]
        dw_br = jnp.stack([_mm(o_n, dbp_n, mode="tn", name="br_dw") for o_n, dbp_n in zip((sv["o_a"], sv["o_b"], sv["o_c"]), dbp)])
        proj = sv["proj"]
        res = _attn_bwd(proj, C_Q // LANES, proj, C_K // LANES, proj, C_V // LANES, sv["o_a"], do[0], sv["lse_a"],
                        cum_r=sv["cum_r"], name="fox_bwd",
                        rider=None if pending is None else _chip_exchange_rider(pending[0], slots, pending[1]), **fox)
        (dq_a, dk_a, dv_a, dcum), slots = (res, slots) if pending is None else res
        dcum = dcum.reshape(B, nb, S, 2).transpose(0, 2, 1, 3).reshape(T, FOX_HEADS)
        dcum = jnp.pad(dcum, ((0, 0), (0, LANES - FOX_HEADS)))
        dzf, dbf = _gate_bwd(dcum, proj, sv["bpad"], B, S, "gate_bwd")
        dqf, dkf, dv_b = _attn_bwd(sv["qf"], 0, sv["kf"], 0, sv["v_b"], 0, sv["o_b"], do[1], sv["lse_b"], dq_dtype=f32,
                                   name="mla_bwd", **mla)
        dqraw = _rope_q(dqf, tabs, bwd=True, name="rope_q_bwd")
        dkv, dkpe = _rope_k_bwd(dkf, dv_b, tabs, "rope_k_bwd")
        dw_uq = _mm(sv["cqn"], dqraw, mode="tn", name="q_up_dw")
        dcqn = _mm(dqraw, W["w_uq"], mode="nt", name="q_up_dx")
        dcq_, dg_cq = _rms_bwd(dcqn, proj, C_CQ, Q_RANK, g_cq[l], "rms_q_bwd")
        dw_ukv = _mm(sv["ckvn"], dkv, mode="tn", name="kv_up_dw")
        dckvn = _mm(dkv, W["w_ukv"], mode="nt", name="kv_up_dx")
        dckv, dg_ckv = _rms_bwd(dckvn, proj, C_CKV, KV_RANK, g_ckv[l], "rms_kv_bwd")
        dqm, dmk, dmv = _attn_bwd(proj, C_QM // MEM_HD, sv["mkv"], 0, sv["mkv"], MEM_HEADS, sv["o_c"], do[2], sv["lse_c"],
                                  name="mem_bwd", **memat)
        dw_mem = _mm(mem2d, jnp.concatenate([dmk, dmv], axis=1), mode="tn", name="mem_kv_dw")
        dproj = jnp.concatenate([dq_a, dk_a, dv_a, dzf, dkpe, dckv, dqm, jnp.zeros((T, C_CQ - C_PAD), bf16), dcq_, *dgl], axis=1)
        dw_in = _lane_permute(_mm(sv["h"], dproj, mode="tn", name="proj_dw"), bwd_map, f32, "w_in_grad_reorder")
        dh = _mm(dproj, W["w_in"], mode="nt", epi="add", extra=du1, alpha=alpha, name="proj_dx")
        grads_l = dict(w_in=dw_in, w_uq=dw_uq, w_ukv=dw_ukv, w_mem_kv=dw_mem, w_br=dw_br, w_out=dw_out, w_ff1=dw_ff1, w_ff2=dw_ff2)
        g3 = [_as3(grads_l[n]) for n, _ in _BIG]
        bufs = _pair_exchange(g3, big_axes, shard_shapes, "grads_pair_exchange")
        pending = ([_pair_sum(g, b, ax, core, "grads_pair_sum") for g, b, ax in zip(g3, bufs, big_axes)], l)
        for n, val in (("b_forget", dbf[:, :FOX_HEADS]), ("g_cq", dg_cq), ("g_ckv", dg_ckv), ("ln1_g", dg1), ("ln1_b", db1),
                       ("ln2_g", dg2), ("ln2_b", db2)):
            small[n][l] = val.reshape(-1)
    slots = _run_rider(_chip_exchange_rider(pending[0], slots, pending[1]), "grads_chip_exchange")
    grad_x, dg_in, db_in = _ln_bwd(dh, xh0, rstd0, ln_in_g, "ln_in_bwd")

    small_list = [("ln_in_g", dg_in.reshape(-1)), ("ln_in_b", db_in.reshape(-1))] + [(n, jnp.stack(v).reshape(-1)) for n, v in small.items()]
    sflat = jnp.concatenate([v for _, v in small_list])
    n_small = sflat.shape[0]
    rs = _round_up(-(-n_small // LANES), 8)
    spacked = jnp.pad(sflat, (0, rs * LANES - n_small)).reshape(1, rs, LANES)
    sparts = _run_rider(_gather_rider([spacked], [1]), "small_all_gather")[0].reshape(N_DEV, rs * LANES)

    grads, deltas, new_m, new_v = {}, {}, {}, {}
    off = 0
    for n, vv in small_list:
        parts = sparts[:, off:off + vv.shape[0]].reshape((N_DEV,) + weights[n].shape)
        off += vv.shape[0]
        grads[n], deltas[n], new_m[n], new_v[n] = _adamw(weights[n], mom_m[n], mom_v[n], parts, "adamw_" + n)
    for (n, _), sl in zip(_BIG, slots):
        parts = sl.reshape((N_CHIP,) + weights[n].shape[:-1] + (sl.shape[-1],))
        grads[n], deltas[n], new_m[n], new_v[n] = _adamw(weights[n], mom_m[n], mom_v[n], parts, "adamw_" + n)
    return (loss, grad_x.reshape(B, S, D), *[grads[n] for n in order], *[deltas[n] for n in order], *[new_m[n] for n in order],
            *[new_v[n] for n in order])
```

```python
import functools

import numpy as np
import jax
import jax.numpy as jnp
from jax import lax
from jax.experimental import pallas as pl
from jax.experimental.pallas import tpu as pltpu

f32, bf16 = jnp.float32, jnp.bfloat16
SDS = jax.ShapeDtypeStruct

N_DEV = 8
MESH = pl.DeviceIdType.MESH
LANES = 128
VMEM_LIMIT = 56 * 1024 * 1024

FOX_HEADS, FOX_HD = 8, 64
MLA_HEADS, NOPE, ROPE, MLA_VD = 8, 64, 32, 64
Q_RANK, KV_RANK = 384, 256
ROPE_BASE = 10000.0
MEM_HEADS, MEM_HD = 4, 128
BW = 512
CHUNK = 64
LN_EPS, RMS_EPS, NEG_INF = 1e-5, 1e-6, -1e30
ADAM_LR, ADAM_B1, ADAM_B2, ADAM_EPS, ADAM_WD, ADAM_STEP = 0.001, 0.9, 0.999, 1e-08, 0.01, 10

ATT_TQ, ATT_TK = 512, 512

C_Q, C_K, C_V, C_F, C_KPE, C_CKV, C_QM, C_PAD, C_CQ, C_GATE = 0, 512, 1024, 1536, 1664, 1792, 2048, 2560, 2688, 3072
O_F, O_CQ, O_CKV, O_KR, O_QM, O_GATE = 1536, 1544, 1928, 2184, 2216, 2728


def _pick(dim, pref):
    if dim <= pref:
        return dim
    for c in (1024, 768, 512, 384, 256, 128, 64, 32, 16, 8):
        if c <= pref and dim % c == 0:
            return c
    return dim


def _round_up(n, m):
    return -(-n // m) * m


def _params(sem=None):
    return pltpu.CompilerParams(dimension_semantics=sem, vmem_limit_bytes=VMEM_LIMIT)


_DIMS = {"nn": (((1,), (0,)), ((), ())), "nt": (((1,), (1,)), ((), ())), "tn": (((0,), (0,)), ((), ()))}


def _mm(a, b, *, mode="nn", out_dtype=f32, epi="none", extra=None, alpha=1.0, tm=1024, tn=1024, tk=None, name):
    if mode == "nn":
        (M, K), (_, N) = a.shape, b.shape
    elif mode == "nt":
        (M, K), (N, _) = a.shape, b.shape
    else:
        (K, M), (_, N) = a.shape, b.shape
    tm, tn, tk = _pick(M, tm), _pick(N, tn), _pick(K, tk or (512 if mode == "tn" else 1024))
    nk = K // tk
    a_spec = pl.BlockSpec((tk, tm), lambda i, j, k: (k, i)) if mode == "tn" else pl.BlockSpec((tm, tk), lambda i, j, k: (i, k))
    b_spec = pl.BlockSpec((tn, tk), lambda i, j, k: (j, k)) if mode == "nt" else pl.BlockSpec((tk, tn), lambda i, j, k: (k, j))
    o_spec = pl.BlockSpec((tm, tn), lambda i, j, k: (i, j))
    n_out = 2 if epi == "relu2" else 1
    dims = _DIMS[mode]

    def body(*refs):
        a_ref, b_ref = refs[0], refs[1]
        e_ref = refs[2] if extra is not None else None
        outs = refs[-1 - n_out:-1]
        acc = refs[-1]
        k = pl.program_id(2)

        @pl.when(k == 0)
        def _():
            acc[...] = jnp.zeros_like(acc)

        acc[...] += lax.dot_general(a_ref[...].astype(bf16), b_ref[...].astype(bf16), dims, preferred_element_type=f32)

        @pl.when(k == nk - 1)
        def _():
            r = acc[...]
            if epi == "none":
                outs[0][...] = r.astype(out_dtype)
            elif epi == "relu2":
                p = jnp.maximum(r, 0.0)
                outs[0][...] = (p * p).astype(bf16)
                outs[1][...] = (2.0 * p).astype(bf16)
            elif epi == "mul":
                outs[0][...] = (r * e_ref[...].astype(f32)).astype(out_dtype)
            else:
                outs[0][...] = (r + alpha * e_ref[...].astype(f32)).astype(out_dtype)

    ins, in_specs = [a, b], [a_spec, b_spec]
    if extra is not None:
        ins.append(extra)
        in_specs.append(o_spec)
    if epi == "relu2":
        out_shape, out_specs = [SDS((M, N), bf16), SDS((M, N), bf16)], [o_spec, o_spec]
    else:
        out_shape, out_specs = SDS((M, N), out_dtype), o_spec
    return pl.pallas_call(
        body, grid=(M // tm, N // tn, nk), in_specs=in_specs, out_specs=out_specs, out_shape=out_shape,
        scratch_shapes=[pltpu.VMEM((tm, tn), f32)], name=name,
        compiler_params=_params(("parallel", "parallel", "arbitrary")))(*ins)


def _lane_permute(src, cmap, out_dtype, name):
    R, Cs = src.shape
    cmap = np.asarray(cmap, np.int64)
    nb = cmap.shape[0] // LANES
    srcs = [sorted({int(c) // LANES for c in cmap[b * LANES:(b + 1) * LANES] if c >= 0}) for b in range(nb)]
    exact = src.dtype == f32
    tr = _pick(R, 256 if exact else 512)

    def body(src_ref, cm_ref, o_ref):
        rows = lax.broadcasted_iota(jnp.int32, (LANES, LANES), 0)
        for b in range(nb):
            tbl = cm_ref[:, b * LANES:(b + 1) * LANES]
            acc = jnp.zeros((tr, LANES), f32)
            for sb in srcs[b]:
                sel = (rows + sb * LANES) == tbl
                blk = src_ref[:, sb * LANES:(sb + 1) * LANES]
                if exact:
                    acc = acc + jnp.dot(blk, sel.astype(f32), precision=lax.Precision.HIGHEST, preferred_element_type=f32)
                else:
                    acc = acc + jnp.dot(blk, sel.astype(blk.dtype), preferred_element_type=f32)
            o_ref[:, b * LANES:(b + 1) * LANES] = acc.astype(out_dtype)

    return pl.pallas_call(
        body, grid=(R // tr,), in_specs=[pl.BlockSpec((tr, Cs), lambda i: (i, 0)), pl.BlockSpec((1, nb * LANES), lambda i: (0, 0))],
        out_specs=pl.BlockSpec((tr, nb * LANES), lambda i: (i, 0)), out_shape=SDS((R, nb * LANES), out_dtype), name=name,
        compiler_params=_params(("parallel",)))(src, jnp.asarray(cmap.astype(np.int32)).reshape(1, -1))


def _ln_fwd(h, y, g, b, alpha, name):
    T, D = y.shape
    tr = _pick(T, 256)
    row = pl.BlockSpec((tr, D), lambda i: (i, 0))
    vec = pl.BlockSpec((1, D), lambda i: (0, 0))
    col = pl.BlockSpec((tr, 1), lambda i: (i, 0))

    def body(*refs):
        if h is None:
            y_ref, g_ref, b_ref, o_ref, ob_ref, xh_ref, rs_ref = refs
            u = y_ref[...]
        else:
            h_ref, y_ref, g_ref, b_ref, o_ref, ob_ref, xh_ref, rs_ref = refs
            u = alpha * h_ref[...] + y_ref[...]
        mu = jnp.mean(u, axis=-1, keepdims=True)
        d = u - mu
        var = jnp.mean(d * d, axis=-1, keepdims=True)
        rstd = lax.rsqrt(var + LN_EPS)
        xh = d * rstd
        xh_ref[...] = xh
        rs_ref[...] = rstd
        o = xh * g_ref[...] + b_ref[...]
        o_ref[...] = o
        ob_ref[...] = o.astype(bf16)

    ins = ([] if h is None else [h]) + [y, g.reshape(1, D), b.reshape(1, D)]
    specs = ([] if h is None else [row]) + [row, vec, vec]
    return pl.pallas_call(
        body, grid=(T // tr,), in_specs=specs, out_specs=[row, row, row, col],
        out_shape=[SDS((T, D), f32), SDS((T, D), bf16), SDS((T, D), f32), SDS((T, 1), f32)], name=name,
        compiler_params=_params(("parallel",)))(*ins)


def _ln_bwd(dy, xh, rstd, g, name):
    T, D = dy.shape
    tr = _pick(T, 256)
    row = pl.BlockSpec((tr, D), lambda i: (i, 0))
    vec = pl.BlockSpec((1, D), lambda i: (0, 0))
    col = pl.BlockSpec((tr, 1), lambda i: (i, 0))

    def body(dy_ref, xh_ref, rs_ref, g_ref, du_ref, dg_ref, db_ref):
        @pl.when(pl.program_id(0) == 0)
        def _():
            dg_ref[...] = jnp.zeros_like(dg_ref)
            db_ref[...] = jnp.zeros_like(db_ref)

        dyv, xhv = dy_ref[...], xh_ref[...]
        dxh = dyv * g_ref[...]
        m1 = jnp.mean(dxh, axis=-1, keepdims=True)
        m2 = jnp.mean(dxh * xhv, axis=-1, keepdims=True)
        du_ref[...] = rs_ref[...] * (dxh - m1 - xhv * m2)
        dg_ref[...] += jnp.sum(dyv * xhv, axis=0, keepdims=True)
        db_ref[...] += jnp.sum(dyv, axis=0, keepdims=True)

    return pl.pallas_call(
        body, grid=(T // tr,), in_specs=[row, row, col, vec], out_specs=[row, vec, vec],
        out_shape=[SDS((T, D), f32), SDS((1, D), f32), SDS((1, D), f32)], name=name,
        compiler_params=_params(("arbitrary",)))(dy, xh, rstd, g.reshape(1, D))


def _rms_fwd(proj, col_off, R, g, name):
    T = proj.shape[0]
    tr = _pick(T, 512)
    cb = col_off // R

    def body(c_ref, g_ref, o_ref):
        c = c_ref[...]
        r = lax.rsqrt(jnp.mean(c * c, axis=-1, keepdims=True) + RMS_EPS)
        o_ref[...] = (c * r * g_ref[...]).astype(bf16)

    return pl.pallas_call(
        body, grid=(T // tr,), in_specs=[pl.BlockSpec((tr, R), lambda i: (i, cb)), pl.BlockSpec((1, R), lambda i: (0, 0))],
        out_specs=pl.BlockSpec((tr, R), lambda i: (i, 0)), out_shape=SDS((T, R), bf16), name=name,
        compiler_params=_params(("parallel",)))(proj, g.reshape(1, R))


def _rms_bwd(dy, proj, col_off, R, g, name):
    T = proj.shape[0]
    tr = _pick(T, 512)
    cb = col_off // R

    def body(dy_ref, c_ref, g_ref, dc_ref, dg_ref):
        @pl.when(pl.program_id(0) == 0)
        def _():
            dg_ref[...] = jnp.zeros_like(dg_ref)

        c, dyv = c_ref[...], dy_ref[...]
        r = lax.rsqrt(jnp.mean(c * c, axis=-1, keepdims=True) + RMS_EPS)
        t = dyv * g_ref[...]
        mt = jnp.mean(t * c, axis=-1, keepdims=True)
        dc_ref[...] = (r * t - c * (r * r * r) * mt).astype(bf16)
        dg_ref[...] += jnp.sum(dyv * c * r, axis=0, keepdims=True)

    return pl.pallas_call(
        body, grid=(T // tr,),
        in_specs=[pl.BlockSpec((tr, R), lambda i: (i, 0)), pl.BlockSpec((tr, R), lambda i: (i, cb)), pl.BlockSpec((1, R), lambda i: (0, 0))],
        out_specs=[pl.BlockSpec((tr, R), lambda i: (i, 0)), pl.BlockSpec((1, R), lambda i: (0, 0))],
        out_shape=[SDS((T, R), bf16), SDS((1, R), f32)], name=name,
        compiler_params=_params(("arbitrary",)))(dy, proj, g.reshape(1, R))


def _tri(n, lower):
    r = lax.broadcasted_iota(jnp.int32, (n, n), 0)
    c = lax.broadcasted_iota(jnp.int32, (n, n), 1)
    return ((r >= c) if lower else (c >= r)).astype(f32)


def _gate_fwd(proj, bpad, B, S, name):
    ch = _pick(S, 256)
    nch = S // ch
    cb = C_F // LANES

    def body(f_ref, b_ref, o_ref):
        tri = _tri(ch, True)
        carry = jnp.zeros((1, LANES), f32)
        for c in range(nch):
            z = f_ref[c * ch:(c + 1) * ch, :] + b_ref[...]
            lf = jnp.minimum(z, 0.0) - jnp.log1p(jnp.exp(-jnp.abs(z)))
            cs = jnp.dot(tri, lf, precision=lax.Precision.HIGHEST, preferred_element_type=f32) + carry
            o_ref[c * ch:(c + 1) * ch, :] = cs
            carry = cs[ch - 1:ch, :]

    return pl.pallas_call(
        body, grid=(B,), in_specs=[pl.BlockSpec((S, LANES), lambda b: (b, cb)), pl.BlockSpec((1, LANES), lambda b: (0, 0))],
        out_specs=pl.BlockSpec((S, LANES), lambda b: (b, 0)), out_shape=SDS((B * S, LANES), f32), name=name,
        compiler_params=_params(("parallel",)))(proj, bpad)


def _gate_bwd(dcum, proj, bpad, B, S, name):
    ch = _pick(S, 256)
    nch = S // ch
    cb = C_F // LANES

    def body(d_ref, f_ref, b_ref, dz_ref, db_ref):
        @pl.when(pl.program_id(0) == 0)
        def _():
            db_ref[...] = jnp.zeros_like(db_ref)

        tri = _tri(ch, False)
        carry = jnp.zeros((1, LANES), f32)
        dbs = jnp.zeros((1, LANES), f32)
        for c in reversed(range(nch)):
            d = d_ref[c * ch:(c + 1) * ch, :]
            dlf = jnp.dot(tri, d, precision=lax.Precision.HIGHEST, preferred_element_type=f32) + carry
            carry = carry + jnp.sum(d, axis=0, keepdims=True)
            z = f_ref[c * ch:(c + 1) * ch, :] + b_ref[...]
            dz = dlf / (1.0 + jnp.exp(z))
            dz_ref[c * ch:(c + 1) * ch, :] = dz.astype(bf16)
            dbs = dbs + jnp.sum(dz, axis=0, keepdims=True)
        db_ref[...] += dbs

    return pl.pallas_call(
        body, grid=(B,),
        in_specs=[pl.BlockSpec((S, LANES), lambda b: (b, 0)), pl.BlockSpec((S, LANES), lambda b: (b, cb)), pl.BlockSpec((1, LANES), lambda b: (0, 0))],
        out_specs=[pl.BlockSpec((S, LANES), lambda b: (b, 0)), pl.BlockSpec((1, LANES), lambda b: (0, 0))],
        out_shape=[SDS((B * S, LANES), bf16), SDS((1, LANES), f32)], name=name,
        compiler_params=_params(("arbitrary",)))(dcum, proj, bpad)


def _rot(v, c, sa, sb):
    return v * c + pltpu.roll(v, LANES - 16, 1) * sa + pltpu.roll(v, 16, 1) * sb


def _rot_t(v, c, sa, sb):
    return v * c + pltpu.roll(v * sa, 16, 1) + pltpu.roll(v * sb, LANES - 16, 1)


def _rope_q(x, tabs, *, bwd, name):
    T = x.shape[0]
    W = MLA_HEADS * LANES
    tr = _pick(T, 512)
    fn = _rot_t if bwd else _rot

    def body(x_ref, c_ref, sa_ref, sb_ref, o_ref):
        c, sa, sb = c_ref[...], sa_ref[...], sb_ref[...]
        for g in range(MLA_HEADS):
            o_ref[:, g * LANES:(g + 1) * LANES] = fn(x_ref[:, g * LANES:(g + 1) * LANES], c, sa, sb).astype(bf16)

    tab = pl.BlockSpec((tr, LANES), lambda i: (i, 0))
    wide = pl.BlockSpec((tr, W), lambda i: (i, 0))
    return pl.pallas_call(body, grid=(T // tr,), in_specs=[wide, tab, tab, tab], out_specs=wide, out_shape=SDS((T, W), bf16),
                          name=name, compiler_params=_params(("parallel",)))(x, *tabs)


def _rope_k(kv, tabs, proj, name):
    T = kv.shape[0]
    W = MLA_HEADS * LANES
    tr = _pick(T, 512)
    kcb = C_KPE // LANES

    def body(x_ref, c_ref, sa_ref, sb_ref, kpe_ref, k_ref, v_ref):
        c, sa, sb, kpe = c_ref[...], sa_ref[...], sb_ref[...], kpe_ref[...]
        low = lax.broadcasted_iota(jnp.int32, (tr, LANES), 1) < NOPE
        halves = []
        for g in range(MLA_HEADS):
            x = x_ref[:, g * LANES:(g + 1) * LANES]
            k_ref[:, g * LANES:(g + 1) * LANES] = _rot(jnp.where(low, x, 0.0) + kpe, c, sa, sb).astype(bf16)
            halves.append(x[:, NOPE:])
            if g % 2 == 1:
                v_ref[:, (g // 2) * LANES:(g // 2 + 1) * LANES] = jnp.concatenate(halves, axis=1).astype(bf16)
                halves = []

    tab = pl.BlockSpec((tr, LANES), lambda i: (i, 0))
    wide = pl.BlockSpec((tr, W), lambda i: (i, 0))
    return pl.pallas_call(
        body, grid=(T // tr,), in_specs=[wide, tab, tab, tab, pl.BlockSpec((tr, LANES), lambda i: (i, kcb))],
        out_specs=[wide, pl.BlockSpec((tr, W // 2), lambda i: (i, 0))], out_shape=[SDS((T, W), bf16), SDS((T, W // 2), bf16)],
        name=name, compiler_params=_params(("parallel",)))(kv, *tabs, proj)


def _rope_k_bwd(dk, dv, tabs, name):
    T = dk.shape[0]
    W = MLA_HEADS * LANES
    tr = _pick(T, 512)

    def body(dk_ref, dv_ref, c_ref, sa_ref, sb_ref, o_ref, s_ref):
        c, sa, sb = c_ref[...], sa_ref[...], sb_ref[...]
        tot = jnp.zeros((tr, LANES), f32)
        for g in range(MLA_HEADS):
            d = _rot_t(dk_ref[:, g * LANES:(g + 1) * LANES], c, sa, sb)
            tot = tot + d
            o_ref[:, g * LANES:(g + 1) * LANES] = jnp.concatenate(
                [d[:, :NOPE].astype(bf16), dv_ref[:, g * MLA_VD:(g + 1) * MLA_VD]], axis=1)
        s_ref[...] = tot.astype(bf16)

    tab = pl.BlockSpec((tr, LANES), lambda i: (i, 0))
    wide = pl.BlockSpec((tr, W), lambda i: (i, 0))
    return pl.pallas_call(
        body, grid=(T // tr,), in_specs=[wide, pl.BlockSpec((tr, W // 2), lambda i: (i, 0)), tab, tab, tab],
        out_specs=[wide, tab], out_shape=[SDS((T, W), bf16), SDS((T, LANES), bf16)], name=name,
        compiler_params=_params(("parallel",)))(dk, dv, *tabs)


def _tile_mask(mode, i, j, tq, tk):
    r = i * tq + lax.broadcasted_iota(jnp.int32, (tq, tk), 0)
    c = j * tk + lax.broadcasted_iota(jnp.int32, (tq, tk), 1)
    if mode == "chunk":
        return (c // CHUNK) <= (r // CHUNK)
    return c <= r


def _attn_tiles(Sq, Sk):
    tk = _pick(Sk, ATT_TK)
    tq = _pick(Sq, ATT_TQ)
    assert (tk % tq == 0 or tq % tk == 0) and tq % CHUNK == 0 and tk % CHUNK == 0
    return tq, tk


def _stage_heads(dst, src_ref, hb, w, mark=None, val=1.0):
    S = src_ref.shape[0]
    ch = _pick(S, 256)
    assert w <= LANES and (mark is None or w <= mark < LANES)

    def step(c, _):
        rows = pl.ds(pl.multiple_of(c * ch, ch), ch)
        for hh in range(hb):
            x = src_ref[rows, hh * w:(hh + 1) * w].astype(bf16)
            if w < LANES:
                pad = jnp.zeros((ch, LANES - w), bf16)
                if mark is not None:
                    lane = lax.broadcasted_iota(jnp.int32, (ch, LANES - w), 1)
                    pad = jnp.where(lane == mark - w, val, 0.0).astype(bf16)
                x = jnp.concatenate([x, pad], axis=1)
            dst[hh, rows, :] = x
        return 0

    lax.fori_loop(0, S // ch, step, 0)


def _carry(rider, refs, n_in, n_out, n_scr, grid):
    if rider is None:
        return refs
    ri, ro = len(rider.inputs), len(rider.out_shapes)
    own = refs[:n_in] + refs[n_in + ri:n_in + ri + n_out] + refs[n_in + ri + n_out + ro:n_in + ri + n_out + ro + n_scr]
    rider.refs = (refs[n_in:n_in + ri], refs[n_in + ri + n_out:n_in + ri + n_out + ro], refs[n_in + ri + n_out + ro + n_scr:])
    first = functools.reduce(jnp.logical_and, [pl.program_id(a) == 0 for a in range(len(grid))])

    @pl.when(first)
    def _():
        rider.start(*rider.refs)

    return own


def _carry_finish(rider, grid):
    if rider is None:
        return
    last = functools.reduce(jnp.logical_and, [pl.program_id(a) == n - 1 for a, n in enumerate(grid)])

    @pl.when(last)
    def _():
        rider.finish(*rider.refs)


def _carrier_call(body, rider, ins, in_specs, out_shape, out_specs, scratch, grid, name):
    if rider is None:
        return pl.pallas_call(body, grid=grid, in_specs=in_specs, out_specs=out_specs, out_shape=out_shape, scratch_shapes=scratch,
                              name=name, compiler_params=_params(("parallel",) * len(grid)))(*ins)
    any_spec = pl.BlockSpec(memory_space=pl.ANY)
    ri, ro = len(rider.inputs), len(rider.out_shapes)
    res = pl.pallas_call(
        body, grid=grid, in_specs=list(in_specs) + [any_spec] * ri, out_specs=list(out_specs) + [any_spec] * ro,
        out_shape=list(out_shape) + rider.out_shapes, scratch_shapes=list(scratch) + rider.scratch,
        input_output_aliases={len(ins) + i: len(out_shape) + o for i, o in rider.aliases.items()}, name=name,
        compiler_params=_params(("arbitrary",) * len(grid)))(*ins, *rider.inputs)
    return res[:len(out_shape)], res[len(out_shape):]


def _attn_fwd(q_arr, q_cb, k_arr, k_cb, v_arr, v_cb, *, B, Sq, Sk, nblk, hb, dqk, dv, mode, scale, cum_r=None, rider=None, name):
    wq, wv = hb * dqk, hb * dv
    tq, tk = _attn_tiles(Sq, Sk)
    fox = mode == "fox"
    n_in = 4 if fox else 3
    ones = dv < LANES

    def body(*refs):
        refs = _carry(rider, refs, n_in, 2, 3, (B, nblk))
        q_ref, k_ref, v_ref = refs[:3]
        cr_ref = refs[3] if fox else None
        o_ref, lse_ref, qh, kh, vh = refs[n_in:]
        _stage_heads(qh, q_ref, hb, dqk)
        _stage_heads(kh, k_ref, hb, dqk)
        _stage_heads(vh, v_ref, hb, dv, dv if ones else None)

        def qstep(i, _):
            rows = pl.ds(pl.multiple_of(i * tq, tq), tq)
            qs = [qh[hh, rows, :] for hh in range(hb)]

            def tile(j, carry, diag):
                cols = pl.ds(pl.multiple_of(j * tk, tk), tk)
                out = []
                for hh in range(hb):
                    m, l, acc = carry[hh]
                    s = lax.dot_general(qs[hh], kh[hh, cols, :], _DIMS["nt"], preferred_element_type=f32) * scale
                    if fox:
                        s = s - cr_ref[0, hh:hh + 1, cols]
                    if diag:
                        s = jnp.where(_tile_mask(mode, i, j, tq, tk), s, NEG_INF)
                    m_new = jnp.maximum(m, jnp.max(s, axis=-1, keepdims=True))
                    a = jnp.exp(m - m_new)
                    p = jnp.exp(s - m_new)
                    if not ones:
                        l = a * l + jnp.sum(p, axis=-1, keepdims=True)
                    acc = a * acc + lax.dot_general(p.astype(bf16), vh[hh, cols, :], _DIMS["nn"], preferred_element_type=f32)
                    out.append((m_new, l, acc))
                return tuple(out)

            carry = tuple((jnp.full((tq, 1), -jnp.inf, f32), jnp.zeros((tq, 1), f32), jnp.zeros((tq, LANES), f32)) for _ in range(hb))
            if mode == "none":
                carry = lax.fori_loop(0, Sk // tk, functools.partial(tile, diag=False), carry)
            else:
                nfull = (i * tq) // tk
                carry = lax.fori_loop(0, nfull, functools.partial(tile, diag=False), carry)
                for d in range(max(1, tq // tk)):
                    carry = tile(nfull + d, carry, True)
            for hh in range(hb):
                m, l, acc = carry[hh]
                if ones:
                    l = acc[:, dv:dv + 1]
                o_ref[rows, hh * dv:(hh + 1) * dv] = acc[:, :dv] / l
                lse_ref[0, rows, hh:hh + 1] = m + jnp.log(l)
            return 0

        lax.fori_loop(0, Sq // tq, qstep, 0)
        _carry_finish(rider, (B, nblk))

    in_specs = [pl.BlockSpec((Sq, wq), lambda b, p: (b, q_cb + p)), pl.BlockSpec((Sk, wq), lambda b, p: (b, k_cb + p)),
                pl.BlockSpec((Sk, wv), lambda b, p: (b, v_cb + p))]
    ins = [q_arr, k_arr, v_arr]
    if fox:
        in_specs.append(pl.BlockSpec((1, hb, Sk), lambda b, p: (b * nblk + p, 0, 0)))
        ins.append(cum_r)
    return _carrier_call(
        body, rider, ins, in_specs,
        [SDS((B * Sq, nblk * wv), f32), SDS((B * nblk, Sq, hb), f32)],
        [pl.BlockSpec((Sq, wv), lambda b, p: (b, p)), pl.BlockSpec((1, Sq, hb), lambda b, p: (b * nblk + p, 0, 0))],
        [pltpu.VMEM((hb, Sq, LANES), bf16), pltpu.VMEM((hb, Sk, LANES), bf16), pltpu.VMEM((hb, Sk, LANES), bf16)],
        (B, nblk), name)


def _attn_bwd(q_arr, q_cb, k_arr, k_cb, v_arr, v_cb, o, do, lse, *, B, Sq, Sk, nblk, hb, dqk, dv, mode, scale, cum_r=None,
              dq_dtype=bf16, rider=None, name):
    wq, wv = hb * dqk, hb * dv
    tq, tk = _attn_tiles(Sq, Sk)
    fox = mode == "fox"
    n_in = 7 if fox else 6
    n_out = 4 if fox else 3
    if fox:
        assert dqk + 2 <= LANES and np.log2(scale) == np.round(np.log2(scale))

    def body(*refs):
        refs = _carry(rider, refs, n_in, n_out, 7 if fox else 6, (B, nblk))
        q_ref, k_ref, v_ref, o_ref, do_ref, lse_ref = refs[:6]
        cr_ref = refs[6] if fox else None
        dq_ref, dk_ref, dv_ref = refs[n_in:n_in + 3]
        dcum_ref = refs[n_in + 3] if fox else None
        qh, kh, vh, doh, dk_acc, dv_acc = refs[n_in + n_out:n_in + n_out + 6]
        dcq_acc = refs[n_in + n_out + 6] if fox else None
        _stage_heads(qh, q_ref, hb, dqk, dqk + 1 if fox else None, 1.0 / scale)
        _stage_heads(kh, k_ref, hb, dqk, dqk if fox else None, 1.0 / scale)
        _stage_heads(vh, v_ref, hb, dv)
        _stage_heads(doh, do_ref, hb, dv)
        dk_acc[...] = jnp.zeros_like(dk_acc)
        dv_acc[...] = jnp.zeros_like(dv_acc)

        def qstep(i, _):
            rows = pl.ds(pl.multiple_of(i * tq, tq), tq)
            qs = [qh[hh, rows, :] for hh in range(hb)]
            dos = [doh[hh, rows, :] for hh in range(hb)]
            dds = [jnp.sum(do_ref[rows, hh * dv:(hh + 1) * dv] * o_ref[rows, hh * dv:(hh + 1) * dv], axis=-1, keepdims=True)
                   for hh in range(hb)]
            lses = [lse_ref[0, rows, hh:hh + 1] for hh in range(hb)]

            def tile(j, carry, diag):
                cols = pl.ds(pl.multiple_of(j * tk, tk), tk)
                out = []
                for hh in range(hb):
                    k = kh[hh, cols, :]
                    s = lax.dot_general(qs[hh], k, _DIMS["nt"], preferred_element_type=f32) * scale
                    if fox:
                        s = s - cr_ref[0, hh:hh + 1, cols]
                    if diag:
                        s = jnp.where(_tile_mask(mode, i, j, tq, tk), s, NEG_INF)
                    p = jnp.exp(s - lses[hh])
                    dv_acc[hh, cols, :] += lax.dot_general(p.astype(bf16), dos[hh], _DIMS["tn"], preferred_element_type=f32)
                    dp = lax.dot_general(dos[hh], vh[hh, cols, :], _DIMS["nt"], preferred_element_type=f32)
                    ds = p * (dp - dds[hh])
                    dsb = (ds * scale).astype(bf16)
                    out.append(carry[hh] + lax.dot_general(dsb, k, _DIMS["nn"], preferred_element_type=f32))
                    dk_acc[hh, cols, :] += lax.dot_general(dsb, qs[hh], _DIMS["tn"], preferred_element_type=f32)
                return tuple(out)

            carry = tuple(jnp.zeros((tq, LANES), f32) for _ in range(hb))
            if mode == "none":
                carry = lax.fori_loop(0, Sk // tk, functools.partial(tile, diag=False), carry)
            else:
                nfull = (i * tq) // tk
                carry = lax.fori_loop(0, nfull, functools.partial(tile, diag=False), carry)
                for d in range(max(1, tq // tk)):
                    carry = tile(nfull + d, carry, True)
            for hh in range(hb):
                dq_ref[rows, hh * dqk:(hh + 1) * dqk] = carry[hh][:, :dqk].astype(dq_dtype)
                if fox:
                    dcq_acc[rows, hh:hh + 1] = carry[hh][:, dqk:dqk + 1]
            return 0

        lax.fori_loop(0, Sq // tq, qstep, 0)
        for hh in range(hb):
            dk_ref[:, hh * dqk:(hh + 1) * dqk] = dk_acc[hh, :, :dqk].astype(dq_dtype)
            dv_ref[:, hh * dv:(hh + 1) * dv] = dv_acc[hh, :, :dv].astype(bf16)
            if fox:
                dcum_ref[0, :, hh:hh + 1] = dcq_acc[:, hh:hh + 1] - dk_acc[hh, :, dqk + 1:dqk + 2]
        _carry_finish(rider, (B, nblk))

    qs_ =pl.BlockSpec((Sq, wq), lambda b, p: (b, q_cb + p))
    ks_ = pl.BlockSpec((Sk, wq), lambda b, p: (b, k_cb + p))
    vs_ = pl.BlockSpec((Sk, wv), lambda b, p: (b, v_cb + p))
    os_ = pl.BlockSpec((Sq, wv), lambda b, p: (b, p))
    st = pl.BlockSpec((1, Sq, hb), lambda b, p: (b * nblk + p, 0, 0))
    rw = pl.BlockSpec((1, hb, Sk), lambda b, p: (b * nblk + p, 0, 0))
    in_specs = [qs_, ks_, vs_, os_, os_, st] + ([rw] if fox else [])
    ins = [q_arr, k_arr, v_arr, o, do, lse] + ([cum_r] if fox else [])
    out_specs = [pl.BlockSpec((Sq, wq), lambda b, p: (b, p)), pl.BlockSpec((Sk, wq), lambda b, p: (b, p)),
                 pl.BlockSpec((Sk, wv), lambda b, p: (b, p))] + ([st] if fox else [])
    out_shape = [SDS((B * Sq, nblk * wq), dq_dtype), SDS((B * Sk, nblk * wq), dq_dtype), SDS((B * Sk, nblk * wv), bf16)]
    scratch = [pltpu.VMEM((hb, Sq, LANES), bf16), pltpu.VMEM((hb, Sk, LANES), bf16), pltpu.VMEM((hb, Sk, LANES), bf16),
               pltpu.VMEM((hb, Sq, LANES), bf16), pltpu.VMEM((hb, Sk, LANES), f32), pltpu.VMEM((hb, Sk, LANES), f32)]
    if fox:
        assert Sq == Sk
        out_shape.append(SDS((B * nblk, Sq, hb), f32))
        scratch.append(pltpu.VMEM((Sq, hb), f32))
    return _carrier_call(body, rider, ins, in_specs, out_shape, out_specs, scratch, (B, nblk), name)


def _merge_fwd(oa, ob, oc, w_br, proj, D, name):
    T = oa.shape[0]
    tr = _pick(T, 256)
    g0 = C_GATE // D
    o_spec = pl.BlockSpec((tr, BW), lambda i: (i, 0))
    w_spec = pl.BlockSpec((3, BW, D), lambda i: (0, 0, 0))

    def body(oa_ref, ob_ref, oc_ref, w_ref, ga_ref, gb_ref, gc_ref, m_ref):
        tot = jnp.zeros((tr, D), f32)
        for n, (o_ref, g_ref) in enumerate(((oa_ref, ga_ref), (ob_ref, gb_ref), (oc_ref, gc_ref))):
            bp = lax.dot_general(o_ref[...].astype(bf16), w_ref[n], _DIMS["nn"], preferred_element_type=f32)
            tot = tot + jax.nn.sigmoid(g_ref[...]) * bp
        m_ref[...] = tot.astype(bf16)

    gspecs = [pl.BlockSpec((tr, D), lambda i, n=n: (i, g0 + n)) for n in range(3)]
    return pl.pallas_call(
        body, grid=(T // tr,), in_specs=[o_spec, o_spec, o_spec, w_spec, *gspecs],
        out_specs=pl.BlockSpec((tr, D), lambda i: (i, 0)), out_shape=SDS((T, D), bf16), name=name,
        compiler_params=_params(("parallel",)))(oa, ob, oc, w_br, proj, proj, proj)


def _merge_bwd(dm, oa, ob, oc, w_br, proj, D, name, rider=None):
    T = oa.shape[0]
    tr = _pick(T, 256)
    g0 = C_GATE // D
    o_spec = pl.BlockSpec((tr, BW), lambda i: (i, 0))
    d_spec = pl.BlockSpec((tr, D), lambda i: (i, 0))
    w_spec = pl.BlockSpec((3, BW, D), lambda i: (0, 0, 0))

    def body(*refs):
        refs = _carry(rider, refs, 8, 9, 0, (T // tr,))
        dm_ref, oa_ref, ob_ref, oc_ref, w_ref, ga_ref, gb_ref, gc_ref = refs[:8]
        outs = refs[8:]
        dmv = dm_ref[...]
        for n, (o_ref, g_ref) in enumerate(((oa_ref, ga_ref), (ob_ref, gb_ref), (oc_ref, gc_ref))):
            bp = lax.dot_general(o_ref[...].astype(bf16), w_ref[n], _DIMS["nn"], preferred_element_type=f32)
            g = jax.nn.sigmoid(g_ref[...])
            dbp = (dmv * g).astype(bf16)
            outs[n][...] = (dmv * bp * g * (1.0 - g)).astype(bf16)
            outs[3 + n][...] = lax.dot_general(dbp, w_ref[n], _DIMS["nt"], preferred_element_type=f32)
            outs[6 + n][...] = dbp
        _carry_finish(rider, (T // tr,))

    gspecs = [pl.BlockSpec((tr, D), lambda i, n=n: (i, g0 + n)) for n in range(3)]
    return _carrier_call(
        body, rider, [dm, oa, ob, oc, w_br, proj, proj, proj], [d_spec, o_spec, o_spec, o_spec, w_spec, *gspecs],
        [SDS((T, D), bf16)] * 3 + [SDS((T, BW), f32)] * 3 + [SDS((T, D), bf16)] * 3,
        [d_spec] * 3 + [o_spec] * 3 + [d_spec] * 3, [], (T // tr,), name)


def _loss(h, target, name):
    T, D = h.shape
    tr = _pick(T, 256)
    nt = T // tr
    row = pl.BlockSpec((tr, D), lambda i: (i, 0))

    def body(h_ref, t_ref, l_ref, d_ref, acc):
        i = pl.program_id(0)

        @pl.when(i == 0)
        def _():
            acc[...] = jnp.zeros_like(acc)

        e = h_ref[...] - t_ref[...]
        d_ref[...] = e * (1.0 / D)
        acc[...] += jnp.sum(e * e, axis=0, keepdims=True)

        @pl.when(i == nt - 1)
        def _():
            l_ref[...] = jnp.sum(acc[...], axis=-1, keepdims=True) * (0.5 / D)

    return pl.pallas_call(
        body, grid=(nt,), in_specs=[row, row], out_specs=[pl.BlockSpec((1, 1), lambda i: (0, 0)), row],
        out_shape=[SDS((1, 1), f32), SDS((T, D), f32)], scratch_shapes=[pltpu.VMEM((1, D), f32)], name=name,
        compiler_params=_params(("arbitrary",)))(h, target)


def _adamw(w, m, v, parts, name):
    shape = w.shape
    C = shape[-1]
    R = int(np.prod(shape[:-1])) if len(shape) > 1 else 1
    n_parts, Cp = parts.shape[0], parts.shape[-1]
    tr = _pick(R, 256)
    spec = pl.BlockSpec((tr, C), lambda i: (i, 0))

    def body(w_ref, m_ref, v_ref, p_ref, g_ref, d_ref, mo_ref, vo_ref):
        gv = p_ref[0].astype(f32)
        for q in range(1, n_parts):
            gv = gv + p_ref[q].astype(f32)
        gv = gv[:, :C]
        mn = ADAM_B1 * m_ref[...] + (1.0 - ADAM_B1) * gv
        vn = ADAM_B2 * v_ref[...] + (1.0 - ADAM_B2) * (gv * gv)
        m_hat = mn / (1.0 - ADAM_B1 ** ADAM_STEP)
        v_hat = vn / (1.0 - ADAM_B2 ** ADAM_STEP)
        g_ref[...] = gv
        d_ref[...] = -ADAM_LR * (m_hat / (jnp.sqrt(v_hat) + ADAM_EPS) + ADAM_WD * w_ref[...])
        mo_ref[...] = mn
        vo_ref[...] = vn

    outs = pl.pallas_call(
        body, grid=(R // tr,), in_specs=[spec] * 3 + [pl.BlockSpec((n_parts, tr, Cp), lambda i: (0, i, 0))], out_specs=[spec] * 4,
        out_shape=[SDS((R, C), f32)] * 4, name=name,
        compiler_params=_params(("parallel",)))(*[a.reshape(R, C) for a in (w, m, v)], parts.reshape(n_parts, R, Cp))
    return [o.reshape(shape) for o in outs]


def _coords():
    return lax.axis_index("x"), lax.axis_index("y"), lax.axis_index("c")


def _window(ref, axis, idx, n):
    start = pl.multiple_of(idx * n, n)
    if axis == 1:
        return ref.at[:, pl.ds(start, n), :]
    return ref.at[:, :, pl.ds(start, n)]


class _Rider:
    def __init__(self, inputs, out_shapes, aliases, scratch, start, finish):
        self.inputs, self.out_shapes, self.aliases, self.scratch = list(inputs), list(out_shapes), dict(aliases), list(scratch)
        self.start, self.finish = start, finish


def _run_rider(r, name):
    ni, no = len(r.inputs), len(r.out_shapes)

    def body(*refs):
        ins, outs, scr = refs[:ni], refs[ni:ni + no], refs[ni + no:]
        r.start(ins, outs, scr)
        r.finish(ins, outs, scr)

    any_spec = pl.BlockSpec(memory_space=pl.ANY)
    return pl.pallas_call(body, out_shape=r.out_shapes, in_specs=[any_spec] * ni, out_specs=[any_spec] * no,
                          input_output_aliases=r.aliases, scratch_shapes=r.scratch, name=name)(*r.inputs)


def _gather_rider(shards, axes):
    nt = len(shards)
    sizes = [s.shape[a] for s, a in zip(shards, axes)]
    out_shapes = [SDS(tuple(d * N_DEV if i == a else d for i, d in enumerate(s.shape)), s.dtype) for s, a in zip(shards, axes)]

    def plan(x_refs, out_refs, scr):
        send_sems, recv_sems, local_sems = scr
        x, y, c = _coords()
        me, sibling = (x, y, c), (x, y, 1 - c)
        chips = [(1 - x, y), (x, 1 - y), (1 - x, 1 - y)]

        def win(t, px, py, pc):
            return _window(out_refs[t], axes[t], 4 * px + 2 * py + pc, sizes[t])

        def copy(t, k, block, to, src=None):
            return pltpu.make_async_remote_copy(
                src_ref=win(t, *block) if src is None else src, dst_ref=win(t, *block),
                send_sem=send_sems.at[7 * t + k], recv_sem=recv_sems.at[7 * t + k], device_id=to, device_id_type=MESH)

        mine = [pltpu.make_async_copy(x_refs[t], win(t, *me), local_sems.at[t]) for t in range(nt)]
        first = []
        for t in range(nt):
            first.append(copy(t, 0, me, sibling, src=x_refs[t]))
            first += [copy(t, 1 + j, me, (*chip, c), src=x_refs[t]) for j, chip in enumerate(chips)]
        return c, me, sibling, chips, copy, mine, first

    def start(x_refs, out_refs, scr):
        _, _, _, _, _, mine, first = plan(x_refs, out_refs, scr)
        for cp in mine + first:
            cp.start()

    def finish(x_refs, out_refs, scr):
        c, me, sibling, chips, copy, mine, first = plan(x_refs, out_refs, scr)
        passed = []
        for j, chip in enumerate(chips):
            for t in range(nt):
                copy(t, 1 + j, (*chip, c), me).wait_recv()
                fwd = copy(t, 4 + j, (*chip, c), sibling)
                fwd.start()
                passed.append(fwd)
        for t in range(nt):
            copy(t, 0, sibling, me).wait_recv()
            for j, chip in enumerate(chips):
                copy(t, 4 + j, (*chip, 1 - c), me).wait_recv()
        for cp in first + passed:
            cp.wait_send()
        for cp in mine:
            cp.wait()

    scratch = [pltpu.SemaphoreType.DMA((7 * nt,)), pltpu.SemaphoreType.DMA((7 * nt,)), pltpu.SemaphoreType.DMA((nt,))]
    return _Rider(shards, out_shapes, {}, scratch, start, finish)


N_CHIP = N_DEV // 2


def _pair_exchange_rider(grads, axes, shards):
    nt = len(grads)
    sizes = [s[a] for s, a in zip(shards, axes)]

    def plan(ins, outs, scr):
        send_sems, recv_sems = scr
        x, y, c = _coords()
        return [pltpu.make_async_remote_copy(
            src_ref=_window(ins[t], axes[t], 2 * q + (1 - c), sizes[t]), dst_ref=outs[t].at[q],
            send_sem=send_sems.at[N_CHIP * t + q], recv_sem=recv_sems.at[N_CHIP * t + q],
            device_id=(x, y, 1 - c), device_id_type=MESH) for t in range(nt) for q in range(N_CHIP)]

    def start(ins, outs, scr):
        for cp in plan(ins, outs, scr):
            cp.start()

    def finish(ins, outs, scr):
        copies = plan(ins, outs, scr)
        for cp in copies:
            cp.wait_recv()
        for cp in copies:
            cp.wait_send()

    scratch = [pltpu.SemaphoreType.DMA((N_CHIP * nt,)), pltpu.SemaphoreType.DMA((N_CHIP * nt,))]
    return _Rider(grads, [SDS((N_CHIP,) + tuple(s), f32) for s in shards], {}, scratch, start, finish)


def _pair_sum(grad, buf, axis, core, name):
    _, G, r, c = buf.shape
    tr = _pick(r, 512)
    nr = r // tr

    def body(core_ref, g_ref, b_ref, o_ref):
        o_ref[0, 0] = (g_ref[0] + b_ref[0, 0]).astype(bf16)

    if axis == 2:
        g_spec = pl.BlockSpec((1, tr, c), lambda q, g, i, core_ref: (g, i, 2 * q + core_ref[0]))
    else:
        g_spec = pl.BlockSpec((1, tr, c), lambda q, g, i, core_ref: (g, (2 * q + core_ref[0]) * nr + i, 0))
    blk = pl.BlockSpec((1, 1, tr, c), lambda q, g, i, core_ref: (q, g, i, 0))
    grid_spec = pltpu.PrefetchScalarGridSpec(num_scalar_prefetch=1, grid=(N_CHIP, G, nr), in_specs=[g_spec, blk], out_specs=blk)
    return pl.pallas_call(body, grid_spec=grid_spec, out_shape=SDS(buf.shape, bf16), name=name,
                          compiler_params=_params(("parallel", "parallel", "parallel")))(core, grad, buf)


def _chip_exchange_rider(chips, slots, layer):
    nt = len(chips)

    def plan(ins, outs, scr):
        send_sems, recv_sems, local_sems = scr
        x, y, c = _coords()
        me = 2 * x + y
        mine = [pltpu.make_async_copy(ins[t].at[me], outs[t].at[me, layer], local_sems.at[t]) for t in range(nt)]
        copies = []
        for k in range(1, N_CHIP):
            px, py = lax.rem(x + (k >> 1), 2), lax.rem(y + (k & 1), 2)
            copies += [pltpu.make_async_remote_copy(
                src_ref=ins[t].at[2 * px + py], dst_ref=outs[t].at[me, layer],
                send_sem=send_sems.at[3 * t + k - 1], recv_sem=recv_sems.at[3 * t + k - 1],
                device_id=(px, py, c), device_id_type=MESH) for t in range(nt)]
        return mine, copies

    def start(ins, outs, scr):
        mine, copies = plan(ins, outs, scr)
        for cp in mine + copies:
            cp.start()

    def finish(ins, outs, scr):
        mine, copies = plan(ins, outs, scr)
        for cp in copies:
            cp.wait_recv()
        for cp in copies:
            cp.wait_send()
        for cp in mine:
            cp.wait()

    scratch = [pltpu.SemaphoreType.DMA((3 * nt,)), pltpu.SemaphoreType.DMA((3 * nt,)), pltpu.SemaphoreType.DMA((nt,))]
    return _Rider(list(chips) + list(slots), [SDS(s.shape, s.dtype) for s in slots], {nt + t: t for t in range(nt)}, scratch,
                  start, finish)


_BIG = (("w_in", 2), ("w_uq", 2), ("w_ukv", 2), ("w_mem_kv", 1), ("w_br", 2), ("w_out", 1), ("w_ff1", 2), ("w_ff2", 1))


def _in_col_maps(D):
    d_in = O_GATE + 3 * D
    sh = d_in // N_DEV
    shp = _round_up(sh, LANES)
    perm_of_orig = np.empty(d_in, np.int64)
    for a, b, dst in ((0, O_F, C_Q), (O_F, O_CQ, C_F), (O_CQ, O_CKV, C_CQ), (O_CKV, O_KR, C_CKV), (O_KR, O_QM, C_KPE + NOPE),
                      (O_QM, O_GATE, C_QM), (O_GATE, d_in, C_GATE)):
        perm_of_orig[a:b] = dst + np.arange(b - a)
    orig = np.arange(d_in)
    gath_of_orig = (orig // sh) * shp + orig % sh
    fwd = -np.ones(C_GATE + 3 * D, np.int64)
    fwd[perm_of_orig] = gath_of_orig
    bwd = -np.ones(N_DEV * shp, np.int64)
    bwd[gath_of_orig] = perm_of_orig
    return fwd, bwd


def _lane_pad(a):
    c = a.shape[-1]
    return jnp.pad(a, [(0, 0)] * (a.ndim - 1) + [(0, _round_up(c, LANES) - c)])


def _as3(a):
    return a.reshape((-1,) + a.shape[-2:]) if a.ndim != 3 else a


def _rope_tables(positions):
    inv_freq = ROPE_BASE ** (-jnp.arange(0, ROPE, 2, dtype=f32) / ROPE)
    ang = positions.astype(f32).reshape(-1, 1) * inv_freq
    cos, sin = jnp.cos(ang), jnp.sin(ang)
    T = ang.shape[0]
    one, zero = jnp.ones((T, NOPE), f32), jnp.zeros((T, 16), f32)
    c = jnp.concatenate([one, cos, cos, jnp.ones((T, 32), f32)], axis=1)
    sa = jnp.concatenate([jnp.zeros((T, NOPE), f32), -sin, zero, jnp.zeros((T, 32), f32)], axis=1)
    sb = jnp.concatenate([jnp.zeros((T, NOPE), f32), zero, sin, jnp.zeros((T, 32), f32)], axis=1)
    return c, sa, sb


def kernel(x, mem, positions, ln_in_g, ln_in_b, w_in, b_forget, w_uq, g_cq, w_ukv, g_ckv, w_mem_kv, w_br, w_out, ln1_g, ln1_b, w_ff1, w_ff2, ln2_g, ln2_b, loss_target, m_ln_in_g, m_ln_in_b, m_w_in, m_b_forget, m_w_uq, m_g_cq, m_w_ukv, m_g_ckv, m_w_mem_kv, m_w_br, m_w_out, m_ln1_g, m_ln1_b, m_w_ff1, m_w_ff2, m_ln2_g, m_ln2_b, v_ln_in_g, v_ln_in_b, v_w_in, v_b_forget, v_w_uq, v_g_cq, v_w_ukv, v_g_ckv, v_w_mem_kv, v_w_br, v_w_out, v_ln1_g, v_ln1_b, v_w_ff1, v_w_ff2, v_ln2_g, v_ln2_b):
    B, S, D = x.shape
    T = B * S
    L = w_in.shape[0]
    NM = mem.shape[1]
    alpha = (2 * L) ** 0.25
    weights = dict(ln_in_g=ln_in_g, ln_in_b=ln_in_b, w_in=w_in, b_forget=b_forget, w_uq=w_uq, g_cq=g_cq, w_ukv=w_ukv, g_ckv=g_ckv,
                   w_mem_kv=w_mem_kv, w_br=w_br, w_out=w_out, ln1_g=ln1_g, ln1_b=ln1_b, w_ff1=w_ff1, w_ff2=w_ff2, ln2_g=ln2_g, ln2_b=ln2_b)
    mom_m = dict(ln_in_g=m_ln_in_g, ln_in_b=m_ln_in_b, w_in=m_w_in, b_forget=m_b_forget, w_uq=m_w_uq, g_cq=m_g_cq, w_ukv=m_w_ukv,
                 g_ckv=m_g_ckv, w_mem_kv=m_w_mem_kv, w_br=m_w_br, w_out=m_w_out, ln1_g=m_ln1_g, ln1_b=m_ln1_b, w_ff1=m_w_ff1,
                 w_ff2=m_w_ff2, ln2_g=m_ln2_g, ln2_b=m_ln2_b)
    mom_v = dict(ln_in_g=v_ln_in_g, ln_in_b=v_ln_in_b, w_in=v_w_in, b_forget=v_b_forget, w_uq=v_w_uq, g_cq=v_g_cq, w_ukv=v_w_ukv,
                 g_ckv=v_g_ckv, w_mem_kv=v_w_mem_kv, w_br=v_w_br, w_out=v_w_out, ln1_g=v_ln1_g, ln1_b=v_ln1_b, w_ff1=v_w_ff1,
                 w_ff2=v_w_ff2, ln2_g=v_ln2_g, ln2_b=v_ln2_b)
    order = list(weights)
    big_axes = [ax for _, ax in _BIG]
    fwd_map, bwd_map = _in_col_maps(D)
    shard3 = {n: _lane_pad(weights[n]) for n, _ in _BIG}

    half = len(_BIG) // 2

    def gather_rider(l, lo=0, hi=len(_BIG)):
        return _gather_rider([_as3(shard3[n][l]).astype(bf16) for n, _ in _BIG[lo:hi]], big_axes[lo:hi])

    def layer_weights(full):
        W = {n: (f if n == "w_br" else f[0]) for (n, _), f in zip(_BIG, full)}
        W["w_in"] = _lane_permute(W["w_in"], fwd_map, bf16, "w_in_reorder")
        return W

    tabs = _rope_tables(positions)
    x2d, mem2d, tgt2d = x.reshape(T, D), mem.reshape(B * NM, D), loss_target.reshape(T, D)
    fox = dict(B=B, Sq=S, Sk=S, nblk=FOX_HEADS // 2, hb=2, dqk=FOX_HD, dv=FOX_HD, mode="fox", scale=FOX_HD ** -0.5)
    mla = dict(B=B, Sq=S, Sk=S, nblk=MLA_HEADS // 2, hb=2, dqk=LANES, dv=MLA_VD, mode="chunk", scale=(NOPE + ROPE) ** -0.5)
    memat = dict(B=B, Sq=S, Sk=NM, nblk=MEM_HEADS, hb=1, dqk=MEM_HD, dv=MEM_HD, mode="none", scale=MEM_HD ** -0.5)
    nb = FOX_HEADS // 2

    h, hb, xh0, rstd0 = _ln_fwd(None, x2d, ln_in_g, ln_in_b, alpha, "ln_in")
    saved = []
    full = _run_rider(gather_rider(0), "weights_all_gather")
    for l in range(L):
        W = layer_weights(full)
        bpad = jnp.pad(b_forget[l].reshape(1, FOX_HEADS), ((0, 0), (0, LANES - FOX_HEADS)))
        proj = _mm(hb, W["w_in"], name="proj")
        cum = _gate_fwd(proj, bpad, B, S, "gate_fwd")
        cum_r = cum[:, :FOX_HEADS].reshape(B, S, nb, 2).transpose(0, 2, 3, 1).reshape(B * nb, 2, S)
        more = l + 1 < L
        res = _attn_fwd(proj, C_Q // LANES, proj, C_K // LANES, proj, C_V // LANES, cum_r=cum_r, name="fox_fwd",
                        rider=gather_rider(l + 1, 0, half) if more else None, **fox)
        (o_a, lse_a), full_lo = res if more else (res, None)
        cqn = _rms_fwd(proj, C_CQ, Q_RANK, g_cq[l], "rms_q")
        ckvn = _rms_fwd(proj, C_CKV, KV_RANK, g_ckv[l], "rms_kv")
        qraw = _mm(cqn, W["w_uq"], name="q_up")
        kv = _mm(ckvn, W["w_ukv"], name="kv_up")
        qf = _rope_q(qraw, tabs, bwd=False, name="rope_q")
        kf, v_b = _rope_k(kv, tabs, proj, "rope_k")
        res = _attn_fwd(qf, 0, kf, 0, v_b, 0, name="mla_fwd", rider=gather_rider(l + 1, half) if more else None, **mla)
        (o_b, lse_b), full_hi = res if more else (res, None)
        full = list(full_lo) + list(full_hi) if more else None
        mkv = _mm(mem2d, W["w_mem_kv"], name="mem_kv")
        o_c, lse_c = _attn_fwd(proj, C_QM // MEM_HD, mkv, 0, mkv, MEM_HEADS, name="mem_fwd", **memat)
        merged = _merge_fwd(o_a, o_b, o_c, W["w_br"], proj, D, "merge_fwd")
        y = _mm(merged, W["w_out"], name="out_proj")
        h1, h1b, xh1, rstd1 = _ln_fwd(h, y, ln1_g[l], ln1_b[l], alpha, "ln1")
        a, r = _mm(h1b, W["w_ff1"], epi="relu2", name="ff1")
        ff = _mm(a, W["w_ff2"], name="ff2")
        h2, h2b, xh2, rstd2 = _ln_fwd(h1, ff, ln2_g[l], ln2_b[l], alpha, "ln2")
        saved.append(dict(W=W, bpad=bpad, h=hb, proj=proj, cum_r=cum_r, o_a=o_a, lse_a=lse_a, cqn=cqn, ckvn=ckvn, qf=qf, kf=kf,
                          v_b=v_b, o_b=o_b, lse_b=lse_b, mkv=mkv, o_c=o_c, lse_c=lse_c, merged=merged, h1=h1b, xh1=xh1,
                          rstd1=rstd1, a=a, r=r, xh2=xh2, rstd2=rstd2))
        h, hb = h2, h2b

    loss_local, dh = _loss(h, tgt2d, "loss")
    loss = lax.psum(loss_local[0, 0], ("x", "y", "c"))

    shard_shapes = [_as3(shard3[n][0]).shape for n, _ in _BIG]
    slots = [lax.empty((N_CHIP, L) + s, bf16) for s in shard_shapes]
    core = lax.axis_index("c").astype(jnp.int32).reshape(1)
    small = {n: [None] * L for n in ("b_forget", "g_cq", "g_ckv", "ln1_g", "ln1_b", "ln2_g", "ln2_b")}
    pending = unpaired = None
    for l in reversed(range(L)):
        sv = saved[l]
        W = sv["W"]
        du2, dg2, db2 = _ln_bwd(dh, sv["xh2"], sv["rstd2"], ln2_g[l], "ln2_bwd")
        dz = _mm(du2, W["w_ff2"], mode="nt", epi="mul", extra=sv["r"], out_dtype=bf16, name="ff2_dx")
        dw_ff2 = _mm(sv["a"], du2, mode="tn", name="ff2_dw")
        dh1 = _mm(dz, W["w_ff1"], mode="nt", epi="add", extra=du2, alpha=alpha, name="ff1_dx")
        dw_ff1 = _mm(sv["h1"], dz, mode="tn", name="ff1_dw")
        du1, dg1, db1 = _ln_bwd(dh1, sv["xh1"], sv["rstd1"], ln1_g[l], "ln1_bwd")
        dmerged = _mm(du1, W["w_out"], mode="nt", name="out_dx")
        dw_out = _mm(sv["merged"], du1, mode="tn", name="out_dw")
        mb = _merge_bwd(dmerged, sv["o_a"], sv["o_b"], sv["o_c"], W["w_br"], sv["proj"], D, "merge_bwd",
                        rider=None if unpaired is None else _pair_exchange_rider(unpaired[0], big_axes, shard_shapes))
        if unpaired is not None:
            mb, bufs = mb
            pending = ([_pair_sum(g, b, ax, core, "grads_pair_sum") for g, b, ax in zip(unpaired[0], bufs, big_axes)], unpaired[1])
        dgl, do, dbp = mb[0:3], mb[3:6], mb[6:9]
        dw_br = jnp.stack([_mm(o_n, dbp_n, mode="tn", name="br_dw") for o_n, dbp_n in zip((sv["o_a"], sv["o_b"], sv["o_c"]), dbp)])
        proj = sv["proj"]
        res = _attn_bwd(proj, C_Q // LANES, proj, C_K // LANES, proj, C_V // LANES, sv["o_a"], do[0], sv["lse_a"],
                        cum_r=sv["cum_r"], name="fox_bwd",
                        rider=None if pending is None else _chip_exchange_rider(pending[0], slots, pending[1]), **fox)
        (dq_a, dk_a, dv_a, dcum), slots = (res, slots) if pending is None else res
        dcum = dcum.reshape(B, nb, S, 2).transpose(0, 2, 1, 3).reshape(T, FOX_HEADS)
        dcum = jnp.pad(dcum, ((0, 0), (0, LANES - FOX_HEADS)))
        dzf, dbf = _gate_bwd(dcum, proj, sv["bpad"], B, S, "gate_bwd")
        dqf, dkf, dv_b = _attn_bwd(sv["qf"], 0, sv["kf"], 0, sv["v_b"], 0, sv["o_b"], do[1], sv["lse_b"], dq_dtype=f32,
                                   name="mla_bwd", **mla)
        dqraw = _rope_q(dqf, tabs, bwd=True, name="rope_q_bwd")
        dkv, dkpe = _rope_k_bwd(dkf, dv_b, tabs, "rope_k_bwd")
        dw_uq = _mm(sv["cqn"], dqraw, mode="tn", name="q_up_dw")
        dcqn = _mm(dqraw, W["w_uq"], mode="nt", name="q_up_dx")
        dcq_, dg_cq = _rms_bwd(dcqn, proj, C_CQ, Q_RANK, g_cq[l], "rms_q_bwd")
        dw_ukv = _mm(sv["ckvn"], dkv, mode="tn", name="kv_up_dw")
        dckvn = _mm(dkv, W["w_ukv"], mode="nt", name="kv_up_dx")
        dckv, dg_ckv = _rms_bwd(dckvn, proj, C_CKV, KV_RANK, g_ckv[l], "rms_kv_bwd")
        dqm, dmk, dmv = _attn_bwd(proj, C_QM // MEM_HD, sv["mkv"], 0, sv["mkv"], MEM_HEADS, sv["o_c"], do[2], sv["lse_c"],
                                  name="mem_bwd", **memat)
        dw_mem = _mm(mem2d, jnp.concatenate([dmk, dmv], axis=1), mode="tn", name="mem_kv_dw")
        dproj = jnp.concatenate([dq_a, dk_a, dv_a, dzf, dkpe, dckv, dqm, jnp.zeros((T, C_CQ - C_PAD), bf16), dcq_, *dgl], axis=1)
        dw_in = _lane_permute(_mm(sv["h"], dproj, mode="tn", name="proj_dw"), bwd_map, f32, "w_in_grad_reorder")
        dh = _mm(dproj, W["w_in"], mode="nt", epi="add", extra=du1, alpha=alpha, name="proj_dx")
        grads_l = dict(w_in=dw_in, w_uq=dw_uq, w_ukv=dw_ukv, w_mem_kv=dw_mem, w_br=dw_br, w_out=dw_out, w_ff1=dw_ff1, w_ff2=dw_ff2)
        unpaired = ([_as3(grads_l[n]) for n, _ in _BIG], l)
        for n, val in (("b_forget", dbf[:, :FOX_HEADS]), ("g_cq", dg_cq), ("g_ckv", dg_ckv), ("ln1_g", dg1), ("ln1_b", db1),
                       ("ln2_g", dg2), ("ln2_b", db2)):
            small[n][l] = val.reshape(-1)
    bufs = _run_rider(_pair_exchange_rider(unpaired[0], big_axes, shard_shapes), "grads_pair_exchange")
    chips = [_pair_sum(g, b, ax, core, "grads_pair_sum") for g, b, ax in zip(unpaired[0], bufs, big_axes)]
    slots = _run_rider(_chip_exchange_rider(chips, slots, unpaired[1]), "grads_chip_exchange")
    grad_x, dg_in, db_in = _ln_bwd(dh, xh0, rstd0, ln_in_g, "ln_in_bwd")

    small_list = [("ln_in_g", dg_in.reshape(-1)), ("ln_in_b", db_in.reshape(-1))] + [(n, jnp.stack(v).reshape(-1)) for n, v in small.items()]
    sflat = jnp.concatenate([v for _, v in small_list])
    n_small = sflat.shape[0]
    rs = _round_up(-(-n_small // LANES), 8)
    spacked = jnp.pad(sflat, (0, rs * LANES - n_small)).reshape(1, rs, LANES)
    sparts = _run_rider(_gather_rider([spacked], [1]), "small_all_gather")[0].reshape(N_DEV, rs * LANES)

    grads, deltas, new_m, new_v = {}, {}, {}, {}
    off = 0
    for n, vv in small_list:
        parts = sparts[:, off:off + vv.shape[0]].reshape((N_DEV,) + weights[n].shape)
        off += vv.shape[0]
        grads[n], deltas[n], new_m[n], new_v[n] = _adamw(weights[n], mom_m[n], mom_v[n], parts, "adamw_" + n)
    for (n, _), sl in zip(_BIG, slots):
        parts = sl.reshape((N_CHIP,) + weights[n].shape[:-1] + (sl.shape[-1],))
        grads[n], deltas[n], new_m[n], new_v[n] = _adamw(weights[n], mom_m[n], mom_v[n], parts, "adamw_" + n)
    return (loss, grad_x.reshape(B, S, D), *[grads[n] for n in order], *[deltas[n] for n in order], *[new_m[n] for n in order],
            *[new_v[n] for n in order])
```

```python
import functools

import numpy as np
import jax
import jax.numpy as jnp
from jax import lax
from jax.experimental import pallas as pl
from jax.experimental.pallas import tpu as pltpu

f32, bf16 = jnp.float32, jnp.bfloat16
SDS = jax.ShapeDtypeStruct

N_DEV = 8
MESH = pl.DeviceIdType.MESH
LANES = 128
VMEM_LIMIT = 56 * 1024 * 1024

FOX_HEADS, FOX_HD = 8, 64
MLA_HEADS, NOPE, ROPE, MLA_VD = 8, 64, 32, 64
Q_RANK, KV_RANK = 384, 256
ROPE_BASE = 10000.0
MEM_HEADS, MEM_HD = 4, 128
BW = 512
CHUNK = 64
LN_EPS, RMS_EPS, NEG_INF = 1e-5, 1e-6, -1e30
ADAM_LR, ADAM_B1, ADAM_B2, ADAM_EPS, ADAM_WD, ADAM_STEP = 0.001, 0.9, 0.999, 1e-08, 0.01, 10

ATT_TQ, ATT_TK = 512, 512

C_Q, C_K, C_V, C_F, C_KPE, C_CKV, C_QM, C_PAD, C_CQ, C_GATE = 0, 512, 1024, 1536, 1664, 1792, 2048, 2560, 2688, 3072
O_F, O_CQ, O_CKV, O_KR, O_QM, O_GATE = 1536, 1544, 1928, 2184, 2216, 2728


def _pick(dim, pref):
    if dim <= pref:
        return dim
    for c in (1024, 768, 512, 384, 256, 128, 64, 32, 16, 8):
        if c <= pref and dim % c == 0:
            return c
    return dim


def _round_up(n, m):
    return -(-n // m) * m


def _params(sem=None):
    return pltpu.CompilerParams(dimension_semantics=sem, vmem_limit_bytes=VMEM_LIMIT)


_DIMS = {"nn": (((1,), (0,)), ((), ())), "nt": (((1,), (1,)), ((), ())), "tn": (((0,), (0,)), ((), ()))}


def _mm(a, b, *, mode="nn", out_dtype=f32, epi="none", extra=None, alpha=1.0, tm=1024, tn=1024, tk=None, name):
    if mode == "nn":
        (M, K), (_, N) = a.shape, b.shape
    elif mode == "nt":
        (M, K), (N, _) = a.shape, b.shape
    else:
        (K, M), (_, N) = a.shape, b.shape
    tm, tn, tk = _pick(M, tm), _pick(N, tn), _pick(K, tk or (2048 if mode == "tn" else 1024))
    nk = K // tk
    a_spec = pl.BlockSpec((tk, tm), lambda i, j, k: (k, i)) if mode == "tn" else pl.BlockSpec((tm, tk), lambda i, j, k: (i, k))
    b_spec = pl.BlockSpec((tn, tk), lambda i, j, k: (j, k)) if mode == "nt" else pl.BlockSpec((tk, tn), lambda i, j, k: (k, j))
    o_spec = pl.BlockSpec((tm, tn), lambda i, j, k: (i, j))
    n_out = 2 if epi == "relu2" else 1
    dims = _DIMS[mode]

    def body(*refs):
        a_ref, b_ref = refs[0], refs[1]
        e_ref = refs[2] if extra is not None else None
        outs = refs[2 + (extra is not None):2 + (extra is not None) + n_out]

        def finish(r):
            if epi == "none":
                outs[0][...] = r.astype(out_dtype)
            elif epi == "relu2":
                p = jnp.maximum(r, 0.0)
                outs[0][...] = (p * p).astype(bf16)
                outs[1][...] = (2.0 * p).astype(bf16)
            elif epi == "mul":
                outs[0][...] = (r * e_ref[...].astype(f32)).astype(out_dtype)
            else:
                outs[0][...] = (r + alpha * e_ref[...].astype(f32)).astype(out_dtype)

        part = lax.dot_general(a_ref[...].astype(bf16), b_ref[...].astype(bf16), dims, preferred_element_type=f32)
        if nk == 1:
            finish(part)
            return
        acc = refs[-1]
        k = pl.program_id(2)

        @pl.when(k == 0)
        def _():
            acc[...] = part

        @pl.when(k > 0)
        def _():
            acc[...] += part

        @pl.when(k == nk - 1)
        def _():
            finish(acc[...])

    ins, in_specs = [a, b], [a_spec, b_spec]
    if extra is not None:
        ins.append(extra)
        in_specs.append(o_spec)
    if epi == "relu2":
        out_shape, out_specs = [SDS((M, N), bf16), SDS((M, N), bf16)], [o_spec, o_spec]
    else:
        out_shape, out_specs = SDS((M, N), out_dtype), o_spec
    return pl.pallas_call(
        body, grid=(M // tm, N // tn, nk), in_specs=in_specs, out_specs=out_specs, out_shape=out_shape,
        scratch_shapes=[pltpu.VMEM((tm, tn), f32)] if nk > 1 else [], name=name,
        compiler_params=_params(("parallel", "parallel", "arbitrary")))(*ins)


def _lane_permute(src, cmap, out_dtype, name):
    R, Cs = src.shape
    cmap = np.asarray(cmap, np.int64)
    nb = cmap.shape[0] // LANES
    srcs = [sorted({int(c) // LANES for c in cmap[b * LANES:(b + 1) * LANES] if c >= 0}) for b in range(nb)]
    exact = src.dtype == f32
    tr = _pick(R, 256 if exact else 512)

    def body(src_ref, cm_ref, o_ref):
        rows = lax.broadcasted_iota(jnp.int32, (LANES, LANES), 0)
        for b in range(nb):
            tbl = cm_ref[:, b * LANES:(b + 1) * LANES]
            acc = jnp.zeros((tr, LANES), f32)
            for sb in srcs[b]:
                sel = (rows + sb * LANES) == tbl
                blk = src_ref[:, sb * LANES:(sb + 1) * LANES]
                if exact:
                    acc = acc + jnp.dot(blk, sel.astype(f32), precision=lax.Precision.HIGHEST, preferred_element_type=f32)
                else:
                    acc = acc + jnp.dot(blk, sel.astype(blk.dtype), preferred_element_type=f32)
            o_ref[:, b * LANES:(b + 1) * LANES] = acc.astype(out_dtype)

    return pl.pallas_call(
        body, grid=(R // tr,), in_specs=[pl.BlockSpec((tr, Cs), lambda i: (i, 0)), pl.BlockSpec((1, nb * LANES), lambda i: (0, 0))],
        out_specs=pl.BlockSpec((tr, nb * LANES), lambda i: (i, 0)), out_shape=SDS((R, nb * LANES), out_dtype), name=name,
        compiler_params=_params(("parallel",)))(src, jnp.asarray(cmap.astype(np.int32)).reshape(1, -1))


def _ln_fwd(h, y, g, b, alpha, name):
    T, D = y.shape
    tr = _pick(T, 256)
    row = pl.BlockSpec((tr, D), lambda i: (i, 0))
    vec = pl.BlockSpec((1, D), lambda i: (0, 0))
    col = pl.BlockSpec((tr, 1), lambda i: (i, 0))

    def body(*refs):
        if h is None:
            y_ref, g_ref, b_ref, o_ref, ob_ref, xh_ref, rs_ref = refs
            u = y_ref[...]
        else:
            h_ref, y_ref, g_ref, b_ref, o_ref, ob_ref, xh_ref, rs_ref = refs
            u = alpha * h_ref[...] + y_ref[...]
        mu = jnp.mean(u, axis=-1, keepdims=True)
        d = u - mu
        var = jnp.mean(d * d, axis=-1, keepdims=True)
        rstd = lax.rsqrt(var + LN_EPS)
        xh = d * rstd
        xh_ref[...] = xh
        rs_ref[...] = rstd
        o = xh * g_ref[...] + b_ref[...]
        o_ref[...] = o
        ob_ref[...] = o.astype(bf16)

    ins = ([] if h is None else [h]) + [y, g.reshape(1, D), b.reshape(1, D)]
    specs = ([] if h is None else [row]) + [row, vec, vec]
    return pl.pallas_call(
        body, grid=(T // tr,), in_specs=specs, out_specs=[row, row, row, col],
        out_shape=[SDS((T, D), f32), SDS((T, D), bf16), SDS((T, D), f32), SDS((T, 1), f32)], name=name,
        compiler_params=_params(("parallel",)))(*ins)


def _ln_bwd(dy, xh, rstd, g, name):
    T, D = dy.shape
    tr = _pick(T, 256)
    row = pl.BlockSpec((tr, D), lambda i: (i, 0))
    vec = pl.BlockSpec((1, D), lambda i: (0, 0))
    col = pl.BlockSpec((tr, 1), lambda i: (i, 0))

    def body(dy_ref, xh_ref, rs_ref, g_ref, du_ref, dub_ref, dg_ref, db_ref):
        @pl.when(pl.program_id(0) == 0)
        def _():
            dg_ref[...] = jnp.zeros_like(dg_ref)
            db_ref[...] = jnp.zeros_like(db_ref)

        dyv, xhv = dy_ref[...], xh_ref[...]
        dxh = dyv * g_ref[...]
        m1 = jnp.mean(dxh, axis=-1, keepdims=True)
        m2 = jnp.mean(dxh * xhv, axis=-1, keepdims=True)
        du = rs_ref[...] * (dxh - m1 - xhv * m2)
        du_ref[...] = du
        dub_ref[...] = du.astype(bf16)
        dg_ref[...] += jnp.sum(dyv * xhv, axis=0, keepdims=True)
        db_ref[...] += jnp.sum(dyv, axis=0, keepdims=True)

    return pl.pallas_call(
        body, grid=(T // tr,), in_specs=[row, row, col, vec], out_specs=[row, row, vec, vec],
        out_shape=[SDS((T, D), f32), SDS((T, D), bf16), SDS((1, D), f32), SDS((1, D), f32)], name=name,
        compiler_params=_params(("arbitrary",)))(dy, xh, rstd, g.reshape(1, D))


def _rms_fwd(proj, col_off, R, g, name):
    T = proj.shape[0]
    tr = _pick(T, 512)
    cb = col_off // R

    def body(c_ref, g_ref, o_ref):
        c = c_ref[...]
        r = lax.rsqrt(jnp.mean(c * c, axis=-1, keepdims=True) + RMS_EPS)
        o_ref[...] = (c * r * g_ref[...]).astype(bf16)

    return pl.pallas_call(
        body, grid=(T // tr,), in_specs=[pl.BlockSpec((tr, R), lambda i: (i, cb)), pl.BlockSpec((1, R), lambda i: (0, 0))],
        out_specs=pl.BlockSpec((tr, R), lambda i: (i, 0)), out_shape=SDS((T, R), bf16), name=name,
        compiler_params=_params(("parallel",)))(proj, g.reshape(1, R))


def _rms_bwd(dy, proj, col_off, R, g, name):
    T = proj.shape[0]
    tr = _pick(T, 512)
    cb = col_off // R

    def body(dy_ref, c_ref, g_ref, dc_ref, dg_ref):
        @pl.when(pl.program_id(0) == 0)
        def _():
            dg_ref[...] = jnp.zeros_like(dg_ref)

        c, dyv = c_ref[...], dy_ref[...]
        r = lax.rsqrt(jnp.mean(c * c, axis=-1, keepdims=True) + RMS_EPS)
        t = dyv * g_ref[...]
        mt = jnp.mean(t * c, axis=-1, keepdims=True)
        dc_ref[...] = (r * t - c * (r * r * r) * mt).astype(bf16)
        dg_ref[...] += jnp.sum(dyv * c * r, axis=0, keepdims=True)

    return pl.pallas_call(
        body, grid=(T // tr,),
        in_specs=[pl.BlockSpec((tr, R), lambda i: (i, 0)), pl.BlockSpec((tr, R), lambda i: (i, cb)), pl.BlockSpec((1, R), lambda i: (0, 0))],
        out_specs=[pl.BlockSpec((tr, R), lambda i: (i, 0)), pl.BlockSpec((1, R), lambda i: (0, 0))],
        out_shape=[SDS((T, R), bf16), SDS((1, R), f32)], name=name,
        compiler_params=_params(("arbitrary",)))(dy, proj, g.reshape(1, R))


def _tri(n, lower):
    r = lax.broadcasted_iota(jnp.int32, (n, n), 0)
    c = lax.broadcasted_iota(jnp.int32, (n, n), 1)
    return ((r >= c) if lower else (c >= r)).astype(f32)


def _gate_fwd(proj, bpad, B, S, name):
    ch = _pick(S, 256)
    nch = S // ch
    cb = C_F // LANES

    def body(f_ref, b_ref, o_ref):
        tri = _tri(ch, True)
        carry = jnp.zeros((1, LANES), f32)
        for c in range(nch):
            z = f_ref[c * ch:(c + 1) * ch, :] + b_ref[...]
            lf = jnp.minimum(z, 0.0) - jnp.log1p(jnp.exp(-jnp.abs(z)))
            cs = jnp.dot(tri, lf, precision=lax.Precision.HIGHEST, preferred_element_type=f32) + carry
            o_ref[c * ch:(c + 1) * ch, :] = cs
            carry = cs[ch - 1:ch, :]

    return pl.pallas_call(
        body, grid=(B,), in_specs=[pl.BlockSpec((S, LANES), lambda b: (b, cb)), pl.BlockSpec((1, LANES), lambda b: (0, 0))],
        out_specs=pl.BlockSpec((S, LANES), lambda b: (b, 0)), out_shape=SDS((B * S, LANES), f32), name=name,
        compiler_params=_params(("parallel",)))(proj, bpad)


def _gate_bwd(dcum, proj, bpad, B, S, name):
    ch = _pick(S, 256)
    nch = S // ch
    cb = C_F // LANES

    def body(d_ref, f_ref, b_ref, dz_ref, db_ref):
        @pl.when(pl.program_id(0) == 0)
        def _():
            db_ref[...] = jnp.zeros_like(db_ref)

        tri = _tri(ch, False)
        carry = jnp.zeros((1, LANES), f32)
        dbs = jnp.zeros((1, LANES), f32)
        for c in reversed(range(nch)):
            d = d_ref[c * ch:(c + 1) * ch, :]
            dlf = jnp.dot(tri, d, precision=lax.Precision.HIGHEST, preferred_element_type=f32) + carry
            carry = carry + jnp.sum(d, axis=0, keepdims=True)
            z = f_ref[c * ch:(c + 1) * ch, :] + b_ref[...]
            dz = dlf / (1.0 + jnp.exp(z))
            dz_ref[c * ch:(c + 1) * ch, :] = dz.astype(bf16)
            dbs = dbs + jnp.sum(dz, axis=0, keepdims=True)
        db_ref[...] += dbs

    return pl.pallas_call(
        body, grid=(B,),
        in_specs=[pl.BlockSpec((S, LANES), lambda b: (b, 0)), pl.BlockSpec((S, LANES), lambda b: (b, cb)), pl.BlockSpec((1, LANES), lambda b: (0, 0))],
        out_specs=[pl.BlockSpec((S, LANES), lambda b: (b, 0)), pl.BlockSpec((1, LANES), lambda b: (0, 0))],
        out_shape=[SDS((B * S, LANES), bf16), SDS((1, LANES), f32)], name=name,
        compiler_params=_params(("arbitrary",)))(dcum, proj, bpad)


def _rot(v, c, sa, sb):
    return v * c + pltpu.roll(v, LANES - 16, 1) * sa + pltpu.roll(v, 16, 1) * sb


def _rot_t(v, c, sa, sb):
    return v * c + pltpu.roll(v * sa, 16, 1) + pltpu.roll(v * sb, LANES - 16, 1)


def _rope_q(x, tabs, *, bwd, name):
    T = x.shape[0]
    W = MLA_HEADS * LANES
    tr = _pick(T, 512)
    fn = _rot_t if bwd else _rot

    def body(x_ref, c_ref, sa_ref, sb_ref, o_ref):
        c, sa, sb = c_ref[...], sa_ref[...], sb_ref[...]
        for g in range(MLA_HEADS):
            o_ref[:, g * LANES:(g + 1) * LANES] = fn(x_ref[:, g * LANES:(g + 1) * LANES], c, sa, sb).astype(bf16)

    tab = pl.BlockSpec((tr, LANES), lambda i: (i, 0))
    wide = pl.BlockSpec((tr, W), lambda i: (i, 0))
    return pl.pallas_call(body, grid=(T // tr,), in_specs=[wide, tab, tab, tab], out_specs=wide, out_shape=SDS((T, W), bf16),
                          name=name, compiler_params=_params(("parallel",)))(x, *tabs)


def _rope_k(kv, tabs, proj, name):
    T = kv.shape[0]
    W = MLA_HEADS * LANES
    tr = _pick(T, 512)
    kcb = C_KPE // LANES

    def body(x_ref, c_ref, sa_ref, sb_ref, kpe_ref, k_ref, v_ref):
        c, sa, sb, kpe = c_ref[...], sa_ref[...], sb_ref[...], kpe_ref[...]
        low = lax.broadcasted_iota(jnp.int32, (tr, LANES), 1) < NOPE
        halves = []
        for g in range(MLA_HEADS):
            x = x_ref[:, g * LANES:(g + 1) * LANES]
            k_ref[:, g * LANES:(g + 1) * LANES] = _rot(jnp.where(low, x, 0.0) + kpe, c, sa, sb).astype(bf16)
            halves.append(x[:, NOPE:])
            if g % 2 == 1:
                v_ref[:, (g // 2) * LANES:(g // 2 + 1) * LANES] = jnp.concatenate(halves, axis=1).astype(bf16)
                halves = []

    tab = pl.BlockSpec((tr, LANES), lambda i: (i, 0))
    wide = pl.BlockSpec((tr, W), lambda i: (i, 0))
    return pl.pallas_call(
        body, grid=(T // tr,), in_specs=[wide, tab, tab, tab, pl.BlockSpec((tr, LANES), lambda i: (i, kcb))],
        out_specs=[wide, pl.BlockSpec((tr, W // 2), lambda i: (i, 0))], out_shape=[SDS((T, W), bf16), SDS((T, W // 2), bf16)],
        name=name, compiler_params=_params(("parallel",)))(kv, *tabs, proj)


def _rope_k_bwd(dk, dv, tabs, name):
    T = dk.shape[0]
    W = MLA_HEADS * LANES
    tr = _pick(T, 512)

    def body(dk_ref, dv_ref, c_ref, sa_ref, sb_ref, o_ref, s_ref):
        c, sa, sb = c_ref[...], sa_ref[...], sb_ref[...]
        tot = jnp.zeros((tr, LANES), f32)
        for g in range(MLA_HEADS):
            d = _rot_t(dk_ref[:, g * LANES:(g + 1) * LANES], c, sa, sb)
            tot = tot + d
            o_ref[:, g * LANES:(g + 1) * LANES] = jnp.concatenate(
                [d[:, :NOPE].astype(bf16), dv_ref[:, g * MLA_VD:(g + 1) * MLA_VD]], axis=1)
        s_ref[...] = tot.astype(bf16)

    tab = pl.BlockSpec((tr, LANES), lambda i: (i, 0))
    wide = pl.BlockSpec((tr, W), lambda i: (i, 0))
    return pl.pallas_call(
        body, grid=(T // tr,), in_specs=[wide, pl.BlockSpec((tr, W // 2), lambda i: (i, 0)), tab, tab, tab],
        out_specs=[wide, tab], out_shape=[SDS((T, W), bf16), SDS((T, LANES), bf16)], name=name,
        compiler_params=_params(("parallel",)))(dk, dv, *tabs)


def _tile_mask(mode, i, j, tq, tk):
    r = i * tq + lax.broadcasted_iota(jnp.int32, (tq, tk), 0)
    c = j * tk + lax.broadcasted_iota(jnp.int32, (tq, tk), 1)
    if mode == "chunk":
        return (c // CHUNK) <= (r // CHUNK)
    return c <= r


def _attn_tiles(Sq, Sk):
    tk = _pick(Sk, ATT_TK)
    tq = _pick(Sq, ATT_TQ)
    assert (tk % tq == 0 or tq % tk == 0) and tq % CHUNK == 0 and tk % CHUNK == 0
    return tq, tk


def _stage_heads(dst, src_ref, hb, w, mark=None, val=1.0):
    S = src_ref.shape[0]
    ch = _pick(S, 256)
    assert w <= LANES and (mark is None or w <= mark < LANES)

    def step(c, _):
        rows = pl.ds(pl.multiple_of(c * ch, ch), ch)
        for hh in range(hb):
            x = src_ref[rows, hh * w:(hh + 1) * w].astype(bf16)
            if w < LANES:
                pad = jnp.zeros((ch, LANES - w), bf16)
                if mark is not None:
                    lane = lax.broadcasted_iota(jnp.int32, (ch, LANES - w), 1)
                    pad = jnp.where(lane == mark - w, val, 0.0).astype(bf16)
                x = jnp.concatenate([x, pad], axis=1)
            dst[hh, rows, :] = x
        return 0

    lax.fori_loop(0, S // ch, step, 0)


def _carry(rider, refs, n_in, n_out, n_scr, grid):
    if rider is None:
        return refs
    ri, ro = len(rider.inputs), len(rider.out_shapes)
    own = refs[:n_in] + refs[n_in + ri:n_in + ri + n_out] + refs[n_in + ri + n_out + ro:n_in + ri + n_out + ro + n_scr]
    rider.refs = (refs[n_in:n_in + ri], refs[n_in + ri + n_out:n_in + ri + n_out + ro], refs[n_in + ri + n_out + ro + n_scr:])
    first = functools.reduce(jnp.logical_and, [pl.program_id(a) == 0 for a in range(len(grid))])

    @pl.when(first)
    def _():
        rider.start(*rider.refs)

    return own


def _carry_finish(rider, grid):
    if rider is None:
        return
    last = functools.reduce(jnp.logical_and, [pl.program_id(a) == n - 1 for a, n in enumerate(grid)])

    @pl.when(last)
    def _():
        rider.finish(*rider.refs)


def _carrier_call(body, rider, ins, in_specs, out_shape, out_specs, scratch, grid, name):
    if rider is None:
        return pl.pallas_call(body, grid=grid, in_specs=in_specs, out_specs=out_specs, out_shape=out_shape, scratch_shapes=scratch,
                              name=name, compiler_params=_params(("parallel",) * len(grid)))(*ins)
    any_spec = pl.BlockSpec(memory_space=pl.ANY)
    ri, ro = len(rider.inputs), len(rider.out_shapes)
    res = pl.pallas_call(
        body, grid=grid, in_specs=list(in_specs) + [any_spec] * ri, out_specs=list(out_specs) + [any_spec] * ro,
        out_shape=list(out_shape) + rider.out_shapes, scratch_shapes=list(scratch) + rider.scratch,
        input_output_aliases={len(ins) + i: len(out_shape) + o for i, o in rider.aliases.items()}, name=name,
        compiler_params=_params(("arbitrary",) * len(grid)))(*ins, *rider.inputs)
    return res[:len(out_shape)], res[len(out_shape):]


def _attn_fwd(q_arr, q_cb, k_arr, k_cb, v_arr, v_cb, *, B, Sq, Sk, nblk, hb, dqk, dv, mode, scale, cum_r=None, rider=None, name):
    wq, wv = hb * dqk, hb * dv
    tq, tk = _attn_tiles(Sq, Sk)
    fox = mode == "fox"
    n_in = 4 if fox else 3
    ones = dv < LANES

    def body(*refs):
        refs = _carry(rider, refs, n_in, 2, 3, (B, nblk))
        q_ref, k_ref, v_ref = refs[:3]
        cr_ref = refs[3] if fox else None
        o_ref, lse_ref, qh, kh, vh = refs[n_in:]
        _stage_heads(qh, q_ref, hb, dqk)
        _stage_heads(kh, k_ref, hb, dqk)
        _stage_heads(vh, v_ref, hb, dv, dv if ones else None)

        def qstep(i, _):
            rows = pl.ds(pl.multiple_of(i * tq, tq), tq)
            qs = [qh[hh, rows, :] for hh in range(hb)]

            def tile(j, carry, diag):
                cols = pl.ds(pl.multiple_of(j * tk, tk), tk)
                out = []
                for hh in range(hb):
                    m, l, acc = carry[hh]
                    s = lax.dot_general(qs[hh], kh[hh, cols, :], _DIMS["nt"], preferred_element_type=f32) * scale
                    if fox:
                        s = s - cr_ref[0, hh:hh + 1, cols]
                    if diag:
                        s = jnp.where(_tile_mask(mode, i, j, tq, tk), s, NEG_INF)
                    m_new = jnp.maximum(m, jnp.max(s, axis=-1, keepdims=True))
                    a = jnp.exp(m - m_new)
                    p = jnp.exp(s - m_new)
                    if not ones:
                        l = a * l + jnp.sum(p, axis=-1, keepdims=True)
                    acc = a * acc + lax.dot_general(p.astype(bf16), vh[hh, cols, :], _DIMS["nn"], preferred_element_type=f32)
                    out.append((m_new, l, acc))
                return tuple(out)

            carry = tuple((jnp.full((tq, 1), -jnp.inf, f32), jnp.zeros((tq, 1), f32), jnp.zeros((tq, LANES), f32)) for _ in range(hb))
            if mode == "none":
                carry = lax.fori_loop(0, Sk // tk, functools.partial(tile, diag=False), carry)
            else:
                nfull = (i * tq) // tk
                carry = lax.fori_loop(0, nfull, functools.partial(tile, diag=False), carry)
                for d in range(max(1, tq // tk)):
                    carry = tile(nfull + d, carry, True)
            for hh in range(hb):
                m, l, acc = carry[hh]
                if ones:
                    l = acc[:, dv:dv + 1]
                o_ref[rows, hh * dv:(hh + 1) * dv] = acc[:, :dv] / l
                lse_ref[0, rows, hh:hh + 1] = m + jnp.log(l)
            return 0

        lax.fori_loop(0, Sq // tq, qstep, 0)
        _carry_finish(rider, (B, nblk))

    in_specs = [pl.BlockSpec((Sq, wq), lambda b, p: (b, q_cb + p)), pl.BlockSpec((Sk, wq), lambda b, p: (b, k_cb + p)),
                pl.BlockSpec((Sk, wv), lambda b, p: (b, v_cb + p))]
    ins = [q_arr, k_arr, v_arr]
    if fox:
        in_specs.append(pl.BlockSpec((1, hb, Sk), lambda b, p: (b * nblk + p, 0, 0)))
        ins.append(cum_r)
    return _carrier_call(
        body, rider, ins, in_specs,
        [SDS((B * Sq, nblk * wv), f32), SDS((B * nblk, Sq, hb), f32)],
        [pl.BlockSpec((Sq, wv), lambda b, p: (b, p)), pl.BlockSpec((1, Sq, hb), lambda b, p: (b * nblk + p, 0, 0))],
        [pltpu.VMEM((hb, Sq, LANES), bf16), pltpu.VMEM((hb, Sk, LANES), bf16), pltpu.VMEM((hb, Sk, LANES), bf16)],
        (B, nblk), name)


def _attn_bwd(q_arr, q_cb, k_arr, k_cb, v_arr, v_cb, o, do, lse, *, B, Sq, Sk, nblk, hb, dqk, dv, mode, scale, cum_r=None,
              dq_dtype=bf16, rider=None, name):
    wq, wv = hb * dqk, hb * dv
    tq, tk = _attn_tiles(Sq, Sk)
    fox = mode == "fox"
    n_in = 7 if fox else 6
    n_out = 4 if fox else 3
    if fox:
        assert dqk + 2 <= LANES and np.log2(scale) == np.round(np.log2(scale))

    def body(*refs):
        refs = _carry(rider, refs, n_in, n_out, 7 if fox else 6, (B, nblk))
        q_ref, k_ref, v_ref, o_ref, do_ref, lse_ref = refs[:6]
        cr_ref = refs[6] if fox else None
        dq_ref, dk_ref, dv_ref = refs[n_in:n_in + 3]
        dcum_ref = refs[n_in + 3] if fox else None
        qh, kh, vh, doh, dk_acc, dv_acc = refs[n_in + n_out:n_in + n_out + 6]
        dcq_acc = refs[n_in + n_out + 6] if fox else None
        _stage_heads(qh, q_ref, hb, dqk, dqk + 1 if fox else None, 1.0 / scale)
        _stage_heads(kh, k_ref, hb, dqk, dqk if fox else None, 1.0 / scale)
        _stage_heads(vh, v_ref, hb, dv)
        _stage_heads(doh, do_ref, hb, dv)
        dk_acc[...] = jnp.zeros_like(dk_acc)
        dv_acc[...] = jnp.zeros_like(dv_acc)

        def qstep(i, _):
            rows = pl.ds(pl.multiple_of(i * tq, tq), tq)
            qs = [qh[hh, rows, :] for hh in range(hb)]
            dos = [doh[hh, rows, :] for hh in range(hb)]
            dds = [jnp.sum(do_ref[rows, hh * dv:(hh + 1) * dv] * o_ref[rows, hh * dv:(hh + 1) * dv], axis=-1, keepdims=True)
                   for hh in range(hb)]
            lses = [lse_ref[0, rows, hh:hh + 1] for hh in range(hb)]

            def tile(j, carry, diag):
                cols = pl.ds(pl.multiple_of(j * tk, tk), tk)
                out = []
                for hh in range(hb):
                    k = kh[hh, cols, :]
                    s = lax.dot_general(qs[hh], k, _DIMS["nt"], preferred_element_type=f32) * scale
                    if fox:
                        s = s - cr_ref[0, hh:hh + 1, cols]
                    if diag:
                        s = jnp.where(_tile_mask(mode, i, j, tq, tk), s, NEG_INF)
                    p = jnp.exp(s - lses[hh])
                    dv_acc[hh, cols, :] += lax.dot_general(p.astype(bf16), dos[hh], _DIMS["tn"], preferred_element_type=f32)
                    dp = lax.dot_general(dos[hh], vh[hh, cols, :], _DIMS["nt"], preferred_element_type=f32)
                    ds = p * (dp - dds[hh])
                    dsb = (ds * scale).astype(bf16)
                    out.append(carry[hh] + lax.dot_general(dsb, k, _DIMS["nn"], preferred_element_type=f32))
                    dk_acc[hh, cols, :] += lax.dot_general(dsb, qs[hh], _DIMS["tn"], preferred_element_type=f32)
                return tuple(out)

            carry = tuple(jnp.zeros((tq, LANES), f32) for _ in range(hb))
            if mode == "none":
                carry = lax.fori_loop(0, Sk // tk, functools.partial(tile, diag=False), carry)
            else:
                nfull = (i * tq) // tk
                carry = lax.fori_loop(0, nfull, functools.partial(tile, diag=False), carry)
                for d in range(max(1, tq // tk)):
                    carry = tile(nfull + d, carry, True)
            for hh in range(hb):
                dq_ref[rows, hh * dqk:(hh + 1) * dqk] = carry[hh][:, :dqk].astype(dq_dtype)
                if fox:
                    dcq_acc[rows, hh:hh + 1] = carry[hh][:, dqk:dqk + 1]
            return 0

        lax.fori_loop(0, Sq // tq, qstep, 0)
        for hh in range(hb):
            dk_ref[:, hh * dqk:(hh + 1) * dqk] = dk_acc[hh, :, :dqk].astype(dq_dtype)
            dv_ref[:, hh * dv:(hh + 1) * dv] = dv_acc[hh, :, :dv].astype(bf16)
            if fox:
                dcum_ref[0, :, hh:hh + 1] = dcq_acc[:, hh:hh + 1] - dk_acc[hh, :, dqk + 1:dqk + 2]
        _carry_finish(rider, (B, nblk))

    qs_ =pl.BlockSpec((Sq, wq), lambda b, p: (b, q_cb + p))
    ks_ = pl.BlockSpec((Sk, wq), lambda b, p: (b, k_cb + p))
    vs_ = pl.BlockSpec((Sk, wv), lambda b, p: (b, v_cb + p))
    os_ = pl.BlockSpec((Sq, wv), lambda b, p: (b, p))
    st = pl.BlockSpec((1, Sq, hb), lambda b, p: (b * nblk + p, 0, 0))
    rw = pl.BlockSpec((1, hb, Sk), lambda b, p: (b * nblk + p, 0, 0))
    in_specs = [qs_, ks_, vs_, os_, os_, st] + ([rw] if fox else [])
    ins = [q_arr, k_arr, v_arr, o, do, lse] + ([cum_r] if fox else [])
    out_specs = [pl.BlockSpec((Sq, wq), lambda b, p: (b, p)), pl.BlockSpec((Sk, wq), lambda b, p: (b, p)),
                 pl.BlockSpec((Sk, wv), lambda b, p: (b, p))] + ([st] if fox else [])
    out_shape = [SDS((B * Sq, nblk * wq), dq_dtype), SDS((B * Sk, nblk * wq), dq_dtype), SDS((B * Sk, nblk * wv), bf16)]
    scratch = [pltpu.VMEM((hb, Sq, LANES), bf16), pltpu.VMEM((hb, Sk, LANES), bf16), pltpu.VMEM((hb, Sk, LANES), bf16),
               pltpu.VMEM((hb, Sq, LANES), bf16), pltpu.VMEM((hb, Sk, LANES), f32), pltpu.VMEM((hb, Sk, LANES), f32)]
    if fox:
        assert Sq == Sk
        out_shape.append(SDS((B * nblk, Sq, hb), f32))
        scratch.append(pltpu.VMEM((Sq, hb), f32))
    return _carrier_call(body, rider, ins, in_specs, out_shape, out_specs, scratch, (B, nblk), name)


def _merge_fwd(oa, ob, oc, w_br, proj, D, name):
    T = oa.shape[0]
    tr = _pick(T, 256)
    g0 = C_GATE // D
    o_spec = pl.BlockSpec((tr, BW), lambda i: (i, 0))
    w_spec = pl.BlockSpec((3, BW, D), lambda i: (0, 0, 0))

    def body(oa_ref, ob_ref, oc_ref, w_ref, ga_ref, gb_ref, gc_ref, m_ref):
        tot = jnp.zeros((tr, D), f32)
        for n, (o_ref, g_ref) in enumerate(((oa_ref, ga_ref), (ob_ref, gb_ref), (oc_ref, gc_ref))):
            bp = lax.dot_general(o_ref[...].astype(bf16), w_ref[n], _DIMS["nn"], preferred_element_type=f32)
            tot = tot + jax.nn.sigmoid(g_ref[...]) * bp
        m_ref[...] = tot.astype(bf16)

    gspecs = [pl.BlockSpec((tr, D), lambda i, n=n: (i, g0 + n)) for n in range(3)]
    return pl.pallas_call(
        body, grid=(T // tr,), in_specs=[o_spec, o_spec, o_spec, w_spec, *gspecs],
        out_specs=pl.BlockSpec((tr, D), lambda i: (i, 0)), out_shape=SDS((T, D), bf16), name=name,
        compiler_params=_params(("parallel",)))(oa, ob, oc, w_br, proj, proj, proj)


def _merge_bwd(dm, oa, ob, oc, w_br, proj, D, name, rider=None):
    T = oa.shape[0]
    tr = _pick(T, 256)
    g0 = C_GATE // D
    o_spec = pl.BlockSpec((tr, BW), lambda i: (i, 0))
    d_spec = pl.BlockSpec((tr, D), lambda i: (i, 0))
    w_spec = pl.BlockSpec((3, BW, D), lambda i: (0, 0, 0))

    def body(*refs):
        refs = _carry(rider, refs, 8, 9, 0, (T // tr,))
        dm_ref, oa_ref, ob_ref, oc_ref, w_ref, ga_ref, gb_ref, gc_ref = refs[:8]
        outs = refs[8:]
        dmv = dm_ref[...]
        for n, (o_ref, g_ref) in enumerate(((oa_ref, ga_ref), (ob_ref, gb_ref), (oc_ref, gc_ref))):
            bp = lax.dot_general(o_ref[...].astype(bf16), w_ref[n], _DIMS["nn"], preferred_element_type=f32)
            g = jax.nn.sigmoid(g_ref[...])
            dbp = (dmv * g).astype(bf16)
            outs[n][...] = (dmv * bp * g * (1.0 - g)).astype(bf16)
            outs[3 + n][...] = lax.dot_general(dbp, w_ref[n], _DIMS["nt"], preferred_element_type=f32)
            outs[6 + n][...] = dbp
        _carry_finish(rider, (T // tr,))

    gspecs = [pl.BlockSpec((tr, D), lambda i, n=n: (i, g0 + n)) for n in range(3)]
    return _carrier_call(
        body, rider, [dm, oa, ob, oc, w_br, proj, proj, proj], [d_spec, o_spec, o_spec, o_spec, w_spec, *gspecs],
        [SDS((T, D), bf16)] * 3 + [SDS((T, BW), f32)] * 3 + [SDS((T, D), bf16)] * 3,
        [d_spec] * 3 + [o_spec] * 3 + [d_spec] * 3, [], (T // tr,), name)


def _loss(h, target, name):
    T, D = h.shape
    tr = _pick(T, 256)
    nt = T // tr
    row = pl.BlockSpec((tr, D), lambda i: (i, 0))

    def body(h_ref, t_ref, l_ref, d_ref, acc):
        i = pl.program_id(0)

        @pl.when(i == 0)
        def _():
            acc[...] = jnp.zeros_like(acc)

        e = h_ref[...] - t_ref[...]
        d_ref[...] = e * (1.0 / D)
        acc[...] += jnp.sum(e * e, axis=0, keepdims=True)

        @pl.when(i == nt - 1)
        def _():
            l_ref[...] = jnp.sum(acc[...], axis=-1, keepdims=True) * (0.5 / D)

    return pl.pallas_call(
        body, grid=(nt,), in_specs=[row, row], out_specs=[pl.BlockSpec((1, 1), lambda i: (0, 0)), row],
        out_shape=[SDS((1, 1), f32), SDS((T, D), f32)], scratch_shapes=[pltpu.VMEM((1, D), f32)], name=name,
        compiler_params=_params(("arbitrary",)))(h, target)


def _adamw(w, m, v, parts, name):
    shape = w.shape
    C = shape[-1]
    R = int(np.prod(shape[:-1])) if len(shape) > 1 else 1
    n_parts, Cp = parts.shape[0], parts.shape[-1]
    tr = _pick(R, 256)
    spec = pl.BlockSpec((tr, C), lambda i: (i, 0))

    def body(w_ref, m_ref, v_ref, p_ref, g_ref, d_ref, mo_ref, vo_ref):
        gv = p_ref[0].astype(f32)
        for q in range(1, n_parts):
            gv = gv + p_ref[q].astype(f32)
        gv = gv[:, :C]
        mn = ADAM_B1 * m_ref[...] + (1.0 - ADAM_B1) * gv
        vn = ADAM_B2 * v_ref[...] + (1.0 - ADAM_B2) * (gv * gv)
        m_hat = mn / (1.0 - ADAM_B1 ** ADAM_STEP)
        v_hat = vn / (1.0 - ADAM_B2 ** ADAM_STEP)
        g_ref[...] = gv
        d_ref[...] = -ADAM_LR * (m_hat / (jnp.sqrt(v_hat) + ADAM_EPS) + ADAM_WD * w_ref[...])
        mo_ref[...] = mn
        vo_ref[...] = vn

    outs = pl.pallas_call(
        body, grid=(R // tr,), in_specs=[spec] * 3 + [pl.BlockSpec((n_parts, tr, Cp), lambda i: (0, i, 0))], out_specs=[spec] * 4,
        out_shape=[SDS((R, C), f32)] * 4, name=name,
        compiler_params=_params(("parallel",)))(*[a.reshape(R, C) for a in (w, m, v)], parts.reshape(n_parts, R, Cp))
    return [o.reshape(shape) for o in outs]


def _coords():
    return lax.axis_index("x"), lax.axis_index("y"), lax.axis_index("c")


def _window(ref, axis, idx, n):
    start = pl.multiple_of(idx * n, n)
    if axis == 1:
        return ref.at[:, pl.ds(start, n), :]
    return ref.at[:, :, pl.ds(start, n)]


class _Rider:
    def __init__(self, inputs, out_shapes, aliases, scratch, start, finish):
        self.inputs, self.out_shapes, self.aliases, self.scratch = list(inputs), list(out_shapes), dict(aliases), list(scratch)
        self.start, self.finish = start, finish


def _join_riders(riders):
    riders = [r for r in riders if r is not None]
    if len(riders) < 2:
        return riders[0] if riders else None
    ni = np.cumsum([0] + [len(r.inputs) for r in riders])
    no = np.cumsum([0] + [len(r.out_shapes) for r in riders])
    ns = np.cumsum([0] + [len(r.scratch) for r in riders])
    aliases = {int(ni[k]) + i: int(no[k]) + o for k, r in enumerate(riders) for i, o in r.aliases.items()}

    def phase(which):
        def run(ins, outs, scr):
            for k, r in enumerate(riders):
                getattr(r, which)(ins[ni[k]:ni[k + 1]], outs[no[k]:no[k + 1]], scr[ns[k]:ns[k + 1]])
        return run

    return _Rider(sum([r.inputs for r in riders], []), sum([r.out_shapes for r in riders], []), aliases,
                  sum([r.scratch for r in riders], []), phase("start"), phase("finish"))


def _run_rider(r, name):
    ni, no = len(r.inputs), len(r.out_shapes)

    def body(*refs):
        ins, outs, scr = refs[:ni], refs[ni:ni + no], refs[ni + no:]
        r.start(ins, outs, scr)
        r.finish(ins, outs, scr)

    any_spec = pl.BlockSpec(memory_space=pl.ANY)
    return pl.pallas_call(body, out_shape=r.out_shapes, in_specs=[any_spec] * ni, out_specs=[any_spec] * no,
                          input_output_aliases=r.aliases, scratch_shapes=r.scratch, name=name)(*r.inputs)


def _gather_rider(shards, axes):
    nt = len(shards)
    sizes = [s.shape[a] for s, a in zip(shards, axes)]
    out_shapes = [SDS(tuple(d * N_DEV if i == a else d for i, d in enumerate(s.shape)), s.dtype) for s, a in zip(shards, axes)]

    def plan(x_refs, out_refs, scr):
        send_sems, recv_sems, local_sems = scr
        x, y, c = _coords()
        me, sibling = (x, y, c), (x, y, 1 - c)
        chips = [(1 - x, y), (x, 1 - y), (1 - x, 1 - y)]

        def win(t, px, py, pc):
            return _window(out_refs[t], axes[t], 4 * px + 2 * py + pc, sizes[t])

        def copy(t, k, block, to, src=None):
            return pltpu.make_async_remote_copy(
                src_ref=win(t, *block) if src is None else src, dst_ref=win(t, *block),
                send_sem=send_sems.at[7 * t + k], recv_sem=recv_sems.at[7 * t + k], device_id=to, device_id_type=MESH)

        mine = [pltpu.make_async_copy(x_refs[t], win(t, *me), local_sems.at[t]) for t in range(nt)]
        first = []
        for t in range(nt):
            first.append(copy(t, 0, me, sibling, src=x_refs[t]))
            first += [copy(t, 1 + j, me, (*chip, c), src=x_refs[t]) for j, chip in enumerate(chips)]
        return c, me, sibling, chips, copy, mine, first

    def start(x_refs, out_refs, scr):
        _, _, _, _, _, mine, first = plan(x_refs, out_refs, scr)
        for cp in mine + first:
            cp.start()

    def finish(x_refs, out_refs, scr):
        c, me, sibling, chips, copy, mine, first = plan(x_refs, out_refs, scr)
        passed = []
        for j, chip in enumerate(chips):
            for t in range(nt):
                copy(t, 1 + j, (*chip, c), me).wait_recv()
                fwd = copy(t, 4 + j, (*chip, c), sibling)
                fwd.start()
                passed.append(fwd)
        for t in range(nt):
            copy(t, 0, sibling, me).wait_recv()
            for j, chip in enumerate(chips):
                copy(t, 4 + j, (*chip, 1 - c), me).wait_recv()
        for cp in first + passed:
            cp.wait_send()
        for cp in mine:
            cp.wait()

    scratch = [pltpu.SemaphoreType.DMA((7 * nt,)), pltpu.SemaphoreType.DMA((7 * nt,)), pltpu.SemaphoreType.DMA((nt,))]
    return _Rider(shards, out_shapes, {}, scratch, start, finish)


N_CHIP = N_DEV // 2


def _pair_exchange_rider(grads, axes, shards):
    nt = len(grads)
    sizes = [s[a] for s, a in zip(shards, axes)]

    def plan(ins, outs, scr):
        send_sems, recv_sems = scr
        x, y, c = _coords()
        return [pltpu.make_async_remote_copy(
            src_ref=_window(ins[t], axes[t], 2 * q + (1 - c), sizes[t]), dst_ref=outs[t].at[q],
            send_sem=send_sems.at[N_CHIP * t + q], recv_sem=recv_sems.at[N_CHIP * t + q],
            device_id=(x, y, 1 - c), device_id_type=MESH) for t in range(nt) for q in range(N_CHIP)]

    def start(ins, outs, scr):
        for cp in plan(ins, outs, scr):
            cp.start()

    def finish(ins, outs, scr):
        copies = plan(ins, outs, scr)
        for cp in copies:
            cp.wait_recv()
        for cp in copies:
            cp.wait_send()

    scratch = [pltpu.SemaphoreType.DMA((N_CHIP * nt,)), pltpu.SemaphoreType.DMA((N_CHIP * nt,))]
    return _Rider(grads, [SDS((N_CHIP,) + tuple(s), f32) for s in shards], {}, scratch, start, finish)


def _pair_sum(grad, buf, axis, core, name):
    _, G, r, c = buf.shape
    tr = _pick(r, 512)
    nr = r // tr

    def body(core_ref, g_ref, b_ref, o_ref):
        o_ref[0, 0] = (g_ref[0] + b_ref[0, 0]).astype(bf16)

    if axis == 2:
        g_spec = pl.BlockSpec((1, tr, c), lambda q, g, i, core_ref: (g, i, 2 * q + core_ref[0]))
    else:
        g_spec = pl.BlockSpec((1, tr, c), lambda q, g, i, core_ref: (g, (2 * q + core_ref[0]) * nr + i, 0))
    blk = pl.BlockSpec((1, 1, tr, c), lambda q, g, i, core_ref: (q, g, i, 0))
    grid_spec = pltpu.PrefetchScalarGridSpec(num_scalar_prefetch=1, grid=(N_CHIP, G, nr), in_specs=[g_spec, blk], out_specs=blk)
    return pl.pallas_call(body, grid_spec=grid_spec, out_shape=SDS(buf.shape, bf16), name=name,
                          compiler_params=_params(("parallel", "parallel", "parallel")))(core, grad, buf)


def _chip_exchange_rider(chips, slots, layer):
    nt = len(chips)

    def plan(ins, outs, scr):
        send_sems, recv_sems, local_sems = scr
        x, y, c = _coords()
        me = 2 * x + y
        mine = [pltpu.make_async_copy(ins[t].at[me], outs[t].at[me, layer], local_sems.at[t]) for t in range(nt)]
        copies = []
        for k in range(1, N_CHIP):
            px, py = lax.rem(x + (k >> 1), 2), lax.rem(y + (k & 1), 2)
            copies += [pltpu.make_async_remote_copy(
                src_ref=ins[t].at[2 * px + py], dst_ref=outs[t].at[me, layer],
                send_sem=send_sems.at[3 * t + k - 1], recv_sem=recv_sems.at[3 * t + k - 1],
                device_id=(px, py, c), device_id_type=MESH) for t in range(nt)]
        return mine, copies

    def start(ins, outs, scr):
        mine, copies = plan(ins, outs, scr)
        for cp in mine + copies:
            cp.start()

    def finish(ins, outs, scr):
        mine, copies = plan(ins, outs, scr)
        for cp in copies:
            cp.wait_recv()
        for cp in copies:
            cp.wait_send()
        for cp in mine:
            cp.wait()

    scratch = [pltpu.SemaphoreType.DMA((3 * nt,)), pltpu.SemaphoreType.DMA((3 * nt,)), pltpu.SemaphoreType.DMA((nt,))]
    return _Rider(list(chips) + list(slots), [SDS(s.shape, s.dtype) for s in slots], {nt + t: t for t in range(nt)}, scratch,
                  start, finish)


_BIG = (("w_in", 2), ("w_uq", 2), ("w_ukv", 2), ("w_mem_kv", 1), ("w_br", 2), ("w_out", 1), ("w_ff1", 2), ("w_ff2", 1))


def _in_col_maps(D):
    d_in = O_GATE + 3 * D
    sh = d_in // N_DEV
    shp = _round_up(sh, LANES)
    perm_of_orig = np.empty(d_in, np.int64)
    for a, b, dst in ((0, O_F, C_Q), (O_F, O_CQ, C_F), (O_CQ, O_CKV, C_CQ), (O_CKV, O_KR, C_CKV), (O_KR, O_QM, C_KPE + NOPE),
                      (O_QM, O_GATE, C_QM), (O_GATE, d_in, C_GATE)):
        perm_of_orig[a:b] = dst + np.arange(b - a)
    orig = np.arange(d_in)
    gath_of_orig = (orig // sh) * shp + orig % sh
    fwd = -np.ones(C_GATE + 3 * D, np.int64)
    fwd[perm_of_orig] = gath_of_orig
    bwd = -np.ones(N_DEV * shp, np.int64)
    bwd[gath_of_orig] = perm_of_orig
    return fwd, bwd


def _lane_pad(a):
    c = a.shape[-1]
    return jnp.pad(a, [(0, 0)] * (a.ndim - 1) + [(0, _round_up(c, LANES) - c)])


def _as3(a):
    return a.reshape((-1,) + a.shape[-2:]) if a.ndim != 3 else a


def _rope_tables(positions):
    inv_freq = ROPE_BASE ** (-jnp.arange(0, ROPE, 2, dtype=f32) / ROPE)
    ang = positions.astype(f32).reshape(-1, 1) * inv_freq
    cos, sin = jnp.cos(ang), jnp.sin(ang)
    T = ang.shape[0]
    one, zero = jnp.ones((T, NOPE), f32), jnp.zeros((T, 16), f32)
    c = jnp.concatenate([one, cos, cos, jnp.ones((T, 32), f32)], axis=1)
    sa = jnp.concatenate([jnp.zeros((T, NOPE), f32), -sin, zero, jnp.zeros((T, 32), f32)], axis=1)
    sb = jnp.concatenate([jnp.zeros((T, NOPE), f32), zero, sin, jnp.zeros((T, 32), f32)], axis=1)
    return c, sa, sb


def kernel(x, mem, positions, ln_in_g, ln_in_b, w_in, b_forget, w_uq, g_cq, w_ukv, g_ckv, w_mem_kv, w_br, w_out, ln1_g, ln1_b, w_ff1, w_ff2, ln2_g, ln2_b, loss_target, m_ln_in_g, m_ln_in_b, m_w_in, m_b_forget, m_w_uq, m_g_cq, m_w_ukv, m_g_ckv, m_w_mem_kv, m_w_br, m_w_out, m_ln1_g, m_ln1_b, m_w_ff1, m_w_ff2, m_ln2_g, m_ln2_b, v_ln_in_g, v_ln_in_b, v_w_in, v_b_forget, v_w_uq, v_g_cq, v_w_ukv, v_g_ckv, v_w_mem_kv, v_w_br, v_w_out, v_ln1_g, v_ln1_b, v_w_ff1, v_w_ff2, v_ln2_g, v_ln2_b):
    B, S, D = x.shape
    T = B * S
    L = w_in.shape[0]
    NM = mem.shape[1]
    alpha = (2 * L) ** 0.25
    weights = dict(ln_in_g=ln_in_g, ln_in_b=ln_in_b, w_in=w_in, b_forget=b_forget, w_uq=w_uq, g_cq=g_cq, w_ukv=w_ukv, g_ckv=g_ckv,
                   w_mem_kv=w_mem_kv, w_br=w_br, w_out=w_out, ln1_g=ln1_g, ln1_b=ln1_b, w_ff1=w_ff1, w_ff2=w_ff2, ln2_g=ln2_g, ln2_b=ln2_b)
    mom_m = dict(ln_in_g=m_ln_in_g, ln_in_b=m_ln_in_b, w_in=m_w_in, b_forget=m_b_forget, w_uq=m_w_uq, g_cq=m_g_cq, w_ukv=m_w_ukv,
                 g_ckv=m_g_ckv, w_mem_kv=m_w_mem_kv, w_br=m_w_br, w_out=m_w_out, ln1_g=m_ln1_g, ln1_b=m_ln1_b, w_ff1=m_w_ff1,
                 w_ff2=m_w_ff2, ln2_g=m_ln2_g, ln2_b=m_ln2_b)
    mom_v = dict(ln_in_g=v_ln_in_g, ln_in_b=v_ln_in_b, w_in=v_w_in, b_forget=v_b_forget, w_uq=v_w_uq, g_cq=v_g_cq, w_ukv=v_w_ukv,
                 g_ckv=v_g_ckv, w_mem_kv=v_w_mem_kv, w_br=v_w_br, w_out=v_w_out, ln1_g=v_ln1_g, ln1_b=v_ln1_b, w_ff1=v_w_ff1,
                 w_ff2=v_w_ff2, ln2_g=v_ln2_g, ln2_b=v_ln2_b)
    order = list(weights)
    big_axes = [ax for _, ax in _BIG]
    fwd_map, bwd_map = _in_col_maps(D)
    shard3 = {n: _lane_pad(weights[n]) for n, _ in _BIG}

    half = len(_BIG) // 2

    def gather_rider(l, lo=0, hi=len(_BIG)):
        return _gather_rider([_as3(shard3[n][l]).astype(bf16) for n, _ in _BIG[lo:hi]], big_axes[lo:hi])

    def named(full, lo, hi):
        return {n: (f if n == "w_br" else f[0]) for (n, _), f in zip(_BIG[lo:hi], full)}

    tabs = _rope_tables(positions)
    x2d, mem2d, tgt2d = x.reshape(T, D), mem.reshape(B * NM, D), loss_target.reshape(T, D)
    fox = dict(B=B, Sq=S, Sk=S, nblk=FOX_HEADS // 2, hb=2, dqk=FOX_HD, dv=FOX_HD, mode="fox", scale=FOX_HD ** -0.5)
    mla = dict(B=B, Sq=S, Sk=S, nblk=MLA_HEADS // 2, hb=2, dqk=LANES, dv=MLA_VD, mode="chunk", scale=(NOPE + ROPE) ** -0.5)
    memat = dict(B=B, Sq=S, Sk=NM, nblk=MEM_HEADS, hb=1, dqk=MEM_HD, dv=MEM_HD, mode="none", scale=MEM_HD ** -0.5)
    nb = FOX_HEADS // 2

    h, hb, xh0, rstd0 = _ln_fwd(None, x2d, ln_in_g, ln_in_b, alpha, "ln_in")
    saved = []
    first = _run_rider(gather_rider(0, 0, half), "weights_all_gather")
    for l in range(L):
        W = named(first, 0, half)
        W["w_in"] = _lane_permute(W["w_in"], fwd_map, bf16, "w_in_reorder")
        bpad = jnp.pad(b_forget[l].reshape(1, FOX_HEADS), ((0, 0), (0, LANES - FOX_HEADS)))
        proj = _mm(hb, W["w_in"], name="proj")
        cum = _gate_fwd(proj, bpad, B, S, "gate_fwd")
        cum_r = cum[:, :FOX_HEADS].reshape(B, S, nb, 2).transpose(0, 2, 3, 1).reshape(B * nb, 2, S)
        more = l + 1 < L
        (o_a, lse_a), second = _attn_fwd(proj, C_Q // LANES, proj, C_K // LANES, proj, C_V // LANES, cum_r=cum_r, name="fox_fwd",
                                         rider=gather_rider(l, half), **fox)
        W.update(named(second, half, len(_BIG)))
        cqn = _rms_fwd(proj, C_CQ, Q_RANK, g_cq[l], "rms_q")
        ckvn = _rms_fwd(proj, C_CKV, KV_RANK, g_ckv[l], "rms_kv")
        qraw = _mm(cqn, W["w_uq"], name="q_up")
        kv = _mm(ckvn, W["w_ukv"], name="kv_up")
        qf = _rope_q(qraw, tabs, bwd=False, name="rope_q")
        kf, v_b = _rope_k(kv, tabs, proj, "rope_k")
        res = _attn_fwd(qf, 0, kf, 0, v_b, 0, name="mla_fwd", rider=gather_rider(l + 1, 0, half) if more else None, **mla)
        (o_b, lse_b), first = res if more else (res, None)
        mkv = _mm(mem2d, W["w_mem_kv"], name="mem_kv")
        o_c, lse_c = _attn_fwd(proj, C_QM // MEM_HD, mkv, 0, mkv, MEM_HEADS, name="mem_fwd", **memat)
        merged = _merge_fwd(o_a, o_b, o_c, W["w_br"], proj, D, "merge_fwd")
        y = _mm(merged, W["w_out"], name="out_proj")
        h1, h1b, xh1, rstd1 = _ln_fwd(h, y, ln1_g[l], ln1_b[l], alpha, "ln1")
        a, r = _mm(h1b, W["w_ff1"], epi="relu2", name="ff1")
        ff = _mm(a, W["w_ff2"], name="ff2")
        h2, h2b, xh2, rstd2 = _ln_fwd(h1, ff, ln2_g[l], ln2_b[l], alpha, "ln2")
        saved.append(dict(W=W, bpad=bpad, h=hb, proj=proj, cum_r=cum_r, o_a=o_a, lse_a=lse_a, cqn=cqn, ckvn=ckvn, qf=qf, kf=kf,
                          v_b=v_b, o_b=o_b, lse_b=lse_b, mkv=mkv, o_c=o_c, lse_c=lse_c, merged=merged, h1=h1b, xh1=xh1,
                          rstd1=rstd1, a=a, r=r, xh2=xh2, rstd2=rstd2))
        h, hb = h2, h2b

    loss_local, dh = _loss(h, tgt2d, "loss")
    loss = lax.psum(loss_local[0, 0], ("x", "y", "c"))

    shard_shapes = [_as3(shard3[n][0]).shape for n, _ in _BIG]
    slots = [lax.empty((N_CHIP, L) + s, bf16) for s in shard_shapes]
    core = lax.axis_index("c").astype(jnp.int32).reshape(1)
    small = {n: [None] * L for n in ("b_forget", "g_cq", "g_ckv", "ln1_g", "ln1_b", "ln2_g", "ln2_b")}
    ax_a, ax_c = big_axes[:half], big_axes[half:]
    sh_a, sh_c = shard_shapes[:half], shard_shapes[half:]

    def pair_sums(gs, bufs, axes):
        return [_pair_sum(g, b, ax, core, "grads_pair_sum") for g, b, ax in zip(gs, bufs, axes)]

    pending = unpaired = None
    for l in reversed(range(L)):
        sv = saved[l]
        W = sv["W"]
        du2, du2b, dg2, db2 = _ln_bwd(dh, sv["xh2"], sv["rstd2"], ln2_g[l], "ln2_bwd")
        dz = _mm(du2b, W["w_ff2"], mode="nt", epi="mul", extra=sv["r"], out_dtype=bf16, name="ff2_dx")
        dw_ff2 = _mm(sv["a"], du2b, mode="tn", name="ff2_dw")
        dh1 = _mm(dz, W["w_ff1"], mode="nt", epi="add", extra=du2, alpha=alpha, name="ff1_dx")
        dw_ff1 = _mm(sv["h1"], dz, mode="tn", name="ff1_dw")
        du1, du1b, dg1, db1 = _ln_bwd(dh1, sv["xh1"], sv["rstd1"], ln1_g[l], "ln1_bwd")
        dmerged = _mm(du1b, W["w_out"], mode="nt", name="out_dx")
        dw_out = _mm(sv["merged"], du1b, mode="tn", name="out_dw")
        mb = _merge_bwd(dmerged, sv["o_a"], sv["o_b"], sv["o_c"], W["w_br"], sv["proj"], D, "merge_bwd",
                        rider=None if unpaired is None else _pair_exchange_rider(unpaired[0], ax_a, sh_a))
        if unpaired is not None:
            mb, bufs = mb
            pending = (pair_sums(unpaired[0], bufs, ax_a), unpaired[1])
        dgl, do, dbp = mb[0:3], mb[3:6], mb[6:9]
        dw_br = jnp.stack([_mm(o_n, dbp_n, mode="tn", name="br_dw") for o_n, dbp_n in zip((sv["o_a"], sv["o_b"], sv["o_c"]), dbp)])
        late = [_as3(g) for g in (dw_br, dw_out, dw_ff1, dw_ff2)]
        proj = sv["proj"]
        chip_rider = None if pending is None else _chip_exchange_rider(pending[0], slots[:half], pending[1])
        own, got = _attn_bwd(proj, C_Q // LANES, proj, C_K // LANES, proj, C_V // LANES, sv["o_a"], do[0], sv["lse_a"],
                             cum_r=sv["cum_r"], name="fox_bwd",
                             rider=_join_riders([chip_rider, _pair_exchange_rider(late, ax_c, sh_c)]), **fox)
        dq_a, dk_a, dv_a, dcum = own
        if chip_rider is not None:
            slots[:half], got = got[:half], got[half:]
        dcum = dcum.reshape(B, nb, S, 2).transpose(0, 2, 1, 3).reshape(T, FOX_HEADS)
        dcum = jnp.pad(dcum, ((0, 0), (0, LANES - FOX_HEADS)))
        dzf, dbf = _gate_bwd(dcum, proj, sv["bpad"], B, S, "gate_bwd")
        (dqf, dkf, dv_b), slots[half:] = _attn_bwd(
            sv["qf"], 0, sv["kf"], 0, sv["v_b"], 0, sv["o_b"], do[1], sv["lse_b"], dq_dtype=f32, name="mla_bwd",
            rider=_chip_exchange_rider(pair_sums(late, got, ax_c), slots[half:], l), **mla)
        dqraw = _rope_q(dqf, tabs, bwd=True, name="rope_q_bwd")
        dkv, dkpe = _rope_k_bwd(dkf, dv_b, tabs, "rope_k_bwd")
        dw_uq = _mm(sv["cqn"], dqraw, mode="tn", name="q_up_dw")
        dcqn = _mm(dqraw, W["w_uq"], mode="nt", name="q_up_dx")
        dcq_, dg_cq = _rms_bwd(dcqn, proj, C_CQ, Q_RANK, g_cq[l], "rms_q_bwd")
        dw_ukv = _mm(sv["ckvn"], dkv, mode="tn", name="kv_up_dw")
        dckvn = _mm(dkv, W["w_ukv"], mode="nt", name="kv_up_dx")
        dckv, dg_ckv = _rms_bwd(dckvn, proj, C_CKV, KV_RANK, g_ckv[l], "rms_kv_bwd")
        dqm, dmk, dmv = _attn_bwd(proj, C_QM // MEM_HD, sv["mkv"], 0, sv["mkv"], MEM_HEADS, sv["o_c"], do[2], sv["lse_c"],
                                  name="mem_bwd", **memat)
        dw_mem = _mm(mem2d, jnp.concatenate([dmk, dmv], axis=1), mode="tn", name="mem_kv_dw")
        dproj = jnp.concatenate([dq_a, dk_a, dv_a, dzf, dkpe, dckv, dqm, jnp.zeros((T, C_CQ - C_PAD), bf16), dcq_, *dgl], axis=1)
        dw_in = _lane_permute(_mm(sv["h"], dproj, mode="tn", name="proj_dw"), bwd_map, f32, "w_in_grad_reorder")
        dh = _mm(dproj, W["w_in"], mode="nt", epi="add", extra=du1, alpha=alpha, name="proj_dx")
        unpaired = ([_as3(g) for g in (dw_in, dw_uq, dw_ukv, dw_mem)], l)
        for n, val in (("b_forget", dbf[:, :FOX_HEADS]), ("g_cq", dg_cq), ("g_ckv", dg_ckv), ("ln1_g", dg1), ("ln1_b", db1),
                       ("ln2_g", dg2), ("ln2_b", db2)):
            small[n][l] = val.reshape(-1)
    bufs = _run_rider(_pair_exchange_rider(unpaired[0], ax_a, sh_a), "grads_pair_exchange")
    slots[:half] = _run_rider(_chip_exchange_rider(pair_sums(unpaired[0], bufs, ax_a), slots[:half], unpaired[1]),
                              "grads_chip_exchange")
    grad_x, _, dg_in, db_in = _ln_bwd(dh, xh0, rstd0, ln_in_g, "ln_in_bwd")

    small_list = [("ln_in_g", dg_in.reshape(-1)), ("ln_in_b", db_in.reshape(-1))] + [(n, jnp.stack(v).reshape(-1)) for n, v in small.items()]
    sflat = jnp.concatenate([v for _, v in small_list])
    n_small = sflat.shape[0]
    rs = _round_up(-(-n_small // LANES), 8)
    spacked = jnp.pad(sflat, (0, rs * LANES - n_small)).reshape(1, rs, LANES)
    sparts = _run_rider(_gather_rider([spacked], [1]), "small_all_gather")[0].reshape(N_DEV, rs * LANES)

    grads, deltas, new_m, new_v = {}, {}, {}, {}
    off = 0
    for n, vv in small_list:
        parts = sparts[:, off:off + vv.shape[0]].reshape((N_DEV,) + weights[n].shape)
        off += vv.shape[0]
        grads[n], deltas[n], new_m[n], new_v[n] = _adamw(weights[n], mom_m[n], mom_v[n], parts, "adamw_" + n)
    for (n, _), sl in zip(_BIG, slots):
        parts = sl.reshape((N_CHIP,) + weights[n].shape[:-1] + (sl.shape[-1],))
        grads[n], deltas[n], new_m[n], new_v[n] = _adamw(weights[n], mom_m[n], mom_v[n], parts, "adamw_" + n)
    return (loss, grad_x.reshape(B, S, D), *[grads[n] for n in order], *[deltas[n] for n in order], *[new_m[n] for n in order],
            *[new_v[n] for n in order])
```

```python
import functools

import numpy as np
import jax
import jax.numpy as jnp
from jax import lax
from jax.experimental import pallas as pl
from jax.experimental.pallas import tpu as pltpu

f32, bf16 = jnp.float32, jnp.bfloat16
SDS = jax.ShapeDtypeStruct

N_DEV = 8
MESH = pl.DeviceIdType.MESH
LANES = 128
VMEM_LIMIT = 56 * 1024 * 1024

FOX_HEADS, FOX_HD = 8, 64
MLA_HEADS, NOPE, ROPE, MLA_VD = 8, 64, 32, 64
Q_RANK, KV_RANK = 384, 256
ROPE_BASE = 10000.0
MEM_HEADS, MEM_HD = 4, 128
BW = 512
CHUNK = 64
LN_EPS, RMS_EPS, NEG_INF = 1e-5, 1e-6, -1e30
ADAM_LR, ADAM_B1, ADAM_B2, ADAM_EPS, ADAM_WD, ADAM_STEP = 0.001, 0.9, 0.999, 1e-08, 0.01, 10

ATT_TQ, ATT_TK = 512, 512

C_Q, C_K, C_V, C_F, C_KPE, C_CKV, C_QM, C_PAD, C_CQ, C_GATE = 0, 512, 1024, 1536, 1664, 1792, 2048, 2560, 2688, 3072
O_F, O_CQ, O_CKV, O_KR, O_QM, O_GATE = 1536, 1544, 1928, 2184, 2216, 2728


def _pick(dim, pref):
    if dim <= pref:
        return dim
    for c in (2048, 1024, 768, 512, 384, 256, 128, 64, 32, 16, 8):
        if c <= pref and dim % c == 0:
            return c
    return dim


def _round_up(n, m):
    return -(-n // m) * m


def _params(sem=None):
    return pltpu.CompilerParams(dimension_semantics=sem, vmem_limit_bytes=VMEM_LIMIT)


_DIMS = {"nn": (((1,), (0,)), ((), ())), "nt": (((1,), (1,)), ((), ())), "tn": (((0,), (0,)), ((), ()))}


def _mm(a, b, *, mode="nn", out_dtype=f32, epi="none", extra=None, alpha=1.0, tm=1024, tn=1024, tk=None, name):
    if mode == "nn":
        (M, K), (_, N) = a.shape, b.shape
    elif mode == "nt":
        (M, K), (N, _) = a.shape, b.shape
    else:
        (K, M), (_, N) = a.shape, b.shape
    tm, tn, tk = _pick(M, tm), _pick(N, tn), _pick(K, tk or 2048)
    nk = K // tk
    a_spec = pl.BlockSpec((tk, tm), lambda i, j, k: (k, i)) if mode == "tn" else pl.BlockSpec((tm, tk), lambda i, j, k: (i, k))
    b_spec = pl.BlockSpec((tn, tk), lambda i, j, k: (j, k)) if mode == "nt" else pl.BlockSpec((tk, tn), lambda i, j, k: (k, j))
    o_spec = pl.BlockSpec((tm, tn), lambda i, j, k: (i, j))
    n_out = 2 if epi == "relu2" else 1
    dims = _DIMS[mode]

    def body(*refs):
        a_ref, b_ref = refs[0], refs[1]
        e_ref = refs[2] if extra is not None else None
        outs = refs[2 + (extra is not None):2 + (extra is not None) + n_out]

        def finish(r):
            if epi == "none":
                outs[0][...] = r.astype(out_dtype)
            elif epi == "relu2":
                p = jnp.maximum(r, 0.0)
                outs[0][...] = (p * p).astype(bf16)
                outs[1][...] = (2.0 * p).astype(bf16)
            elif epi == "mul":
                outs[0][...] = (r * e_ref[...].astype(f32)).astype(out_dtype)
            else:
                outs[0][...] = (r + alpha * e_ref[...].astype(f32)).astype(out_dtype)

        part = lax.dot_general(a_ref[...].astype(bf16), b_ref[...].astype(bf16), dims, preferred_element_type=f32)
        if nk == 1:
            finish(part)
            return
        acc = refs[-1]
        k = pl.program_id(2)

        @pl.when(k == 0)
        def _():
            acc[...] = part

        @pl.when(k > 0)
        def _():
            acc[...] += part

        @pl.when(k == nk - 1)
        def _():
            finish(acc[...])

    ins, in_specs = [a, b], [a_spec, b_spec]
    if extra is not None:
        ins.append(extra)
        in_specs.append(o_spec)
    if epi == "relu2":
        out_shape, out_specs = [SDS((M, N), bf16), SDS((M, N), bf16)], [o_spec, o_spec]
    else:
        out_shape, out_specs = SDS((M, N), out_dtype), o_spec
    return pl.pallas_call(
        body, grid=(M // tm, N // tn, nk), in_specs=in_specs, out_specs=out_specs, out_shape=out_shape,
        scratch_shapes=[pltpu.VMEM((tm, tn), f32)] if nk > 1 else [], name=name,
        compiler_params=_params(("parallel", "parallel", "arbitrary")))(*ins)


def _lane_permute(src, cmap, out_dtype, name):
    R, Cs = src.shape
    cmap = np.asarray(cmap, np.int64)
    nb = cmap.shape[0] // LANES
    srcs = [sorted({int(c) // LANES for c in cmap[b * LANES:(b + 1) * LANES] if c >= 0}) for b in range(nb)]
    exact = src.dtype == f32
    tr = _pick(R, 256 if exact else 512)

    def body(src_ref, cm_ref, o_ref):
        rows = lax.broadcasted_iota(jnp.int32, (LANES, LANES), 0)
        for b in range(nb):
            tbl = cm_ref[:, b * LANES:(b + 1) * LANES]
            acc = jnp.zeros((tr, LANES), f32)
            for sb in srcs[b]:
                sel = (rows + sb * LANES) == tbl
                blk = src_ref[:, sb * LANES:(sb + 1) * LANES]
                if exact:
                    acc = acc + jnp.dot(blk, sel.astype(f32), precision=lax.Precision.HIGHEST, preferred_element_type=f32)
                else:
                    acc = acc + jnp.dot(blk, sel.astype(blk.dtype), preferred_element_type=f32)
            o_ref[:, b * LANES:(b + 1) * LANES] = acc.astype(out_dtype)

    return pl.pallas_call(
        body, grid=(R // tr,), in_specs=[pl.BlockSpec((tr, Cs), lambda i: (i, 0)), pl.BlockSpec((1, nb * LANES), lambda i: (0, 0))],
        out_specs=pl.BlockSpec((tr, nb * LANES), lambda i: (i, 0)), out_shape=SDS((R, nb * LANES), out_dtype), name=name,
        compiler_params=_params(("parallel",)))(src, jnp.asarray(cmap.astype(np.int32)).reshape(1, -1))


def _ln_fwd(h, y, g, b, alpha, name):
    T, D = y.shape
    tr = _pick(T, 256)
    row = pl.BlockSpec((tr, D), lambda i: (i, 0))
    vec = pl.BlockSpec((1, D), lambda i: (0, 0))
    col = pl.BlockSpec((tr, 1), lambda i: (i, 0))

    def body(*refs):
        if h is None:
            y_ref, g_ref, b_ref, o_ref, ob_ref, xh_ref, rs_ref = refs
            u = y_ref[...]
        else:
            h_ref, y_ref, g_ref, b_ref, o_ref, ob_ref, xh_ref, rs_ref = refs
            u = alpha * h_ref[...] + y_ref[...]
        mu = jnp.mean(u, axis=-1, keepdims=True)
        d = u - mu
        var = jnp.mean(d * d, axis=-1, keepdims=True)
        rstd = lax.rsqrt(var + LN_EPS)
        xh = d * rstd
        xh_ref[...] = xh
        rs_ref[...] = rstd
        o = xh * g_ref[...] + b_ref[...]
        o_ref[...] = o
        ob_ref[...] = o.astype(bf16)

    ins = ([] if h is None else [h]) + [y, g.reshape(1, D), b.reshape(1, D)]
    specs = ([] if h is None else [row]) + [row, vec, vec]
    return pl.pallas_call(
        body, grid=(T // tr,), in_specs=specs, out_specs=[row, row, row, col],
        out_shape=[SDS((T, D), f32), SDS((T, D), bf16), SDS((T, D), f32), SDS((T, 1), f32)], name=name,
        compiler_params=_params(("parallel",)))(*ins)


def _ln_bwd(dy, xh, rstd, g, name):
    T, D = dy.shape
    tr = _pick(T, 256)
    row = pl.BlockSpec((tr, D), lambda i: (i, 0))
    vec = pl.BlockSpec((1, D), lambda i: (0, 0))
    col = pl.BlockSpec((tr, 1), lambda i: (i, 0))

    def body(dy_ref, xh_ref, rs_ref, g_ref, du_ref, dub_ref, dg_ref, db_ref):
        @pl.when(pl.program_id(0) == 0)
        def _():
            dg_ref[...] = jnp.zeros_like(dg_ref)
            db_ref[...] = jnp.zeros_like(db_ref)

        dyv, xhv = dy_ref[...], xh_ref[...]
        dxh = dyv * g_ref[...]
        m1 = jnp.mean(dxh, axis=-1, keepdims=True)
        m2 = jnp.mean(dxh * xhv, axis=-1, keepdims=True)
        du = rs_ref[...] * (dxh - m1 - xhv * m2)
        du_ref[...] = du
        dub_ref[...] = du.astype(bf16)
        dg_ref[...] += jnp.sum(dyv * xhv, axis=0, keepdims=True)
        db_ref[...] += jnp.sum(dyv, axis=0, keepdims=True)

    return pl.pallas_call(
        body, grid=(T // tr,), in_specs=[row, row, col, vec], out_specs=[row, row, vec, vec],
        out_shape=[SDS((T, D), f32), SDS((T, D), bf16), SDS((1, D), f32), SDS((1, D), f32)], name=name,
        compiler_params=_params(("arbitrary",)))(dy, xh, rstd, g.reshape(1, D))


def _rms_fwd(proj, col_off, R, g, name):
    T = proj.shape[0]
    tr = _pick(T, 512)
    cb = col_off // R

    def body(c_ref, g_ref, o_ref):
        c = c_ref[...]
        r = lax.rsqrt(jnp.mean(c * c, axis=-1, keepdims=True) + RMS_EPS)
        o_ref[...] = (c * r * g_ref[...]).astype(bf16)

    return pl.pallas_call(
        body, grid=(T // tr,), in_specs=[pl.BlockSpec((tr, R), lambda i: (i, cb)), pl.BlockSpec((1, R), lambda i: (0, 0))],
        out_specs=pl.BlockSpec((tr, R), lambda i: (i, 0)), out_shape=SDS((T, R), bf16), name=name,
        compiler_params=_params(("parallel",)))(proj, g.reshape(1, R))


def _rms_bwd(dy, proj, col_off, R, g, name):
    T = proj.shape[0]
    tr = _pick(T, 512)
    cb = col_off // R

    def body(dy_ref, c_ref, g_ref, dc_ref, dg_ref):
        @pl.when(pl.program_id(0) == 0)
        def _():
            dg_ref[...] = jnp.zeros_like(dg_ref)

        c, dyv = c_ref[...], dy_ref[...]
        r = lax.rsqrt(jnp.mean(c * c, axis=-1, keepdims=True) + RMS_EPS)
        t = dyv * g_ref[...]
        mt = jnp.mean(t * c, axis=-1, keepdims=True)
        dc_ref[...] = (r * t - c * (r * r * r) * mt).astype(bf16)
        dg_ref[...] += jnp.sum(dyv * c * r, axis=0, keepdims=True)

    return pl.pallas_call(
        body, grid=(T // tr,),
        in_specs=[pl.BlockSpec((tr, R), lambda i: (i, 0)), pl.BlockSpec((tr, R), lambda i: (i, cb)), pl.BlockSpec((1, R), lambda i: (0, 0))],
        out_specs=[pl.BlockSpec((tr, R), lambda i: (i, 0)), pl.BlockSpec((1, R), lambda i: (0, 0))],
        out_shape=[SDS((T, R), bf16), SDS((1, R), f32)], name=name,
        compiler_params=_params(("arbitrary",)))(dy, proj, g.reshape(1, R))


def _tri(n, lower):
    r = lax.broadcasted_iota(jnp.int32, (n, n), 0)
    c = lax.broadcasted_iota(jnp.int32, (n, n), 1)
    return ((r >= c) if lower else (c >= r)).astype(f32)


def _gate_fwd(proj, bpad, B, S, name):
    ch = _pick(S, 256)
    nch = S // ch
    cb = C_F // LANES

    def body(f_ref, b_ref, o_ref):
        tri = _tri(ch, True)
        carry = jnp.zeros((1, LANES), f32)
        for c in range(nch):
            z = f_ref[c * ch:(c + 1) * ch, :] + b_ref[...]
            lf = jnp.minimum(z, 0.0) - jnp.log1p(jnp.exp(-jnp.abs(z)))
            cs = jnp.dot(tri, lf, precision=lax.Precision.HIGHEST, preferred_element_type=f32) + carry
            o_ref[c * ch:(c + 1) * ch, :] = cs
            carry = cs[ch - 1:ch, :]

    return pl.pallas_call(
        body, grid=(B,), in_specs=[pl.BlockSpec((S, LANES), lambda b: (b, cb)), pl.BlockSpec((1, LANES), lambda b: (0, 0))],
        out_specs=pl.BlockSpec((S, LANES), lambda b: (b, 0)), out_shape=SDS((B * S, LANES), f32), name=name,
        compiler_params=_params(("parallel",)))(proj, bpad)


def _gate_bwd(dcum, proj, bpad, B, S, name):
    ch = _pick(S, 256)
    nch = S // ch
    cb = C_F // LANES

    def body(d_ref, f_ref, b_ref, dz_ref, db_ref):
        @pl.when(pl.program_id(0) == 0)
        def _():
            db_ref[...] = jnp.zeros_like(db_ref)

        tri = _tri(ch, False)
        carry = jnp.zeros((1, LANES), f32)
        dbs = jnp.zeros((1, LANES), f32)
        for c in reversed(range(nch)):
            d = d_ref[c * ch:(c + 1) * ch, :]
            dlf = jnp.dot(tri, d, precision=lax.Precision.HIGHEST, preferred_element_type=f32) + carry
            carry = carry + jnp.sum(d, axis=0, keepdims=True)
            z = f_ref[c * ch:(c + 1) * ch, :] + b_ref[...]
            dz = dlf / (1.0 + jnp.exp(z))
            dz_ref[c * ch:(c + 1) * ch, :] = dz.astype(bf16)
            dbs = dbs + jnp.sum(dz, axis=0, keepdims=True)
        db_ref[...] += dbs

    return pl.pallas_call(
        body, grid=(B,),
        in_specs=[pl.BlockSpec((S, LANES), lambda b: (b, 0)), pl.BlockSpec((S, LANES), lambda b: (b, cb)), pl.BlockSpec((1, LANES), lambda b: (0, 0))],
        out_specs=[pl.BlockSpec((S, LANES), lambda b: (b, 0)), pl.BlockSpec((1, LANES), lambda b: (0, 0))],
        out_shape=[SDS((B * S, LANES), bf16), SDS((1, LANES), f32)], name=name,
        compiler_params=_params(("arbitrary",)))(dcum, proj, bpad)


def _rot(v, c, sa, sb):
    return v * c + pltpu.roll(v, LANES - 16, 1) * sa + pltpu.roll(v, 16, 1) * sb


def _rot_t(v, c, sa, sb):
    return v * c + pltpu.roll(v * sa, 16, 1) + pltpu.roll(v * sb, LANES - 16, 1)


def _rope_q(x, tabs, *, bwd, name):
    T = x.shape[0]
    W = MLA_HEADS * LANES
    tr = _pick(T, 512)
    fn = _rot_t if bwd else _rot

    def body(x_ref, c_ref, sa_ref, sb_ref, o_ref):
        c, sa, sb = c_ref[...], sa_ref[...], sb_ref[...]
        for g in range(MLA_HEADS):
            o_ref[:, g * LANES:(g + 1) * LANES] = fn(x_ref[:, g * LANES:(g + 1) * LANES], c, sa, sb).astype(bf16)

    tab = pl.BlockSpec((tr, LANES), lambda i: (i, 0))
    wide = pl.BlockSpec((tr, W), lambda i: (i, 0))
    return pl.pallas_call(body, grid=(T // tr,), in_specs=[wide, tab, tab, tab], out_specs=wide, out_shape=SDS((T, W), bf16),
                          name=name, compiler_params=_params(("parallel",)))(x, *tabs)


def _rope_k(kv, tabs, proj, name):
    T = kv.shape[0]
    W = MLA_HEADS * LANES
    tr = _pick(T, 512)
    kcb = C_KPE // LANES

    def body(x_ref, c_ref, sa_ref, sb_ref, kpe_ref, k_ref, v_ref):
        c, sa, sb, kpe = c_ref[...], sa_ref[...], sb_ref[...], kpe_ref[...]
        low = lax.broadcasted_iota(jnp.int32, (tr, LANES), 1) < NOPE
        halves = []
        for g in range(MLA_HEADS):
            x = x_ref[:, g * LANES:(g + 1) * LANES]
            k_ref[:, g * LANES:(g + 1) * LANES] = _rot(jnp.where(low, x, 0.0) + kpe, c, sa, sb).astype(bf16)
            halves.append(x[:, NOPE:])
            if g % 2 == 1:
                v_ref[:, (g // 2) * LANES:(g // 2 + 1) * LANES] = jnp.concatenate(halves, axis=1).astype(bf16)
                halves = []

    tab = pl.BlockSpec((tr, LANES), lambda i: (i, 0))
    wide = pl.BlockSpec((tr, W), lambda i: (i, 0))
    return pl.pallas_call(
        body, grid=(T // tr,), in_specs=[wide, tab, tab, tab, pl.BlockSpec((tr, LANES), lambda i: (i, kcb))],
        out_specs=[wide, pl.BlockSpec((tr, W // 2), lambda i: (i, 0))], out_shape=[SDS((T, W), bf16), SDS((T, W // 2), bf16)],
        name=name, compiler_params=_params(("parallel",)))(kv, *tabs, proj)


def _rope_k_bwd(dk, dv, tabs, name):
    T = dk.shape[0]
    W = MLA_HEADS * LANES
    tr = _pick(T, 512)

    def body(dk_ref, dv_ref, c_ref, sa_ref, sb_ref, o_ref, s_ref):
        c, sa, sb = c_ref[...], sa_ref[...], sb_ref[...]
        tot = jnp.zeros((tr, LANES), f32)
        for g in range(MLA_HEADS):
            d = _rot_t(dk_ref[:, g * LANES:(g + 1) * LANES], c, sa, sb)
            tot = tot + d
            o_ref[:, g * LANES:(g + 1) * LANES] = jnp.concatenate(
                [d[:, :NOPE].astype(bf16), dv_ref[:, g * MLA_VD:(g + 1) * MLA_VD]], axis=1)
        s_ref[...] = tot.astype(bf16)

    tab = pl.BlockSpec((tr, LANES), lambda i: (i, 0))
    wide = pl.BlockSpec((tr, W), lambda i: (i, 0))
    return pl.pallas_call(
        body, grid=(T // tr,), in_specs=[wide, pl.BlockSpec((tr, W // 2), lambda i: (i, 0)), tab, tab, tab],
        out_specs=[wide, tab], out_shape=[SDS((T, W), bf16), SDS((T, LANES), bf16)], name=name,
        compiler_params=_params(("parallel",)))(dk, dv, *tabs)


def _tile_mask(mode, r0, c0, nr, nc):
    r = r0 + lax.broadcasted_iota(jnp.int32, (nr, nc), 0)
    c = c0 + lax.broadcasted_iota(jnp.int32, (nr, nc), 1)
    if mode == "chunk":
        return (c // CHUNK) <= (r // CHUNK)
    return c <= r


def _diagonal(block, carry, i, tq, tk):
    if tq == tk and tk % (2 * CHUNK) == 0:
        hw = tk // 2
        carry = block(i * tq, hw, 0, tq, carry, True)
        low = block(i * tq + hw, hw, hw, tq - hw, jax.tree.map(lambda x: x[hw:], carry), True)
        return jax.tree.map(lambda x, y: jnp.concatenate([x[:hw], y], axis=0), carry, low)
    nfull = (i * tq) // tk
    for d in range(max(1, tq // tk)):
        carry = block((nfull + d) * tk, tk, 0, tq, carry, True)
    return carry


def _attn_tiles(Sq, Sk):
    tk = _pick(Sk, ATT_TK)
    tq = _pick(Sq, ATT_TQ)
    assert (tk % tq == 0 or tq % tk == 0) and tq % CHUNK == 0 and tk % CHUNK == 0
    return tq, tk


def _stage_heads(dst, src_ref, hb, w, mark=None, val=1.0):
    S = src_ref.shape[0]
    ch = _pick(S, 256)
    assert w <= LANES and (mark is None or w <= mark < LANES)

    def step(c, _):
        rows = pl.ds(pl.multiple_of(c * ch, ch), ch)
        for hh in range(hb):
            x = src_ref[rows, hh * w:(hh + 1) * w].astype(bf16)
            if w < LANES:
                pad = jnp.zeros((ch, LANES - w), bf16)
                if mark is not None:
                    lane = lax.broadcasted_iota(jnp.int32, (ch, LANES - w), 1)
                    pad = jnp.where(lane == mark - w, val, 0.0).astype(bf16)
                x = jnp.concatenate([x, pad], axis=1)
            dst[hh, rows, :] = x
        return 0

    lax.fori_loop(0, S // ch, step, 0)


def _carry(rider, refs, n_in, n_out, n_scr, grid):
    if rider is None:
        return refs
    ri, ro = len(rider.inputs), len(rider.out_shapes)
    own = refs[:n_in] + refs[n_in + ri:n_in + ri + n_out] + refs[n_in + ri + n_out + ro:n_in + ri + n_out + ro + n_scr]
    rider.refs = (refs[n_in:n_in + ri], refs[n_in + ri + n_out:n_in + ri + n_out + ro], refs[n_in + ri + n_out + ro + n_scr:])
    first = functools.reduce(jnp.logical_and, [pl.program_id(a) == 0 for a in range(len(grid))])

    @pl.when(first)
    def _():
        rider.start(*rider.refs)

    return own


def _carry_finish(rider, grid):
    if rider is None:
        return
    last = functools.reduce(jnp.logical_and, [pl.program_id(a) == n - 1 for a, n in enumerate(grid)])

    @pl.when(last)
    def _():
        rider.finish(*rider.refs)


def _carrier_call(body, rider, ins, in_specs, out_shape, out_specs, scratch, grid, name):
    if rider is None:
        return pl.pallas_call(body, grid=grid, in_specs=in_specs, out_specs=out_specs, out_shape=out_shape, scratch_shapes=scratch,
                              name=name, compiler_params=_params(("parallel",) * len(grid)))(*ins)
    any_spec = pl.BlockSpec(memory_space=pl.ANY)
    ri, ro = len(rider.inputs), len(rider.out_shapes)
    res = pl.pallas_call(
        body, grid=grid, in_specs=list(in_specs) + [any_spec] * ri, out_specs=list(out_specs) + [any_spec] * ro,
        out_shape=list(out_shape) + rider.out_shapes, scratch_shapes=list(scratch) + rider.scratch,
        input_output_aliases={len(ins) + i: len(out_shape) + o for i, o in rider.aliases.items()}, name=name,
        compiler_params=_params(("arbitrary",) * len(grid)))(*ins, *rider.inputs)
    return res[:len(out_shape)], res[len(out_shape):]


def _attn_fwd(q_arr, q_cb, k_arr, k_cb, v_arr, v_cb, *, B, Sq, Sk, nblk, hb, dqk, dv, mode, scale, cum_r=None, rider=None, name):
    wq, wv = hb * dqk, hb * dv
    tq, tk = _attn_tiles(Sq, Sk)
    fox = mode == "fox"
    n_in = 4 if fox else 3
    ones = dv < LANES

    def body(*refs):
        refs = _carry(rider, refs, n_in, 2, 3, (B, nblk))
        q_ref, k_ref, v_ref = refs[:3]
        cr_ref = refs[3] if fox else None
        o_ref, lse_ref, qh, kh, vh = refs[n_in:]
        _stage_heads(qh, q_ref, hb, dqk)
        _stage_heads(kh, k_ref, hb, dqk)
        _stage_heads(vh, v_ref, hb, dv, dv if ones else None)

        def qstep(i, _):
            rows = pl.ds(pl.multiple_of(i * tq, tq), tq)
            qs = [qh[hh, rows, :] for hh in range(hb)]

            def block(c0, w, r0, nr, carry, diag):
                cols = pl.ds(pl.multiple_of(c0, w), w)
                out = []
                for hh in range(hb):
                    m, l, acc = carry[hh]
                    s = lax.dot_general(qs[hh][r0:r0 + nr], kh[hh, cols, :], _DIMS["nt"], preferred_element_type=f32) * scale
                    if fox:
                        s = s - cr_ref[0, hh:hh + 1, cols]
                    if diag:
                        s = jnp.where(_tile_mask(mode, i * tq + r0, c0, nr, w), s, NEG_INF)
                    m_new = jnp.maximum(m, jnp.max(s, axis=-1, keepdims=True))
                    a = jnp.exp(m - m_new)
                    p = jnp.exp(s - m_new)
                    if not ones:
                        l = a * l + jnp.sum(p, axis=-1, keepdims=True)
                    acc = a * acc + lax.dot_general(p.astype(bf16), vh[hh, cols, :], _DIMS["nn"], preferred_element_type=f32)
                    out.append((m_new, l, acc))
                return tuple(out)

            def tile(j, carry):
                return block(j * tk, tk, 0, tq, carry, False)

            carry = tuple((jnp.full((tq, 1), -jnp.inf, f32), jnp.zeros((tq, 1), f32), jnp.zeros((tq, LANES), f32)) for _ in range(hb))
            if mode == "none":
                carry = lax.fori_loop(0, Sk // tk, tile, carry)
            else:
                nfull = (i * tq) // tk
                carry = lax.fori_loop(0, nfull, tile, carry)
                carry = _diagonal(block, carry, i, tq, tk)
            for hh in range(hb):
                m, l, acc = carry[hh]
                if ones:
                    l = acc[:, dv:dv + 1]
                o_ref[rows, hh * dv:(hh + 1) * dv] = acc[:, :dv] / l
                lse_ref[0, rows, hh:hh + 1] = m + jnp.log(l)
            return 0

        lax.fori_loop(0, Sq // tq, qstep, 0)
        _carry_finish(rider, (B, nblk))

    in_specs = [pl.BlockSpec((Sq, wq), lambda b, p: (b, q_cb + p)), pl.BlockSpec((Sk, wq), lambda b, p: (b, k_cb + p)),
                pl.BlockSpec((Sk, wv), lambda b, p: (b, v_cb + p))]
    ins = [q_arr, k_arr, v_arr]
    if fox:
        in_specs.append(pl.BlockSpec((1, hb, Sk), lambda b, p: (b * nblk + p, 0, 0)))
        ins.append(cum_r)
    return _carrier_call(
        body, rider, ins, in_specs,
        [SDS((B * Sq, nblk * wv), f32), SDS((B * nblk, Sq, hb), f32)],
        [pl.BlockSpec((Sq, wv), lambda b, p: (b, p)), pl.BlockSpec((1, Sq, hb), lambda b, p: (b * nblk + p, 0, 0))],
        [pltpu.VMEM((hb, Sq, LANES), bf16), pltpu.VMEM((hb, Sk, LANES), bf16), pltpu.VMEM((hb, Sk, LANES), bf16)],
        (B, nblk), name)


def _attn_bwd(q_arr, q_cb, k_arr, k_cb, v_arr, v_cb, o, do, lse, *, B, Sq, Sk, nblk, hb, dqk, dv, mode, scale, cum_r=None,
              dq_dtype=bf16, rider=None, name):
    wq, wv = hb * dqk, hb * dv
    tq, tk = _attn_tiles(Sq, Sk)
    fox = mode == "fox"
    n_in = 7 if fox else 6
    n_out = 4 if fox else 3
    if fox:
        assert dqk + 2 <= LANES and np.log2(scale) == np.round(np.log2(scale))

    def body(*refs):
        refs = _carry(rider, refs, n_in, n_out, 7 if fox else 6, (B, nblk))
        q_ref, k_ref, v_ref, o_ref, do_ref, lse_ref = refs[:6]
        cr_ref = refs[6] if fox else None
        dq_ref, dk_ref, dv_ref = refs[n_in:n_in + 3]
        dcum_ref = refs[n_in + 3] if fox else None
        qh, kh, vh, doh, dk_acc, dv_acc = refs[n_in + n_out:n_in + n_out + 6]
        dcq_acc = refs[n_in + n_out + 6] if fox else None
        _stage_heads(qh, q_ref, hb, dqk, dqk + 1 if fox else None, 1.0 / scale)
        _stage_heads(kh, k_ref, hb, dqk, dqk if fox else None, 1.0 / scale)
        _stage_heads(vh, v_ref, hb, dv)
        _stage_heads(doh, do_ref, hb, dv)
        dk_acc[...] = jnp.zeros_like(dk_acc)
        dv_acc[...] = jnp.zeros_like(dv_acc)

        def qstep(i, _):
            rows = pl.ds(pl.multiple_of(i * tq, tq), tq)
            qs = [qh[hh, rows, :] for hh in range(hb)]
            dos = [doh[hh, rows, :] for hh in range(hb)]
            dds = [jnp.sum(do_ref[rows, hh * dv:(hh + 1) * dv] * o_ref[rows, hh * dv:(hh + 1) * dv], axis=-1, keepdims=True)
                   for hh in range(hb)]
            lses = [lse_ref[0, rows, hh:hh + 1] for hh in range(hb)]

            def block(c0, w, r0, nr, carry, diag):
                cols = pl.ds(pl.multiple_of(c0, w), w)
                rs = slice(r0, r0 + nr)
                out = []
                for hh in range(hb):
                    k = kh[hh, cols, :]
                    q, do_ = qs[hh][rs], dos[hh][rs]
                    s = lax.dot_general(q, k, _DIMS["nt"], preferred_element_type=f32) * scale
                    if fox:
                        s = s - cr_ref[0, hh:hh + 1, cols]
                    if diag:
                        s = jnp.where(_tile_mask(mode, i * tq + r0, c0, nr, w), s, NEG_INF)
                    p = jnp.exp(s - lses[hh][rs])
                    dv_acc[hh, cols, :] += lax.dot_general(p.astype(bf16), do_, _DIMS["tn"], preferred_element_type=f32)
                    dp = lax.dot_general(do_, vh[hh, cols, :], _DIMS["nt"], preferred_element_type=f32)
                    ds = p * (dp - dds[hh][rs])
                    dsb = (ds * scale).astype(bf16)
                    out.append(carry[hh] + lax.dot_general(dsb, k, _DIMS["nn"], preferred_element_type=f32))
                    dk_acc[hh, cols, :] += lax.dot_general(dsb, q, _DIMS["tn"], preferred_element_type=f32)
                return tuple(out)

            def tile(j, carry):
                return block(j * tk, tk, 0, tq, carry, False)

            carry = tuple(jnp.zeros((tq, LANES), f32) for _ in range(hb))
            if mode == "none":
                carry = lax.fori_loop(0, Sk // tk, tile, carry)
            else:
                nfull = (i * tq) // tk
                carry = lax.fori_loop(0, nfull, tile, carry)
                carry = _diagonal(block, carry, i, tq, tk)
            for hh in range(hb):
                dq_ref[rows, hh * dqk:(hh + 1) * dqk] = carry[hh][:, :dqk].astype(dq_dtype)
                if fox:
                    dcq_acc[rows, hh:hh + 1] = carry[hh][:, dqk:dqk + 1]
            return 0

        lax.fori_loop(0, Sq // tq, qstep, 0)
        for hh in range(hb):
            dk_ref[:, hh * dqk:(hh + 1) * dqk] = dk_acc[hh, :, :dqk].astype(dq_dtype)
            dv_ref[:, hh * dv:(hh + 1) * dv] = dv_acc[hh, :, :dv].astype(bf16)
            if fox:
                dcum_ref[0, :, hh:hh + 1] = dcq_acc[:, hh:hh + 1] - dk_acc[hh, :, dqk + 1:dqk + 2]
        _carry_finish(rider, (B, nblk))

    qs_ =pl.BlockSpec((Sq, wq), lambda b, p: (b, q_cb + p))
    ks_ = pl.BlockSpec((Sk, wq), lambda b, p: (b, k_cb + p))
    vs_ = pl.BlockSpec((Sk, wv), lambda b, p: (b, v_cb + p))
    os_ = pl.BlockSpec((Sq, wv), lambda b, p: (b, p))
    st = pl.BlockSpec((1, Sq, hb), lambda b, p: (b * nblk + p, 0, 0))
    rw = pl.BlockSpec((1, hb, Sk), lambda b, p: (b * nblk + p, 0, 0))
    in_specs = [qs_, ks_, vs_, os_, os_, st] + ([rw] if fox else [])
    ins = [q_arr, k_arr, v_arr, o, do, lse] + ([cum_r] if fox else [])
    out_specs = [pl.BlockSpec((Sq, wq), lambda b, p: (b, p)), pl.BlockSpec((Sk, wq), lambda b, p: (b, p)),
                 pl.BlockSpec((Sk, wv), lambda b, p: (b, p))] + ([st] if fox else [])
    out_shape = [SDS((B * Sq, nblk * wq), dq_dtype), SDS((B * Sk, nblk * wq), dq_dtype), SDS((B * Sk, nblk * wv), bf16)]
    scratch = [pltpu.VMEM((hb, Sq, LANES), bf16), pltpu.VMEM((hb, Sk, LANES), bf16), pltpu.VMEM((hb, Sk, LANES), bf16),
               pltpu.VMEM((hb, Sq, LANES), bf16), pltpu.VMEM((hb, Sk, LANES), f32), pltpu.VMEM((hb, Sk, LANES), f32)]
    if fox:
        assert Sq == Sk
        out_shape.append(SDS((B * nblk, Sq, hb), f32))
        scratch.append(pltpu.VMEM((Sq, hb), f32))
    return _carrier_call(body, rider, ins, in_specs, out_shape, out_specs, scratch, (B, nblk), name)


def _merge_fwd(oa, ob, oc, w_br, proj, D, name):
    T = oa.shape[0]
    tr = _pick(T, 256)
    g0 = C_GATE // D
    o_spec = pl.BlockSpec((tr, BW), lambda i: (i, 0))
    w_spec = pl.BlockSpec((3, BW, D), lambda i: (0, 0, 0))

    def body(oa_ref, ob_ref, oc_ref, w_ref, ga_ref, gb_ref, gc_ref, m_ref):
        tot = jnp.zeros((tr, D), f32)
        for n, (o_ref, g_ref) in enumerate(((oa_ref, ga_ref), (ob_ref, gb_ref), (oc_ref, gc_ref))):
            bp = lax.dot_general(o_ref[...].astype(bf16), w_ref[n], _DIMS["nn"], preferred_element_type=f32)
            tot = tot + jax.nn.sigmoid(g_ref[...]) * bp
        m_ref[...] = tot.astype(bf16)

    gspecs = [pl.BlockSpec((tr, D), lambda i, n=n: (i, g0 + n)) for n in range(3)]
    return pl.pallas_call(
        body, grid=(T // tr,), in_specs=[o_spec, o_spec, o_spec, w_spec, *gspecs],
        out_specs=pl.BlockSpec((tr, D), lambda i: (i, 0)), out_shape=SDS((T, D), bf16), name=name,
        compiler_params=_params(("parallel",)))(oa, ob, oc, w_br, proj, proj, proj)


def _merge_bwd(dm, oa, ob, oc, w_br, proj, D, name, rider=None):
    T = oa.shape[0]
    tr = _pick(T, 256)
    g0 = C_GATE // D
    o_spec = pl.BlockSpec((tr, BW), lambda i: (i, 0))
    d_spec = pl.BlockSpec((tr, D), lambda i: (i, 0))
    w_spec = pl.BlockSpec((3, BW, D), lambda i: (0, 0, 0))

    def body(*refs):
        refs = _carry(rider, refs, 8, 9, 0, (T // tr,))
        dm_ref, oa_ref, ob_ref, oc_ref, w_ref, ga_ref, gb_ref, gc_ref = refs[:8]
        outs = refs[8:]
        dmv = dm_ref[...]
        for n, (o_ref, g_ref) in enumerate(((oa_ref, ga_ref), (ob_ref, gb_ref), (oc_ref, gc_ref))):
            bp = lax.dot_general(o_ref[...].astype(bf16), w_ref[n], _DIMS["nn"], preferred_element_type=f32)
            g = jax.nn.sigmoid(g_ref[...])
            dbp = (dmv * g).astype(bf16)
            outs[n][...] = (dmv * bp * g * (1.0 - g)).astype(bf16)
            outs[3 + n][...] = lax.dot_general(dbp, w_ref[n], _DIMS["nt"], preferred_element_type=f32)
            outs[6 + n][...] = dbp
        _carry_finish(rider, (T // tr,))

    gspecs = [pl.BlockSpec((tr, D), lambda i, n=n: (i, g0 + n)) for n in range(3)]
    return _carrier_call(
        body, rider, [dm, oa, ob, oc, w_br, proj, proj, proj], [d_spec, o_spec, o_spec, o_spec, w_spec, *gspecs],
        [SDS((T, D), bf16)] * 3 + [SDS((T, BW), f32)] * 3 + [SDS((T, D), bf16)] * 3,
        [d_spec] * 3 + [o_spec] * 3 + [d_spec] * 3, [], (T // tr,), name)


def _loss(h, target, name):
    T, D = h.shape
    tr = _pick(T, 256)
    nt = T // tr
    row = pl.BlockSpec((tr, D), lambda i: (i, 0))

    def body(h_ref, t_ref, l_ref, d_ref, acc):
        i = pl.program_id(0)

        @pl.when(i == 0)
        def _():
            acc[...] = jnp.zeros_like(acc)

        e = h_ref[...] - t_ref[...]
        d_ref[...] = e * (1.0 / D)
        acc[...] += jnp.sum(e * e, axis=0, keepdims=True)

        @pl.when(i == nt - 1)
        def _():
            l_ref[...] = jnp.sum(acc[...], axis=-1, keepdims=True) * (0.5 / D)

    return pl.pallas_call(
        body, grid=(nt,), in_specs=[row, row], out_specs=[pl.BlockSpec((1, 1), lambda i: (0, 0)), row],
        out_shape=[SDS((1, 1), f32), SDS((T, D), f32)], scratch_shapes=[pltpu.VMEM((1, D), f32)], name=name,
        compiler_params=_params(("arbitrary",)))(h, target)


def _adamw(w, m, v, parts, name):
    shape = w.shape
    C = shape[-1]
    R = int(np.prod(shape[:-1])) if len(shape) > 1 else 1
    n_parts, Cp = parts.shape[0], parts.shape[-1]
    tr = _pick(R, 256)
    spec = pl.BlockSpec((tr, C), lambda i: (i, 0))

    def body(w_ref, m_ref, v_ref, p_ref, g_ref, d_ref, mo_ref, vo_ref):
        gv = p_ref[0].astype(f32)
        for q in range(1, n_parts):
            gv = gv + p_ref[q].astype(f32)
        gv = gv[:, :C]
        mn = ADAM_B1 * m_ref[...] + (1.0 - ADAM_B1) * gv
        vn = ADAM_B2 * v_ref[...] + (1.0 - ADAM_B2) * (gv * gv)
        m_hat = mn / (1.0 - ADAM_B1 ** ADAM_STEP)
        v_hat = vn / (1.0 - ADAM_B2 ** ADAM_STEP)
        g_ref[...] = gv
        d_ref[...] = -ADAM_LR * (m_hat / (jnp.sqrt(v_hat) + ADAM_EPS) + ADAM_WD * w_ref[...])
        mo_ref[...] = mn
        vo_ref[...] = vn

    outs = pl.pallas_call(
        body, grid=(R // tr,), in_specs=[spec] * 3 + [pl.BlockSpec((n_parts, tr, Cp), lambda i: (0, i, 0))], out_specs=[spec] * 4,
        out_shape=[SDS((R, C), f32)] * 4, name=name,
        compiler_params=_params(("parallel",)))(*[a.reshape(R, C) for a in (w, m, v)], parts.reshape(n_parts, R, Cp))
    return [o.reshape(shape) for o in outs]


def _coords():
    return lax.axis_index("x"), lax.axis_index("y"), lax.axis_index("c")


def _window(ref, axis, idx, n):
    start = pl.multiple_of(idx * n, n)
    if axis == 1:
        return ref.at[:, pl.ds(start, n), :]
    return ref.at[:, :, pl.ds(start, n)]


class _Rider:
    def __init__(self, inputs, out_shapes, aliases, scratch, start, finish):
        self.inputs, self.out_shapes, self.aliases, self.scratch = list(inputs), list(out_shapes), dict(aliases), list(scratch)
        self.start, self.finish = start, finish


def _join_riders(riders):
    riders = [r for r in riders if r is not None]
    if len(riders) < 2:
        return riders[0] if riders else None
    ni = np.cumsum([0] + [len(r.inputs) for r in riders])
    no = np.cumsum([0] + [len(r.out_shapes) for r in riders])
    ns = np.cumsum([0] + [len(r.scratch) for r in riders])
    aliases = {int(ni[k]) + i: int(no[k]) + o for k, r in enumerate(riders) for i, o in r.aliases.items()}

    def phase(which):
        def run(ins, outs, scr):
            for k, r in enumerate(riders):
                getattr(r, which)(ins[ni[k]:ni[k + 1]], outs[no[k]:no[k + 1]], scr[ns[k]:ns[k + 1]])
        return run

    return _Rider(sum([r.inputs for r in riders], []), sum([r.out_shapes for r in riders], []), aliases,
                  sum([r.scratch for r in riders], []), phase("start"), phase("finish"))


def _run_rider(r, name):
    ni, no = len(r.inputs), len(r.out_shapes)

    def body(*refs):
        ins, outs, scr = refs[:ni], refs[ni:ni + no], refs[ni + no:]
        r.start(ins, outs, scr)
        r.finish(ins, outs, scr)

    any_spec = pl.BlockSpec(memory_space=pl.ANY)
    return pl.pallas_call(body, out_shape=r.out_shapes, in_specs=[any_spec] * ni, out_specs=[any_spec] * no,
                          input_output_aliases=r.aliases, scratch_shapes=r.scratch, name=name)(*r.inputs)


def _gather_rider(shards, axes):
    nt = len(shards)
    sizes = [s.shape[a] for s, a in zip(shards, axes)]
    out_shapes = [SDS(tuple(d * N_DEV if i == a else d for i, d in enumerate(s.shape)), s.dtype) for s, a in zip(shards, axes)]

    def plan(x_refs, out_refs, scr):
        send_sems, recv_sems, local_sems = scr
        x, y, c = _coords()
        me, sibling = (x, y, c), (x, y, 1 - c)
        chips = [(1 - x, y), (x, 1 - y), (1 - x, 1 - y)]

        def win(t, px, py, pc):
            return _window(out_refs[t], axes[t], 4 * px + 2 * py + pc, sizes[t])

        def copy(t, k, block, to, src=None):
            return pltpu.make_async_remote_copy(
                src_ref=win(t, *block) if src is None else src, dst_ref=win(t, *block),
                send_sem=send_sems.at[7 * t + k], recv_sem=recv_sems.at[7 * t + k], device_id=to, device_id_type=MESH)

        mine = [pltpu.make_async_copy(x_refs[t], win(t, *me), local_sems.at[t]) for t in range(nt)]
        first = []
        for t in range(nt):
            first.append(copy(t, 0, me, sibling, src=x_refs[t]))
            first += [copy(t, 1 + j, me, (*chip, c), src=x_refs[t]) for j, chip in enumerate(chips)]
        return c, me, sibling, chips, copy, mine, first

    def start(x_refs, out_refs, scr):
        _, _, _, _, _, mine, first = plan(x_refs, out_refs, scr)
        for cp in mine + first:
            cp.start()

    def finish(x_refs, out_refs, scr):
        c, me, sibling, chips, copy, mine, first = plan(x_refs, out_refs, scr)
        passed = []
        for j, chip in enumerate(chips):
            for t in range(nt):
                copy(t, 1 + j, (*chip, c), me).wait_recv()
                fwd = copy(t, 4 + j, (*chip, c), sibling)
                fwd.start()
                passed.append(fwd)
        for t in range(nt):
            copy(t, 0, sibling, me).wait_recv()
            for j, chip in enumerate(chips):
                copy(t, 4 + j, (*chip, 1 - c), me).wait_recv()
        for cp in first + passed:
            cp.wait_send()
        for cp in mine:
            cp.wait()

    scratch = [pltpu.SemaphoreType.DMA((7 * nt,)), pltpu.SemaphoreType.DMA((7 * nt,)), pltpu.SemaphoreType.DMA((nt,))]
    return _Rider(shards, out_shapes, {}, scratch, start, finish)


N_CHIP = N_DEV // 2


def _pair_exchange_rider(grads, axes, shards):
    nt = len(grads)
    sizes = [s[a] for s, a in zip(shards, axes)]

    def plan(ins, outs, scr):
        send_sems, recv_sems = scr
        x, y, c = _coords()
        return [pltpu.make_async_remote_copy(
            src_ref=_window(ins[t], axes[t], 2 * q + (1 - c), sizes[t]), dst_ref=outs[t].at[q],
            send_sem=send_sems.at[N_CHIP * t + q], recv_sem=recv_sems.at[N_CHIP * t + q],
            device_id=(x, y, 1 - c), device_id_type=MESH) for t in range(nt) for q in range(N_CHIP)]

    def start(ins, outs, scr):
        for cp in plan(ins, outs, scr):
            cp.start()

    def finish(ins, outs, scr):
        copies = plan(ins, outs, scr)
        for cp in copies:
            cp.wait_recv()
        for cp in copies:
            cp.wait_send()

    scratch = [pltpu.SemaphoreType.DMA((N_CHIP * nt,)), pltpu.SemaphoreType.DMA((N_CHIP * nt,))]
    return _Rider(grads, [SDS((N_CHIP,) + tuple(s), f32) for s in shards], {}, scratch, start, finish)


def _pair_sum(grad, buf, axis, core, name):
    _, G, r, c = buf.shape
    tr = _pick(r, 512)
    nr = r // tr

    def body(core_ref, g_ref, b_ref, o_ref):
        o_ref[0, 0] = (g_ref[0] + b_ref[0, 0]).astype(bf16)

    if axis == 2:
        g_spec = pl.BlockSpec((1, tr, c), lambda q, g, i, core_ref: (g, i, 2 * q + core_ref[0]))
    else:
        g_spec = pl.BlockSpec((1, tr, c), lambda q, g, i, core_ref: (g, (2 * q + core_ref[0]) * nr + i, 0))
    blk = pl.BlockSpec((1, 1, tr, c), lambda q, g, i, core_ref: (q, g, i, 0))
    grid_spec = pltpu.PrefetchScalarGridSpec(num_scalar_prefetch=1, grid=(N_CHIP, G, nr), in_specs=[g_spec, blk], out_specs=blk)
    return pl.pallas_call(body, grid_spec=grid_spec, out_shape=SDS(buf.shape, bf16), name=name,
                          compiler_params=_params(("parallel", "parallel", "parallel")))(core, grad, buf)


def _chip_exchange_rider(chips, slots, layer):
    nt = len(chips)

    def plan(ins, outs, scr):
        send_sems, recv_sems, local_sems = scr
        x, y, c = _coords()
        me = 2 * x + y
        mine = [pltpu.make_async_copy(ins[t].at[me], outs[t].at[me, layer], local_sems.at[t]) for t in range(nt)]
        copies = []
        for k in range(1, N_CHIP):
            px, py = lax.rem(x + (k >> 1), 2), lax.rem(y + (k & 1), 2)
            copies += [pltpu.make_async_remote_copy(
                src_ref=ins[t].at[2 * px + py], dst_ref=outs[t].at[me, layer],
                send_sem=send_sems.at[3 * t + k - 1], recv_sem=recv_sems.at[3 * t + k - 1],
                device_id=(px, py, c), device_id_type=MESH) for t in range(nt)]
        return mine, copies

    def start(ins, outs, scr):
        mine, copies = plan(ins, outs, scr)
        for cp in mine + copies:
            cp.start()

    def finish(ins, outs, scr):
        mine, copies = plan(ins, outs, scr)
        for cp in copies:
            cp.wait_recv()
        for cp in copies:
            cp.wait_send()
        for cp in mine:
            cp.wait()

    scratch = [pltpu.SemaphoreType.DMA((3 * nt,)), pltpu.SemaphoreType.DMA((3 * nt,)), pltpu.SemaphoreType.DMA((nt,))]
    return _Rider(list(chips) + list(slots), [SDS(s.shape, s.dtype) for s in slots], {nt + t: t for t in range(nt)}, scratch,
                  start, finish)


_BIG = (("w_in", 2), ("w_uq", 2), ("w_ukv", 2), ("w_mem_kv", 1), ("w_br", 2), ("w_out", 1), ("w_ff1", 2), ("w_ff2", 1))


def _in_col_maps(D):
    d_in = O_GATE + 3 * D
    sh = d_in // N_DEV
    shp = _round_up(sh, LANES)
    perm_of_orig = np.empty(d_in, np.int64)
    for a, b, dst in ((0, O_F, C_Q), (O_F, O_CQ, C_F), (O_CQ, O_CKV, C_CQ), (O_CKV, O_KR, C_CKV), (O_KR, O_QM, C_KPE + NOPE),
                      (O_QM, O_GATE, C_QM), (O_GATE, d_in, C_GATE)):
        perm_of_orig[a:b] = dst + np.arange(b - a)
    orig = np.arange(d_in)
    gath_of_orig = (orig // sh) * shp + orig % sh
    fwd = -np.ones(C_GATE + 3 * D, np.int64)
    fwd[perm_of_orig] = gath_of_orig
    bwd = -np.ones(N_DEV * shp, np.int64)
    bwd[gath_of_orig] = perm_of_orig
    return fwd, bwd


def _lane_pad(a):
    c = a.shape[-1]
    return jnp.pad(a, [(0, 0)] * (a.ndim - 1) + [(0, _round_up(c, LANES) - c)])


def _as3(a):
    return a.reshape((-1,) + a.shape[-2:]) if a.ndim != 3 else a


def _rope_tables(positions):
    inv_freq = ROPE_BASE ** (-jnp.arange(0, ROPE, 2, dtype=f32) / ROPE)
    ang = positions.astype(f32).reshape(-1, 1) * inv_freq
    cos, sin = jnp.cos(ang), jnp.sin(ang)
    T = ang.shape[0]
    one, zero = jnp.ones((T, NOPE), f32), jnp.zeros((T, 16), f32)
    c = jnp.concatenate([one, cos, cos, jnp.ones((T, 32), f32)], axis=1)
    sa = jnp.concatenate([jnp.zeros((T, NOPE), f32), -sin, zero, jnp.zeros((T, 32), f32)], axis=1)
    sb = jnp.concatenate([jnp.zeros((T, NOPE), f32), zero, sin, jnp.zeros((T, 32), f32)], axis=1)
    return c, sa, sb


def kernel(x, mem, positions, ln_in_g, ln_in_b, w_in, b_forget, w_uq, g_cq, w_ukv, g_ckv, w_mem_kv, w_br, w_out, ln1_g, ln1_b, w_ff1, w_ff2, ln2_g, ln2_b, loss_target, m_ln_in_g, m_ln_in_b, m_w_in, m_b_forget, m_w_uq, m_g_cq, m_w_ukv, m_g_ckv, m_w_mem_kv, m_w_br, m_w_out, m_ln1_g, m_ln1_b, m_w_ff1, m_w_ff2, m_ln2_g, m_ln2_b, v_ln_in_g, v_ln_in_b, v_w_in, v_b_forget, v_w_uq, v_g_cq, v_w_ukv, v_g_ckv, v_w_mem_kv, v_w_br, v_w_out, v_ln1_g, v_ln1_b, v_w_ff1, v_w_ff2, v_ln2_g, v_ln2_b):
    B, S, D = x.shape
    T = B * S
    L = w_in.shape[0]
    NM = mem.shape[1]
    alpha = (2 * L) ** 0.25
    weights = dict(ln_in_g=ln_in_g, ln_in_b=ln_in_b, w_in=w_in, b_forget=b_forget, w_uq=w_uq, g_cq=g_cq, w_ukv=w_ukv, g_ckv=g_ckv,
                   w_mem_kv=w_mem_kv, w_br=w_br, w_out=w_out, ln1_g=ln1_g, ln1_b=ln1_b, w_ff1=w_ff1, w_ff2=w_ff2, ln2_g=ln2_g, ln2_b=ln2_b)
    mom_m = dict(ln_in_g=m_ln_in_g, ln_in_b=m_ln_in_b, w_in=m_w_in, b_forget=m_b_forget, w_uq=m_w_uq, g_cq=m_g_cq, w_ukv=m_w_ukv,
                 g_ckv=m_g_ckv, w_mem_kv=m_w_mem_kv, w_br=m_w_br, w_out=m_w_out, ln1_g=m_ln1_g, ln1_b=m_ln1_b, w_ff1=m_w_ff1,
                 w_ff2=m_w_ff2, ln2_g=m_ln2_g, ln2_b=m_ln2_b)
    mom_v = dict(ln_in_g=v_ln_in_g, ln_in_b=v_ln_in_b, w_in=v_w_in, b_forget=v_b_forget, w_uq=v_w_uq, g_cq=v_g_cq, w_ukv=v_w_ukv,
                 g_ckv=v_g_ckv, w_mem_kv=v_w_mem_kv, w_br=v_w_br, w_out=v_w_out, ln1_g=v_ln1_g, ln1_b=v_ln1_b, w_ff1=v_w_ff1,
                 w_ff2=v_w_ff2, ln2_g=v_ln2_g, ln2_b=v_ln2_b)
    order = list(weights)
    big_axes = [ax for _, ax in _BIG]
    fwd_map, bwd_map = _in_col_maps(D)
    shard3 = {n: _lane_pad(weights[n]) for n, _ in _BIG}

    half = len(_BIG) // 2

    def gather_rider(l, lo=0, hi=len(_BIG)):
        return _gather_rider([_as3(shard3[n][l]).astype(bf16) for n, _ in _BIG[lo:hi]], big_axes[lo:hi])

    def named(full, lo, hi):
        return {n: (f if n == "w_br" else f[0]) for (n, _), f in zip(_BIG[lo:hi], full)}

    tabs = _rope_tables(positions)
    x2d, mem2d, tgt2d = x.reshape(T, D), mem.reshape(B * NM, D), loss_target.reshape(T, D)
    fox = dict(B=B, Sq=S, Sk=S, nblk=FOX_HEADS // 2, hb=2, dqk=FOX_HD, dv=FOX_HD, mode="fox", scale=FOX_HD ** -0.5)
    mla = dict(B=B, Sq=S, Sk=S, nblk=MLA_HEADS // 2, hb=2, dqk=LANES, dv=MLA_VD, mode="chunk", scale=(NOPE + ROPE) ** -0.5)
    memat = dict(B=B, Sq=S, Sk=NM, nblk=MEM_HEADS, hb=1, dqk=MEM_HD, dv=MEM_HD, mode="none", scale=MEM_HD ** -0.5)
    nb = FOX_HEADS // 2

    h, hb, xh0, rstd0 = _ln_fwd(None, x2d, ln_in_g, ln_in_b, alpha, "ln_in")
    saved = []
    first = _run_rider(gather_rider(0, 0, half), "weights_all_gather")
    for l in range(L):
        W = named(first, 0, half)
        W["w_in"] = _lane_permute(W["w_in"], fwd_map, bf16, "w_in_reorder")
        bpad = jnp.pad(b_forget[l].reshape(1, FOX_HEADS), ((0, 0), (0, LANES - FOX_HEADS)))
        proj = _mm(hb, W["w_in"], name="proj")
        cum = _gate_fwd(proj, bpad, B, S, "gate_fwd")
        cum_r = cum[:, :FOX_HEADS].reshape(B, S, nb, 2).transpose(0, 2, 3, 1).reshape(B * nb, 2, S)
        more = l + 1 < L
        (o_a, lse_a), second = _attn_fwd(proj, C_Q // LANES, proj, C_K // LANES, proj, C_V // LANES, cum_r=cum_r, name="fox_fwd",
                                         rider=gather_rider(l, half), **fox)
        W.update(named(second, half, len(_BIG)))
        cqn = _rms_fwd(proj, C_CQ, Q_RANK, g_cq[l], "rms_q")
        ckvn = _rms_fwd(proj, C_CKV, KV_RANK, g_ckv[l], "rms_kv")
        qraw = _mm(cqn, W["w_uq"], name="q_up")
        kv = _mm(ckvn, W["w_ukv"], name="kv_up")
        qf = _rope_q(qraw, tabs, bwd=False, name="rope_q")
        kf, v_b = _rope_k(kv, tabs, proj, "rope_k")
        res = _attn_fwd(qf, 0, kf, 0, v_b, 0, name="mla_fwd", rider=gather_rider(l + 1, 0, half) if more else None, **mla)
        (o_b, lse_b), first = res if more else (res, None)
        mkv = _mm(mem2d, W["w_mem_kv"], name="mem_kv")
        o_c, lse_c = _attn_fwd(proj, C_QM // MEM_HD, mkv, 0, mkv, MEM_HEADS, name="mem_fwd", **memat)
        merged = _merge_fwd(o_a, o_b, o_c, W["w_br"], proj, D, "merge_fwd")
        y = _mm(merged, W["w_out"], name="out_proj")
        h1, h1b, xh1, rstd1 = _ln_fwd(h, y, ln1_g[l], ln1_b[l], alpha, "ln1")
        a, r = _mm(h1b, W["w_ff1"], epi="relu2", name="ff1")
        ff = _mm(a, W["w_ff2"], name="ff2")
        h2, h2b, xh2, rstd2 = _ln_fwd(h1, ff, ln2_g[l], ln2_b[l], alpha, "ln2")
        saved.append(dict(W=W, bpad=bpad, h=hb, proj=proj, cum_r=cum_r, o_a=o_a, lse_a=lse_a, cqn=cqn, ckvn=ckvn, qf=qf, kf=kf,
                          v_b=v_b, o_b=o_b, lse_b=lse_b, mkv=mkv, o_c=o_c, lse_c=lse_c, merged=merged, h1=h1b, xh1=xh1,
                          rstd1=rstd1, a=a, r=r, xh2=xh2, rstd2=rstd2))
        h, hb = h2, h2b

    loss_local, dh = _loss(h, tgt2d, "loss")
    loss = lax.psum(loss_local[0, 0], ("x", "y", "c"))

    shard_shapes = [_as3(shard3[n][0]).shape for n, _ in _BIG]
    slots = [lax.empty((N_CHIP, L) + s, bf16) for s in shard_shapes]
    core = lax.axis_index("c").astype(jnp.int32).reshape(1)
    small = {n: [None] * L for n in ("b_forget", "g_cq", "g_ckv", "ln1_g", "ln1_b", "ln2_g", "ln2_b")}
    ax_a, ax_c = big_axes[:half], big_axes[half:]
    sh_a, sh_c = shard_shapes[:half], shard_shapes[half:]

    def pair_sums(gs, bufs, axes):
        return [_pair_sum(g, b, ax, core, "grads_pair_sum") for g, b, ax in zip(gs, bufs, axes)]

    pending = unpaired = None
    for l in reversed(range(L)):
        sv = saved[l]
        W = sv["W"]
        du2, du2b, dg2, db2 = _ln_bwd(dh, sv["xh2"], sv["rstd2"], ln2_g[l], "ln2_bwd")
        dz = _mm(du2b, W["w_ff2"], mode="nt", epi="mul", extra=sv["r"], out_dtype=bf16, name="ff2_dx")
        dw_ff2 = _mm(sv["a"], du2b, mode="tn", name="ff2_dw")
        dh1 = _mm(dz, W["w_ff1"], mode="nt", epi="add", extra=du2, alpha=alpha, name="ff1_dx")
        dw_ff1 = _mm(sv["h1"], dz, mode="tn", name="ff1_dw")
        du1, du1b, dg1, db1 = _ln_bwd(dh1, sv["xh1"], sv["rstd1"], ln1_g[l], "ln1_bwd")
        dmerged = _mm(du1b, W["w_out"], mode="nt", name="out_dx")
        dw_out = _mm(sv["merged"], du1b, mode="tn", name="out_dw")
        mb = _merge_bwd(dmerged, sv["o_a"], sv["o_b"], sv["o_c"], W["w_br"], sv["proj"], D, "merge_bwd",
                        rider=None if unpaired is None else _pair_exchange_rider(unpaired[0], ax_a, sh_a))
        if unpaired is not None:
            mb, bufs = mb
            pending = (pair_sums(unpaired[0], bufs, ax_a), unpaired[1])
        dgl, do, dbp = mb[0:3], mb[3:6], mb[6:9]
        dw_br = jnp.stack([_mm(o_n, dbp_n, mode="tn", name="br_dw") for o_n, dbp_n in zip((sv["o_a"], sv["o_b"], sv["o_c"]), dbp)])
        late = [_as3(g) for g in (dw_br, dw_out, dw_ff1, dw_ff2)]
        proj = sv["proj"]
        chip_rider = None if pending is None else _chip_exchange_rider(pending[0], slots[:half], pending[1])
        own, got = _attn_bwd(proj, C_Q // LANES, proj, C_K // LANES, proj, C_V // LANES, sv["o_a"], do[0], sv["lse_a"],
                             cum_r=sv["cum_r"], name="fox_bwd",
                             rider=_join_riders([chip_rider, _pair_exchange_rider(late, ax_c, sh_c)]), **fox)
        dq_a, dk_a, dv_a, dcum = own
        if chip_rider is not None:
            slots[:half], got = got[:half], got[half:]
        dcum = dcum.reshape(B, nb, S, 2).transpose(0, 2, 1, 3).reshape(T, FOX_HEADS)
        dcum = jnp.pad(dcum, ((0, 0), (0, LANES - FOX_HEADS)))
        dzf, dbf = _gate_bwd(dcum, proj, sv["bpad"], B, S, "gate_bwd")
        (dqf, dkf, dv_b), slots[half:] = _attn_bwd(
            sv["qf"], 0, sv["kf"], 0, sv["v_b"], 0, sv["o_b"], do[1], sv["lse_b"], dq_dtype=f32, name="mla_bwd",
            rider=_chip_exchange_rider(pair_sums(late, got, ax_c), slots[half:], l), **mla)
        dqraw = _rope_q(dqf, tabs, bwd=True, name="rope_q_bwd")
        dkv, dkpe = _rope_k_bwd(dkf, dv_b, tabs, "rope_k_bwd")
        dw_uq = _mm(sv["cqn"], dqraw, mode="tn", name="q_up_dw")
        dcqn = _mm(dqraw, W["w_uq"], mode="nt", name="q_up_dx")
        dcq_, dg_cq = _rms_bwd(dcqn, proj, C_CQ, Q_RANK, g_cq[l], "rms_q_bwd")
        dw_ukv = _mm(sv["ckvn"], dkv, mode="tn", name="kv_up_dw")
        dckvn = _mm(dkv, W["w_ukv"], mode="nt", name="kv_up_dx")
        dckv, dg_ckv = _rms_bwd(dckvn, proj, C_CKV, KV_RANK, g_ckv[l], "rms_kv_bwd")
        dqm, dmk, dmv = _attn_bwd(proj, C_QM // MEM_HD, sv["mkv"], 0, sv["mkv"], MEM_HEADS, sv["o_c"], do[2], sv["lse_c"],
                                  name="mem_bwd", **memat)
        dw_mem = _mm(mem2d, jnp.concatenate([dmk, dmv], axis=1), mode="tn", name="mem_kv_dw")
        dproj = jnp.concatenate([dq_a, dk_a, dv_a, dzf, dkpe, dckv, dqm, jnp.zeros((T, C_CQ - C_PAD), bf16), dcq_, *dgl], axis=1)
        dw_in = _lane_permute(_mm(sv["h"], dproj, mode="tn", name="proj_dw"), bwd_map, f32, "w_in_grad_reorder")
        dh = _mm(dproj, W["w_in"], mode="nt", epi="add", extra=du1, alpha=alpha, name="proj_dx")
        unpaired = ([_as3(g) for g in (dw_in, dw_uq, dw_ukv, dw_mem)], l)
        for n, val in (("b_forget", dbf[:, :FOX_HEADS]), ("g_cq", dg_cq), ("g_ckv", dg_ckv), ("ln1_g", dg1), ("ln1_b", db1),
                       ("ln2_g", dg2), ("ln2_b", db2)):
            small[n][l] = val.reshape(-1)
    bufs = _run_rider(_pair_exchange_rider(unpaired[0], ax_a, sh_a), "grads_pair_exchange")
    slots[:half] = _run_rider(_chip_exchange_rider(pair_sums(unpaired[0], bufs, ax_a), slots[:half], unpaired[1]),
                              "grads_chip_exchange")
    grad_x, _, dg_in, db_in = _ln_bwd(dh, xh0, rstd0, ln_in_g, "ln_in_bwd")

    small_list = [("ln_in_g", dg_in.reshape(-1)), ("ln_in_b", db_in.reshape(-1))] + [(n, jnp.stack(v).reshape(-1)) for n, v in small.items()]
    sflat = jnp.concatenate([v for _, v in small_list])
    n_small = sflat.shape[0]
    rs = _round_up(-(-n_small // LANES), 8)
    spacked = jnp.pad(sflat, (0, rs * LANES - n_small)).reshape(1, rs, LANES)
    sparts = _run_rider(_gather_rider([spacked], [1]), "small_all_gather")[0].reshape(N_DEV, rs * LANES)

    grads, deltas, new_m, new_v = {}, {}, {}, {}
    off = 0
    for n, vv in small_list:
        parts = sparts[:, off:off + vv.shape[0]].reshape((N_DEV,) + weights[n].shape)
        off += vv.shape[0]
        grads[n], deltas[n], new_m[n], new_v[n] = _adamw(weights[n], mom_m[n], mom_v[n], parts, "adamw_" + n)
    for (n, _), sl in zip(_BIG, slots):
        parts = sl.reshape((N_CHIP,) + weights[n].shape[:-1] + (sl.shape[-1],))
        grads[n], deltas[n], new_m[n], new_v[n] = _adamw(weights[n], mom_m[n], mom_v[n], parts, "adamw_" + n)
    return (loss, grad_x.reshape(B, S, D), *[grads[n] for n in order], *[deltas[n] for n in order], *[new_m[n] for n in order],
            *[new_v[n] for n in order])
```

```python
import functools

import numpy as np
import jax
import jax.numpy as jnp
from jax import lax
from jax.experimental import pallas as pl
from jax.experimental.pallas import tpu as pltpu

f32, bf16 = jnp.float32, jnp.bfloat16
SDS = jax.ShapeDtypeStruct

N_DEV = 8
MESH = pl.DeviceIdType.MESH
LANES = 128
VMEM_LIMIT = 56 * 1024 * 1024

FOX_HEADS, FOX_HD = 8, 64
MLA_HEADS, NOPE, ROPE, MLA_VD = 8, 64, 32, 64
Q_RANK, KV_RANK = 384, 256
ROPE_BASE = 10000.0
MEM_HEADS, MEM_HD = 4, 128
BW = 512
CHUNK = 64
LN_EPS, RMS_EPS, NEG_INF = 1e-5, 1e-6, -1e30
ADAM_LR, ADAM_B1, ADAM_B2, ADAM_EPS, ADAM_WD, ADAM_STEP = 0.001, 0.9, 0.999, 1e-08, 0.01, 10

ATT_TQ, ATT_TK = 512, 512

C_Q, C_K, C_V, C_F, C_KPE, C_CKV, C_QM, C_PAD, C_CQ, C_GATE = 0, 512, 1024, 1536, 1664, 1792, 2048, 2560, 2688, 3072
O_F, O_CQ, O_CKV, O_KR, O_QM, O_GATE = 1536, 1544, 1928, 2184, 2216, 2728


def _pick(dim, pref):
    if dim <= pref:
        return dim
    for c in (2048, 1024, 768, 512, 384, 256, 128, 64, 32, 16, 8):
        if c <= pref and dim % c == 0:
            return c
    return dim


def _round_up(n, m):
    return -(-n // m) * m


def _params(sem=None):
    return pltpu.CompilerParams(dimension_semantics=sem, vmem_limit_bytes=VMEM_LIMIT)


_DIMS = {"nn": (((1,), (0,)), ((), ())), "nt": (((1,), (1,)), ((), ())), "tn": (((0,), (0,)), ((), ()))}


def _mm(a, b, *, mode="nn", out_dtype=f32, epi="none", extra=None, alpha=1.0, tm=1024, tn=1024, tk=None, name):
    if mode == "nn":
        (M, K), (_, N) = a.shape, b.shape
    elif mode == "nt":
        (M, K), (N, _) = a.shape, b.shape
    else:
        (K, M), (_, N) = a.shape, b.shape
    tm, tn, tk = _pick(M, tm), _pick(N, tn), _pick(K, tk or 2048)
    nk = K // tk
    a_spec = pl.BlockSpec((tk, tm), lambda i, j, k: (k, i)) if mode == "tn" else pl.BlockSpec((tm, tk), lambda i, j, k: (i, k))
    b_spec = pl.BlockSpec((tn, tk), lambda i, j, k: (j, k)) if mode == "nt" else pl.BlockSpec((tk, tn), lambda i, j, k: (k, j))
    o_spec = pl.BlockSpec((tm, tn), lambda i, j, k: (i, j))
    n_out = 2 if epi == "relu2" else 1
    dims = _DIMS[mode]

    def body(*refs):
        a_ref, b_ref = refs[0], refs[1]
        e_ref = refs[2] if extra is not None else None
        outs = refs[2 + (extra is not None):2 + (extra is not None) + n_out]

        def finish(r):
            if epi == "none":
                outs[0][...] = r.astype(out_dtype)
            elif epi == "relu2":
                p = jnp.maximum(r, 0.0)
                outs[0][...] = (p * p).astype(bf16)
                outs[1][...] = (2.0 * p).astype(bf16)
            elif epi == "mul":
                outs[0][...] = (r * e_ref[...].astype(f32)).astype(out_dtype)
            else:
                outs[0][...] = (r + alpha * e_ref[...].astype(f32)).astype(out_dtype)

        part = lax.dot_general(a_ref[...].astype(bf16), b_ref[...].astype(bf16), dims, preferred_element_type=f32)
        if nk == 1:
            finish(part)
            return
        acc = refs[-1]
        k = pl.program_id(2)

        @pl.when(k == 0)
        def _():
            acc[...] = part

        @pl.when(k > 0)
        def _():
            acc[...] += part

        @pl.when(k == nk - 1)
        def _():
            finish(acc[...])

    ins, in_specs = [a, b], [a_spec, b_spec]
    if extra is not None:
        ins.append(extra)
        in_specs.append(o_spec)
    if epi == "relu2":
        out_shape, out_specs = [SDS((M, N), bf16), SDS((M, N), bf16)], [o_spec, o_spec]
    else:
        out_shape, out_specs = SDS((M, N), out_dtype), o_spec
    return pl.pallas_call(
        body, grid=(M // tm, N // tn, nk), in_specs=in_specs, out_specs=out_specs, out_shape=out_shape,
        scratch_shapes=[pltpu.VMEM((tm, tn), f32)] if nk > 1 else [], name=name,
        compiler_params=_params(("parallel", "parallel", "arbitrary")))(*ins)


def _lane_permute(src, cmap, out_dtype, name):
    R, Cs = src.shape
    cmap = np.asarray(cmap, np.int64)
    nb = cmap.shape[0] // LANES
    srcs = [sorted({int(c) // LANES for c in cmap[b * LANES:(b + 1) * LANES] if c >= 0}) for b in range(nb)]
    exact = src.dtype == f32
    tr = _pick(R, 256 if exact else 512)

    def body(src_ref, cm_ref, o_ref):
        rows = lax.broadcasted_iota(jnp.int32, (LANES, LANES), 0)
        for b in range(nb):
            tbl = cm_ref[:, b * LANES:(b + 1) * LANES]
            acc = jnp.zeros((tr, LANES), f32)
            for sb in srcs[b]:
                sel = (rows + sb * LANES) == tbl
                blk = src_ref[:, sb * LANES:(sb + 1) * LANES]
                if exact:
                    acc = acc + jnp.dot(blk, sel.astype(f32), precision=lax.Precision.HIGHEST, preferred_element_type=f32)
                else:
                    acc = acc + jnp.dot(blk, sel.astype(blk.dtype), preferred_element_type=f32)
            o_ref[:, b * LANES:(b + 1) * LANES] = acc.astype(out_dtype)

    return pl.pallas_call(
        body, grid=(R // tr,), in_specs=[pl.BlockSpec((tr, Cs), lambda i: (i, 0)), pl.BlockSpec((1, nb * LANES), lambda i: (0, 0))],
        out_specs=pl.BlockSpec((tr, nb * LANES), lambda i: (i, 0)), out_shape=SDS((R, nb * LANES), out_dtype), name=name,
        compiler_params=_params(("parallel",)))(src, jnp.asarray(cmap.astype(np.int32)).reshape(1, -1))


def _ln_fwd(h, y, g, b, alpha, name):
    T, D = y.shape
    tr = _pick(T, 256)
    row = pl.BlockSpec((tr, D), lambda i: (i, 0))
    vec = pl.BlockSpec((1, D), lambda i: (0, 0))
    col = pl.BlockSpec((tr, 1), lambda i: (i, 0))

    def body(*refs):
        if h is None:
            y_ref, g_ref, b_ref, o_ref, ob_ref, xh_ref, rs_ref = refs
            u = y_ref[...]
        else:
            h_ref, y_ref, g_ref, b_ref, o_ref, ob_ref, xh_ref, rs_ref = refs
            u = alpha * h_ref[...] + y_ref[...]
        mu = jnp.mean(u, axis=-1, keepdims=True)
        d = u - mu
        var = jnp.mean(d * d, axis=-1, keepdims=True)
        rstd = lax.rsqrt(var + LN_EPS)
        xh = d * rstd
        xh_ref[...] = xh
        rs_ref[...] = rstd
        o = xh * g_ref[...] + b_ref[...]
        o_ref[...] = o
        ob_ref[...] = o.astype(bf16)

    ins = ([] if h is None else [h]) + [y, g.reshape(1, D), b.reshape(1, D)]
    specs = ([] if h is None else [row]) + [row, vec, vec]
    return pl.pallas_call(
        body, grid=(T // tr,), in_specs=specs, out_specs=[row, row, row, col],
        out_shape=[SDS((T, D), f32), SDS((T, D), bf16), SDS((T, D), f32), SDS((T, 1), f32)], name=name,
        compiler_params=_params(("parallel",)))(*ins)


def _ln_bwd(dy, xh, rstd, g, name):
    T, D = dy.shape
    tr = _pick(T, 256)
    row = pl.BlockSpec((tr, D), lambda i: (i, 0))
    vec = pl.BlockSpec((1, D), lambda i: (0, 0))
    col = pl.BlockSpec((tr, 1), lambda i: (i, 0))

    def body(dy_ref, xh_ref, rs_ref, g_ref, du_ref, dub_ref, dg_ref, db_ref):
        @pl.when(pl.program_id(0) == 0)
        def _():
            dg_ref[...] = jnp.zeros_like(dg_ref)
            db_ref[...] = jnp.zeros_like(db_ref)

        dyv, xhv = dy_ref[...], xh_ref[...]
        dxh = dyv * g_ref[...]
        m1 = jnp.mean(dxh, axis=-1, keepdims=True)
        m2 = jnp.mean(dxh * xhv, axis=-1, keepdims=True)
        du = rs_ref[...] * (dxh - m1 - xhv * m2)
        du_ref[...] = du
        dub_ref[...] = du.astype(bf16)
        dg_ref[...] += jnp.sum(dyv * xhv, axis=0, keepdims=True)
        db_ref[...] += jnp.sum(dyv, axis=0, keepdims=True)

    return pl.pallas_call(
        body, grid=(T // tr,), in_specs=[row, row, col, vec], out_specs=[row, row, vec, vec],
        out_shape=[SDS((T, D), f32), SDS((T, D), bf16), SDS((1, D), f32), SDS((1, D), f32)], name=name,
        compiler_params=_params(("arbitrary",)))(dy, xh, rstd, g.reshape(1, D))


def _rms_fwd(proj, col_off, R, g, name):
    T = proj.shape[0]
    tr = _pick(T, 512)
    cb = col_off // R

    def body(c_ref, g_ref, o_ref):
        c = c_ref[...]
        r = lax.rsqrt(jnp.mean(c * c, axis=-1, keepdims=True) + RMS_EPS)
        o_ref[...] = (c * r * g_ref[...]).astype(bf16)

    return pl.pallas_call(
        body, grid=(T // tr,), in_specs=[pl.BlockSpec((tr, R), lambda i: (i, cb)), pl.BlockSpec((1, R), lambda i: (0, 0))],
        out_specs=pl.BlockSpec((tr, R), lambda i: (i, 0)), out_shape=SDS((T, R), bf16), name=name,
        compiler_params=_params(("parallel",)))(proj, g.reshape(1, R))


def _rms_bwd(dy, proj, col_off, R, g, name):
    T = proj.shape[0]
    tr = _pick(T, 512)
    cb = col_off // R

    def body(dy_ref, c_ref, g_ref, dc_ref, dg_ref):
        @pl.when(pl.program_id(0) == 0)
        def _():
            dg_ref[...] = jnp.zeros_like(dg_ref)

        c, dyv = c_ref[...], dy_ref[...]
        r = lax.rsqrt(jnp.mean(c * c, axis=-1, keepdims=True) + RMS_EPS)
        t = dyv * g_ref[...]
        mt = jnp.mean(t * c, axis=-1, keepdims=True)
        dc_ref[...] = (r * t - c * (r * r * r) * mt).astype(bf16)
        dg_ref[...] += jnp.sum(dyv * c * r, axis=0, keepdims=True)

    return pl.pallas_call(
        body, grid=(T // tr,),
        in_specs=[pl.BlockSpec((tr, R), lambda i: (i, 0)), pl.BlockSpec((tr, R), lambda i: (i, cb)), pl.BlockSpec((1, R), lambda i: (0, 0))],
        out_specs=[pl.BlockSpec((tr, R), lambda i: (i, 0)), pl.BlockSpec((1, R), lambda i: (0, 0))],
        out_shape=[SDS((T, R), bf16), SDS((1, R), f32)], name=name,
        compiler_params=_params(("arbitrary",)))(dy, proj, g.reshape(1, R))


def _tri(n, lower):
    r = lax.broadcasted_iota(jnp.int32, (n, n), 0)
    c = lax.broadcasted_iota(jnp.int32, (n, n), 1)
    return ((r >= c) if lower else (c >= r)).astype(f32)


def _gate_fwd(proj, bpad, B, S, name):
    ch = _pick(S, 256)
    nch = S // ch
    cb = C_F // LANES

    def body(f_ref, b_ref, o_ref):
        tri = _tri(ch, True)
        carry = jnp.zeros((1, LANES), f32)
        for c in range(nch):
            z = f_ref[c * ch:(c + 1) * ch, :] + b_ref[...]
            lf = jnp.minimum(z, 0.0) - jnp.log1p(jnp.exp(-jnp.abs(z)))
            cs = jnp.dot(tri, lf, precision=lax.Precision.HIGHEST, preferred_element_type=f32) + carry
            o_ref[c * ch:(c + 1) * ch, :] = cs
            carry = cs[ch - 1:ch, :]

    return pl.pallas_call(
        body, grid=(B,), in_specs=[pl.BlockSpec((S, LANES), lambda b: (b, cb)), pl.BlockSpec((1, LANES), lambda b: (0, 0))],
        out_specs=pl.BlockSpec((S, LANES), lambda b: (b, 0)), out_shape=SDS((B * S, LANES), f32), name=name,
        compiler_params=_params(("parallel",)))(proj, bpad)


def _gate_bwd(dcum, proj, bpad, B, S, name):
    ch = _pick(S, 256)
    nch = S // ch
    cb = C_F // LANES

    def body(d_ref, f_ref, b_ref, dz_ref, db_ref):
        @pl.when(pl.program_id(0) == 0)
        def _():
            db_ref[...] = jnp.zeros_like(db_ref)

        tri = _tri(ch, False)
        carry = jnp.zeros((1, LANES), f32)
        dbs = jnp.zeros((1, LANES), f32)
        for c in reversed(range(nch)):
            d = d_ref[c * ch:(c + 1) * ch, :]
            dlf = jnp.dot(tri, d, precision=lax.Precision.HIGHEST, preferred_element_type=f32) + carry
            carry = carry + jnp.sum(d, axis=0, keepdims=True)
            z = f_ref[c * ch:(c + 1) * ch, :] + b_ref[...]
            dz = dlf / (1.0 + jnp.exp(z))
            dz_ref[c * ch:(c + 1) * ch, :] = dz.astype(bf16)
            dbs = dbs + jnp.sum(dz, axis=0, keepdims=True)
        db_ref[...] += dbs

    return pl.pallas_call(
        body, grid=(B,),
        in_specs=[pl.BlockSpec((S, LANES), lambda b: (b, 0)), pl.BlockSpec((S, LANES), lambda b: (b, cb)), pl.BlockSpec((1, LANES), lambda b: (0, 0))],
        out_specs=[pl.BlockSpec((S, LANES), lambda b: (b, 0)), pl.BlockSpec((1, LANES), lambda b: (0, 0))],
        out_shape=[SDS((B * S, LANES), bf16), SDS((1, LANES), f32)], name=name,
        compiler_params=_params(("arbitrary",)))(dcum, proj, bpad)


def _rot(v, c, sa, sb):
    return v * c + pltpu.roll(v, LANES - 16, 1) * sa + pltpu.roll(v, 16, 1) * sb


def _rot_t(v, c, sa, sb):
    return v * c + pltpu.roll(v * sa, 16, 1) + pltpu.roll(v * sb, LANES - 16, 1)


def _rope_q(x, tabs, *, bwd, name):
    T = x.shape[0]
    W = MLA_HEADS * LANES
    tr = _pick(T, 512)
    fn = _rot_t if bwd else _rot

    def body(x_ref, c_ref, sa_ref, sb_ref, o_ref):
        c, sa, sb = c_ref[...], sa_ref[...], sb_ref[...]
        for g in range(MLA_HEADS):
            o_ref[:, g * LANES:(g + 1) * LANES] = fn(x_ref[:, g * LANES:(g + 1) * LANES], c, sa, sb).astype(bf16)

    tab = pl.BlockSpec((tr, LANES), lambda i: (i, 0))
    wide = pl.BlockSpec((tr, W), lambda i: (i, 0))
    return pl.pallas_call(body, grid=(T // tr,), in_specs=[wide, tab, tab, tab], out_specs=wide, out_shape=SDS((T, W), bf16),
                          name=name, compiler_params=_params(("parallel",)))(x, *tabs)


def _rope_k(kv, tabs, proj, name):
    T = kv.shape[0]
    W = MLA_HEADS * LANES
    tr = _pick(T, 512)
    kcb = C_KPE // LANES

    def body(x_ref, c_ref, sa_ref, sb_ref, kpe_ref, k_ref, v_ref):
        c, sa, sb, kpe = c_ref[...], sa_ref[...], sb_ref[...], kpe_ref[...]
        low = lax.broadcasted_iota(jnp.int32, (tr, LANES), 1) < NOPE
        halves = []
        for g in range(MLA_HEADS):
            x = x_ref[:, g * LANES:(g + 1) * LANES]
            k_ref[:, g * LANES:(g + 1) * LANES] = _rot(jnp.where(low, x, 0.0) + kpe, c, sa, sb).astype(bf16)
            halves.append(x[:, NOPE:])
            if g % 2 == 1:
                v_ref[:, (g // 2) * LANES:(g // 2 + 1) * LANES] = jnp.concatenate(halves, axis=1).astype(bf16)
                halves = []

    tab = pl.BlockSpec((tr, LANES), lambda i: (i, 0))
    wide = pl.BlockSpec((tr, W), lambda i: (i, 0))
    return pl.pallas_call(
        body, grid=(T // tr,), in_specs=[wide, tab, tab, tab, pl.BlockSpec((tr, LANES), lambda i: (i, kcb))],
        out_specs=[wide, pl.BlockSpec((tr, W // 2), lambda i: (i, 0))], out_shape=[SDS((T, W), bf16), SDS((T, W // 2), bf16)],
        name=name, compiler_params=_params(("parallel",)))(kv, *tabs, proj)


def _rope_k_bwd(dk, dv, tabs, name):
    T = dk.shape[0]
    W = MLA_HEADS * LANES
    tr = _pick(T, 512)

    def body(dk_ref, dv_ref, c_ref, sa_ref, sb_ref, o_ref, s_ref):
        c, sa, sb = c_ref[...], sa_ref[...], sb_ref[...]
        tot = jnp.zeros((tr, LANES), f32)
        for g in range(MLA_HEADS):
            d = _rot_t(dk_ref[:, g * LANES:(g + 1) * LANES], c, sa, sb)
            tot = tot + d
            o_ref[:, g * LANES:(g + 1) * LANES] = jnp.concatenate(
                [d[:, :NOPE].astype(bf16), dv_ref[:, g * MLA_VD:(g + 1) * MLA_VD]], axis=1)
        s_ref[...] = tot.astype(bf16)

    tab = pl.BlockSpec((tr, LANES), lambda i: (i, 0))
    wide = pl.BlockSpec((tr, W), lambda i: (i, 0))
    return pl.pallas_call(
        body, grid=(T // tr,), in_specs=[wide, pl.BlockSpec((tr, W // 2), lambda i: (i, 0)), tab, tab, tab],
        out_specs=[wide, tab], out_shape=[SDS((T, W), bf16), SDS((T, LANES), bf16)], name=name,
        compiler_params=_params(("parallel",)))(dk, dv, *tabs)


def _tile_mask(mode, r0, c0, nr, nc):
    r = r0 + lax.broadcasted_iota(jnp.int32, (nr, nc), 0)
    c = c0 + lax.broadcasted_iota(jnp.int32, (nr, nc), 1)
    if mode == "chunk":
        return (c // CHUNK) <= (r // CHUNK)
    return c <= r


def _diagonal(block, carry, i, tq, tk):
    nfull = (i * tq) // tk
    for d in range(max(1, tq // tk)):
        carry = block((nfull + d) * tk, tk, 0, tq, carry, True)
    return carry


def _attn_tiles(Sq, Sk):
    tk = _pick(Sk, ATT_TK)
    tq = _pick(Sq, ATT_TQ)
    assert (tk % tq == 0 or tq % tk == 0) and tq % CHUNK == 0 and tk % CHUNK == 0
    return tq, tk


def _stage_heads(dst, src_ref, hb, w, mark=None, val=1.0):
    S = src_ref.shape[0]
    ch = _pick(S, 256)
    assert w <= LANES and (mark is None or w <= mark < LANES)

    def step(c, _):
        rows = pl.ds(pl.multiple_of(c * ch, ch), ch)
        for hh in range(hb):
            x = src_ref[rows, hh * w:(hh + 1) * w].astype(bf16)
            if w < LANES:
                pad = jnp.zeros((ch, LANES - w), bf16)
                if mark is not None:
                    lane = lax.broadcasted_iota(jnp.int32, (ch, LANES - w), 1)
                    pad = jnp.where(lane == mark - w, val, 0.0).astype(bf16)
                x = jnp.concatenate([x, pad], axis=1)
            dst[hh, rows, :] = x
        return 0

    lax.fori_loop(0, S // ch, step, 0)


def _carry(rider, refs, n_in, n_out, n_scr, grid):
    if rider is None:
        return refs
    ri, ro = len(rider.inputs), len(rider.out_shapes)
    own = refs[:n_in] + refs[n_in + ri:n_in + ri + n_out] + refs[n_in + ri + n_out + ro:n_in + ri + n_out + ro + n_scr]
    rider.refs = (refs[n_in:n_in + ri], refs[n_in + ri + n_out:n_in + ri + n_out + ro], refs[n_in + ri + n_out + ro + n_scr:])
    first = functools.reduce(jnp.logical_and, [pl.program_id(a) == 0 for a in range(len(grid))])

    @pl.when(first)
    def _():
        rider.start(*rider.refs)

    return own


def _carry_finish(rider, grid):
    if rider is None:
        return
    last = functools.reduce(jnp.logical_and, [pl.program_id(a) == n - 1 for a, n in enumerate(grid)])

    @pl.when(last)
    def _():
        rider.finish(*rider.refs)


def _carrier_call(body, rider, ins, in_specs, out_shape, out_specs, scratch, grid, name):
    if rider is None:
        return pl.pallas_call(body, grid=grid, in_specs=in_specs, out_specs=out_specs, out_shape=out_shape, scratch_shapes=scratch,
                              name=name, compiler_params=_params(("parallel",) * len(grid)))(*ins)
    any_spec = pl.BlockSpec(memory_space=pl.ANY)
    ri, ro = len(rider.inputs), len(rider.out_shapes)
    res = pl.pallas_call(
        body, grid=grid, in_specs=list(in_specs) + [any_spec] * ri, out_specs=list(out_specs) + [any_spec] * ro,
        out_shape=list(out_shape) + rider.out_shapes, scratch_shapes=list(scratch) + rider.scratch,
        input_output_aliases={len(ins) + i: len(out_shape) + o for i, o in rider.aliases.items()}, name=name,
        compiler_params=_params(("arbitrary",) * len(grid)))(*ins, *rider.inputs)
    return res[:len(out_shape)], res[len(out_shape):]


def _attn_fwd(q_arr, q_cb, k_arr, k_cb, v_arr, v_cb, *, B, Sq, Sk, nblk, hb, dqk, dv, mode, scale, cum_r=None, rider=None, name):
    wq, wv = hb * dqk, hb * dv
    tq, tk = _attn_tiles(Sq, Sk)
    fox = mode == "fox"
    n_in = 4 if fox else 3
    ones = dv < LANES

    def body(*refs):
        refs = _carry(rider, refs, n_in, 2, 3, (B, nblk))
        q_ref, k_ref, v_ref = refs[:3]
        cr_ref = refs[3] if fox else None
        o_ref, lse_ref, qh, kh, vh = refs[n_in:]
        _stage_heads(qh, q_ref, hb, dqk)
        _stage_heads(kh, k_ref, hb, dqk)
        _stage_heads(vh, v_ref, hb, dv, dv if ones else None)

        def qstep(i, _):
            rows = pl.ds(pl.multiple_of(i * tq, tq), tq)
            qs = [qh[hh, rows, :] for hh in range(hb)]

            def block(c0, w, r0, nr, carry, diag):
                cols = pl.ds(pl.multiple_of(c0, w), w)
                out = []
                for hh in range(hb):
                    m, l, acc = carry[hh]
                    s = lax.dot_general(qs[hh][r0:r0 + nr], kh[hh, cols, :], _DIMS["nt"], preferred_element_type=f32) * scale
                    if fox:
                        s = s - cr_ref[0, hh:hh + 1, cols]
                    if diag:
                        s = jnp.where(_tile_mask(mode, i * tq + r0, c0, nr, w), s, NEG_INF)
                    m_new = jnp.maximum(m, jnp.max(s, axis=-1, keepdims=True))
                    a = jnp.exp(m - m_new)
                    p = jnp.exp(s - m_new)
                    if not ones:
                        l = a * l + jnp.sum(p, axis=-1, keepdims=True)
                    acc = a * acc + lax.dot_general(p.astype(bf16), vh[hh, cols, :], _DIMS["nn"], preferred_element_type=f32)
                    out.append((m_new, l, acc))
                return tuple(out)

            def tile(j, carry):
                return block(j * tk, tk, 0, tq, carry, False)

            carry = tuple((jnp.full((tq, 1), -jnp.inf, f32), jnp.zeros((tq, 1), f32), jnp.zeros((tq, LANES), f32)) for _ in range(hb))
            if mode == "none":
                carry = lax.fori_loop(0, Sk // tk, tile, carry)
            else:
                nfull = (i * tq) // tk
                carry = lax.fori_loop(0, nfull, tile, carry)
                carry = _diagonal(block, carry, i, tq, tk)
            for hh in range(hb):
                m, l, acc = carry[hh]
                if ones:
                    l = acc[:, dv:dv + 1]
                o_ref[rows, hh * dv:(hh + 1) * dv] = acc[:, :dv] / l
                lse_ref[0, rows, hh:hh + 1] = m + jnp.log(l)
            return 0

        lax.fori_loop(0, Sq // tq, qstep, 0)
        _carry_finish(rider, (B, nblk))

    in_specs = [pl.BlockSpec((Sq, wq), lambda b, p: (b, q_cb + p)), pl.BlockSpec((Sk, wq), lambda b, p: (b, k_cb + p)),
                pl.BlockSpec((Sk, wv), lambda b, p: (b, v_cb + p))]
    ins = [q_arr, k_arr, v_arr]
    if fox:
        in_specs.append(pl.BlockSpec((1, hb, Sk), lambda b, p: (b * nblk + p, 0, 0)))
        ins.append(cum_r)
    return _carrier_call(
        body, rider, ins, in_specs,
        [SDS((B * Sq, nblk * wv), f32), SDS((B * nblk, Sq, hb), f32)],
        [pl.BlockSpec((Sq, wv), lambda b, p: (b, p)), pl.BlockSpec((1, Sq, hb), lambda b, p: (b * nblk + p, 0, 0))],
        [pltpu.VMEM((hb, Sq, LANES), bf16), pltpu.VMEM((hb, Sk, LANES), bf16), pltpu.VMEM((hb, Sk, LANES), bf16)],
        (B, nblk), name)


def _attn_bwd(q_arr, q_cb, k_arr, k_cb, v_arr, v_cb, o, do, lse, *, B, Sq, Sk, nblk, hb, dqk, dv, mode, scale, cum_r=None,
              dq_dtype=bf16, rider=None, name):
    wq, wv = hb * dqk, hb * dv
    tq, tk = _attn_tiles(Sq, Sk)
    fox = mode == "fox"
    n_in = 7 if fox else 6
    n_out = 4 if fox else 3
    if fox:
        assert dqk + 2 <= LANES and np.log2(scale) == np.round(np.log2(scale))

    def body(*refs):
        refs = _carry(rider, refs, n_in, n_out, 7 if fox else 6, (B, nblk))
        q_ref, k_ref, v_ref, o_ref, do_ref, lse_ref = refs[:6]
        cr_ref = refs[6] if fox else None
        dq_ref, dk_ref, dv_ref = refs[n_in:n_in + 3]
        dcum_ref = refs[n_in + 3] if fox else None
        qh, kh, vh, doh, dk_acc, dv_acc = refs[n_in + n_out:n_in + n_out + 6]
        dcq_acc = refs[n_in + n_out + 6] if fox else None
        _stage_heads(qh, q_ref, hb, dqk, dqk + 1 if fox else None, 1.0 / scale)
        _stage_heads(kh, k_ref, hb, dqk, dqk if fox else None, 1.0 / scale)
        _stage_heads(vh, v_ref, hb, dv)
        _stage_heads(doh, do_ref, hb, dv)
        dk_acc[...] = jnp.zeros_like(dk_acc)
        dv_acc[...] = jnp.zeros_like(dv_acc)

        def qstep(i, _):
            rows = pl.ds(pl.multiple_of(i * tq, tq), tq)
            qs = [qh[hh, rows, :] for hh in range(hb)]
            dos = [doh[hh, rows, :] for hh in range(hb)]
            dds = [jnp.sum(do_ref[rows, hh * dv:(hh + 1) * dv] * o_ref[rows, hh * dv:(hh + 1) * dv], axis=-1, keepdims=True)
                   for hh in range(hb)]
            lses = [lse_ref[0, rows, hh:hh + 1] for hh in range(hb)]

            def block(c0, w, r0, nr, carry, diag):
                cols = pl.ds(pl.multiple_of(c0, w), w)
                rs = slice(r0, r0 + nr)
                out = []
                for hh in range(hb):
                    k = kh[hh, cols, :]
                    q, do_ = qs[hh][rs], dos[hh][rs]
                    s = lax.dot_general(q, k, _DIMS["nt"], preferred_element_type=f32) * scale
                    if fox:
                        s = s - cr_ref[0, hh:hh + 1, cols]
                    if diag:
                        s = jnp.where(_tile_mask(mode, i * tq + r0, c0, nr, w), s, NEG_INF)
                    p = jnp.exp(s - lses[hh][rs])
                    dv_acc[hh, cols, :] += lax.dot_general(p.astype(bf16), do_, _DIMS["tn"], preferred_element_type=f32)
                    dp = lax.dot_general(do_, vh[hh, cols, :], _DIMS["nt"], preferred_element_type=f32)
                    ds = p * (dp - dds[hh][rs])
                    dsb = (ds * scale).astype(bf16)
                    out.append(carry[hh] + lax.dot_general(dsb, k, _DIMS["nn"], preferred_element_type=f32))
                    dk_acc[hh, cols, :] += lax.dot_general(dsb, q, _DIMS["tn"], preferred_element_type=f32)
                return tuple(out)

            def tile(j, carry):
                return block(j * tk, tk, 0, tq, carry, False)

            carry = tuple(jnp.zeros((tq, LANES), f32) for _ in range(hb))
            if mode == "none":
                carry = lax.fori_loop(0, Sk // tk, tile, carry)
            else:
                nfull = (i * tq) // tk
                carry = lax.fori_loop(0, nfull, tile, carry)
                carry = _diagonal(block, carry, i, tq, tk)
            for hh in range(hb):
                dq_ref[rows, hh * dqk:(hh + 1) * dqk] = carry[hh][:, :dqk].astype(dq_dtype)
                if fox:
                    dcq_acc[rows, hh:hh + 1] = carry[hh][:, dqk:dqk + 1]
            return 0

        lax.fori_loop(0, Sq // tq, qstep, 0)
        for hh in range(hb):
            dk_ref[:, hh * dqk:(hh + 1) * dqk] = dk_acc[hh, :, :dqk].astype(dq_dtype)
            dv_ref[:, hh * dv:(hh + 1) * dv] = dv_acc[hh, :, :dv].astype(bf16)
            if fox:
                dcum_ref[0, :, hh:hh + 1] = dcq_acc[:, hh:hh + 1] - dk_acc[hh, :, dqk + 1:dqk + 2]
        _carry_finish(rider, (B, nblk))

    qs_ =pl.BlockSpec((Sq, wq), lambda b, p: (b, q_cb + p))
    ks_ = pl.BlockSpec((Sk, wq), lambda b, p: (b, k_cb + p))
    vs_ = pl.BlockSpec((Sk, wv), lambda b, p: (b, v_cb + p))
    os_ = pl.BlockSpec((Sq, wv), lambda b, p: (b, p))
    st = pl.BlockSpec((1, Sq, hb), lambda b, p: (b * nblk + p, 0, 0))
    rw = pl.BlockSpec((1, hb, Sk), lambda b, p: (b * nblk + p, 0, 0))
    in_specs = [qs_, ks_, vs_, os_, os_, st] + ([rw] if fox else [])
    ins = [q_arr, k_arr, v_arr, o, do, lse] + ([cum_r] if fox else [])
    out_specs = [pl.BlockSpec((Sq, wq), lambda b, p: (b, p)), pl.BlockSpec((Sk, wq), lambda b, p: (b, p)),
                 pl.BlockSpec((Sk, wv), lambda b, p: (b, p))] + ([st] if fox else [])
    out_shape = [SDS((B * Sq, nblk * wq), dq_dtype), SDS((B * Sk, nblk * wq), dq_dtype), SDS((B * Sk, nblk * wv), bf16)]
    scratch = [pltpu.VMEM((hb, Sq, LANES), bf16), pltpu.VMEM((hb, Sk, LANES), bf16), pltpu.VMEM((hb, Sk, LANES), bf16),
               pltpu.VMEM((hb, Sq, LANES), bf16), pltpu.VMEM((hb, Sk, LANES), f32), pltpu.VMEM((hb, Sk, LANES), f32)]
    if fox:
        assert Sq == Sk
        out_shape.append(SDS((B * nblk, Sq, hb), f32))
        scratch.append(pltpu.VMEM((Sq, hb), f32))
    return _carrier_call(body, rider, ins, in_specs, out_shape, out_specs, scratch, (B, nblk), name)


def _merge_fwd(oa, ob, oc, w_br, proj, D, name):
    T = oa.shape[0]
    tr = _pick(T, 256)
    g0 = C_GATE // D
    o_spec = pl.BlockSpec((tr, BW), lambda i: (i, 0))
    w_spec = pl.BlockSpec((3, BW, D), lambda i: (0, 0, 0))

    def body(oa_ref, ob_ref, oc_ref, w_ref, ga_ref, gb_ref, gc_ref, m_ref):
        tot = jnp.zeros((tr, D), f32)
        for n, (o_ref, g_ref) in enumerate(((oa_ref, ga_ref), (ob_ref, gb_ref), (oc_ref, gc_ref))):
            bp = lax.dot_general(o_ref[...].astype(bf16), w_ref[n], _DIMS["nn"], preferred_element_type=f32)
            tot = tot + jax.nn.sigmoid(g_ref[...]) * bp
        m_ref[...] = tot.astype(bf16)

    gspecs = [pl.BlockSpec((tr, D), lambda i, n=n: (i, g0 + n)) for n in range(3)]
    return pl.pallas_call(
        body, grid=(T // tr,), in_specs=[o_spec, o_spec, o_spec, w_spec, *gspecs],
        out_specs=pl.BlockSpec((tr, D), lambda i: (i, 0)), out_shape=SDS((T, D), bf16), name=name,
        compiler_params=_params(("parallel",)))(oa, ob, oc, w_br, proj, proj, proj)


def _merge_bwd(dm, oa, ob, oc, w_br, proj, D, name, rider=None):
    T = oa.shape[0]
    tr = _pick(T, 256)
    g0 = C_GATE // D
    o_spec = pl.BlockSpec((tr, BW), lambda i: (i, 0))
    d_spec = pl.BlockSpec((tr, D), lambda i: (i, 0))
    w_spec = pl.BlockSpec((3, BW, D), lambda i: (0, 0, 0))

    def body(*refs):
        refs = _carry(rider, refs, 8, 9, 0, (T // tr,))
        dm_ref, oa_ref, ob_ref, oc_ref, w_ref, ga_ref, gb_ref, gc_ref = refs[:8]
        outs = refs[8:]
        dmv = dm_ref[...]
        for n, (o_ref, g_ref) in enumerate(((oa_ref, ga_ref), (ob_ref, gb_ref), (oc_ref, gc_ref))):
            bp = lax.dot_general(o_ref[...].astype(bf16), w_ref[n], _DIMS["nn"], preferred_element_type=f32)
            g = jax.nn.sigmoid(g_ref[...])
            dbp = (dmv * g).astype(bf16)
            outs[n][...] = (dmv * bp * g * (1.0 - g)).astype(bf16)
            outs[3 + n][...] = lax.dot_general(dbp, w_ref[n], _DIMS["nt"], preferred_element_type=f32)
            outs[6 + n][...] = dbp
        _carry_finish(rider, (T // tr,))

    gspecs = [pl.BlockSpec((tr, D), lambda i, n=n: (i, g0 + n)) for n in range(3)]
    return _carrier_call(
        body, rider, [dm, oa, ob, oc, w_br, proj, proj, proj], [d_spec, o_spec, o_spec, o_spec, w_spec, *gspecs],
        [SDS((T, D), bf16)] * 3 + [SDS((T, BW), f32)] * 3 + [SDS((T, D), bf16)] * 3,
        [d_spec] * 3 + [o_spec] * 3 + [d_spec] * 3, [], (T // tr,), name)


def _loss(h, target, name):
    T, D = h.shape
    tr = _pick(T, 256)
    nt = T // tr
    row = pl.BlockSpec((tr, D), lambda i: (i, 0))

    def body(h_ref, t_ref, l_ref, d_ref, acc):
        i = pl.program_id(0)

        @pl.when(i == 0)
        def _():
            acc[...] = jnp.zeros_like(acc)

        e = h_ref[...] - t_ref[...]
        d_ref[...] = e * (1.0 / D)
        acc[...] += jnp.sum(e * e, axis=0, keepdims=True)

        @pl.when(i == nt - 1)
        def _():
            l_ref[...] = jnp.sum(acc[...], axis=-1, keepdims=True) * (0.5 / D)

    return pl.pallas_call(
        body, grid=(nt,), in_specs=[row, row], out_specs=[pl.BlockSpec((1, 1), lambda i: (0, 0)), row],
        out_shape=[SDS((1, 1), f32), SDS((T, D), f32)], scratch_shapes=[pltpu.VMEM((1, D), f32)], name=name,
        compiler_params=_params(("arbitrary",)))(h, target)


def _adamw(w, m, v, parts, name):
    shape = w.shape
    C = shape[-1]
    R = int(np.prod(shape[:-1])) if len(shape) > 1 else 1
    n_parts, Cp = parts.shape[0], parts.shape[-1]
    tr = _pick(R, 256)
    spec = pl.BlockSpec((tr, C), lambda i: (i, 0))

    def body(w_ref, m_ref, v_ref, p_ref, g_ref, d_ref, mo_ref, vo_ref):
        gv = p_ref[0].astype(f32)
        for q in range(1, n_parts):
            gv = gv + p_ref[q].astype(f32)
        gv = gv[:, :C]
        mn = ADAM_B1 * m_ref[...] + (1.0 - ADAM_B1) * gv
        vn = ADAM_B2 * v_ref[...] + (1.0 - ADAM_B2) * (gv * gv)
        m_hat = mn / (1.0 - ADAM_B1 ** ADAM_STEP)
        v_hat = vn / (1.0 - ADAM_B2 ** ADAM_STEP)
        g_ref[...] = gv
        d_ref[...] = -ADAM_LR * (m_hat / (jnp.sqrt(v_hat) + ADAM_EPS) + ADAM_WD * w_ref[...])
        mo_ref[...] = mn
        vo_ref[...] = vn

    outs = pl.pallas_call(
        body, grid=(R // tr,), in_specs=[spec] * 3 + [pl.BlockSpec((n_parts, tr, Cp), lambda i: (0, i, 0))], out_specs=[spec] * 4,
        out_shape=[SDS((R, C), f32)] * 4, name=name,
        compiler_params=_params(("parallel",)))(*[a.reshape(R, C) for a in (w, m, v)], parts.reshape(n_parts, R, Cp))
    return [o.reshape(shape) for o in outs]


def _coords():
    return lax.axis_index("x"), lax.axis_index("y"), lax.axis_index("c")


def _window(ref, axis, idx, n):
    start = pl.multiple_of(idx * n, n)
    if axis == 1:
        return ref.at[:, pl.ds(start, n), :]
    return ref.at[:, :, pl.ds(start, n)]


class _Rider:
    def __init__(self, inputs, out_shapes, aliases, scratch, start, finish):
        self.inputs, self.out_shapes, self.aliases, self.scratch = list(inputs), list(out_shapes), dict(aliases), list(scratch)
        self.start, self.finish = start, finish


def _join_riders(riders):
    riders = [r for r in riders if r is not None]
    if len(riders) < 2:
        return riders[0] if riders else None
    ni = np.cumsum([0] + [len(r.inputs) for r in riders])
    no = np.cumsum([0] + [len(r.out_shapes) for r in riders])
    ns = np.cumsum([0] + [len(r.scratch) for r in riders])
    aliases = {int(ni[k]) + i: int(no[k]) + o for k, r in enumerate(riders) for i, o in r.aliases.items()}

    def phase(which):
        def run(ins, outs, scr):
            for k, r in enumerate(riders):
                getattr(r, which)(ins[ni[k]:ni[k + 1]], outs[no[k]:no[k + 1]], scr[ns[k]:ns[k + 1]])
        return run

    return _Rider(sum([r.inputs for r in riders], []), sum([r.out_shapes for r in riders], []), aliases,
                  sum([r.scratch for r in riders], []), phase("start"), phase("finish"))


def _run_rider(r, name):
    ni, no = len(r.inputs), len(r.out_shapes)

    def body(*refs):
        ins, outs, scr = refs[:ni], refs[ni:ni + no], refs[ni + no:]
        r.start(ins, outs, scr)
        r.finish(ins, outs, scr)

    any_spec = pl.BlockSpec(memory_space=pl.ANY)
    return pl.pallas_call(body, out_shape=r.out_shapes, in_specs=[any_spec] * ni, out_specs=[any_spec] * no,
                          input_output_aliases=r.aliases, scratch_shapes=r.scratch, name=name)(*r.inputs)


def _gather_rider(shards, axes):
    nt = len(shards)
    sizes = [s.shape[a] for s, a in zip(shards, axes)]
    out_shapes = [SDS(tuple(d * N_DEV if i == a else d for i, d in enumerate(s.shape)), s.dtype) for s, a in zip(shards, axes)]

    def plan(x_refs, out_refs, scr):
        send_sems, recv_sems, local_sems = scr
        x, y, c = _coords()
        me, sibling = (x, y, c), (x, y, 1 - c)
        chips = [(1 - x, y), (x, 1 - y), (1 - x, 1 - y)]

        def win(t, px, py, pc):
            return _window(out_refs[t], axes[t], 4 * px + 2 * py + pc, sizes[t])

        def copy(t, k, block, to, src=None):
            return pltpu.make_async_remote_copy(
                src_ref=win(t, *block) if src is None else src, dst_ref=win(t, *block),
                send_sem=send_sems.at[7 * t + k], recv_sem=recv_sems.at[7 * t + k], device_id=to, device_id_type=MESH)

        mine = [pltpu.make_async_copy(x_refs[t], win(t, *me), local_sems.at[t]) for t in range(nt)]
        first = []
        for t in range(nt):
            first.append(copy(t, 0, me, sibling, src=x_refs[t]))
            first += [copy(t, 1 + j, me, (*chip, c), src=x_refs[t]) for j, chip in enumerate(chips)]
        return c, me, sibling, chips, copy, mine, first

    def start(x_refs, out_refs, scr):
        _, _, _, _, _, mine, first = plan(x_refs, out_refs, scr)
        for cp in mine + first:
            cp.start()

    def finish(x_refs, out_refs, scr):
        c, me, sibling, chips, copy, mine, first = plan(x_refs, out_refs, scr)
        passed = []
        for j, chip in enumerate(chips):
            for t in range(nt):
                copy(t, 1 + j, (*chip, c), me).wait_recv()
                fwd = copy(t, 4 + j, (*chip, c), sibling)
                fwd.start()
                passed.append(fwd)
        for t in range(nt):
            copy(t, 0, sibling, me).wait_recv()
            for j, chip in enumerate(chips):
                copy(t, 4 + j, (*chip, 1 - c), me).wait_recv()
        for cp in first + passed:
            cp.wait_send()
        for cp in mine:
            cp.wait()

    scratch = [pltpu.SemaphoreType.DMA((7 * nt,)), pltpu.SemaphoreType.DMA((7 * nt,)), pltpu.SemaphoreType.DMA((nt,))]
    return _Rider(shards, out_shapes, {}, scratch, start, finish)


N_CHIP = N_DEV // 2


def _pair_exchange_rider(grads, axes, shards):
    nt = len(grads)
    sizes = [s[a] for s, a in zip(shards, axes)]

    def plan(ins, outs, scr):
        send_sems, recv_sems = scr
        x, y, c = _coords()
        return [pltpu.make_async_remote_copy(
            src_ref=_window(ins[t], axes[t], 2 * q + (1 - c), sizes[t]), dst_ref=outs[t].at[q],
            send_sem=send_sems.at[N_CHIP * t + q], recv_sem=recv_sems.at[N_CHIP * t + q],
            device_id=(x, y, 1 - c), device_id_type=MESH) for t in range(nt) for q in range(N_CHIP)]

    def start(ins, outs, scr):
        for cp in plan(ins, outs, scr):
            cp.start()

    def finish(ins, outs, scr):
        copies = plan(ins, outs, scr)
        for cp in copies:
            cp.wait_recv()
        for cp in copies:
            cp.wait_send()

    scratch = [pltpu.SemaphoreType.DMA((N_CHIP * nt,)), pltpu.SemaphoreType.DMA((N_CHIP * nt,))]
    return _Rider(grads, [SDS((N_CHIP,) + tuple(s), g.dtype) for s, g in zip(shards, grads)], {}, scratch, start, finish)


def _pair_sum(grad, buf, axis, core, name):
    _, G, r, c = buf.shape
    tr = _pick(r, 512)
    nr = r // tr

    def body(core_ref, g_ref, b_ref, o_ref):
        o_ref[0, 0] = (g_ref[0].astype(f32) + b_ref[0, 0].astype(f32)).astype(bf16)

    if axis == 2:
        g_spec = pl.BlockSpec((1, tr, c), lambda q, g, i, core_ref: (g, i, 2 * q + core_ref[0]))
    else:
        g_spec = pl.BlockSpec((1, tr, c), lambda q, g, i, core_ref: (g, (2 * q + core_ref[0]) * nr + i, 0))
    blk = pl.BlockSpec((1, 1, tr, c), lambda q, g, i, core_ref: (q, g, i, 0))
    grid_spec = pltpu.PrefetchScalarGridSpec(num_scalar_prefetch=1, grid=(N_CHIP, G, nr), in_specs=[g_spec, blk], out_specs=blk)
    return pl.pallas_call(body, grid_spec=grid_spec, out_shape=SDS(buf.shape, bf16), name=name,
                          compiler_params=_params(("parallel", "parallel", "parallel")))(core, grad, buf)


def _chip_exchange_rider(chips, slots, layer):
    nt = len(chips)

    def plan(ins, outs, scr):
        send_sems, recv_sems, local_sems = scr
        x, y, c = _coords()
        me = 2 * x + y
        mine = [pltpu.make_async_copy(ins[t].at[me], outs[t].at[me, layer], local_sems.at[t]) for t in range(nt)]
        copies = []
        for k in range(1, N_CHIP):
            px, py = lax.rem(x + (k >> 1), 2), lax.rem(y + (k & 1), 2)
            copies += [pltpu.make_async_remote_copy(
                src_ref=ins[t].at[2 * px + py], dst_ref=outs[t].at[me, layer],
                send_sem=send_sems.at[3 * t + k - 1], recv_sem=recv_sems.at[3 * t + k - 1],
                device_id=(px, py, c), device_id_type=MESH) for t in range(nt)]
        return mine, copies

    def start(ins, outs, scr):
        mine, copies = plan(ins, outs, scr)
        for cp in mine + copies:
            cp.start()

    def finish(ins, outs, scr):
        mine, copies = plan(ins, outs, scr)
        for cp in copies:
            cp.wait_recv()
        for cp in copies:
            cp.wait_send()
        for cp in mine:
            cp.wait()

    scratch = [pltpu.SemaphoreType.DMA((3 * nt,)), pltpu.SemaphoreType.DMA((3 * nt,)), pltpu.SemaphoreType.DMA((nt,))]
    return _Rider(list(chips) + list(slots), [SDS(s.shape, s.dtype) for s in slots], {nt + t: t for t in range(nt)}, scratch,
                  start, finish)


_BIG = (("w_in", 2), ("w_uq", 2), ("w_ukv", 2), ("w_mem_kv", 1), ("w_br", 2), ("w_out", 1), ("w_ff1", 2), ("w_ff2", 1))


def _in_col_maps(D):
    d_in = O_GATE + 3 * D
    sh = d_in // N_DEV
    shp = _round_up(sh, LANES)
    perm_of_orig = np.empty(d_in, np.int64)
    for a, b, dst in ((0, O_F, C_Q), (O_F, O_CQ, C_F), (O_CQ, O_CKV, C_CQ), (O_CKV, O_KR, C_CKV), (O_KR, O_QM, C_KPE + NOPE),
                      (O_QM, O_GATE, C_QM), (O_GATE, d_in, C_GATE)):
        perm_of_orig[a:b] = dst + np.arange(b - a)
    orig = np.arange(d_in)
    gath_of_orig = (orig // sh) * shp + orig % sh
    fwd = -np.ones(C_GATE + 3 * D, np.int64)
    fwd[perm_of_orig] = gath_of_orig
    bwd = -np.ones(N_DEV * shp, np.int64)
    bwd[gath_of_orig] = perm_of_orig
    return fwd, bwd


def _lane_pad(a):
    c = a.shape[-1]
    return jnp.pad(a, [(0, 0)] * (a.ndim - 1) + [(0, _round_up(c, LANES) - c)])


def _as3(a):
    return a.reshape((-1,) + a.shape[-2:]) if a.ndim != 3 else a


def _rope_tables(positions):
    inv_freq = ROPE_BASE ** (-jnp.arange(0, ROPE, 2, dtype=f32) / ROPE)
    ang = positions.astype(f32).reshape(-1, 1) * inv_freq
    cos, sin = jnp.cos(ang), jnp.sin(ang)
    T = ang.shape[0]
    one, zero = jnp.ones((T, NOPE), f32), jnp.zeros((T, 16), f32)
    c = jnp.concatenate([one, cos, cos, jnp.ones((T, 32), f32)], axis=1)
    sa = jnp.concatenate([jnp.zeros((T, NOPE), f32), -sin, zero, jnp.zeros((T, 32), f32)], axis=1)
    sb = jnp.concatenate([jnp.zeros((T, NOPE), f32), zero, sin, jnp.zeros((T, 32), f32)], axis=1)
    return c, sa, sb


def kernel(x, mem, positions, ln_in_g, ln_in_b, w_in, b_forget, w_uq, g_cq, w_ukv, g_ckv, w_mem_kv, w_br, w_out, ln1_g, ln1_b, w_ff1, w_ff2, ln2_g, ln2_b, loss_target, m_ln_in_g, m_ln_in_b, m_w_in, m_b_forget, m_w_uq, m_g_cq, m_w_ukv, m_g_ckv, m_w_mem_kv, m_w_br, m_w_out, m_ln1_g, m_ln1_b, m_w_ff1, m_w_ff2, m_ln2_g, m_ln2_b, v_ln_in_g, v_ln_in_b, v_w_in, v_b_forget, v_w_uq, v_g_cq, v_w_ukv, v_g_ckv, v_w_mem_kv, v_w_br, v_w_out, v_ln1_g, v_ln1_b, v_w_ff1, v_w_ff2, v_ln2_g, v_ln2_b):
    B, S, D = x.shape
    T = B * S
    L = w_in.shape[0]
    NM = mem.shape[1]
    alpha = (2 * L) ** 0.25
    weights = dict(ln_in_g=ln_in_g, ln_in_b=ln_in_b, w_in=w_in, b_forget=b_forget, w_uq=w_uq, g_cq=g_cq, w_ukv=w_ukv, g_ckv=g_ckv,
                   w_mem_kv=w_mem_kv, w_br=w_br, w_out=w_out, ln1_g=ln1_g, ln1_b=ln1_b, w_ff1=w_ff1, w_ff2=w_ff2, ln2_g=ln2_g, ln2_b=ln2_b)
    mom_m = dict(ln_in_g=m_ln_in_g, ln_in_b=m_ln_in_b, w_in=m_w_in, b_forget=m_b_forget, w_uq=m_w_uq, g_cq=m_g_cq, w_ukv=m_w_ukv,
                 g_ckv=m_g_ckv, w_mem_kv=m_w_mem_kv, w_br=m_w_br, w_out=m_w_out, ln1_g=m_ln1_g, ln1_b=m_ln1_b, w_ff1=m_w_ff1,
                 w_ff2=m_w_ff2, ln2_g=m_ln2_g, ln2_b=m_ln2_b)
    mom_v = dict(ln_in_g=v_ln_in_g, ln_in_b=v_ln_in_b, w_in=v_w_in, b_forget=v_b_forget, w_uq=v_w_uq, g_cq=v_g_cq, w_ukv=v_w_ukv,
                 g_ckv=v_g_ckv, w_mem_kv=v_w_mem_kv, w_br=v_w_br, w_out=v_w_out, ln1_g=v_ln1_g, ln1_b=v_ln1_b, w_ff1=v_w_ff1,
                 w_ff2=v_w_ff2, ln2_g=v_ln2_g, ln2_b=v_ln2_b)
    order = list(weights)
    big_axes = [ax for _, ax in _BIG]
    fwd_map, bwd_map = _in_col_maps(D)
    shard3 = {n: _lane_pad(weights[n]) for n, _ in _BIG}

    half = len(_BIG) // 2

    def gather_rider(l, lo=0, hi=len(_BIG)):
        return _gather_rider([_as3(shard3[n][l]).astype(bf16) for n, _ in _BIG[lo:hi]], big_axes[lo:hi])

    def named(full, lo, hi):
        return {n: (f if n == "w_br" else f[0]) for (n, _), f in zip(_BIG[lo:hi], full)}

    tabs = _rope_tables(positions)
    x2d, mem2d, tgt2d = x.reshape(T, D), mem.reshape(B * NM, D), loss_target.reshape(T, D)
    fox = dict(B=B, Sq=S, Sk=S, nblk=FOX_HEADS // 2, hb=2, dqk=FOX_HD, dv=FOX_HD, mode="fox", scale=FOX_HD ** -0.5)
    mla = dict(B=B, Sq=S, Sk=S, nblk=MLA_HEADS // 2, hb=2, dqk=LANES, dv=MLA_VD, mode="chunk", scale=(NOPE + ROPE) ** -0.5)
    memat = dict(B=B, Sq=S, Sk=NM, nblk=MEM_HEADS, hb=1, dqk=MEM_HD, dv=MEM_HD, mode="none", scale=MEM_HD ** -0.5)
    nb = FOX_HEADS // 2

    h, hb, xh0, rstd0 = _ln_fwd(None, x2d, ln_in_g, ln_in_b, alpha, "ln_in")
    saved = []
    first = _run_rider(gather_rider(0, 0, half), "weights_all_gather")
    for l in range(L):
        W = named(first, 0, half)
        W["w_in"] = _lane_permute(W["w_in"], fwd_map, bf16, "w_in_reorder")
        bpad = jnp.pad(b_forget[l].reshape(1, FOX_HEADS), ((0, 0), (0, LANES - FOX_HEADS)))
        proj = _mm(hb, W["w_in"], name="proj")
        cum = _gate_fwd(proj, bpad, B, S, "gate_fwd")
        cum_r = cum[:, :FOX_HEADS].reshape(B, S, nb, 2).transpose(0, 2, 3, 1).reshape(B * nb, 2, S)
        more = l + 1 < L
        (o_a, lse_a), second = _attn_fwd(proj, C_Q // LANES, proj, C_K // LANES, proj, C_V // LANES, cum_r=cum_r, name="fox_fwd",
                                         rider=gather_rider(l, half), **fox)
        W.update(named(second, half, len(_BIG)))
        cqn = _rms_fwd(proj, C_CQ, Q_RANK, g_cq[l], "rms_q")
        ckvn = _rms_fwd(proj, C_CKV, KV_RANK, g_ckv[l], "rms_kv")
        qraw = _mm(cqn, W["w_uq"], name="q_up")
        kv = _mm(ckvn, W["w_ukv"], name="kv_up")
        qf = _rope_q(qraw, tabs, bwd=False, name="rope_q")
        kf, v_b = _rope_k(kv, tabs, proj, "rope_k")
        res = _attn_fwd(qf, 0, kf, 0, v_b, 0, name="mla_fwd", rider=gather_rider(l + 1, 0, half) if more else None, **mla)
        (o_b, lse_b), first = res if more else (res, None)
        mkv = _mm(mem2d, W["w_mem_kv"], name="mem_kv")
        o_c, lse_c = _attn_fwd(proj, C_QM // MEM_HD, mkv, 0, mkv, MEM_HEADS, name="mem_fwd", **memat)
        merged = _merge_fwd(o_a, o_b, o_c, W["w_br"], proj, D, "merge_fwd")
        y = _mm(merged, W["w_out"], name="out_proj")
        h1, h1b, xh1, rstd1 = _ln_fwd(h, y, ln1_g[l], ln1_b[l], alpha, "ln1")
        a, r = _mm(h1b, W["w_ff1"], epi="relu2", name="ff1")
        ff = _mm(a, W["w_ff2"], name="ff2")
        h2, h2b, xh2, rstd2 = _ln_fwd(h1, ff, ln2_g[l], ln2_b[l], alpha, "ln2")
        saved.append(dict(W=W, bpad=bpad, h=hb, proj=proj, cum_r=cum_r, o_a=o_a, lse_a=lse_a, cqn=cqn, ckvn=ckvn, qf=qf, kf=kf,
                          v_b=v_b, o_b=o_b, lse_b=lse_b, mkv=mkv, o_c=o_c, lse_c=lse_c, merged=merged, h1=h1b, xh1=xh1,
                          rstd1=rstd1, a=a, r=r, xh2=xh2, rstd2=rstd2))
        h, hb = h2, h2b

    loss_local, dh = _loss(h, tgt2d, "loss")
    loss = lax.psum(loss_local[0, 0], ("x", "y", "c"))

    shard_shapes = [_as3(shard3[n][0]).shape for n, _ in _BIG]
    slots = [lax.empty((N_CHIP, L) + s, bf16) for s in shard_shapes]
    core = lax.axis_index("c").astype(jnp.int32).reshape(1)
    small = {n: [None] * L for n in ("b_forget", "g_cq", "g_ckv", "ln1_g", "ln1_b", "ln2_g", "ln2_b")}
    ax_a, ax_c = big_axes[:half], big_axes[half:]
    sh_a, sh_c = shard_shapes[:half], shard_shapes[half:]

    def pair_sums(gs, bufs, axes):
        return [_pair_sum(g, b, ax, core, "grads_pair_sum") for g, b, ax in zip(gs, bufs, axes)]

    pending = unpaired = None
    for l in reversed(range(L)):
        sv = saved[l]
        W = sv["W"]
        du2, du2b, dg2, db2 = _ln_bwd(dh, sv["xh2"], sv["rstd2"], ln2_g[l], "ln2_bwd")
        dz = _mm(du2b, W["w_ff2"], mode="nt", epi="mul", extra=sv["r"], out_dtype=bf16, name="ff2_dx")
        dw_ff2 = _mm(sv["a"], du2b, mode="tn", out_dtype=bf16, name="ff2_dw")
        dh1 = _mm(dz, W["w_ff1"], mode="nt", epi="add", extra=du2, alpha=alpha, name="ff1_dx")
        dw_ff1 = _mm(sv["h1"], dz, mode="tn", out_dtype=bf16, name="ff1_dw")
        du1, du1b, dg1, db1 = _ln_bwd(dh1, sv["xh1"], sv["rstd1"], ln1_g[l], "ln1_bwd")
        dmerged = _mm(du1b, W["w_out"], mode="nt", name="out_dx")
        dw_out = _mm(sv["merged"], du1b, mode="tn", out_dtype=bf16, name="out_dw")
        mb = _merge_bwd(dmerged, sv["o_a"], sv["o_b"], sv["o_c"], W["w_br"], sv["proj"], D, "merge_bwd",
                        rider=None if unpaired is None else _pair_exchange_rider(unpaired[0], ax_a, sh_a))
        if unpaired is not None:
            mb, bufs = mb
            pending = (pair_sums(unpaired[0], bufs, ax_a), unpaired[1])
        dgl, do, dbp = mb[0:3], mb[3:6], mb[6:9]
        dw_br = jnp.stack([_mm(o_n, dbp_n, mode="tn", out_dtype=bf16, name="br_dw") for o_n, dbp_n in zip((sv["o_a"], sv["o_b"], sv["o_c"]), dbp)])
        late = [_as3(g) for g in (dw_br, dw_out, dw_ff1, dw_ff2)]
        proj = sv["proj"]
        chip_rider = None if pending is None else _chip_exchange_rider(pending[0], slots[:half], pending[1])
        own, got = _attn_bwd(proj, C_Q // LANES, proj, C_K // LANES, proj, C_V // LANES, sv["o_a"], do[0], sv["lse_a"],
                             cum_r=sv["cum_r"], name="fox_bwd",
                             rider=_join_riders([chip_rider, _pair_exchange_rider(late, ax_c, sh_c)]), **fox)
        dq_a, dk_a, dv_a, dcum = own
        if chip_rider is not None:
            slots[:half], got = got[:half], got[half:]
        dcum = dcum.reshape(B, nb, S, 2).transpose(0, 2, 1, 3).reshape(T, FOX_HEADS)
        dcum = jnp.pad(dcum, ((0, 0), (0, LANES - FOX_HEADS)))
        dzf, dbf = _gate_bwd(dcum, proj, sv["bpad"], B, S, "gate_bwd")
        (dqf, dkf, dv_b), slots[half:] = _attn_bwd(
            sv["qf"], 0, sv["kf"], 0, sv["v_b"], 0, sv["o_b"], do[1], sv["lse_b"], dq_dtype=f32, name="mla_bwd",
            rider=_chip_exchange_rider(pair_sums(late, got, ax_c), slots[half:], l), **mla)
        dqraw = _rope_q(dqf, tabs, bwd=True, name="rope_q_bwd")
        dkv, dkpe = _rope_k_bwd(dkf, dv_b, tabs, "rope_k_bwd")
        dw_uq = _mm(sv["cqn"], dqraw, mode="tn", out_dtype=bf16, name="q_up_dw")
        dcqn = _mm(dqraw, W["w_uq"], mode="nt", name="q_up_dx")
        dcq_, dg_cq = _rms_bwd(dcqn, proj, C_CQ, Q_RANK, g_cq[l], "rms_q_bwd")
        dw_ukv = _mm(sv["ckvn"], dkv, mode="tn", out_dtype=bf16, name="kv_up_dw")
        dckvn = _mm(dkv, W["w_ukv"], mode="nt", name="kv_up_dx")
        dckv, dg_ckv = _rms_bwd(dckvn, proj, C_CKV, KV_RANK, g_ckv[l], "rms_kv_bwd")
        dqm, dmk, dmv = _attn_bwd(proj, C_QM // MEM_HD, sv["mkv"], 0, sv["mkv"], MEM_HEADS, sv["o_c"], do[2], sv["lse_c"],
                                  name="mem_bwd", **memat)
        dw_mem = _mm(mem2d, jnp.concatenate([dmk, dmv], axis=1), mode="tn", out_dtype=bf16, name="mem_kv_dw")
        dproj = jnp.concatenate([dq_a, dk_a, dv_a, dzf, dkpe, dckv, dqm, jnp.zeros((T, C_CQ - C_PAD), bf16), dcq_, *dgl], axis=1)
        dw_in = _lane_permute(_mm(sv["h"], dproj, mode="tn", out_dtype=bf16, name="proj_dw"), bwd_map, bf16, "w_in_grad_reorder")
        dh = _mm(dproj, W["w_in"], mode="nt", epi="add", extra=du1, alpha=alpha, name="proj_dx")
        unpaired = ([_as3(g) for g in (dw_in, dw_uq, dw_ukv, dw_mem)], l)
        for n, val in (("b_forget", dbf[:, :FOX_HEADS]), ("g_cq", dg_cq), ("g_ckv", dg_ckv), ("ln1_g", dg1), ("ln1_b", db1),
                       ("ln2_g", dg2), ("ln2_b", db2)):
            small[n][l] = val.reshape(-1)
    bufs = _run_rider(_pair_exchange_rider(unpaired[0], ax_a, sh_a), "grads_pair_exchange")
    slots[:half] = _run_rider(_chip_exchange_rider(pair_sums(unpaired[0], bufs, ax_a), slots[:half], unpaired[1]),
                              "grads_chip_exchange")
    grad_x, _, dg_in, db_in = _ln_bwd(dh, xh0, rstd0, ln_in_g, "ln_in_bwd")

    small_list = [("ln_in_g", dg_in.reshape(-1)), ("ln_in_b", db_in.reshape(-1))] + [(n, jnp.stack(v).reshape(-1)) for n, v in small.items()]
    sflat = jnp.concatenate([v for _, v in small_list])
    n_small = sflat.shape[0]
    rs = _round_up(-(-n_small // LANES), 8)
    spacked = jnp.pad(sflat, (0, rs * LANES - n_small)).reshape(1, rs, LANES)
    sparts = _run_rider(_gather_rider([spacked], [1]), "small_all_gather")[0].reshape(N_DEV, rs * LANES)

    grads, deltas, new_m, new_v = {}, {}, {}, {}
    off = 0
    for n, vv in small_list:
        parts = sparts[:, off:off + vv.shape[0]].reshape((N_DEV,) + weights[n].shape)
        off += vv.shape[0]
        grads[n], deltas[n], new_m[n], new_v[n] = _adamw(weights[n], mom_m[n], mom_v[n], parts, "adamw_" + n)
    for (n, _), sl in zip(_BIG, slots):
        parts = sl.reshape((N_CHIP,) + weights[n].shape[:-1] + (sl.shape[-1],))
        grads[n], deltas[n], new_m[n], new_v[n] = _adamw(weights[n], mom_m[n], mom_v[n], parts, "adamw_" + n)
    return (loss, grad_x.reshape(B, S, D), *[grads[n] for n in order], *[deltas[n] for n in order], *[new_m[n] for n in order],
            *[new_v[n] for n in order])
```

```python
import functools

import numpy as np
import jax
import jax.numpy as jnp
from jax import lax
from jax.experimental import pallas as pl
from jax.experimental.pallas import tpu as pltpu

f32, bf16 = jnp.float32, jnp.bfloat16
SDS = jax.ShapeDtypeStruct

N_DEV = 8
MESH = pl.DeviceIdType.MESH
LANES = 128
VMEM_LIMIT = 56 * 1024 * 1024

FOX_HEADS, FOX_HD = 8, 64
MLA_HEADS, NOPE, ROPE, MLA_VD = 8, 64, 32, 64
Q_RANK, KV_RANK = 384, 256
ROPE_BASE = 10000.0
MEM_HEADS, MEM_HD = 4, 128
BW = 512
CHUNK = 64
LN_EPS, RMS_EPS, NEG_INF = 1e-5, 1e-6, -1e30
ADAM_LR, ADAM_B1, ADAM_B2, ADAM_EPS, ADAM_WD, ADAM_STEP = 0.001, 0.9, 0.999, 1e-08, 0.01, 10

ATT_TQ, ATT_TK = 512, 512

C_Q, C_K, C_V, C_F, C_KPE, C_CKV, C_QM, C_PAD, C_CQ, C_GATE = 0, 512, 1024, 1536, 1664, 1792, 2048, 2560, 2688, 3072
O_F, O_CQ, O_CKV, O_KR, O_QM, O_GATE = 1536, 1544, 1928, 2184, 2216, 2728


def _pick(dim, pref):
    if dim <= pref:
        return dim
    for c in (2048, 1024, 768, 512, 384, 256, 128, 64, 32, 16, 8):
        if c <= pref and dim % c == 0:
            return c
    return dim


def _round_up(n, m):
    return -(-n // m) * m


def _params(sem=None):
    return pltpu.CompilerParams(dimension_semantics=sem, vmem_limit_bytes=VMEM_LIMIT)


_DIMS = {"nn": (((1,), (0,)), ((), ())), "nt": (((1,), (1,)), ((), ())), "tn": (((0,), (0,)), ((), ()))}


def _mm(a, b, *, mode="nn", out_dtype=f32, epi="none", extra=None, alpha=1.0, tm=1024, tn=1024, tk=None, name):
    if mode == "nn":
        (M, K), (_, N) = a.shape, b.shape
    elif mode == "nt":
        (M, K), (N, _) = a.shape, b.shape
    else:
        (K, M), (_, N) = a.shape, b.shape
    tm, tn, tk = _pick(M, tm), _pick(N, tn), _pick(K, tk or 2048)
    nk = K // tk
    a_spec = pl.BlockSpec((tk, tm), lambda i, j, k: (k, i)) if mode == "tn" else pl.BlockSpec((tm, tk), lambda i, j, k: (i, k))
    b_spec = pl.BlockSpec((tn, tk), lambda i, j, k: (j, k)) if mode == "nt" else pl.BlockSpec((tk, tn), lambda i, j, k: (k, j))
    o_spec = pl.BlockSpec((tm, tn), lambda i, j, k: (i, j))
    n_out = 2 if epi == "relu2" else 1
    dims = _DIMS[mode]

    def body(*refs):
        a_ref, b_ref = refs[0], refs[1]
        e_ref = refs[2] if extra is not None else None
        outs = refs[2 + (extra is not None):2 + (extra is not None) + n_out]

        def finish(r):
            if epi == "none":
                outs[0][...] = r.astype(out_dtype)
            elif epi == "relu2":
                p = jnp.maximum(r, 0.0)
                outs[0][...] = (p * p).astype(bf16)
                outs[1][...] = (2.0 * p).astype(bf16)
            elif epi == "mul":
                outs[0][...] = (r * e_ref[...].astype(f32)).astype(out_dtype)
            else:
                outs[0][...] = (r + alpha * e_ref[...].astype(f32)).astype(out_dtype)

        part = lax.dot_general(a_ref[...].astype(bf16), b_ref[...].astype(bf16), dims, preferred_element_type=f32)
        if nk == 1:
            finish(part)
            return
        acc = refs[-1]
        k = pl.program_id(2)

        @pl.when(k == 0)
        def _():
            acc[...] = part

        @pl.when(k > 0)
        def _():
            acc[...] += part

        @pl.when(k == nk - 1)
        def _():
            finish(acc[...])

    ins, in_specs = [a, b], [a_spec, b_spec]
    if extra is not None:
        ins.append(extra)
        in_specs.append(o_spec)
    if epi == "relu2":
        out_shape, out_specs = [SDS((M, N), bf16), SDS((M, N), bf16)], [o_spec, o_spec]
    else:
        out_shape, out_specs = SDS((M, N), out_dtype), o_spec
    return pl.pallas_call(
        body, grid=(M // tm, N // tn, nk), in_specs=in_specs, out_specs=out_specs, out_shape=out_shape,
        scratch_shapes=[pltpu.VMEM((tm, tn), f32)] if nk > 1 else [], name=name,
        compiler_params=_params(("parallel", "parallel", "arbitrary")))(*ins)


def _lane_permute(src, cmap, out_dtype, name):
    R, Cs = src.shape
    cmap = np.asarray(cmap, np.int64)
    nb = cmap.shape[0] // LANES
    srcs = [sorted({int(c) // LANES for c in cmap[b * LANES:(b + 1) * LANES] if c >= 0}) for b in range(nb)]
    exact = src.dtype == f32
    tr = _pick(R, 256 if exact else 512)

    def body(src_ref, cm_ref, o_ref):
        rows = lax.broadcasted_iota(jnp.int32, (LANES, LANES), 0)
        for b in range(nb):
            tbl = cm_ref[:, b * LANES:(b + 1) * LANES]
            acc = jnp.zeros((tr, LANES), f32)
            for sb in srcs[b]:
                sel = (rows + sb * LANES) == tbl
                blk = src_ref[:, sb * LANES:(sb + 1) * LANES]
                if exact:
                    acc = acc + jnp.dot(blk, sel.astype(f32), precision=lax.Precision.HIGHEST, preferred_element_type=f32)
                else:
                    acc = acc + jnp.dot(blk, sel.astype(blk.dtype), preferred_element_type=f32)
            o_ref[:, b * LANES:(b + 1) * LANES] = acc.astype(out_dtype)

    return pl.pallas_call(
        body, grid=(R // tr,), in_specs=[pl.BlockSpec((tr, Cs), lambda i: (i, 0)), pl.BlockSpec((1, nb * LANES), lambda i: (0, 0))],
        out_specs=pl.BlockSpec((tr, nb * LANES), lambda i: (i, 0)), out_shape=SDS((R, nb * LANES), out_dtype), name=name,
        compiler_params=_params(("parallel",)))(src, jnp.asarray(cmap.astype(np.int32)).reshape(1, -1))


def _ln_fwd(h, y, g, b, alpha, name):
    T, D = y.shape
    tr = _pick(T, 256)
    row = pl.BlockSpec((tr, D), lambda i: (i, 0))
    vec = pl.BlockSpec((1, D), lambda i: (0, 0))
    col = pl.BlockSpec((tr, 1), lambda i: (i, 0))

    def body(*refs):
        if h is None:
            y_ref, g_ref, b_ref, o_ref, ob_ref, xh_ref, rs_ref = refs
            u = y_ref[...]
        else:
            h_ref, y_ref, g_ref, b_ref, o_ref, ob_ref, xh_ref, rs_ref = refs
            u = alpha * h_ref[...] + y_ref[...]
        mu = jnp.mean(u, axis=-1, keepdims=True)
        d = u - mu
        var = jnp.mean(d * d, axis=-1, keepdims=True)
        rstd = lax.rsqrt(var + LN_EPS)
        xh = d * rstd
        xh_ref[...] = xh
        rs_ref[...] = rstd
        o = xh * g_ref[...] + b_ref[...]
        o_ref[...] = o
        ob_ref[...] = o.astype(bf16)

    ins = ([] if h is None else [h]) + [y, g.reshape(1, D), b.reshape(1, D)]
    specs = ([] if h is None else [row]) + [row, vec, vec]
    return pl.pallas_call(
        body, grid=(T // tr,), in_specs=specs, out_specs=[row, row, row, col],
        out_shape=[SDS((T, D), f32), SDS((T, D), bf16), SDS((T, D), f32), SDS((T, 1), f32)], name=name,
        compiler_params=_params(("parallel",)))(*ins)


def _ln_bwd(dy, xh, rstd, g, name):
    T, D = dy.shape
    tr = _pick(T, 256)
    row = pl.BlockSpec((tr, D), lambda i: (i, 0))
    vec = pl.BlockSpec((1, D), lambda i: (0, 0))
    col = pl.BlockSpec((tr, 1), lambda i: (i, 0))

    def body(dy_ref, xh_ref, rs_ref, g_ref, du_ref, dub_ref, dg_ref, db_ref):
        @pl.when(pl.program_id(0) == 0)
        def _():
            dg_ref[...] = jnp.zeros_like(dg_ref)
            db_ref[...] = jnp.zeros_like(db_ref)

        dyv, xhv = dy_ref[...], xh_ref[...]
        dxh = dyv * g_ref[...]
        m1 = jnp.mean(dxh, axis=-1, keepdims=True)
        m2 = jnp.mean(dxh * xhv, axis=-1, keepdims=True)
        du = rs_ref[...] * (dxh - m1 - xhv * m2)
        du_ref[...] = du
        dub_ref[...] = du.astype(bf16)
        dg_ref[...] += jnp.sum(dyv * xhv, axis=0, keepdims=True)
        db_ref[...] += jnp.sum(dyv, axis=0, keepdims=True)

    return pl.pallas_call(
        body, grid=(T // tr,), in_specs=[row, row, col, vec], out_specs=[row, row, vec, vec],
        out_shape=[SDS((T, D), f32), SDS((T, D), bf16), SDS((1, D), f32), SDS((1, D), f32)], name=name,
        compiler_params=_params(("arbitrary",)))(dy, xh, rstd, g.reshape(1, D))


def _rms_fwd(proj, col_off, R, g, name):
    T = proj.shape[0]
    tr = _pick(T, 512)
    cb = col_off // R

    def body(c_ref, g_ref, o_ref):
        c = c_ref[...]
        r = lax.rsqrt(jnp.mean(c * c, axis=-1, keepdims=True) + RMS_EPS)
        o_ref[...] = (c * r * g_ref[...]).astype(bf16)

    return pl.pallas_call(
        body, grid=(T // tr,), in_specs=[pl.BlockSpec((tr, R), lambda i: (i, cb)), pl.BlockSpec((1, R), lambda i: (0, 0))],
        out_specs=pl.BlockSpec((tr, R), lambda i: (i, 0)), out_shape=SDS((T, R), bf16), name=name,
        compiler_params=_params(("parallel",)))(proj, g.reshape(1, R))


def _rms_bwd(dy, proj, col_off, R, g, name):
    T = proj.shape[0]
    tr = _pick(T, 512)
    cb = col_off // R

    def body(dy_ref, c_ref, g_ref, dc_ref, dg_ref):
        @pl.when(pl.program_id(0) == 0)
        def _():
            dg_ref[...] = jnp.zeros_like(dg_ref)

        c, dyv = c_ref[...], dy_ref[...]
        r = lax.rsqrt(jnp.mean(c * c, axis=-1, keepdims=True) + RMS_EPS)
        t = dyv * g_ref[...]
        mt = jnp.mean(t * c, axis=-1, keepdims=True)
        dc_ref[...] = (r * t - c * (r * r * r) * mt).astype(bf16)
        dg_ref[...] += jnp.sum(dyv * c * r, axis=0, keepdims=True)

    return pl.pallas_call(
        body, grid=(T // tr,),
        in_specs=[pl.BlockSpec((tr, R), lambda i: (i, 0)), pl.BlockSpec((tr, R), lambda i: (i, cb)), pl.BlockSpec((1, R), lambda i: (0, 0))],
        out_specs=[pl.BlockSpec((tr, R), lambda i: (i, 0)), pl.BlockSpec((1, R), lambda i: (0, 0))],
        out_shape=[SDS((T, R), bf16), SDS((1, R), f32)], name=name,
        compiler_params=_params(("arbitrary",)))(dy, proj, g.reshape(1, R))


def _tri(n, lower):
    r = lax.broadcasted_iota(jnp.int32, (n, n), 0)
    c = lax.broadcasted_iota(jnp.int32, (n, n), 1)
    return ((r >= c) if lower else (c >= r)).astype(f32)


def _gate_fwd(proj, bpad, B, S, name):
    ch = _pick(S, 256)
    nch = S // ch
    cb = C_F // LANES

    def body(f_ref, b_ref, o_ref):
        tri = _tri(ch, True)
        carry = jnp.zeros((1, LANES), f32)
        for c in range(nch):
            z = f_ref[c * ch:(c + 1) * ch, :] + b_ref[...]
            lf = jnp.minimum(z, 0.0) - jnp.log1p(jnp.exp(-jnp.abs(z)))
            cs = jnp.dot(tri, lf, precision=lax.Precision.HIGHEST, preferred_element_type=f32) + carry
            o_ref[c * ch:(c + 1) * ch, :] = cs
            carry = cs[ch - 1:ch, :]

    return pl.pallas_call(
        body, grid=(B,), in_specs=[pl.BlockSpec((S, LANES), lambda b: (b, cb)), pl.BlockSpec((1, LANES), lambda b: (0, 0))],
        out_specs=pl.BlockSpec((S, LANES), lambda b: (b, 0)), out_shape=SDS((B * S, LANES), f32), name=name,
        compiler_params=_params(("parallel",)))(proj, bpad)


def _gate_bwd(dcum, proj, bpad, B, S, name):
    ch = _pick(S, 256)
    nch = S // ch
    cb = C_F // LANES

    def body(d_ref, f_ref, b_ref, dz_ref, db_ref):
        @pl.when(pl.program_id(0) == 0)
        def _():
            db_ref[...] = jnp.zeros_like(db_ref)

        tri = _tri(ch, False)
        carry = jnp.zeros((1, LANES), f32)
        dbs = jnp.zeros((1, LANES), f32)
        for c in reversed(range(nch)):
            d = d_ref[c * ch:(c + 1) * ch, :]
            dlf = jnp.dot(tri, d, precision=lax.Precision.HIGHEST, preferred_element_type=f32) + carry
            carry = carry + jnp.sum(d, axis=0, keepdims=True)
            z = f_ref[c * ch:(c + 1) * ch, :] + b_ref[...]
            dz = dlf / (1.0 + jnp.exp(z))
            dz_ref[c * ch:(c + 1) * ch, :] = dz.astype(bf16)
            dbs = dbs + jnp.sum(dz, axis=0, keepdims=True)
        db_ref[...] += dbs

    return pl.pallas_call(
        body, grid=(B,),
        in_specs=[pl.BlockSpec((S, LANES), lambda b: (b, 0)), pl.BlockSpec((S, LANES), lambda b: (b, cb)), pl.BlockSpec((1, LANES), lambda b: (0, 0))],
        out_specs=[pl.BlockSpec((S, LANES), lambda b: (b, 0)), pl.BlockSpec((1, LANES), lambda b: (0, 0))],
        out_shape=[SDS((B * S, LANES), bf16), SDS((1, LANES), f32)], name=name,
        compiler_params=_params(("arbitrary",)))(dcum, proj, bpad)


def _rot(v, c, sa, sb):
    return v * c + pltpu.roll(v, LANES - 16, 1) * sa + pltpu.roll(v, 16, 1) * sb


def _rot_t(v, c, sa, sb):
    return v * c + pltpu.roll(v * sa, 16, 1) + pltpu.roll(v * sb, LANES - 16, 1)


def _rope_q(x, tabs, *, bwd, mul, name):
    T = x.shape[0]
    W = MLA_HEADS * LANES
    tr = _pick(T, 512)
    fn = _rot_t if bwd else _rot

    def body(x_ref, c_ref, sa_ref, sb_ref, o_ref):
        c, sa, sb = c_ref[...], sa_ref[...], sb_ref[...]
        for g in range(MLA_HEADS):
            o_ref[:, g * LANES:(g + 1) * LANES] = (fn(x_ref[:, g * LANES:(g + 1) * LANES], c, sa, sb) * mul).astype(bf16)

    tab = pl.BlockSpec((tr, LANES), lambda i: (i, 0))
    wide = pl.BlockSpec((tr, W), lambda i: (i, 0))
    return pl.pallas_call(body, grid=(T // tr,), in_specs=[wide, tab, tab, tab], out_specs=wide, out_shape=SDS((T, W), bf16),
                          name=name, compiler_params=_params(("parallel",)))(x, *tabs)


def _rope_k(kv, tabs, proj, name):
    T = kv.shape[0]
    W = MLA_HEADS * LANES
    tr = _pick(T, 512)
    kcb = C_KPE // LANES

    def body(x_ref, c_ref, sa_ref, sb_ref, kpe_ref, k_ref, v_ref):
        c, sa, sb, kpe = c_ref[...], sa_ref[...], sb_ref[...], kpe_ref[...]
        low = lax.broadcasted_iota(jnp.int32, (tr, LANES), 1) < NOPE
        halves = []
        for g in range(MLA_HEADS):
            x = x_ref[:, g * LANES:(g + 1) * LANES]
            k_ref[:, g * LANES:(g + 1) * LANES] = _rot(jnp.where(low, x, 0.0) + kpe, c, sa, sb).astype(bf16)
            halves.append(x[:, NOPE:])
            if g % 2 == 1:
                v_ref[:, (g // 2) * LANES:(g // 2 + 1) * LANES] = jnp.concatenate(halves, axis=1).astype(bf16)
                halves = []

    tab = pl.BlockSpec((tr, LANES), lambda i: (i, 0))
    wide = pl.BlockSpec((tr, W), lambda i: (i, 0))
    return pl.pallas_call(
        body, grid=(T // tr,), in_specs=[wide, tab, tab, tab, pl.BlockSpec((tr, LANES), lambda i: (i, kcb))],
        out_specs=[wide, pl.BlockSpec((tr, W // 2), lambda i: (i, 0))], out_shape=[SDS((T, W), bf16), SDS((T, W // 2), bf16)],
        name=name, compiler_params=_params(("parallel",)))(kv, *tabs, proj)


def _rope_k_bwd(dk, dv, tabs, name):
    T = dk.shape[0]
    W = MLA_HEADS * LANES
    tr = _pick(T, 512)

    def body(dk_ref, dv_ref, c_ref, sa_ref, sb_ref, o_ref, s_ref):
        c, sa, sb = c_ref[...], sa_ref[...], sb_ref[...]
        tot = jnp.zeros((tr, LANES), f32)
        for g in range(MLA_HEADS):
            d = _rot_t(dk_ref[:, g * LANES:(g + 1) * LANES], c, sa, sb)
            tot = tot + d
            o_ref[:, g * LANES:(g + 1) * LANES] = jnp.concatenate(
                [d[:, :NOPE].astype(bf16), dv_ref[:, g * MLA_VD:(g + 1) * MLA_VD]], axis=1)
        s_ref[...] = tot.astype(bf16)

    tab = pl.BlockSpec((tr, LANES), lambda i: (i, 0))
    wide = pl.BlockSpec((tr, W), lambda i: (i, 0))
    return pl.pallas_call(
        body, grid=(T // tr,), in_specs=[wide, pl.BlockSpec((tr, W // 2), lambda i: (i, 0)), tab, tab, tab],
        out_specs=[wide, tab], out_shape=[SDS((T, W), bf16), SDS((T, LANES), bf16)], name=name,
        compiler_params=_params(("parallel",)))(dk, dv, *tabs)


def _tile_mask(mode, r0, c0, nr, nc):
    r = r0 + lax.broadcasted_iota(jnp.int32, (nr, nc), 0)
    c = c0 + lax.broadcasted_iota(jnp.int32, (nr, nc), 1)
    if mode == "chunk":
        return (c // CHUNK) <= (r // CHUNK)
    return c <= r


def _diagonal(block, carry, i, tq, tk):
    nfull = (i * tq) // tk
    for d in range(max(1, tq // tk)):
        carry = block((nfull + d) * tk, tk, 0, tq, carry, True)
    return carry


def _attn_tiles(Sq, Sk):
    tk = _pick(Sk, ATT_TK)
    tq = _pick(Sq, ATT_TQ)
    assert (tk % tq == 0 or tq % tk == 0) and tq % CHUNK == 0 and tk % CHUNK == 0
    return tq, tk


def _stage_heads(dst, src_ref, hb, w, mark=None, mul=1.0):
    S = src_ref.shape[0]
    ch = _pick(S, 256)
    assert w <= LANES and (mark is None or w <= mark < LANES)

    def step(c, _):
        rows = pl.ds(pl.multiple_of(c * ch, ch), ch)
        for hh in range(hb):
            x = src_ref[rows, hh * w:(hh + 1) * w]
            x = (x.astype(f32) * mul).astype(bf16) if mul != 1.0 else x.astype(bf16)
            if w < LANES:
                pad = jnp.zeros((ch, LANES - w), bf16)
                if mark is not None:
                    lane = lax.broadcasted_iota(jnp.int32, (ch, LANES - w), 1)
                    pad = jnp.where(lane == mark - w, 1.0, 0.0).astype(bf16)
                x = jnp.concatenate([x, pad], axis=1)
            dst[hh, rows, :] = x
        return 0

    lax.fori_loop(0, S // ch, step, 0)


def _carry(rider, refs, n_in, n_out, n_scr, grid):
    if rider is None:
        return refs
    ri, ro = len(rider.inputs), len(rider.out_shapes)
    own = refs[:n_in] + refs[n_in + ri:n_in + ri + n_out] + refs[n_in + ri + n_out + ro:n_in + ri + n_out + ro + n_scr]
    rider.refs = (refs[n_in:n_in + ri], refs[n_in + ri + n_out:n_in + ri + n_out + ro], refs[n_in + ri + n_out + ro + n_scr:])
    first = functools.reduce(jnp.logical_and, [pl.program_id(a) == 0 for a in range(len(grid))])

    @pl.when(first)
    def _():
        rider.start(*rider.refs)

    return own


def _carry_finish(rider, grid):
    if rider is None:
        return
    last = functools.reduce(jnp.logical_and, [pl.program_id(a) == n - 1 for a, n in enumerate(grid)])

    @pl.when(last)
    def _():
        rider.finish(*rider.refs)


def _carrier_call(body, rider, ins, in_specs, out_shape, out_specs, scratch, grid, name):
    if rider is None:
        return pl.pallas_call(body, grid=grid, in_specs=in_specs, out_specs=out_specs, out_shape=out_shape, scratch_shapes=scratch,
                              name=name, compiler_params=_params(("parallel",) * len(grid)))(*ins)
    any_spec = pl.BlockSpec(memory_space=pl.ANY)
    ri, ro = len(rider.inputs), len(rider.out_shapes)
    res = pl.pallas_call(
        body, grid=grid, in_specs=list(in_specs) + [any_spec] * ri, out_specs=list(out_specs) + [any_spec] * ro,
        out_shape=list(out_shape) + rider.out_shapes, scratch_shapes=list(scratch) + rider.scratch,
        input_output_aliases={len(ins) + i: len(out_shape) + o for i, o in rider.aliases.items()}, name=name,
        compiler_params=_params(("arbitrary",) * len(grid)))(*ins, *rider.inputs)
    return res[:len(out_shape)], res[len(out_shape):]


def _attn_fwd(q_arr, q_cb, k_arr, k_cb, v_arr, v_cb, *, B, Sq, Sk, nblk, hb, dqk, dv, mode, scale, cum_r=None, rider=None, name):
    wq, wv = hb * dqk, hb * dv
    tq, tk = _attn_tiles(Sq, Sk)
    fox = mode == "fox"
    n_in = 4 if fox else 3
    ones = dv < LANES

    def body(*refs):
        refs = _carry(rider, refs, n_in, 2, 3, (B, nblk))
        q_ref, k_ref, v_ref = refs[:3]
        cr_ref = refs[3] if fox else None
        o_ref, lse_ref, qh, kh, vh = refs[n_in:]
        _stage_heads(qh, q_ref, hb, dqk, mul=scale)
        _stage_heads(kh, k_ref, hb, dqk)
        _stage_heads(vh, v_ref, hb, dv, dv if ones else None)

        def qstep(i, _):
            rows = pl.ds(pl.multiple_of(i * tq, tq), tq)
            qs = [qh[hh, rows, :] for hh in range(hb)]

            def block(c0, w, r0, nr, carry, diag):
                cols = pl.ds(pl.multiple_of(c0, w), w)
                out = []
                for hh in range(hb):
                    m, l, acc = carry[hh]
                    s = lax.dot_general(qs[hh][r0:r0 + nr], kh[hh, cols, :], _DIMS["nt"], preferred_element_type=f32)
                    if fox:
                        s = s - cr_ref[0, hh:hh + 1, cols]
                    if diag:
                        s = jnp.where(_tile_mask(mode, i * tq + r0, c0, nr, w), s, NEG_INF)
                    m_new = jnp.maximum(m, jnp.max(s, axis=-1, keepdims=True))
                    a = jnp.exp(m - m_new)
                    p = jnp.exp(s - m_new)
                    if not ones:
                        l = a * l + jnp.sum(p, axis=-1, keepdims=True)
                    acc = a * acc + lax.dot_general(p.astype(bf16), vh[hh, cols, :], _DIMS["nn"], preferred_element_type=f32)
                    out.append((m_new, l, acc))
                return tuple(out)

            def tile(j, carry):
                return block(j * tk, tk, 0, tq, carry, False)

            carry = tuple((jnp.full((tq, 1), -jnp.inf, f32), jnp.zeros((tq, 1), f32), jnp.zeros((tq, LANES), f32)) for _ in range(hb))
            if mode == "none":
                carry = lax.fori_loop(0, Sk // tk, tile, carry)
            else:
                nfull = (i * tq) // tk
                carry = lax.fori_loop(0, nfull, tile, carry)
                carry = _diagonal(block, carry, i, tq, tk)
            for hh in range(hb):
                m, l, acc = carry[hh]
                if ones:
                    l = acc[:, dv:dv + 1]
                o_ref[rows, hh * dv:(hh + 1) * dv] = acc[:, :dv] / l
                lse_ref[0, rows, hh:hh + 1] = m + jnp.log(l)
            return 0

        lax.fori_loop(0, Sq // tq, qstep, 0)
        _carry_finish(rider, (B, nblk))

    in_specs = [pl.BlockSpec((Sq, wq), lambda b, p: (b, q_cb + p)), pl.BlockSpec((Sk, wq), lambda b, p: (b, k_cb + p)),
                pl.BlockSpec((Sk, wv), lambda b, p: (b, v_cb + p))]
    ins = [q_arr, k_arr, v_arr]
    if fox:
        in_specs.append(pl.BlockSpec((1, hb, Sk), lambda b, p: (b * nblk + p, 0, 0)))
        ins.append(cum_r)
    return _carrier_call(
        body, rider, ins, in_specs,
        [SDS((B * Sq, nblk * wv), f32), SDS((B * nblk, Sq, hb), f32)],
        [pl.BlockSpec((Sq, wv), lambda b, p: (b, p)), pl.BlockSpec((1, Sq, hb), lambda b, p: (b * nblk + p, 0, 0))],
        [pltpu.VMEM((hb, Sq, LANES), bf16), pltpu.VMEM((hb, Sk, LANES), bf16), pltpu.VMEM((hb, Sk, LANES), bf16)],
        (B, nblk), name)


def _attn_bwd(q_arr, q_cb, k_arr, k_cb, v_arr, v_cb, o, do, lse, *, B, Sq, Sk, nblk, hb, dqk, dv, mode, scale, cum_r=None,
              dq_dtype=bf16, rider=None, name):
    wq, wv = hb * dqk, hb * dv
    tq, tk = _attn_tiles(Sq, Sk)
    fox = mode == "fox"
    n_in = 7 if fox else 6
    n_out = 4 if fox else 3
    if fox:
        assert dqk + 2 <= LANES

    def body(*refs):
        refs = _carry(rider, refs, n_in, n_out, 7 if fox else 6, (B, nblk))
        q_ref, k_ref, v_ref, o_ref, do_ref, lse_ref = refs[:6]
        cr_ref = refs[6] if fox else None
        dq_ref, dk_ref, dv_ref = refs[n_in:n_in + 3]
        dcum_ref = refs[n_in + 3] if fox else None
        qh, kh, vh, doh, dk_acc, dv_acc = refs[n_in + n_out:n_in + n_out + 6]
        dcq_acc = refs[n_in + n_out + 6] if fox else None
        _stage_heads(qh, q_ref, hb, dqk, dqk + 1 if fox else None, mul=scale)
        _stage_heads(kh, k_ref, hb, dqk, dqk if fox else None)
        _stage_heads(vh, v_ref, hb, dv)
        _stage_heads(doh, do_ref, hb, dv)
        dk_acc[...] = jnp.zeros_like(dk_acc)
        dv_acc[...] = jnp.zeros_like(dv_acc)

        def qstep(i, _):
            rows = pl.ds(pl.multiple_of(i * tq, tq), tq)
            qs = [qh[hh, rows, :] for hh in range(hb)]
            dos = [doh[hh, rows, :] for hh in range(hb)]
            dds = [jnp.sum(do_ref[rows, hh * dv:(hh + 1) * dv] * o_ref[rows, hh * dv:(hh + 1) * dv], axis=-1, keepdims=True)
                   for hh in range(hb)]
            lses = [lse_ref[0, rows, hh:hh + 1] for hh in range(hb)]

            def block(c0, w, r0, nr, carry, diag):
                cols = pl.ds(pl.multiple_of(c0, w), w)
                rs = slice(r0, r0 + nr)
                out = []
                for hh in range(hb):
                    k = kh[hh, cols, :]
                    q, do_ = qs[hh][rs], dos[hh][rs]
                    s = lax.dot_general(q, k, _DIMS["nt"], preferred_element_type=f32)
                    if fox:
                        s = s - cr_ref[0, hh:hh + 1, cols]
                    if diag:
                        s = jnp.where(_tile_mask(mode, i * tq + r0, c0, nr, w), s, NEG_INF)
                    p = jnp.exp(s - lses[hh][rs])
                    dv_acc[hh, cols, :] += lax.dot_general(p.astype(bf16), do_, _DIMS["tn"], preferred_element_type=f32)
                    dp = lax.dot_general(do_, vh[hh, cols, :], _DIMS["nt"], preferred_element_type=f32)
                    ds = p * (dp - dds[hh][rs])
                    dsb = ds.astype(bf16)
                    out.append(carry[hh] + lax.dot_general(dsb, k, _DIMS["nn"], preferred_element_type=f32))
                    dk_acc[hh, cols, :] += lax.dot_general(dsb, q, _DIMS["tn"], preferred_element_type=f32)
                return tuple(out)

            def tile(j, carry):
                return block(j * tk, tk, 0, tq, carry, False)

            carry = tuple(jnp.zeros((tq, LANES), f32) for _ in range(hb))
            if mode == "none":
                carry = lax.fori_loop(0, Sk // tk, tile, carry)
            else:
                nfull = (i * tq) // tk
                carry = lax.fori_loop(0, nfull, tile, carry)
                carry = _diagonal(block, carry, i, tq, tk)
            for hh in range(hb):
                dq_ref[rows, hh * dqk:(hh + 1) * dqk] = (carry[hh][:, :dqk] * scale).astype(dq_dtype)
                if fox:
                    dcq_acc[rows, hh:hh + 1] = carry[hh][:, dqk:dqk + 1]
            return 0

        lax.fori_loop(0, Sq // tq, qstep, 0)
        for hh in range(hb):
            dk_ref[:, hh * dqk:(hh + 1) * dqk] = dk_acc[hh, :, :dqk].astype(dq_dtype)
            dv_ref[:, hh * dv:(hh + 1) * dv] = dv_acc[hh, :, :dv].astype(bf16)
            if fox:
                dcum_ref[0, :, hh:hh + 1] = dcq_acc[:, hh:hh + 1] - dk_acc[hh, :, dqk + 1:dqk + 2]
        _carry_finish(rider, (B, nblk))

    qs_ =pl.BlockSpec((Sq, wq), lambda b, p: (b, q_cb + p))
    ks_ = pl.BlockSpec((Sk, wq), lambda b, p: (b, k_cb + p))
    vs_ = pl.BlockSpec((Sk, wv), lambda b, p: (b, v_cb + p))
    os_ = pl.BlockSpec((Sq, wv), lambda b, p: (b, p))
    st = pl.BlockSpec((1, Sq, hb), lambda b, p: (b * nblk + p, 0, 0))
    rw = pl.BlockSpec((1, hb, Sk), lambda b, p: (b * nblk + p, 0, 0))
    in_specs = [qs_, ks_, vs_, os_, os_, st] + ([rw] if fox else [])
    ins = [q_arr, k_arr, v_arr, o, do, lse] + ([cum_r] if fox else [])
    out_specs = [pl.BlockSpec((Sq, wq), lambda b, p: (b, p)), pl.BlockSpec((Sk, wq), lambda b, p: (b, p)),
                 pl.BlockSpec((Sk, wv), lambda b, p: (b, p))] + ([st] if fox else [])
    out_shape = [SDS((B * Sq, nblk * wq), dq_dtype), SDS((B * Sk, nblk * wq), dq_dtype), SDS((B * Sk, nblk * wv), bf16)]
    scratch = [pltpu.VMEM((hb, Sq, LANES), bf16), pltpu.VMEM((hb, Sk, LANES), bf16), pltpu.VMEM((hb, Sk, LANES), bf16),
               pltpu.VMEM((hb, Sq, LANES), bf16), pltpu.VMEM((hb, Sk, LANES), f32), pltpu.VMEM((hb, Sk, LANES), f32)]
    if fox:
        assert Sq == Sk
        out_shape.append(SDS((B * nblk, Sq, hb), f32))
        scratch.append(pltpu.VMEM((Sq, hb), f32))
    return _carrier_call(body, rider, ins, in_specs, out_shape, out_specs, scratch, (B, nblk), name)


def _merge_fwd(oa, ob, oc, w_br, proj, D, name):
    T = oa.shape[0]
    tr = _pick(T, 256)
    g0 = C_GATE // D
    o_spec = pl.BlockSpec((tr, BW), lambda i: (i, 0))
    w_spec = pl.BlockSpec((3, BW, D), lambda i: (0, 0, 0))

    def body(oa_ref, ob_ref, oc_ref, w_ref, ga_ref, gb_ref, gc_ref, m_ref):
        tot = jnp.zeros((tr, D), f32)
        for n, (o_ref, g_ref) in enumerate(((oa_ref, ga_ref), (ob_ref, gb_ref), (oc_ref, gc_ref))):
            bp = lax.dot_general(o_ref[...].astype(bf16), w_ref[n], _DIMS["nn"], preferred_element_type=f32)
            tot = tot + jax.nn.sigmoid(g_ref[...]) * bp
        m_ref[...] = tot.astype(bf16)

    gspecs = [pl.BlockSpec((tr, D), lambda i, n=n: (i, g0 + n)) for n in range(3)]
    return pl.pallas_call(
        body, grid=(T // tr,), in_specs=[o_spec, o_spec, o_spec, w_spec, *gspecs],
        out_specs=pl.BlockSpec((tr, D), lambda i: (i, 0)), out_shape=SDS((T, D), bf16), name=name,
        compiler_params=_params(("parallel",)))(oa, ob, oc, w_br, proj, proj, proj)


def _merge_bwd(dm, oa, ob, oc, w_br, proj, D, name, rider=None):
    T = oa.shape[0]
    tr = _pick(T, 256)
    g0 = C_GATE // D
    o_spec = pl.BlockSpec((tr, BW), lambda i: (i, 0))
    d_spec = pl.BlockSpec((tr, D), lambda i: (i, 0))
    w_spec = pl.BlockSpec((3, BW, D), lambda i: (0, 0, 0))

    def body(*refs):
        refs = _carry(rider, refs, 8, 9, 0, (T // tr,))
        dm_ref, oa_ref, ob_ref, oc_ref, w_ref, ga_ref, gb_ref, gc_ref = refs[:8]
        outs = refs[8:]
        dmv = dm_ref[...]
        for n, (o_ref, g_ref) in enumerate(((oa_ref, ga_ref), (ob_ref, gb_ref), (oc_ref, gc_ref))):
            bp = lax.dot_general(o_ref[...].astype(bf16), w_ref[n], _DIMS["nn"], preferred_element_type=f32)
            g = jax.nn.sigmoid(g_ref[...])
            dbp = (dmv * g).astype(bf16)
            outs[n][...] = (dmv * bp * g * (1.0 - g)).astype(bf16)
            outs[3 + n][...] = lax.dot_general(dbp, w_ref[n], _DIMS["nt"], preferred_element_type=f32)
            outs[6 + n][...] = dbp
        _carry_finish(rider, (T // tr,))

    gspecs = [pl.BlockSpec((tr, D), lambda i, n=n: (i, g0 + n)) for n in range(3)]
    return _carrier_call(
        body, rider, [dm, oa, ob, oc, w_br, proj, proj, proj], [d_spec, o_spec, o_spec, o_spec, w_spec, *gspecs],
        [SDS((T, D), bf16)] * 3 + [SDS((T, BW), f32)] * 3 + [SDS((T, D), bf16)] * 3,
        [d_spec] * 3 + [o_spec] * 3 + [d_spec] * 3, [], (T // tr,), name)


def _loss(h, target, name):
    T, D = h.shape
    tr = _pick(T, 256)
    nt = T // tr
    row = pl.BlockSpec((tr, D), lambda i: (i, 0))

    def body(h_ref, t_ref, l_ref, d_ref, acc):
        i = pl.program_id(0)

        @pl.when(i == 0)
        def _():
            acc[...] = jnp.zeros_like(acc)

        e = h_ref[...] - t_ref[...]
        d_ref[...] = e * (1.0 / D)
        acc[...] += jnp.sum(e * e, axis=0, keepdims=True)

        @pl.when(i == nt - 1)
        def _():
            l_ref[...] = jnp.sum(acc[...], axis=-1, keepdims=True) * (0.5 / D)

    return pl.pallas_call(
        body, grid=(nt,), in_specs=[row, row], out_specs=[pl.BlockSpec((1, 1), lambda i: (0, 0)), row],
        out_shape=[SDS((1, 1), f32), SDS((T, D), f32)], scratch_shapes=[pltpu.VMEM((1, D), f32)], name=name,
        compiler_params=_params(("arbitrary",)))(h, target)


def _adamw(w, m, v, parts, name):
    shape = w.shape
    C = shape[-1]
    R = int(np.prod(shape[:-1])) if len(shape) > 1 else 1
    n_parts, Cp = parts.shape[0], parts.shape[-1]
    tr = _pick(R, 256)
    spec = pl.BlockSpec((tr, C), lambda i: (i, 0))

    def body(w_ref, m_ref, v_ref, p_ref, g_ref, d_ref, mo_ref, vo_ref):
        gv = p_ref[0].astype(f32)
        for q in range(1, n_parts):
            gv = gv + p_ref[q].astype(f32)
        gv = gv[:, :C]
        mn = ADAM_B1 * m_ref[...] + (1.0 - ADAM_B1) * gv
        vn = ADAM_B2 * v_ref[...] + (1.0 - ADAM_B2) * (gv * gv)
        m_hat = mn / (1.0 - ADAM_B1 ** ADAM_STEP)
        v_hat = vn / (1.0 - ADAM_B2 ** ADAM_STEP)
        g_ref[...] = gv
        d_ref[...] = -ADAM_LR * (m_hat / (jnp.sqrt(v_hat) + ADAM_EPS) + ADAM_WD * w_ref[...])
        mo_ref[...] = mn
        vo_ref[...] = vn

    outs = pl.pallas_call(
        body, grid=(R // tr,), in_specs=[spec] * 3 + [pl.BlockSpec((n_parts, tr, Cp), lambda i: (0, i, 0))], out_specs=[spec] * 4,
        out_shape=[SDS((R, C), f32)] * 4, name=name,
        compiler_params=_params(("parallel",)))(*[a.reshape(R, C) for a in (w, m, v)], parts.reshape(n_parts, R, Cp))
    return [o.reshape(shape) for o in outs]


def _coords():
    return lax.axis_index("x"), lax.axis_index("y"), lax.axis_index("c")


def _window(ref, axis, idx, n):
    start = pl.multiple_of(idx * n, n)
    if axis == 1:
        return ref.at[:, pl.ds(start, n), :]
    return ref.at[:, :, pl.ds(start, n)]


class _Rider:
    def __init__(self, inputs, out_shapes, aliases, scratch, start, finish):
        self.inputs, self.out_shapes, self.aliases, self.scratch = list(inputs), list(out_shapes), dict(aliases), list(scratch)
        self.start, self.finish = start, finish


def _join_riders(riders):
    riders = [r for r in riders if r is not None]
    if len(riders) < 2:
        return riders[0] if riders else None
    ni = np.cumsum([0] + [len(r.inputs) for r in riders])
    no = np.cumsum([0] + [len(r.out_shapes) for r in riders])
    ns = np.cumsum([0] + [len(r.scratch) for r in riders])
    aliases = {int(ni[k]) + i: int(no[k]) + o for k, r in enumerate(riders) for i, o in r.aliases.items()}

    def phase(which):
        def run(ins, outs, scr):
            for k, r in enumerate(riders):
                getattr(r, which)(ins[ni[k]:ni[k + 1]], outs[no[k]:no[k + 1]], scr[ns[k]:ns[k + 1]])
        return run

    return _Rider(sum([r.inputs for r in riders], []), sum([r.out_shapes for r in riders], []), aliases,
                  sum([r.scratch for r in riders], []), phase("start"), phase("finish"))


def _run_rider(r, name):
    ni, no = len(r.inputs), len(r.out_shapes)

    def body(*refs):
        ins, outs, scr = refs[:ni], refs[ni:ni + no], refs[ni + no:]
        r.start(ins, outs, scr)
        r.finish(ins, outs, scr)

    any_spec = pl.BlockSpec(memory_space=pl.ANY)
    return pl.pallas_call(body, out_shape=r.out_shapes, in_specs=[any_spec] * ni, out_specs=[any_spec] * no,
                          input_output_aliases=r.aliases, scratch_shapes=r.scratch, name=name)(*r.inputs)


def _gather_rider(shards, axes):
    nt = len(shards)
    sizes = [s.shape[a] for s, a in zip(shards, axes)]
    out_shapes = [SDS(tuple(d * N_DEV if i == a else d for i, d in enumerate(s.shape)), s.dtype) for s, a in zip(shards, axes)]

    def plan(x_refs, out_refs, scr):
        send_sems, recv_sems, local_sems = scr
        x, y, c = _coords()
        me, sibling = (x, y, c), (x, y, 1 - c)
        chips = [(1 - x, y), (x, 1 - y), (1 - x, 1 - y)]

        def win(t, px, py, pc):
            return _window(out_refs[t], axes[t], 4 * px + 2 * py + pc, sizes[t])

        def copy(t, k, block, to, src=None):
            return pltpu.make_async_remote_copy(
                src_ref=win(t, *block) if src is None else src, dst_ref=win(t, *block),
                send_sem=send_sems.at[7 * t + k], recv_sem=recv_sems.at[7 * t + k], device_id=to, device_id_type=MESH)

        mine = [pltpu.make_async_copy(x_refs[t], win(t, *me), local_sems.at[t]) for t in range(nt)]
        first = []
        for t in range(nt):
            first.append(copy(t, 0, me, sibling, src=x_refs[t]))
            first += [copy(t, 1 + j, me, (*chip, c), src=x_refs[t]) for j, chip in enumerate(chips)]
        return c, me, sibling, chips, copy, mine, first

    def start(x_refs, out_refs, scr):
        _, _, _, _, _, mine, first = plan(x_refs, out_refs, scr)
        for cp in mine + first:
            cp.start()

    def finish(x_refs, out_refs, scr):
        c, me, sibling, chips, copy, mine, first = plan(x_refs, out_refs, scr)
        passed = []
        for j, chip in enumerate(chips):
            for t in range(nt):
                copy(t, 1 + j, (*chip, c), me).wait_recv()
                fwd = copy(t, 4 + j, (*chip, c), sibling)
                fwd.start()
                passed.append(fwd)
        for t in range(nt):
            copy(t, 0, sibling, me).wait_recv()
            for j, chip in enumerate(chips):
                copy(t, 4 + j, (*chip, 1 - c), me).wait_recv()
        for cp in first + passed:
            cp.wait_send()
        for cp in mine:
            cp.wait()

    scratch = [pltpu.SemaphoreType.DMA((7 * nt,)), pltpu.SemaphoreType.DMA((7 * nt,)), pltpu.SemaphoreType.DMA((nt,))]
    return _Rider(shards, out_shapes, {}, scratch, start, finish)


N_CHIP = N_DEV // 2


def _pair_exchange_rider(grads, axes, shards):
    nt = len(grads)
    sizes = [s[a] for s, a in zip(shards, axes)]

    def plan(ins, outs, scr):
        send_sems, recv_sems = scr
        x, y, c = _coords()
        return [pltpu.make_async_remote_copy(
            src_ref=_window(ins[t], axes[t], 2 * q + (1 - c), sizes[t]), dst_ref=outs[t].at[q],
            send_sem=send_sems.at[N_CHIP * t + q], recv_sem=recv_sems.at[N_CHIP * t + q],
            device_id=(x, y, 1 - c), device_id_type=MESH) for t in range(nt) for q in range(N_CHIP)]

    def start(ins, outs, scr):
        for cp in plan(ins, outs, scr):
            cp.start()

    def finish(ins, outs, scr):
        copies = plan(ins, outs, scr)
        for cp in copies:
            cp.wait_recv()
        for cp in copies:
            cp.wait_send()

    scratch = [pltpu.SemaphoreType.DMA((N_CHIP * nt,)), pltpu.SemaphoreType.DMA((N_CHIP * nt,))]
    return _Rider(grads, [SDS((N_CHIP,) + tuple(s), g.dtype) for s, g in zip(shards, grads)], {}, scratch, start, finish)


def _pair_sum(grad, buf, axis, core, name):
    _, G, r, c = buf.shape
    tr = _pick(r, 512)
    nr = r // tr

    def body(core_ref, g_ref, b_ref, o_ref):
        o_ref[0, 0] = (g_ref[0].astype(f32) + b_ref[0, 0].astype(f32)).astype(bf16)

    if axis == 2:
        g_spec = pl.BlockSpec((1, tr, c), lambda q, g, i, core_ref: (g, i, 2 * q + core_ref[0]))
    else:
        g_spec = pl.BlockSpec((1, tr, c), lambda q, g, i, core_ref: (g, (2 * q + core_ref[0]) * nr + i, 0))
    blk = pl.BlockSpec((1, 1, tr, c), lambda q, g, i, core_ref: (q, g, i, 0))
    grid_spec = pltpu.PrefetchScalarGridSpec(num_scalar_prefetch=1, grid=(N_CHIP, G, nr), in_specs=[g_spec, blk], out_specs=blk)
    return pl.pallas_call(body, grid_spec=grid_spec, out_shape=SDS(buf.shape, bf16), name=name,
                          compiler_params=_params(("parallel", "parallel", "parallel")))(core, grad, buf)


def _chip_exchange_rider(chips, slots, layer):
    nt = len(chips)

    def plan(ins, outs, scr):
        send_sems, recv_sems, local_sems = scr
        x, y, c = _coords()
        me = 2 * x + y
        mine = [pltpu.make_async_copy(ins[t].at[me], outs[t].at[me, layer], local_sems.at[t]) for t in range(nt)]
        copies = []
        for k in range(1, N_CHIP):
            px, py = lax.rem(x + (k >> 1), 2), lax.rem(y + (k & 1), 2)
            copies += [pltpu.make_async_remote_copy(
                src_ref=ins[t].at[2 * px + py], dst_ref=outs[t].at[me, layer],
                send_sem=send_sems.at[3 * t + k - 1], recv_sem=recv_sems.at[3 * t + k - 1],
                device_id=(px, py, c), device_id_type=MESH) for t in range(nt)]
        return mine, copies

    def start(ins, outs, scr):
        mine, copies = plan(ins, outs, scr)
        for cp in mine + copies:
            cp.start()

    def finish(ins, outs, scr):
        mine, copies = plan(ins, outs, scr)
        for cp in copies:
            cp.wait_recv()
        for cp in copies:
            cp.wait_send()
        for cp in mine:
            cp.wait()

    scratch = [pltpu.SemaphoreType.DMA((3 * nt,)), pltpu.SemaphoreType.DMA((3 * nt,)), pltpu.SemaphoreType.DMA((nt,))]
    return _Rider(list(chips) + list(slots), [SDS(s.shape, s.dtype) for s in slots], {nt + t: t for t in range(nt)}, scratch,
                  start, finish)


_BIG = (("w_in", 2), ("w_uq", 2), ("w_ukv", 2), ("w_mem_kv", 1), ("w_br", 2), ("w_out", 1), ("w_ff1", 2), ("w_ff2", 1))


def _in_col_maps(D):
    d_in = O_GATE + 3 * D
    sh = d_in // N_DEV
    shp = _round_up(sh, LANES)
    perm_of_orig = np.empty(d_in, np.int64)
    for a, b, dst in ((0, O_F, C_Q), (O_F, O_CQ, C_F), (O_CQ, O_CKV, C_CQ), (O_CKV, O_KR, C_CKV), (O_KR, O_QM, C_KPE + NOPE),
                      (O_QM, O_GATE, C_QM), (O_GATE, d_in, C_GATE)):
        perm_of_orig[a:b] = dst + np.arange(b - a)
    orig = np.arange(d_in)
    gath_of_orig = (orig // sh) * shp + orig % sh
    fwd = -np.ones(C_GATE + 3 * D, np.int64)
    fwd[perm_of_orig] = gath_of_orig
    bwd = -np.ones(N_DEV * shp, np.int64)
    bwd[gath_of_orig] = perm_of_orig
    return fwd, bwd


def _lane_pad(a):
    c = a.shape[-1]
    return jnp.pad(a, [(0, 0)] * (a.ndim - 1) + [(0, _round_up(c, LANES) - c)])


def _as3(a):
    return a.reshape((-1,) + a.shape[-2:]) if a.ndim != 3 else a


def _rope_tables(positions):
    inv_freq = ROPE_BASE ** (-jnp.arange(0, ROPE, 2, dtype=f32) / ROPE)
    ang = positions.astype(f32).reshape(-1, 1) * inv_freq
    cos, sin = jnp.cos(ang), jnp.sin(ang)
    T = ang.shape[0]
    one, zero = jnp.ones((T, NOPE), f32), jnp.zeros((T, 16), f32)
    c = jnp.concatenate([one, cos, cos, jnp.ones((T, 32), f32)], axis=1)
    sa = jnp.concatenate([jnp.zeros((T, NOPE), f32), -sin, zero, jnp.zeros((T, 32), f32)], axis=1)
    sb = jnp.concatenate([jnp.zeros((T, NOPE), f32), zero, sin, jnp.zeros((T, 32), f32)], axis=1)
    return c, sa, sb


def kernel(x, mem, positions, ln_in_g, ln_in_b, w_in, b_forget, w_uq, g_cq, w_ukv, g_ckv, w_mem_kv, w_br, w_out, ln1_g, ln1_b, w_ff1, w_ff2, ln2_g, ln2_b, loss_target, m_ln_in_g, m_ln_in_b, m_w_in, m_b_forget, m_w_uq, m_g_cq, m_w_ukv, m_g_ckv, m_w_mem_kv, m_w_br, m_w_out, m_ln1_g, m_ln1_b, m_w_ff1, m_w_ff2, m_ln2_g, m_ln2_b, v_ln_in_g, v_ln_in_b, v_w_in, v_b_forget, v_w_uq, v_g_cq, v_w_ukv, v_g_ckv, v_w_mem_kv, v_w_br, v_w_out, v_ln1_g, v_ln1_b, v_w_ff1, v_w_ff2, v_ln2_g, v_ln2_b):
    B, S, D = x.shape
    T = B * S
    L = w_in.shape[0]
    NM = mem.shape[1]
    alpha = (2 * L) ** 0.25
    weights = dict(ln_in_g=ln_in_g, ln_in_b=ln_in_b, w_in=w_in, b_forget=b_forget, w_uq=w_uq, g_cq=g_cq, w_ukv=w_ukv, g_ckv=g_ckv,
                   w_mem_kv=w_mem_kv, w_br=w_br, w_out=w_out, ln1_g=ln1_g, ln1_b=ln1_b, w_ff1=w_ff1, w_ff2=w_ff2, ln2_g=ln2_g, ln2_b=ln2_b)
    mom_m = dict(ln_in_g=m_ln_in_g, ln_in_b=m_ln_in_b, w_in=m_w_in, b_forget=m_b_forget, w_uq=m_w_uq, g_cq=m_g_cq, w_ukv=m_w_ukv,
                 g_ckv=m_g_ckv, w_mem_kv=m_w_mem_kv, w_br=m_w_br, w_out=m_w_out, ln1_g=m_ln1_g, ln1_b=m_ln1_b, w_ff1=m_w_ff1,
                 w_ff2=m_w_ff2, ln2_g=m_ln2_g, ln2_b=m_ln2_b)
    mom_v = dict(ln_in_g=v_ln_in_g, ln_in_b=v_ln_in_b, w_in=v_w_in, b_forget=v_b_forget, w_uq=v_w_uq, g_cq=v_g_cq, w_ukv=v_w_ukv,
                 g_ckv=v_g_ckv, w_mem_kv=v_w_mem_kv, w_br=v_w_br, w_out=v_w_out, ln1_g=v_ln1_g, ln1_b=v_ln1_b, w_ff1=v_w_ff1,
                 w_ff2=v_w_ff2, ln2_g=v_ln2_g, ln2_b=v_ln2_b)
    order = list(weights)
    big_axes = [ax for _, ax in _BIG]
    fwd_map, bwd_map = _in_col_maps(D)
    shard3 = {n: _lane_pad(weights[n]) for n, _ in _BIG}

    half = len(_BIG) // 2

    def gather_rider(l, lo=0, hi=len(_BIG)):
        return _gather_rider([_as3(shard3[n][l]).astype(bf16) for n, _ in _BIG[lo:hi]], big_axes[lo:hi])

    def named(full, lo, hi):
        return {n: (f if n == "w_br" else f[0]) for (n, _), f in zip(_BIG[lo:hi], full)}

    tabs = _rope_tables(positions)
    x2d, mem2d, tgt2d = x.reshape(T, D), mem.reshape(B * NM, D), loss_target.reshape(T, D)
    fox = dict(B=B, Sq=S, Sk=S, nblk=FOX_HEADS // 2, hb=2, dqk=FOX_HD, dv=FOX_HD, mode="fox", scale=FOX_HD ** -0.5)
    mla_scale = (NOPE + ROPE) ** -0.5
    mla = dict(B=B, Sq=S, Sk=S, nblk=MLA_HEADS // 2, hb=2, dqk=LANES, dv=MLA_VD, mode="chunk", scale=1.0)
    memat = dict(B=B, Sq=S, Sk=NM, nblk=MEM_HEADS, hb=1, dqk=MEM_HD, dv=MEM_HD, mode="none", scale=MEM_HD ** -0.5)
    nb = FOX_HEADS // 2

    h, hb, xh0, rstd0 = _ln_fwd(None, x2d, ln_in_g, ln_in_b, alpha, "ln_in")
    saved = []
    first = _run_rider(gather_rider(0, 0, half), "weights_all_gather")
    for l in range(L):
        W = named(first, 0, half)
        W["w_in"] = _lane_permute(W["w_in"], fwd_map, bf16, "w_in_reorder")
        bpad = jnp.pad(b_forget[l].reshape(1, FOX_HEADS), ((0, 0), (0, LANES - FOX_HEADS)))
        proj = _mm(hb, W["w_in"], name="proj")
        cum = _gate_fwd(proj, bpad, B, S, "gate_fwd")
        cum_r = cum[:, :FOX_HEADS].reshape(B, S, nb, 2).transpose(0, 2, 3, 1).reshape(B * nb, 2, S)
        more = l + 1 < L
        (o_a, lse_a), second = _attn_fwd(proj, C_Q // LANES, proj, C_K // LANES, proj, C_V // LANES, cum_r=cum_r, name="fox_fwd",
                                         rider=gather_rider(l, half), **fox)
        W.update(named(second, half, len(_BIG)))
        cqn = _rms_fwd(proj, C_CQ, Q_RANK, g_cq[l], "rms_q")
        ckvn = _rms_fwd(proj, C_CKV, KV_RANK, g_ckv[l], "rms_kv")
        qraw = _mm(cqn, W["w_uq"], name="q_up")
        kv = _mm(ckvn, W["w_ukv"], name="kv_up")
        qf = _rope_q(qraw, tabs, bwd=False, mul=mla_scale, name="rope_q")
        kf, v_b = _rope_k(kv, tabs, proj, "rope_k")
        res = _attn_fwd(qf, 0, kf, 0, v_b, 0, name="mla_fwd", rider=gather_rider(l + 1, 0, half) if more else None, **mla)
        (o_b, lse_b), first = res if more else (res, None)
        mkv = _mm(mem2d, W["w_mem_kv"], name="mem_kv")
        o_c, lse_c = _attn_fwd(proj, C_QM // MEM_HD, mkv, 0, mkv, MEM_HEADS, name="mem_fwd", **memat)
        merged = _merge_fwd(o_a, o_b, o_c, W["w_br"], proj, D, "merge_fwd")
        y = _mm(merged, W["w_out"], name="out_proj")
        h1, h1b, xh1, rstd1 = _ln_fwd(h, y, ln1_g[l], ln1_b[l], alpha, "ln1")
        a, r = _mm(h1b, W["w_ff1"], epi="relu2", name="ff1")
        ff = _mm(a, W["w_ff2"], name="ff2")
        h2, h2b, xh2, rstd2 = _ln_fwd(h1, ff, ln2_g[l], ln2_b[l], alpha, "ln2")
        saved.append(dict(W=W, bpad=bpad, h=hb, proj=proj, cum_r=cum_r, o_a=o_a, lse_a=lse_a, cqn=cqn, ckvn=ckvn, qf=qf, kf=kf,
                          v_b=v_b, o_b=o_b, lse_b=lse_b, mkv=mkv, o_c=o_c, lse_c=lse_c, merged=merged, h1=h1b, xh1=xh1,
                          rstd1=rstd1, a=a, r=r, xh2=xh2, rstd2=rstd2))
        h, hb = h2, h2b

    loss_local, dh = _loss(h, tgt2d, "loss")
    loss = lax.psum(loss_local[0, 0], ("x", "y", "c"))

    shard_shapes = [_as3(shard3[n][0]).shape for n, _ in _BIG]
    slots = [lax.empty((N_CHIP, L) + s, bf16) for s in shard_shapes]
    core = lax.axis_index("c").astype(jnp.int32).reshape(1)
    small = {n: [None] * L for n in ("b_forget", "g_cq", "g_ckv", "ln1_g", "ln1_b", "ln2_g", "ln2_b")}
    ax_a, ax_c = big_axes[:half], big_axes[half:]
    sh_a, sh_c = shard_shapes[:half], shard_shapes[half:]

    def pair_sums(gs, bufs, axes):
        return [_pair_sum(g, b, ax, core, "grads_pair_sum") for g, b, ax in zip(gs, bufs, axes)]

    pending = unpaired = None
    for l in reversed(range(L)):
        sv = saved[l]
        W = sv["W"]
        du2, du2b, dg2, db2 = _ln_bwd(dh, sv["xh2"], sv["rstd2"], ln2_g[l], "ln2_bwd")
        dz = _mm(du2b, W["w_ff2"], mode="nt", epi="mul", extra=sv["r"], out_dtype=bf16, name="ff2_dx")
        dw_ff2 = _mm(sv["a"], du2b, mode="tn", out_dtype=bf16, name="ff2_dw")
        dh1 = _mm(dz, W["w_ff1"], mode="nt", epi="add", extra=du2, alpha=alpha, name="ff1_dx")
        dw_ff1 = _mm(sv["h1"], dz, mode="tn", out_dtype=bf16, name="ff1_dw")
        du1, du1b, dg1, db1 = _ln_bwd(dh1, sv["xh1"], sv["rstd1"], ln1_g[l], "ln1_bwd")
        dmerged = _mm(du1b, W["w_out"], mode="nt", name="out_dx")
        dw_out = _mm(sv["merged"], du1b, mode="tn", out_dtype=bf16, name="out_dw")
        mb = _merge_bwd(dmerged, sv["o_a"], sv["o_b"], sv["o_c"], W["w_br"], sv["proj"], D, "merge_bwd",
                        rider=None if unpaired is None else _pair_exchange_rider(unpaired[0], ax_a, sh_a))
        if unpaired is not None:
            mb, bufs = mb
            pending = (pair_sums(unpaired[0], bufs, ax_a), unpaired[1])
        dgl, do, dbp = mb[0:3], mb[3:6], mb[6:9]
        dw_br = jnp.stack([_mm(o_n, dbp_n, mode="tn", out_dtype=bf16, name="br_dw") for o_n, dbp_n in zip((sv["o_a"], sv["o_b"], sv["o_c"]), dbp)])
        late = [_as3(g) for g in (dw_br, dw_out, dw_ff1, dw_ff2)]
        proj = sv["proj"]
        chip_rider = None if pending is None else _chip_exchange_rider(pending[0], slots[:half], pending[1])
        own, got = _attn_bwd(proj, C_Q // LANES, proj, C_K // LANES, proj, C_V // LANES, sv["o_a"], do[0], sv["lse_a"],
                             cum_r=sv["cum_r"], name="fox_bwd",
                             rider=_join_riders([chip_rider, _pair_exchange_rider(late, ax_c, sh_c)]), **fox)
        dq_a, dk_a, dv_a, dcum = own
        if chip_rider is not None:
            slots[:half], got = got[:half], got[half:]
        dcum = dcum.reshape(B, nb, S, 2).transpose(0, 2, 1, 3).reshape(T, FOX_HEADS)
        dcum = jnp.pad(dcum, ((0, 0), (0, LANES - FOX_HEADS)))
        dzf, dbf = _gate_bwd(dcum, proj, sv["bpad"], B, S, "gate_bwd")
        (dqf, dkf, dv_b), slots[half:] = _attn_bwd(
            sv["qf"], 0, sv["kf"], 0, sv["v_b"], 0, sv["o_b"], do[1], sv["lse_b"], dq_dtype=f32, name="mla_bwd",
            rider=_chip_exchange_rider(pair_sums(late, got, ax_c), slots[half:], l), **mla)
        dqraw = _rope_q(dqf, tabs, bwd=True, mul=mla_scale, name="rope_q_bwd")
        dkv, dkpe = _rope_k_bwd(dkf, dv_b, tabs, "rope_k_bwd")
        dw_uq = _mm(sv["cqn"], dqraw, mode="tn", out_dtype=bf16, name="q_up_dw")
        dcqn = _mm(dqraw, W["w_uq"], mode="nt", name="q_up_dx")
        dcq_, dg_cq = _rms_bwd(dcqn, proj, C_CQ, Q_RANK, g_cq[l], "rms_q_bwd")
        dw_ukv = _mm(sv["ckvn"], dkv, mode="tn", out_dtype=bf16, name="kv_up_dw")
        dckvn = _mm(dkv, W["w_ukv"], mode="nt", name="kv_up_dx")
        dckv, dg_ckv = _rms_bwd(dckvn, proj, C_CKV, KV_RANK, g_ckv[l], "rms_kv_bwd")
        dqm, dmk, dmv = _attn_bwd(proj, C_QM // MEM_HD, sv["mkv"], 0, sv["mkv"], MEM_HEADS, sv["o_c"], do[2], sv["lse_c"],
                                  name="mem_bwd", **memat)
        dw_mem = _mm(mem2d, jnp.concatenate([dmk, dmv], axis=1), mode="tn", out_dtype=bf16, name="mem_kv_dw")
        dproj = jnp.concatenate([dq_a, dk_a, dv_a, dzf, dkpe, dckv, dqm, jnp.zeros((T, C_CQ - C_PAD), bf16), dcq_, *dgl], axis=1)
        dw_in = _lane_permute(_mm(sv["h"], dproj, mode="tn", out_dtype=bf16, name="proj_dw"), bwd_map, bf16, "w_in_grad_reorder")
        dh = _mm(dproj, W["w_in"], mode="nt", epi="add", extra=du1, alpha=alpha, name="proj_dx")
        unpaired = ([_as3(g) for g in (dw_in, dw_uq, dw_ukv, dw_mem)], l)
        for n, val in (("b_forget", dbf[:, :FOX_HEADS]), ("g_cq", dg_cq), ("g_ckv", dg_ckv), ("ln1_g", dg1), ("ln1_b", db1),
                       ("ln2_g", dg2), ("ln2_b", db2)):
            small[n][l] = val.reshape(-1)
    bufs = _run_rider(_pair_exchange_rider(unpaired[0], ax_a, sh_a), "grads_pair_exchange")
    slots[:half] = _run_rider(_chip_exchange_rider(pair_sums(unpaired[0], bufs, ax_a), slots[:half], unpaired[1]),
                              "grads_chip_exchange")
    grad_x, _, dg_in, db_in = _ln_bwd(dh, xh0, rstd0, ln_in_g, "ln_in_bwd")

    small_list = [("ln_in_g", dg_in.reshape(-1)), ("ln_in_b", db_in.reshape(-1))] + [(n, jnp.stack(v).reshape(-1)) for n, v in small.items()]
    sflat = jnp.concatenate([v for _, v in small_list])
    n_small = sflat.shape[0]
    rs = _round_up(-(-n_small // LANES), 8)
    spacked = jnp.pad(sflat, (0, rs * LANES - n_small)).reshape(1, rs, LANES)
    sparts = _run_rider(_gather_rider([spacked], [1]), "small_all_gather")[0].reshape(N_DEV, rs * LANES)

    grads, deltas, new_m, new_v = {}, {}, {}, {}
    off = 0
    for n, vv in small_list:
        parts = sparts[:, off:off + vv.shape[0]].reshape((N_DEV,) + weights[n].shape)
        off += vv.shape[0]
        grads[n], deltas[n], new_m[n], new_v[n] = _adamw(weights[n], mom_m[n], mom_v[n], parts, "adamw_" + n)
    for (n, _), sl in zip(_BIG, slots):
        parts = sl.reshape((N_CHIP,) + weights[n].shape[:-1] + (sl.shape[-1],))
        grads[n], deltas[n], new_m[n], new_v[n] = _adamw(weights[n], mom_m[n], mom_v[n], parts, "adamw_" + n)
    return (loss, grad_x.reshape(B, S, D), *[grads[n] for n in order], *[deltas[n] for n in order], *[new_m[n] for n in order],
            *[new_v[n] for n in order])
```

```python
import functools

import numpy as np
import jax
import jax.numpy as jnp
from jax import lax
from jax.experimental import pallas as pl
from jax.experimental.pallas import tpu as pltpu

f32, bf16 = jnp.float32, jnp.bfloat16
SDS = jax.ShapeDtypeStruct

N_DEV = 8
MESH = pl.DeviceIdType.MESH
LANES = 128
VMEM_LIMIT = 56 * 1024 * 1024

FOX_HEADS, FOX_HD = 8, 64
MLA_HEADS, NOPE, ROPE, MLA_VD = 8, 64, 32, 64
Q_RANK, KV_RANK = 384, 256
ROPE_BASE = 10000.0
MEM_HEADS, MEM_HD = 4, 128
BW = 512
CHUNK = 64
LN_EPS, RMS_EPS, NEG_INF = 1e-5, 1e-6, -1e30
ADAM_LR, ADAM_B1, ADAM_B2, ADAM_EPS, ADAM_WD, ADAM_STEP = 0.001, 0.9, 0.999, 1e-08, 0.01, 10

ATT_TQ, ATT_TK = 512, 512

C_Q, C_K, C_V, C_F, C_KPE, C_CKV, C_QM, C_PAD, C_CQ, C_GATE = 0, 512, 1024, 1536, 1664, 1792, 2048, 2560, 2688, 3072
O_F, O_CQ, O_CKV, O_KR, O_QM, O_GATE = 1536, 1544, 1928, 2184, 2216, 2728


def _pick(dim, pref):
    if dim <= pref:
        return dim
    for c in (2048, 1024, 768, 512, 384, 256, 128, 64, 32, 16, 8):
        if c <= pref and dim % c == 0:
            return c
    return dim


def _round_up(n, m):
    return -(-n // m) * m


def _params(sem=None):
    return pltpu.CompilerParams(dimension_semantics=sem, vmem_limit_bytes=VMEM_LIMIT)


_DIMS = {"nn": (((1,), (0,)), ((), ())), "nt": (((1,), (1,)), ((), ())), "tn": (((0,), (0,)), ((), ()))}


def _mm(a, b, *, mode="nn", out_dtype=f32, epi="none", extra=None, alpha=1.0, tm=1024, tn=1024, tk=None, name):
    if mode == "nn":
        (M, K), (_, N) = a.shape, b.shape
    elif mode == "nt":
        (M, K), (N, _) = a.shape, b.shape
    else:
        (K, M), (_, N) = a.shape, b.shape
    tm, tn, tk = _pick(M, tm), _pick(N, tn), _pick(K, tk or 2048)
    nk = K // tk
    a_spec = pl.BlockSpec((tk, tm), lambda i, j, k: (k, i)) if mode == "tn" else pl.BlockSpec((tm, tk), lambda i, j, k: (i, k))
    b_spec = pl.BlockSpec((tn, tk), lambda i, j, k: (j, k)) if mode == "nt" else pl.BlockSpec((tk, tn), lambda i, j, k: (k, j))
    o_spec = pl.BlockSpec((tm, tn), lambda i, j, k: (i, j))
    n_out = 2 if epi == "relu2" else 1
    dims = _DIMS[mode]

    def body(*refs):
        a_ref, b_ref = refs[0], refs[1]
        e_ref = refs[2] if extra is not None else None
        outs = refs[2 + (extra is not None):2 + (extra is not None) + n_out]

        def finish(r):
            if epi == "none":
                outs[0][...] = r.astype(out_dtype)
            elif epi == "relu2":
                p = jnp.maximum(r, 0.0)
                outs[0][...] = (p * p).astype(bf16)
                outs[1][...] = (2.0 * p).astype(bf16)
            elif epi == "mul":
                outs[0][...] = (r * e_ref[...].astype(f32)).astype(out_dtype)
            else:
                outs[0][...] = (r + alpha * e_ref[...].astype(f32)).astype(out_dtype)

        part = lax.dot_general(a_ref[...].astype(bf16), b_ref[...].astype(bf16), dims, preferred_element_type=f32)
        if nk == 1:
            finish(part)
            return
        acc = refs[-1]
        k = pl.program_id(2)

        @pl.when(k == 0)
        def _():
            acc[...] = part

        @pl.when(k > 0)
        def _():
            acc[...] += part

        @pl.when(k == nk - 1)
        def _():
            finish(acc[...])

    ins, in_specs = [a, b], [a_spec, b_spec]
    if extra is not None:
        ins.append(extra)
        in_specs.append(o_spec)
    if epi == "relu2":
        out_shape, out_specs = [SDS((M, N), bf16), SDS((M, N), bf16)], [o_spec, o_spec]
    else:
        out_shape, out_specs = SDS((M, N), out_dtype), o_spec
    return pl.pallas_call(
        body, grid=(M // tm, N // tn, nk), in_specs=in_specs, out_specs=out_specs, out_shape=out_shape,
        scratch_shapes=[pltpu.VMEM((tm, tn), f32)] if nk > 1 else [], name=name,
        compiler_params=_params(("parallel", "parallel", "arbitrary")))(*ins)


def _lane_permute(src, cmap, out_dtype, name):
    R, Cs = src.shape
    cmap = np.asarray(cmap, np.int64)
    nb = cmap.shape[0] // LANES
    srcs = [sorted({int(c) // LANES for c in cmap[b * LANES:(b + 1) * LANES] if c >= 0}) for b in range(nb)]
    exact = src.dtype == f32
    tr = _pick(R, 256 if exact else 512)

    def body(src_ref, cm_ref, o_ref):
        rows = lax.broadcasted_iota(jnp.int32, (LANES, LANES), 0)
        for b in range(nb):
            tbl = cm_ref[:, b * LANES:(b + 1) * LANES]
            acc = jnp.zeros((tr, LANES), f32)
            for sb in srcs[b]:
                sel = (rows + sb * LANES) == tbl
                blk = src_ref[:, sb * LANES:(sb + 1) * LANES]
                if exact:
                    acc = acc + jnp.dot(blk, sel.astype(f32), precision=lax.Precision.HIGHEST, preferred_element_type=f32)
                else:
                    acc = acc + jnp.dot(blk, sel.astype(blk.dtype), preferred_element_type=f32)
            o_ref[:, b * LANES:(b + 1) * LANES] = acc.astype(out_dtype)

    return pl.pallas_call(
        body, grid=(R // tr,), in_specs=[pl.BlockSpec((tr, Cs), lambda i: (i, 0)), pl.BlockSpec((1, nb * LANES), lambda i: (0, 0))],
        out_specs=pl.BlockSpec((tr, nb * LANES), lambda i: (i, 0)), out_shape=SDS((R, nb * LANES), out_dtype), name=name,
        compiler_params=_params(("parallel",)))(src, jnp.asarray(cmap.astype(np.int32)).reshape(1, -1))


def _ln_fwd(h, y, g, b, alpha, name):
    T, D = y.shape
    tr = _pick(T, 256)
    row = pl.BlockSpec((tr, D), lambda i: (i, 0))
    vec = pl.BlockSpec((1, D), lambda i: (0, 0))
    col = pl.BlockSpec((tr, 1), lambda i: (i, 0))

    def body(*refs):
        if h is None:
            y_ref, g_ref, b_ref, o_ref, ob_ref, xh_ref, rs_ref = refs
            u = y_ref[...]
        else:
            h_ref, y_ref, g_ref, b_ref, o_ref, ob_ref, xh_ref, rs_ref = refs
            u = alpha * h_ref[...] + y_ref[...]
        mu = jnp.mean(u, axis=-1, keepdims=True)
        d = u - mu
        var = jnp.mean(d * d, axis=-1, keepdims=True)
        rstd = lax.rsqrt(var + LN_EPS)
        xh = d * rstd
        xh_ref[...] = xh
        rs_ref[...] = rstd
        o = xh * g_ref[...] + b_ref[...]
        o_ref[...] = o
        ob_ref[...] = o.astype(bf16)

    ins = ([] if h is None else [h]) + [y, g.reshape(1, D), b.reshape(1, D)]
    specs = ([] if h is None else [row]) + [row, vec, vec]
    return pl.pallas_call(
        body, grid=(T // tr,), in_specs=specs, out_specs=[row, row, row, col],
        out_shape=[SDS((T, D), f32), SDS((T, D), bf16), SDS((T, D), f32), SDS((T, 1), f32)], name=name,
        compiler_params=_params(("parallel",)))(*ins)


def _ln_bwd(dy, xh, rstd, g, name):
    T, D = dy.shape
    tr = _pick(T, 256)
    row = pl.BlockSpec((tr, D), lambda i: (i, 0))
    vec = pl.BlockSpec((1, D), lambda i: (0, 0))
    col = pl.BlockSpec((tr, 1), lambda i: (i, 0))

    def body(dy_ref, xh_ref, rs_ref, g_ref, du_ref, dub_ref, dg_ref, db_ref):
        @pl.when(pl.program_id(0) == 0)
        def _():
            dg_ref[...] = jnp.zeros_like(dg_ref)
            db_ref[...] = jnp.zeros_like(db_ref)

        dyv, xhv = dy_ref[...], xh_ref[...]
        dxh = dyv * g_ref[...]
        m1 = jnp.mean(dxh, axis=-1, keepdims=True)
        m2 = jnp.mean(dxh * xhv, axis=-1, keepdims=True)
        du = rs_ref[...] * (dxh - m1 - xhv * m2)
        du_ref[...] = du
        dub_ref[...] = du.astype(bf16)
        dg_ref[...] += jnp.sum(dyv * xhv, axis=0, keepdims=True)
        db_ref[...] += jnp.sum(dyv, axis=0, keepdims=True)

    return pl.pallas_call(
        body, grid=(T // tr,), in_specs=[row, row, col, vec], out_specs=[row, row, vec, vec],
        out_shape=[SDS((T, D), f32), SDS((T, D), bf16), SDS((1, D), f32), SDS((1, D), f32)], name=name,
        compiler_params=_params(("arbitrary",)))(dy, xh, rstd, g.reshape(1, D))


def _rms_fwd(proj, col_off, R, g, name):
    T = proj.shape[0]
    tr = _pick(T, 512)
    cb = col_off // R

    def body(c_ref, g_ref, o_ref):
        c = c_ref[...]
        r = lax.rsqrt(jnp.mean(c * c, axis=-1, keepdims=True) + RMS_EPS)
        o_ref[...] = (c * r * g_ref[...]).astype(bf16)

    return pl.pallas_call(
        body, grid=(T // tr,), in_specs=[pl.BlockSpec((tr, R), lambda i: (i, cb)), pl.BlockSpec((1, R), lambda i: (0, 0))],
        out_specs=pl.BlockSpec((tr, R), lambda i: (i, 0)), out_shape=SDS((T, R), bf16), name=name,
        compiler_params=_params(("parallel",)))(proj, g.reshape(1, R))


def _rms_bwd(dy, proj, col_off, R, g, name):
    T = proj.shape[0]
    tr = _pick(T, 512)
    cb = col_off // R

    def body(dy_ref, c_ref, g_ref, dc_ref, dg_ref):
        @pl.when(pl.program_id(0) == 0)
        def _():
            dg_ref[...] = jnp.zeros_like(dg_ref)

        c, dyv = c_ref[...], dy_ref[...]
        r = lax.rsqrt(jnp.mean(c * c, axis=-1, keepdims=True) + RMS_EPS)
        t = dyv * g_ref[...]
        mt = jnp.mean(t * c, axis=-1, keepdims=True)
        dc_ref[...] = (r * t - c * (r * r * r) * mt).astype(bf16)
        dg_ref[...] += jnp.sum(dyv * c * r, axis=0, keepdims=True)

    return pl.pallas_call(
        body, grid=(T // tr,),
        in_specs=[pl.BlockSpec((tr, R), lambda i: (i, 0)), pl.BlockSpec((tr, R), lambda i: (i, cb)), pl.BlockSpec((1, R), lambda i: (0, 0))],
        out_specs=[pl.BlockSpec((tr, R), lambda i: (i, 0)), pl.BlockSpec((1, R), lambda i: (0, 0))],
        out_shape=[SDS((T, R), bf16), SDS((1, R), f32)], name=name,
        compiler_params=_params(("arbitrary",)))(dy, proj, g.reshape(1, R))


def _tri(n, lower):
    r = lax.broadcasted_iota(jnp.int32, (n, n), 0)
    c = lax.broadcasted_iota(jnp.int32, (n, n), 1)
    return ((r >= c) if lower else (c >= r)).astype(f32)


def _gate_fwd(proj, bpad, B, S, name):
    ch = _pick(S, 256)
    nch = S // ch
    cb = C_F // LANES

    def body(f_ref, b_ref, o_ref):
        tri = _tri(ch, True)
        carry = jnp.zeros((1, LANES), f32)
        for c in range(nch):
            z = f_ref[c * ch:(c + 1) * ch, :] + b_ref[...]
            lf = jnp.minimum(z, 0.0) - jnp.log1p(jnp.exp(-jnp.abs(z)))
            cs = jnp.dot(tri, lf, precision=lax.Precision.HIGHEST, preferred_element_type=f32) + carry
            o_ref[c * ch:(c + 1) * ch, :] = cs
            carry = cs[ch - 1:ch, :]

    return pl.pallas_call(
        body, grid=(B,), in_specs=[pl.BlockSpec((S, LANES), lambda b: (b, cb)), pl.BlockSpec((1, LANES), lambda b: (0, 0))],
        out_specs=pl.BlockSpec((S, LANES), lambda b: (b, 0)), out_shape=SDS((B * S, LANES), f32), name=name,
        compiler_params=_params(("parallel",)))(proj, bpad)


def _gate_bwd(dcum, proj, bpad, B, S, name):
    ch = _pick(S, 256)
    nch = S // ch
    cb = C_F // LANES

    def body(d_ref, f_ref, b_ref, dz_ref, db_ref):
        @pl.when(pl.program_id(0) == 0)
        def _():
            db_ref[...] = jnp.zeros_like(db_ref)

        tri = _tri(ch, False)
        carry = jnp.zeros((1, LANES), f32)
        dbs = jnp.zeros((1, LANES), f32)
        for c in reversed(range(nch)):
            d = d_ref[c * ch:(c + 1) * ch, :]
            dlf = jnp.dot(tri, d, precision=lax.Precision.HIGHEST, preferred_element_type=f32) + carry
            carry = carry + jnp.sum(d, axis=0, keepdims=True)
            z = f_ref[c * ch:(c + 1) * ch, :] + b_ref[...]
            dz = dlf / (1.0 + jnp.exp(z))
            dz_ref[c * ch:(c + 1) * ch, :] = dz.astype(bf16)
            dbs = dbs + jnp.sum(dz, axis=0, keepdims=True)
        db_ref[...] += dbs

    return pl.pallas_call(
        body, grid=(B,),
        in_specs=[pl.BlockSpec((S, LANES), lambda b: (b, 0)), pl.BlockSpec((S, LANES), lambda b: (b, cb)), pl.BlockSpec((1, LANES), lambda b: (0, 0))],
        out_specs=[pl.BlockSpec((S, LANES), lambda b: (b, 0)), pl.BlockSpec((1, LANES), lambda b: (0, 0))],
        out_shape=[SDS((B * S, LANES), bf16), SDS((1, LANES), f32)], name=name,
        compiler_params=_params(("arbitrary",)))(dcum, proj, bpad)


def _rot(v, c, sa, sb):
    return v * c + pltpu.roll(v, LANES - 16, 1) * sa + pltpu.roll(v, 16, 1) * sb


def _rot_t(v, c, sa, sb):
    return v * c + pltpu.roll(v * sa, 16, 1) + pltpu.roll(v * sb, LANES - 16, 1)


def _rope_q(x, tabs, *, bwd, mul, name):
    T = x.shape[0]
    W = MLA_HEADS * LANES
    tr = _pick(T, 512)
    fn = _rot_t if bwd else _rot

    def body(x_ref, c_ref, sa_ref, sb_ref, o_ref):
        c, sa, sb = c_ref[...], sa_ref[...], sb_ref[...]
        for g in range(MLA_HEADS):
            o_ref[:, g * LANES:(g + 1) * LANES] = (fn(x_ref[:, g * LANES:(g + 1) * LANES], c, sa, sb) * mul).astype(bf16)

    tab = pl.BlockSpec((tr, LANES), lambda i: (i, 0))
    wide = pl.BlockSpec((tr, W), lambda i: (i, 0))
    return pl.pallas_call(body, grid=(T // tr,), in_specs=[wide, tab, tab, tab], out_specs=wide, out_shape=SDS((T, W), bf16),
                          name=name, compiler_params=_params(("parallel",)))(x, *tabs)


def _rope_k(kv, tabs, proj, name):
    T = kv.shape[0]
    W = MLA_HEADS * LANES
    tr = _pick(T, 512)
    kcb = C_KPE // LANES

    def body(x_ref, c_ref, sa_ref, sb_ref, kpe_ref, k_ref, v_ref):
        c, sa, sb, kpe = c_ref[...], sa_ref[...], sb_ref[...], kpe_ref[...]
        low = lax.broadcasted_iota(jnp.int32, (tr, LANES), 1) < NOPE
        halves = []
        for g in range(MLA_HEADS):
            x = x_ref[:, g * LANES:(g + 1) * LANES]
            k_ref[:, g * LANES:(g + 1) * LANES] = _rot(jnp.where(low, x, 0.0) + kpe, c, sa, sb).astype(bf16)
            halves.append(x[:, NOPE:])
            if g % 2 == 1:
                v_ref[:, (g // 2) * LANES:(g // 2 + 1) * LANES] = jnp.concatenate(halves, axis=1).astype(bf16)
                halves = []

    tab = pl.BlockSpec((tr, LANES), lambda i: (i, 0))
    wide = pl.BlockSpec((tr, W), lambda i: (i, 0))
    return pl.pallas_call(
        body, grid=(T // tr,), in_specs=[wide, tab, tab, tab, pl.BlockSpec((tr, LANES), lambda i: (i, kcb))],
        out_specs=[wide, pl.BlockSpec((tr, W // 2), lambda i: (i, 0))], out_shape=[SDS((T, W), bf16), SDS((T, W // 2), bf16)],
        name=name, compiler_params=_params(("parallel",)))(kv, *tabs, proj)


def _rope_k_bwd(dk, dv, tabs, name):
    T = dk.shape[0]
    W = MLA_HEADS * LANES
    tr = _pick(T, 512)

    def body(dk_ref, dv_ref, c_ref, sa_ref, sb_ref, o_ref, s_ref):
        c, sa, sb = c_ref[...], sa_ref[...], sb_ref[...]
        tot = jnp.zeros((tr, LANES), f32)
        for g in range(MLA_HEADS):
            d = _rot_t(dk_ref[:, g * LANES:(g + 1) * LANES], c, sa, sb)
            tot = tot + d
            o_ref[:, g * LANES:(g + 1) * LANES] = jnp.concatenate(
                [d[:, :NOPE].astype(bf16), dv_ref[:, g * MLA_VD:(g + 1) * MLA_VD]], axis=1)
        s_ref[...] = tot.astype(bf16)

    tab = pl.BlockSpec((tr, LANES), lambda i: (i, 0))
    wide = pl.BlockSpec((tr, W), lambda i: (i, 0))
    return pl.pallas_call(
        body, grid=(T // tr,), in_specs=[wide, pl.BlockSpec((tr, W // 2), lambda i: (i, 0)), tab, tab, tab],
        out_specs=[wide, tab], out_shape=[SDS((T, W), bf16), SDS((T, LANES), bf16)], name=name,
        compiler_params=_params(("parallel",)))(dk, dv, *tabs)


def _tile_mask(mode, r0, c0, nr, nc):
    r = r0 + lax.broadcasted_iota(jnp.int32, (nr, nc), 0)
    c = c0 + lax.broadcasted_iota(jnp.int32, (nr, nc), 1)
    if mode == "chunk":
        return (c // CHUNK) <= (r // CHUNK)
    return c <= r


def _diagonal(block, carry, i, tq, tk):
    nfull = (i * tq) // tk
    for d in range(max(1, tq // tk)):
        carry = block((nfull + d) * tk, tk, 0, tq, carry, True)
    return carry


def _attn_tiles(Sq, Sk):
    tk = _pick(Sk, ATT_TK)
    tq = _pick(Sq, ATT_TQ)
    assert (tk % tq == 0 or tq % tk == 0) and tq % CHUNK == 0 and tk % CHUNK == 0
    return tq, tk


def _stage_heads(dst, src_ref, hb, w, mark=None, mul=1.0):
    S = src_ref.shape[0]
    ch = _pick(S, 256)
    assert w <= LANES and (mark is None or w <= mark < LANES)

    def step(c, _):
        rows = pl.ds(pl.multiple_of(c * ch, ch), ch)
        for hh in range(hb):
            x = src_ref[rows, hh * w:(hh + 1) * w]
            x = (x.astype(f32) * mul).astype(bf16) if mul != 1.0 else x.astype(bf16)
            if w < LANES:
                pad = jnp.zeros((ch, LANES - w), bf16)
                if mark is not None:
                    lane = lax.broadcasted_iota(jnp.int32, (ch, LANES - w), 1)
                    pad = jnp.where(lane == mark - w, 1.0, 0.0).astype(bf16)
                x = jnp.concatenate([x, pad], axis=1)
            dst[hh, rows, :] = x
        return 0

    lax.fori_loop(0, S // ch, step, 0)


def _carry(rider, refs, n_in, n_out, n_scr, grid):
    if rider is None:
        return refs
    ri, ro = len(rider.inputs), len(rider.out_shapes)
    own = refs[:n_in] + refs[n_in + ri:n_in + ri + n_out] + refs[n_in + ri + n_out + ro:n_in + ri + n_out + ro + n_scr]
    rider.refs = (refs[n_in:n_in + ri], refs[n_in + ri + n_out:n_in + ri + n_out + ro], refs[n_in + ri + n_out + ro + n_scr:])
    first = functools.reduce(jnp.logical_and, [pl.program_id(a) == 0 for a in range(len(grid))])

    @pl.when(first)
    def _():
        rider.start(*rider.refs)

    return own


def _carry_finish(rider, grid):
    if rider is None:
        return
    last = functools.reduce(jnp.logical_and, [pl.program_id(a) == n - 1 for a, n in enumerate(grid)])

    @pl.when(last)
    def _():
        rider.finish(*rider.refs)


def _carrier_call(body, rider, ins, in_specs, out_shape, out_specs, scratch, grid, name):
    if rider is None:
        return pl.pallas_call(body, grid=grid, in_specs=in_specs, out_specs=out_specs, out_shape=out_shape, scratch_shapes=scratch,
                              name=name, compiler_params=_params(("parallel",) * len(grid)))(*ins)
    any_spec = pl.BlockSpec(memory_space=pl.ANY)
    ri, ro = len(rider.inputs), len(rider.out_shapes)
    res = pl.pallas_call(
        body, grid=grid, in_specs=list(in_specs) + [any_spec] * ri, out_specs=list(out_specs) + [any_spec] * ro,
        out_shape=list(out_shape) + rider.out_shapes, scratch_shapes=list(scratch) + rider.scratch,
        input_output_aliases={len(ins) + i: len(out_shape) + o for i, o in rider.aliases.items()}, name=name,
        compiler_params=_params(("arbitrary",) * len(grid)))(*ins, *rider.inputs)
    return res[:len(out_shape)], res[len(out_shape):]


def _attn_fwd(q_arr, q_cb, k_arr, k_cb, v_arr, v_cb, *, B, Sq, Sk, nblk, hb, dqk, dv, mode, scale, cum_r=None, rider=None, name):
    wq, wv = hb * dqk, hb * dv
    tq, tk = _attn_tiles(Sq, Sk)
    fox = mode == "fox"
    n_in = 4 if fox else 3
    ones = dv < LANES

    def body(*refs):
        refs = _carry(rider, refs, n_in, 2, 3, (B, nblk))
        q_ref, k_ref, v_ref = refs[:3]
        cr_ref = refs[3] if fox else None
        o_ref, lse_ref, qh, kh, vh = refs[n_in:]
        _stage_heads(qh, q_ref, hb, dqk, mul=scale)
        _stage_heads(kh, k_ref, hb, dqk)
        _stage_heads(vh, v_ref, hb, dv, dv if ones else None)

        def qstep(i, _):
            rows = pl.ds(pl.multiple_of(i * tq, tq), tq)
            qs = [qh[hh, rows, :] for hh in range(hb)]

            def block(c0, w, r0, nr, carry, diag):
                cols = pl.ds(pl.multiple_of(c0, w), w)
                out = []
                for hh in range(hb):
                    m, l, acc = carry[hh]
                    s = lax.dot_general(qs[hh][r0:r0 + nr], kh[hh, cols, :], _DIMS["nt"], preferred_element_type=f32)
                    if fox:
                        s = s - cr_ref[0, hh:hh + 1, cols]
                    if diag:
                        s = jnp.where(_tile_mask(mode, i * tq + r0, c0, nr, w), s, NEG_INF)
                    m_new = jnp.maximum(m, jnp.max(s, axis=-1, keepdims=True))
                    a = jnp.exp(m - m_new)
                    p = jnp.exp(s - m_new)
                    if not ones:
                        l = a * l + jnp.sum(p, axis=-1, keepdims=True)
                    acc = a * acc + lax.dot_general(p.astype(bf16), vh[hh, cols, :], _DIMS["nn"], preferred_element_type=f32)
                    out.append((m_new, l, acc))
                return tuple(out)

            def tile(j, carry):
                return block(j * tk, tk, 0, tq, carry, False)

            carry = tuple((jnp.full((tq, 1), -jnp.inf, f32), jnp.zeros((tq, 1), f32), jnp.zeros((tq, LANES), f32)) for _ in range(hb))
            if mode == "none":
                carry = lax.fori_loop(0, Sk // tk, tile, carry)
            else:
                nfull = (i * tq) // tk
                carry = lax.fori_loop(0, nfull, tile, carry)
                carry = _diagonal(block, carry, i, tq, tk)
            for hh in range(hb):
                m, l, acc = carry[hh]
                if ones:
                    l = acc[:, dv:dv + 1]
                o_ref[rows, hh * dv:(hh + 1) * dv] = acc[:, :dv] / l
                lse_ref[0, rows, hh:hh + 1] = m + jnp.log(l)
            return 0

        lax.fori_loop(0, Sq // tq, qstep, 0)
        _carry_finish(rider, (B, nblk))

    in_specs = [pl.BlockSpec((Sq, wq), lambda b, p: (b, q_cb + p)), pl.BlockSpec((Sk, wq), lambda b, p: (b, k_cb + p)),
                pl.BlockSpec((Sk, wv), lambda b, p: (b, v_cb + p))]
    ins = [q_arr, k_arr, v_arr]
    if fox:
        in_specs.append(pl.BlockSpec((1, hb, Sk), lambda b, p: (b * nblk + p, 0, 0)))
        ins.append(cum_r)
    return _carrier_call(
        body, rider, ins, in_specs,
        [SDS((B * Sq, nblk * wv), f32), SDS((B * nblk, Sq, hb), f32)],
        [pl.BlockSpec((Sq, wv), lambda b, p: (b, p)), pl.BlockSpec((1, Sq, hb), lambda b, p: (b * nblk + p, 0, 0))],
        [pltpu.VMEM((hb, Sq, LANES), bf16), pltpu.VMEM((hb, Sk, LANES), bf16), pltpu.VMEM((hb, Sk, LANES), bf16)],
        (B, nblk), name)


def _attn_bwd(q_arr, q_cb, k_arr, k_cb, v_arr, v_cb, o, do, lse, *, B, Sq, Sk, nblk, hb, dqk, dv, mode, scale, cum_r=None,
              dq_dtype=bf16, rider=None, name):
    wq, wv = hb * dqk, hb * dv
    tq, tk = _attn_tiles(Sq, Sk)
    fox = mode == "fox"
    n_in = 7 if fox else 6
    n_out = 4 if fox else 3
    if fox:
        assert dqk + 2 <= LANES

    def body(*refs):
        refs = _carry(rider, refs, n_in, n_out, 7 if fox else 6, (B, nblk))
        q_ref, k_ref, v_ref, o_ref, do_ref, lse_ref = refs[:6]
        cr_ref = refs[6] if fox else None
        dq_ref, dk_ref, dv_ref = refs[n_in:n_in + 3]
        dcum_ref = refs[n_in + 3] if fox else None
        qh, kh, vh, doh, dk_acc, dv_acc = refs[n_in + n_out:n_in + n_out + 6]
        dcq_acc = refs[n_in + n_out + 6] if fox else None
        _stage_heads(qh, q_ref, hb, dqk, dqk + 1 if fox else None, mul=scale)
        _stage_heads(kh, k_ref, hb, dqk, dqk if fox else None)
        _stage_heads(vh, v_ref, hb, dv)
        _stage_heads(doh, do_ref, hb, dv)
        dk_acc[...] = jnp.zeros_like(dk_acc)
        dv_acc[...] = jnp.zeros_like(dv_acc)

        def qstep(i, _):
            rows = pl.ds(pl.multiple_of(i * tq, tq), tq)
            qs = [qh[hh, rows, :] for hh in range(hb)]
            dos = [doh[hh, rows, :] for hh in range(hb)]
            dds = [jnp.sum(do_ref[rows, hh * dv:(hh + 1) * dv] * o_ref[rows, hh * dv:(hh + 1) * dv], axis=-1, keepdims=True)
                   for hh in range(hb)]
            lses = [lse_ref[0, rows, hh:hh + 1] for hh in range(hb)]

            def block(c0, w, r0, nr, carry, diag):
                cols = pl.ds(pl.multiple_of(c0, w), w)
                rs = slice(r0, r0 + nr)
                out = []
                for hh in range(hb):
                    k = kh[hh, cols, :]
                    q, do_ = qs[hh][rs], dos[hh][rs]
                    s = lax.dot_general(q, k, _DIMS["nt"], preferred_element_type=f32)
                    if fox:
                        s = s - cr_ref[0, hh:hh + 1, cols]
                    if diag:
                        s = jnp.where(_tile_mask(mode, i * tq + r0, c0, nr, w), s, NEG_INF)
                    p = jnp.exp(s - lses[hh][rs])
                    dv_acc[hh, cols, :] += lax.dot_general(p.astype(bf16), do_, _DIMS["tn"], preferred_element_type=f32)
                    dp = lax.dot_general(do_, vh[hh, cols, :], _DIMS["nt"], preferred_element_type=f32)
                    ds = p * (dp - dds[hh][rs])
                    dsb = ds.astype(bf16)
                    out.append(carry[hh] + lax.dot_general(dsb, k, _DIMS["nn"], preferred_element_type=f32))
                    dk_acc[hh, cols, :] += lax.dot_general(dsb, q, _DIMS["tn"], preferred_element_type=f32)
                return tuple(out)

            def tile(j, carry):
                return block(j * tk, tk, 0, tq, carry, False)

            carry = tuple(jnp.zeros((tq, LANES), f32) for _ in range(hb))
            if mode == "none":
                carry = lax.fori_loop(0, Sk // tk, tile, carry)
            else:
                nfull = (i * tq) // tk
                carry = lax.fori_loop(0, nfull, tile, carry)
                carry = _diagonal(block, carry, i, tq, tk)
            for hh in range(hb):
                dq_ref[rows, hh * dqk:(hh + 1) * dqk] = (carry[hh][:, :dqk] * scale).astype(dq_dtype)
                if fox:
                    dcq_acc[rows, hh:hh + 1] = carry[hh][:, dqk:dqk + 1]
            return 0

        lax.fori_loop(0, Sq // tq, qstep, 0)
        for hh in range(hb):
            dk_ref[:, hh * dqk:(hh + 1) * dqk] = dk_acc[hh, :, :dqk].astype(dq_dtype)
            dv_ref[:, hh * dv:(hh + 1) * dv] = dv_acc[hh, :, :dv].astype(bf16)
            if fox:
                dcum_ref[0, :, hh:hh + 1] = dcq_acc[:, hh:hh + 1] - dk_acc[hh, :, dqk + 1:dqk + 2]
        _carry_finish(rider, (B, nblk))

    qs_ =pl.BlockSpec((Sq, wq), lambda b, p: (b, q_cb + p))
    ks_ = pl.BlockSpec((Sk, wq), lambda b, p: (b, k_cb + p))
    vs_ = pl.BlockSpec((Sk, wv), lambda b, p: (b, v_cb + p))
    os_ = pl.BlockSpec((Sq, wv), lambda b, p: (b, p))
    st = pl.BlockSpec((1, Sq, hb), lambda b, p: (b * nblk + p, 0, 0))
    rw = pl.BlockSpec((1, hb, Sk), lambda b, p: (b * nblk + p, 0, 0))
    in_specs = [qs_, ks_, vs_, os_, os_, st] + ([rw] if fox else [])
    ins = [q_arr, k_arr, v_arr, o, do, lse] + ([cum_r] if fox else [])
    out_specs = [pl.BlockSpec((Sq, wq), lambda b, p: (b, p)), pl.BlockSpec((Sk, wq), lambda b, p: (b, p)),
                 pl.BlockSpec((Sk, wv), lambda b, p: (b, p))] + ([st] if fox else [])
    out_shape = [SDS((B * Sq, nblk * wq), dq_dtype), SDS((B * Sk, nblk * wq), dq_dtype), SDS((B * Sk, nblk * wv), bf16)]
    scratch = [pltpu.VMEM((hb, Sq, LANES), bf16), pltpu.VMEM((hb, Sk, LANES), bf16), pltpu.VMEM((hb, Sk, LANES), bf16),
               pltpu.VMEM((hb, Sq, LANES), bf16), pltpu.VMEM((hb, Sk, LANES), f32), pltpu.VMEM((hb, Sk, LANES), f32)]
    if fox:
        assert Sq == Sk
        out_shape.append(SDS((B * nblk, Sq, hb), f32))
        scratch.append(pltpu.VMEM((Sq, hb), f32))
    return _carrier_call(body, rider, ins, in_specs, out_shape, out_specs, scratch, (B, nblk), name)


def _merge_fwd(oa, ob, oc, w_br, gates, D, name):
    T = oa.shape[0]
    tr = _pick(T, 256)
    o_spec = pl.BlockSpec((tr, BW), lambda i: (i, 0))
    w_spec = pl.BlockSpec((3, BW, D), lambda i: (0, 0, 0))

    def body(oa_ref, ob_ref, oc_ref, w_ref, g_ref, m_ref):
        tot = jnp.zeros((tr, D), f32)
        for n, o_ref in enumerate((oa_ref, ob_ref, oc_ref)):
            bp = lax.dot_general(o_ref[...].astype(bf16), w_ref[n], _DIMS["nn"], preferred_element_type=f32)
            tot = tot + jax.nn.sigmoid(g_ref[:, n * D:(n + 1) * D].astype(f32)) * bp
        m_ref[...] = tot.astype(bf16)

    return pl.pallas_call(
        body, grid=(T // tr,), in_specs=[o_spec, o_spec, o_spec, w_spec, pl.BlockSpec((tr, 3 * D), lambda i: (i, 0))],
        out_specs=pl.BlockSpec((tr, D), lambda i: (i, 0)), out_shape=SDS((T, D), bf16), name=name,
        compiler_params=_params(("parallel",)))(oa, ob, oc, w_br, gates)


def _merge_bwd(dm, oa, ob, oc, w_br, gates, D, name, rider=None):
    T = oa.shape[0]
    tr = _pick(T, 256)
    o_spec = pl.BlockSpec((tr, BW), lambda i: (i, 0))
    d_spec = pl.BlockSpec((tr, D), lambda i: (i, 0))
    g_spec = pl.BlockSpec((tr, 3 * D), lambda i: (i, 0))
    w_spec = pl.BlockSpec((3, BW, D), lambda i: (0, 0, 0))

    def body(*refs):
        refs = _carry(rider, refs, 6, 7, 0, (T // tr,))
        dm_ref, oa_ref, ob_ref, oc_ref, w_ref, g_ref = refs[:6]
        dg_ref, outs = refs[6], refs[7:]
        dmv = dm_ref[...]
        for n, o_ref in enumerate((oa_ref, ob_ref, oc_ref)):
            bp = lax.dot_general(o_ref[...].astype(bf16), w_ref[n], _DIMS["nn"], preferred_element_type=f32)
            g = jax.nn.sigmoid(g_ref[:, n * D:(n + 1) * D].astype(f32))
            dbp = (dmv * g).astype(bf16)
            dg_ref[:, n * D:(n + 1) * D] = (dmv * bp * g * (1.0 - g)).astype(bf16)
            outs[n][...] = lax.dot_general(dbp, w_ref[n], _DIMS["nt"], preferred_element_type=f32)
            outs[3 + n][...] = dbp
        _carry_finish(rider, (T // tr,))

    return _carrier_call(
        body, rider, [dm, oa, ob, oc, w_br, gates], [d_spec, o_spec, o_spec, o_spec, w_spec, g_spec],
        [SDS((T, 3 * D), bf16)] + [SDS((T, BW), f32)] * 3 + [SDS((T, D), bf16)] * 3,
        [g_spec] + [o_spec] * 3 + [d_spec] * 3, [], (T // tr,), name)


def _loss(h, target, name):
    T, D = h.shape
    tr = _pick(T, 256)
    nt = T // tr
    row = pl.BlockSpec((tr, D), lambda i: (i, 0))

    def body(h_ref, t_ref, l_ref, d_ref, acc):
        i = pl.program_id(0)

        @pl.when(i == 0)
        def _():
            acc[...] = jnp.zeros_like(acc)

        e = h_ref[...] - t_ref[...]
        d_ref[...] = e * (1.0 / D)
        acc[...] += jnp.sum(e * e, axis=0, keepdims=True)

        @pl.when(i == nt - 1)
        def _():
            l_ref[...] = jnp.sum(acc[...], axis=-1, keepdims=True) * (0.5 / D)

    return pl.pallas_call(
        body, grid=(nt,), in_specs=[row, row], out_specs=[pl.BlockSpec((1, 1), lambda i: (0, 0)), row],
        out_shape=[SDS((1, 1), f32), SDS((T, D), f32)], scratch_shapes=[pltpu.VMEM((1, D), f32)], name=name,
        compiler_params=_params(("arbitrary",)))(h, target)


def _adamw(w, m, v, parts, name):
    shape = w.shape
    C = shape[-1]
    R = int(np.prod(shape[:-1])) if len(shape) > 1 else 1
    n_parts, Cp = parts.shape[0], parts.shape[-1]
    tr = _pick(R, 256)
    spec = pl.BlockSpec((tr, C), lambda i: (i, 0))

    def body(w_ref, m_ref, v_ref, p_ref, g_ref, d_ref, mo_ref, vo_ref):
        gv = p_ref[0].astype(f32)
        for q in range(1, n_parts):
            gv = gv + p_ref[q].astype(f32)
        gv = gv[:, :C]
        mn = ADAM_B1 * m_ref[...] + (1.0 - ADAM_B1) * gv
        vn = ADAM_B2 * v_ref[...] + (1.0 - ADAM_B2) * (gv * gv)
        m_hat = mn / (1.0 - ADAM_B1 ** ADAM_STEP)
        v_hat = vn / (1.0 - ADAM_B2 ** ADAM_STEP)
        g_ref[...] = gv
        d_ref[...] = -ADAM_LR * (m_hat / (jnp.sqrt(v_hat) + ADAM_EPS) + ADAM_WD * w_ref[...])
        mo_ref[...] = mn
        vo_ref[...] = vn

    outs = pl.pallas_call(
        body, grid=(R // tr,), in_specs=[spec] * 3 + [pl.BlockSpec((n_parts, tr, Cp), lambda i: (0, i, 0))], out_specs=[spec] * 4,
        out_shape=[SDS((R, C), f32)] * 4, name=name,
        compiler_params=_params(("parallel",)))(*[a.reshape(R, C) for a in (w, m, v)], parts.reshape(n_parts, R, Cp))
    return [o.reshape(shape) for o in outs]


def _coords():
    return lax.axis_index("x"), lax.axis_index("y"), lax.axis_index("c")


def _window(ref, axis, idx, n):
    start = pl.multiple_of(idx * n, n)
    if axis == 1:
        return ref.at[:, pl.ds(start, n), :]
    return ref.at[:, :, pl.ds(start, n)]


class _Rider:
    def __init__(self, inputs, out_shapes, aliases, scratch, start, finish):
        self.inputs, self.out_shapes, self.aliases, self.scratch = list(inputs), list(out_shapes), dict(aliases), list(scratch)
        self.start, self.finish = start, finish


def _join_riders(riders):
    riders = [r for r in riders if r is not None]
    if len(riders) < 2:
        return riders[0] if riders else None
    ni = np.cumsum([0] + [len(r.inputs) for r in riders])
    no = np.cumsum([0] + [len(r.out_shapes) for r in riders])
    ns = np.cumsum([0] + [len(r.scratch) for r in riders])
    aliases = {int(ni[k]) + i: int(no[k]) + o for k, r in enumerate(riders) for i, o in r.aliases.items()}

    def phase(which):
        def run(ins, outs, scr):
            for k, r in enumerate(riders):
                getattr(r, which)(ins[ni[k]:ni[k + 1]], outs[no[k]:no[k + 1]], scr[ns[k]:ns[k + 1]])
        return run

    return _Rider(sum([r.inputs for r in riders], []), sum([r.out_shapes for r in riders], []), aliases,
                  sum([r.scratch for r in riders], []), phase("start"), phase("finish"))


def _run_rider(r, name):
    ni, no = len(r.inputs), len(r.out_shapes)

    def body(*refs):
        ins, outs, scr = refs[:ni], refs[ni:ni + no], refs[ni + no:]
        r.start(ins, outs, scr)
        r.finish(ins, outs, scr)

    any_spec = pl.BlockSpec(memory_space=pl.ANY)
    return pl.pallas_call(body, out_shape=r.out_shapes, in_specs=[any_spec] * ni, out_specs=[any_spec] * no,
                          input_output_aliases=r.aliases, scratch_shapes=r.scratch, name=name)(*r.inputs)


def _gather_rider(shards, axes):
    nt = len(shards)
    sizes = [s.shape[a] for s, a in zip(shards, axes)]
    out_shapes = [SDS(tuple(d * N_DEV if i == a else d for i, d in enumerate(s.shape)), s.dtype) for s, a in zip(shards, axes)]

    def plan(x_refs, out_refs, scr):
        send_sems, recv_sems, local_sems = scr
        x, y, c = _coords()
        me, sibling = (x, y, c), (x, y, 1 - c)
        chips = [(1 - x, y), (x, 1 - y), (1 - x, 1 - y)]

        def win(t, px, py, pc):
            return _window(out_refs[t], axes[t], 4 * px + 2 * py + pc, sizes[t])

        def copy(t, k, block, to, src=None):
            return pltpu.make_async_remote_copy(
                src_ref=win(t, *block) if src is None else src, dst_ref=win(t, *block),
                send_sem=send_sems.at[7 * t + k], recv_sem=recv_sems.at[7 * t + k], device_id=to, device_id_type=MESH)

        mine = [pltpu.make_async_copy(x_refs[t], win(t, *me), local_sems.at[t]) for t in range(nt)]
        first = []
        for t in range(nt):
            first.append(copy(t, 0, me, sibling, src=x_refs[t]))
            first += [copy(t, 1 + j, me, (*chip, c), src=x_refs[t]) for j, chip in enumerate(chips)]
        return c, me, sibling, chips, copy, mine, first

    def start(x_refs, out_refs, scr):
        _, _, _, _, _, mine, first = plan(x_refs, out_refs, scr)
        for cp in mine + first:
            cp.start()

    def finish(x_refs, out_refs, scr):
        c, me, sibling, chips, copy, mine, first = plan(x_refs, out_refs, scr)
        passed = []
        for j, chip in enumerate(chips):
            for t in range(nt):
                copy(t, 1 + j, (*chip, c), me).wait_recv()
                fwd = copy(t, 4 + j, (*chip, c), sibling)
                fwd.start()
                passed.append(fwd)
        for t in range(nt):
            copy(t, 0, sibling, me).wait_recv()
            for j, chip in enumerate(chips):
                copy(t, 4 + j, (*chip, 1 - c), me).wait_recv()
        for cp in first + passed:
            cp.wait_send()
        for cp in mine:
            cp.wait()

    scratch = [pltpu.SemaphoreType.DMA((7 * nt,)), pltpu.SemaphoreType.DMA((7 * nt,)), pltpu.SemaphoreType.DMA((nt,))]
    return _Rider(shards, out_shapes, {}, scratch, start, finish)


N_CHIP = N_DEV // 2


def _pair_exchange_rider(grads, axes, shards):
    nt = len(grads)
    sizes = [s[a] for s, a in zip(shards, axes)]

    def plan(ins, outs, scr):
        send_sems, recv_sems = scr
        x, y, c = _coords()
        return [pltpu.make_async_remote_copy(
            src_ref=_window(ins[t], axes[t], 2 * q + (1 - c), sizes[t]), dst_ref=outs[t].at[q],
            send_sem=send_sems.at[N_CHIP * t + q], recv_sem=recv_sems.at[N_CHIP * t + q],
            device_id=(x, y, 1 - c), device_id_type=MESH) for t in range(nt) for q in range(N_CHIP)]

    def start(ins, outs, scr):
        for cp in plan(ins, outs, scr):
            cp.start()

    def finish(ins, outs, scr):
        copies = plan(ins, outs, scr)
        for cp in copies:
            cp.wait_recv()
        for cp in copies:
            cp.wait_send()

    scratch = [pltpu.SemaphoreType.DMA((N_CHIP * nt,)), pltpu.SemaphoreType.DMA((N_CHIP * nt,))]
    return _Rider(grads, [SDS((N_CHIP,) + tuple(s), g.dtype) for s, g in zip(shards, grads)], {}, scratch, start, finish)


def _pair_sum(grad, buf, axis, core, name):
    _, G, r, c = buf.shape
    tr = _pick(r, 512)
    nr = r // tr

    def body(core_ref, g_ref, b_ref, o_ref):
        o_ref[0, 0] = (g_ref[0].astype(f32) + b_ref[0, 0].astype(f32)).astype(bf16)

    if axis == 2:
        g_spec = pl.BlockSpec((1, tr, c), lambda q, g, i, core_ref: (g, i, 2 * q + core_ref[0]))
    else:
        g_spec = pl.BlockSpec((1, tr, c), lambda q, g, i, core_ref: (g, (2 * q + core_ref[0]) * nr + i, 0))
    blk = pl.BlockSpec((1, 1, tr, c), lambda q, g, i, core_ref: (q, g, i, 0))
    grid_spec = pltpu.PrefetchScalarGridSpec(num_scalar_prefetch=1, grid=(N_CHIP, G, nr), in_specs=[g_spec, blk], out_specs=blk)
    return pl.pallas_call(body, grid_spec=grid_spec, out_shape=SDS(buf.shape, bf16), name=name,
                          compiler_params=_params(("parallel", "parallel", "parallel")))(core, grad, buf)


def _chip_exchange_rider(chips, slots, layer):
    nt = len(chips)

    def plan(ins, outs, scr):
        send_sems, recv_sems, local_sems = scr
        x, y, c = _coords()
        me = 2 * x + y
        mine = [pltpu.make_async_copy(ins[t].at[me], outs[t].at[me, layer], local_sems.at[t]) for t in range(nt)]
        copies = []
        for k in range(1, N_CHIP):
            px, py = lax.rem(x + (k >> 1), 2), lax.rem(y + (k & 1), 2)
            copies += [pltpu.make_async_remote_copy(
                src_ref=ins[t].at[2 * px + py], dst_ref=outs[t].at[me, layer],
                send_sem=send_sems.at[3 * t + k - 1], recv_sem=recv_sems.at[3 * t + k - 1],
                device_id=(px, py, c), device_id_type=MESH) for t in range(nt)]
        return mine, copies

    def start(ins, outs, scr):
        mine, copies = plan(ins, outs, scr)
        for cp in mine + copies:
            cp.start()

    def finish(ins, outs, scr):
        mine, copies = plan(ins, outs, scr)
        for cp in copies:
            cp.wait_recv()
        for cp in copies:
            cp.wait_send()
        for cp in mine:
            cp.wait()

    scratch = [pltpu.SemaphoreType.DMA((3 * nt,)), pltpu.SemaphoreType.DMA((3 * nt,)), pltpu.SemaphoreType.DMA((nt,))]
    return _Rider(list(chips) + list(slots), [SDS(s.shape, s.dtype) for s in slots], {nt + t: t for t in range(nt)}, scratch,
                  start, finish)


_BIG = (("w_in", 2), ("w_uq", 2), ("w_ukv", 2), ("w_mem_kv", 1), ("w_br", 2), ("w_out", 1), ("w_ff1", 2), ("w_ff2", 1))


def _in_col_maps(D):
    d_in = O_GATE + 3 * D
    sh = d_in // N_DEV
    shp = _round_up(sh, LANES)
    perm_of_orig = np.empty(d_in, np.int64)
    for a, b, dst in ((0, O_F, C_Q), (O_F, O_CQ, C_F), (O_CQ, O_CKV, C_CQ), (O_CKV, O_KR, C_CKV), (O_KR, O_QM, C_KPE + NOPE),
                      (O_QM, O_GATE, C_QM), (O_GATE, d_in, C_GATE)):
        perm_of_orig[a:b] = dst + np.arange(b - a)
    orig = np.arange(d_in)
    gath_of_orig = (orig // sh) * shp + orig % sh
    fwd = -np.ones(C_GATE + 3 * D, np.int64)
    fwd[perm_of_orig] = gath_of_orig
    bwd = -np.ones(N_DEV * shp, np.int64)
    bwd[gath_of_orig] = perm_of_orig
    return fwd, bwd


def _lane_pad(a):
    c = a.shape[-1]
    return jnp.pad(a, [(0, 0)] * (a.ndim - 1) + [(0, _round_up(c, LANES) - c)])


def _as3(a):
    return a.reshape((-1,) + a.shape[-2:]) if a.ndim != 3 else a


def _rope_tables(positions):
    inv_freq = ROPE_BASE ** (-jnp.arange(0, ROPE, 2, dtype=f32) / ROPE)
    ang = positions.astype(f32).reshape(-1, 1) * inv_freq
    cos, sin = jnp.cos(ang), jnp.sin(ang)
    T = ang.shape[0]
    one, zero = jnp.ones((T, NOPE), f32), jnp.zeros((T, 16), f32)
    c = jnp.concatenate([one, cos, cos, jnp.ones((T, 32), f32)], axis=1)
    sa = jnp.concatenate([jnp.zeros((T, NOPE), f32), -sin, zero, jnp.zeros((T, 32), f32)], axis=1)
    sb = jnp.concatenate([jnp.zeros((T, NOPE), f32), zero, sin, jnp.zeros((T, 32), f32)], axis=1)
    return c, sa, sb


def kernel(x, mem, positions, ln_in_g, ln_in_b, w_in, b_forget, w_uq, g_cq, w_ukv, g_ckv, w_mem_kv, w_br, w_out, ln1_g, ln1_b, w_ff1, w_ff2, ln2_g, ln2_b, loss_target, m_ln_in_g, m_ln_in_b, m_w_in, m_b_forget, m_w_uq, m_g_cq, m_w_ukv, m_g_ckv, m_w_mem_kv, m_w_br, m_w_out, m_ln1_g, m_ln1_b, m_w_ff1, m_w_ff2, m_ln2_g, m_ln2_b, v_ln_in_g, v_ln_in_b, v_w_in, v_b_forget, v_w_uq, v_g_cq, v_w_ukv, v_g_ckv, v_w_mem_kv, v_w_br, v_w_out, v_ln1_g, v_ln1_b, v_w_ff1, v_w_ff2, v_ln2_g, v_ln2_b):
    B, S, D = x.shape
    T = B * S
    L = w_in.shape[0]
    NM = mem.shape[1]
    alpha = (2 * L) ** 0.25
    weights = dict(ln_in_g=ln_in_g, ln_in_b=ln_in_b, w_in=w_in, b_forget=b_forget, w_uq=w_uq, g_cq=g_cq, w_ukv=w_ukv, g_ckv=g_ckv,
                   w_mem_kv=w_mem_kv, w_br=w_br, w_out=w_out, ln1_g=ln1_g, ln1_b=ln1_b, w_ff1=w_ff1, w_ff2=w_ff2, ln2_g=ln2_g, ln2_b=ln2_b)
    mom_m = dict(ln_in_g=m_ln_in_g, ln_in_b=m_ln_in_b, w_in=m_w_in, b_forget=m_b_forget, w_uq=m_w_uq, g_cq=m_g_cq, w_ukv=m_w_ukv,
                 g_ckv=m_g_ckv, w_mem_kv=m_w_mem_kv, w_br=m_w_br, w_out=m_w_out, ln1_g=m_ln1_g, ln1_b=m_ln1_b, w_ff1=m_w_ff1,
                 w_ff2=m_w_ff2, ln2_g=m_ln2_g, ln2_b=m_ln2_b)
    mom_v = dict(ln_in_g=v_ln_in_g, ln_in_b=v_ln_in_b, w_in=v_w_in, b_forget=v_b_forget, w_uq=v_w_uq, g_cq=v_g_cq, w_ukv=v_w_ukv,
                 g_ckv=v_g_ckv, w_mem_kv=v_w_mem_kv, w_br=v_w_br, w_out=v_w_out, ln1_g=v_ln1_g, ln1_b=v_ln1_b, w_ff1=v_w_ff1,
                 w_ff2=v_w_ff2, ln2_g=v_ln2_g, ln2_b=v_ln2_b)
    order = list(weights)
    big_axes = [ax for _, ax in _BIG]
    fwd_map, bwd_map = _in_col_maps(D)
    shard3 = {n: _lane_pad(weights[n]) for n, _ in _BIG}

    half = len(_BIG) // 2

    def gather_rider(l, lo=0, hi=len(_BIG)):
        return _gather_rider([_as3(shard3[n][l]).astype(bf16) for n, _ in _BIG[lo:hi]], big_axes[lo:hi])

    def named(full, lo, hi):
        return {n: (f if n == "w_br" else f[0]) for (n, _), f in zip(_BIG[lo:hi], full)}

    tabs = _rope_tables(positions)
    x2d, mem2d, tgt2d = x.reshape(T, D), mem.reshape(B * NM, D), loss_target.reshape(T, D)
    fox = dict(B=B, Sq=S, Sk=S, nblk=FOX_HEADS // 2, hb=2, dqk=FOX_HD, dv=FOX_HD, mode="fox", scale=FOX_HD ** -0.5)
    mla_scale = (NOPE + ROPE) ** -0.5
    mla = dict(B=B, Sq=S, Sk=S, nblk=MLA_HEADS // 2, hb=2, dqk=LANES, dv=MLA_VD, mode="chunk", scale=1.0)
    memat = dict(B=B, Sq=S, Sk=NM, nblk=MEM_HEADS, hb=1, dqk=MEM_HD, dv=MEM_HD, mode="none", scale=MEM_HD ** -0.5)
    nb = FOX_HEADS // 2

    h, hb, xh0, rstd0 = _ln_fwd(None, x2d, ln_in_g, ln_in_b, alpha, "ln_in")
    saved = []
    first = _run_rider(gather_rider(0, 0, half), "weights_all_gather")
    for l in range(L):
        W = named(first, 0, half)
        W["w_lo"] = _lane_permute(W["w_in"], fwd_map[:C_GATE], bf16, "w_in_reorder")
        W["w_hi"] = _lane_permute(W["w_in"], fwd_map[C_GATE:], bf16, "w_in_reorder")
        bpad = jnp.pad(b_forget[l].reshape(1, FOX_HEADS), ((0, 0), (0, LANES - FOX_HEADS)))
        proj = _mm(hb, W["w_lo"], name="proj")
        gates = _mm(hb, W["w_hi"], out_dtype=bf16, name="proj_gates")
        cum = _gate_fwd(proj, bpad, B, S, "gate_fwd")
        cum_r = cum[:, :FOX_HEADS].reshape(B, S, nb, 2).transpose(0, 2, 3, 1).reshape(B * nb, 2, S)
        more = l + 1 < L
        (o_a, lse_a), second = _attn_fwd(proj, C_Q // LANES, proj, C_K // LANES, proj, C_V // LANES, cum_r=cum_r, name="fox_fwd",
                                         rider=gather_rider(l, half), **fox)
        W.update(named(second, half, len(_BIG)))
        cqn = _rms_fwd(proj, C_CQ, Q_RANK, g_cq[l], "rms_q")
        ckvn = _rms_fwd(proj, C_CKV, KV_RANK, g_ckv[l], "rms_kv")
        qraw = _mm(cqn, W["w_uq"], name="q_up")
        kv = _mm(ckvn, W["w_ukv"], name="kv_up")
        qf = _rope_q(qraw, tabs, bwd=False, mul=mla_scale, name="rope_q")
        kf, v_b = _rope_k(kv, tabs, proj, "rope_k")
        res = _attn_fwd(qf, 0, kf, 0, v_b, 0, name="mla_fwd", rider=gather_rider(l + 1, 0, half) if more else None, **mla)
        (o_b, lse_b), first = res if more else (res, None)
        mkv = _mm(mem2d, W["w_mem_kv"], name="mem_kv")
        o_c, lse_c = _attn_fwd(proj, C_QM // MEM_HD, mkv, 0, mkv, MEM_HEADS, name="mem_fwd", **memat)
        merged = _merge_fwd(o_a, o_b, o_c, W["w_br"], gates, D, "merge_fwd")
        y = _mm(merged, W["w_out"], name="out_proj")
        h1, h1b, xh1, rstd1 = _ln_fwd(h, y, ln1_g[l], ln1_b[l], alpha, "ln1")
        a, r = _mm(h1b, W["w_ff1"], epi="relu2", name="ff1")
        ff = _mm(a, W["w_ff2"], name="ff2")
        h2, h2b, xh2, rstd2 = _ln_fwd(h1, ff, ln2_g[l], ln2_b[l], alpha, "ln2")
        saved.append(dict(W=W, bpad=bpad, h=hb, proj=proj, gates=gates, cum_r=cum_r, o_a=o_a, lse_a=lse_a, cqn=cqn, ckvn=ckvn, qf=qf, kf=kf,
                          v_b=v_b, o_b=o_b, lse_b=lse_b, mkv=mkv, o_c=o_c, lse_c=lse_c, merged=merged, h1=h1b, xh1=xh1,
                          rstd1=rstd1, a=a, r=r, xh2=xh2, rstd2=rstd2))
        h, hb = h2, h2b

    loss_local, dh = _loss(h, tgt2d, "loss")
    loss = lax.psum(loss_local[0, 0], ("x", "y", "c"))

    shard_shapes = [_as3(shard3[n][0]).shape for n, _ in _BIG]
    slots = [lax.empty((N_CHIP, L) + s, bf16) for s in shard_shapes]
    core = lax.axis_index("c").astype(jnp.int32).reshape(1)
    small = {n: [None] * L for n in ("b_forget", "g_cq", "g_ckv", "ln1_g", "ln1_b", "ln2_g", "ln2_b")}
    ax_a, ax_c = big_axes[:half], big_axes[half:]
    sh_a, sh_c = shard_shapes[:half], shard_shapes[half:]

    def pair_sums(gs, bufs, axes):
        return [_pair_sum(g, b, ax, core, "grads_pair_sum") for g, b, ax in zip(gs, bufs, axes)]

    pending = unpaired = None
    for l in reversed(range(L)):
        sv = saved[l]
        W = sv["W"]
        du2, du2b, dg2, db2 = _ln_bwd(dh, sv["xh2"], sv["rstd2"], ln2_g[l], "ln2_bwd")
        dz = _mm(du2b, W["w_ff2"], mode="nt", epi="mul", extra=sv["r"], out_dtype=bf16, name="ff2_dx")
        dw_ff2 = _mm(sv["a"], du2b, mode="tn", out_dtype=bf16, name="ff2_dw")
        dh1 = _mm(dz, W["w_ff1"], mode="nt", epi="add", extra=du2, alpha=alpha, name="ff1_dx")
        dw_ff1 = _mm(sv["h1"], dz, mode="tn", out_dtype=bf16, name="ff1_dw")
        du1, du1b, dg1, db1 = _ln_bwd(dh1, sv["xh1"], sv["rstd1"], ln1_g[l], "ln1_bwd")
        dmerged = _mm(du1b, W["w_out"], mode="nt", name="out_dx")
        dw_out = _mm(sv["merged"], du1b, mode="tn", out_dtype=bf16, name="out_dw")
        mb = _merge_bwd(dmerged, sv["o_a"], sv["o_b"], sv["o_c"], W["w_br"], sv["gates"], D, "merge_bwd",
                        rider=None if unpaired is None else _pair_exchange_rider(unpaired[0], ax_a, sh_a))
        if unpaired is not None:
            mb, bufs = mb
            pending = (pair_sums(unpaired[0], bufs, ax_a), unpaired[1])
        dgates, do, dbp = mb[0], mb[1:4], mb[4:7]
        dw_br = jnp.stack([_mm(o_n, dbp_n, mode="tn", out_dtype=bf16, name="br_dw") for o_n, dbp_n in zip((sv["o_a"], sv["o_b"], sv["o_c"]), dbp)])
        late = [_as3(g) for g in (dw_br, dw_out, dw_ff1, dw_ff2)]
        proj = sv["proj"]
        chip_rider = None if pending is None else _chip_exchange_rider(pending[0], slots[:half], pending[1])
        own, got = _attn_bwd(proj, C_Q // LANES, proj, C_K // LANES, proj, C_V // LANES, sv["o_a"], do[0], sv["lse_a"],
                             cum_r=sv["cum_r"], name="fox_bwd",
                             rider=_join_riders([chip_rider, _pair_exchange_rider(late, ax_c, sh_c)]), **fox)
        dq_a, dk_a, dv_a, dcum = own
        if chip_rider is not None:
            slots[:half], got = got[:half], got[half:]
        dcum = dcum.reshape(B, nb, S, 2).transpose(0, 2, 1, 3).reshape(T, FOX_HEADS)
        dcum = jnp.pad(dcum, ((0, 0), (0, LANES - FOX_HEADS)))
        dzf, dbf = _gate_bwd(dcum, proj, sv["bpad"], B, S, "gate_bwd")
        (dqf, dkf, dv_b), slots[half:] = _attn_bwd(
            sv["qf"], 0, sv["kf"], 0, sv["v_b"], 0, sv["o_b"], do[1], sv["lse_b"], dq_dtype=f32, name="mla_bwd",
            rider=_chip_exchange_rider(pair_sums(late, got, ax_c), slots[half:], l), **mla)
        dqraw = _rope_q(dqf, tabs, bwd=True, mul=mla_scale, name="rope_q_bwd")
        dkv, dkpe = _rope_k_bwd(dkf, dv_b, tabs, "rope_k_bwd")
        dw_uq = _mm(sv["cqn"], dqraw, mode="tn", out_dtype=bf16, name="q_up_dw")
        dcqn = _mm(dqraw, W["w_uq"], mode="nt", name="q_up_dx")
        dcq_, dg_cq = _rms_bwd(dcqn, proj, C_CQ, Q_RANK, g_cq[l], "rms_q_bwd")
        dw_ukv = _mm(sv["ckvn"], dkv, mode="tn", out_dtype=bf16, name="kv_up_dw")
        dckvn = _mm(dkv, W["w_ukv"], mode="nt", name="kv_up_dx")
        dckv, dg_ckv = _rms_bwd(dckvn, proj, C_CKV, KV_RANK, g_ckv[l], "rms_kv_bwd")
        dqm, dmk, dmv = _attn_bwd(proj, C_QM // MEM_HD, sv["mkv"], 0, sv["mkv"], MEM_HEADS, sv["o_c"], do[2], sv["lse_c"],
                                  name="mem_bwd", **memat)
        dw_mem = _mm(mem2d, jnp.concatenate([dmk, dmv], axis=1), mode="tn", out_dtype=bf16, name="mem_kv_dw")
        dproj = jnp.concatenate([dq_a, dk_a, dv_a, dzf, dkpe, dckv, dqm, jnp.zeros((T, C_CQ - C_PAD), bf16), dcq_], axis=1)
        dw_perm = jnp.concatenate([_mm(sv["h"], dproj, mode="tn", out_dtype=bf16, name="proj_dw"),
                                   _mm(sv["h"], dgates, mode="tn", out_dtype=bf16, name="proj_gates_dw")], axis=1)
        dw_in = _lane_permute(dw_perm, bwd_map, bf16, "w_in_grad_reorder")
        dh = _mm(dproj, W["w_lo"], mode="nt", epi="add", extra=du1, alpha=alpha, name="proj_dx")
        dh = _mm(dgates, W["w_hi"], mode="nt", epi="add", extra=dh, name="proj_gates_dx")
        unpaired = ([_as3(g) for g in (dw_in, dw_uq, dw_ukv, dw_mem)], l)
        for n, val in (("b_forget", dbf[:, :FOX_HEADS]), ("g_cq", dg_cq), ("g_ckv", dg_ckv), ("ln1_g", dg1), ("ln1_b", db1),
                       ("ln2_g", dg2), ("ln2_b", db2)):
            small[n][l] = val.reshape(-1)
    bufs = _run_rider(_pair_exchange_rider(unpaired[0], ax_a, sh_a), "grads_pair_exchange")
    slots[:half] = _run_rider(_chip_exchange_rider(pair_sums(unpaired[0], bufs, ax_a), slots[:half], unpaired[1]),
                              "grads_chip_exchange")
    grad_x, _, dg_in, db_in = _ln_bwd(dh, xh0, rstd0, ln_in_g, "ln_in_bwd")

    small_list = [("ln_in_g", dg_in.reshape(-1)), ("ln_in_b", db_in.reshape(-1))] + [(n, jnp.stack(v).reshape(-1)) for n, v in small.items()]
    sflat = jnp.concatenate([v for _, v in small_list])
    n_small = sflat.shape[0]
    rs = _round_up(-(-n_small // LANES), 8)
    spacked = jnp.pad(sflat, (0, rs * LANES - n_small)).reshape(1, rs, LANES)
    sparts = _run_rider(_gather_rider([spacked], [1]), "small_all_gather")[0].reshape(N_DEV, rs * LANES)

    grads, deltas, new_m, new_v = {}, {}, {}, {}
    off = 0
    for n, vv in small_list:
        parts = sparts[:, off:off + vv.shape[0]].reshape((N_DEV,) + weights[n].shape)
        off += vv.shape[0]
        grads[n], deltas[n], new_m[n], new_v[n] = _adamw(weights[n], mom_m[n], mom_v[n], parts, "adamw_" + n)
    for (n, _), sl in zip(_BIG, slots):
        parts = sl.reshape((N_CHIP,) + weights[n].shape[:-1] + (sl.shape[-1],))
        grads[n], deltas[n], new_m[n], new_v[n] = _adamw(weights[n], mom_m[n], mom_v[n], parts, "adamw_" + n)
    return (loss, grad_x.reshape(B, S, D), *[grads[n] for n in order], *[deltas[n] for n in order], *[new_m[n] for n in order],
            *[new_v[n] for n in order])
```

```python
import functools

import numpy as np
import jax
import jax.numpy as jnp
from jax import lax
from jax.experimental import pallas as pl
from jax.experimental.pallas import tpu as pltpu

f32, bf16 = jnp.float32, jnp.bfloat16
SDS = jax.ShapeDtypeStruct

N_DEV = 8
MESH = pl.DeviceIdType.MESH
LANES = 128
VMEM_LIMIT = 56 * 1024 * 1024

FOX_HEADS, FOX_HD = 8, 64
MLA_HEADS, NOPE, ROPE, MLA_VD = 8, 64, 32, 64
Q_RANK, KV_RANK = 384, 256
ROPE_BASE = 10000.0
MEM_HEADS, MEM_HD = 4, 128
BW = 512
CHUNK = 64
LN_EPS, RMS_EPS, NEG_INF = 1e-5, 1e-6, -1e30
ADAM_LR, ADAM_B1, ADAM_B2, ADAM_EPS, ADAM_WD, ADAM_STEP = 0.001, 0.9, 0.999, 1e-08, 0.01, 10

ATT_TQ, ATT_TK = 512, 512

C_Q, C_K, C_V, C_F, C_KPE, C_CKV, C_QM, C_PAD, C_CQ, C_GATE = 0, 512, 1024, 1536, 1664, 1792, 2048, 2560, 2688, 3072
O_F, O_CQ, O_CKV, O_KR, O_QM, O_GATE = 1536, 1544, 1928, 2184, 2216, 2728


def _pick(dim, pref):
    if dim <= pref:
        return dim
    for c in (2048, 1024, 768, 512, 384, 256, 128, 64, 32, 16, 8):
        if c <= pref and dim % c == 0:
            return c
    return dim


def _round_up(n, m):
    return -(-n // m) * m


def _params(sem=None):
    return pltpu.CompilerParams(dimension_semantics=sem, vmem_limit_bytes=VMEM_LIMIT)


_DIMS = {"nn": (((1,), (0,)), ((), ())), "nt": (((1,), (1,)), ((), ())), "tn": (((0,), (0,)), ((), ()))}


def _mm(a, b, *, mode="nn", out_dtype=f32, epi="none", extra=None, alpha=1.0, tm=1024, tn=1024, tk=None, name):
    if mode == "nn":
        (M, K), (_, N) = a.shape, b.shape
    elif mode == "nt":
        (M, K), (N, _) = a.shape, b.shape
    else:
        (K, M), (_, N) = a.shape, b.shape
    tm, tn, tk = _pick(M, tm), _pick(N, tn), _pick(K, tk or 2048)
    nk = K // tk
    a_spec = pl.BlockSpec((tk, tm), lambda i, j, k: (k, i)) if mode == "tn" else pl.BlockSpec((tm, tk), lambda i, j, k: (i, k))
    b_spec = pl.BlockSpec((tn, tk), lambda i, j, k: (j, k)) if mode == "nt" else pl.BlockSpec((tk, tn), lambda i, j, k: (k, j))
    o_spec = pl.BlockSpec((tm, tn), lambda i, j, k: (i, j))
    n_out = 2 if epi == "relu2" else 1
    dims = _DIMS[mode]

    def body(*refs):
        a_ref, b_ref = refs[0], refs[1]
        e_ref = refs[2] if extra is not None else None
        outs = refs[2 + (extra is not None):2 + (extra is not None) + n_out]

        def finish(r):
            if epi == "none":
                outs[0][...] = r.astype(out_dtype)
            elif epi == "relu2":
                p = jnp.maximum(r, 0.0)
                outs[0][...] = (p * p).astype(bf16)
                outs[1][...] = (2.0 * p).astype(bf16)
            elif epi == "mul":
                outs[0][...] = (r * e_ref[...].astype(f32)).astype(out_dtype)
            else:
                outs[0][...] = (r + alpha * e_ref[...].astype(f32)).astype(out_dtype)

        part = lax.dot_general(a_ref[...].astype(bf16), b_ref[...].astype(bf16), dims, preferred_element_type=f32)
        if nk == 1:
            finish(part)
            return
        acc = refs[-1]
        k = pl.program_id(2)

        @pl.when(k == 0)
        def _():
            acc[...] = part

        @pl.when(k > 0)
        def _():
            acc[...] += part

        @pl.when(k == nk - 1)
        def _():
            finish(acc[...])

    ins, in_specs = [a, b], [a_spec, b_spec]
    if extra is not None:
        ins.append(extra)
        in_specs.append(o_spec)
    if epi == "relu2":
        out_shape, out_specs = [SDS((M, N), bf16), SDS((M, N), bf16)], [o_spec, o_spec]
    else:
        out_shape, out_specs = SDS((M, N), out_dtype), o_spec
    return pl.pallas_call(
        body, grid=(M // tm, N // tn, nk), in_specs=in_specs, out_specs=out_specs, out_shape=out_shape,
        scratch_shapes=[pltpu.VMEM((tm, tn), f32)] if nk > 1 else [], name=name,
        compiler_params=_params(("parallel", "parallel", "arbitrary")))(*ins)


def _lane_permute(src, cmap, out_dtype, name):
    R, Cs = src.shape
    cmap = np.asarray(cmap, np.int64)
    nb = cmap.shape[0] // LANES
    srcs = [sorted({int(c) // LANES for c in cmap[b * LANES:(b + 1) * LANES] if c >= 0}) for b in range(nb)]
    exact = src.dtype == f32
    tr = _pick(R, 256 if exact else 512)

    def body(src_ref, cm_ref, o_ref):
        rows = lax.broadcasted_iota(jnp.int32, (LANES, LANES), 0)
        for b in range(nb):
            tbl = cm_ref[:, b * LANES:(b + 1) * LANES]
            acc = jnp.zeros((tr, LANES), f32)
            for sb in srcs[b]:
                sel = (rows + sb * LANES) == tbl
                blk = src_ref[:, sb * LANES:(sb + 1) * LANES]
                if exact:
                    acc = acc + jnp.dot(blk, sel.astype(f32), precision=lax.Precision.HIGHEST, preferred_element_type=f32)
                else:
                    acc = acc + jnp.dot(blk, sel.astype(blk.dtype), preferred_element_type=f32)
            o_ref[:, b * LANES:(b + 1) * LANES] = acc.astype(out_dtype)

    return pl.pallas_call(
        body, grid=(R // tr,), in_specs=[pl.BlockSpec((tr, Cs), lambda i: (i, 0)), pl.BlockSpec((1, nb * LANES), lambda i: (0, 0))],
        out_specs=pl.BlockSpec((tr, nb * LANES), lambda i: (i, 0)), out_shape=SDS((R, nb * LANES), out_dtype), name=name,
        compiler_params=_params(("parallel",)))(src, jnp.asarray(cmap.astype(np.int32)).reshape(1, -1))


def _ln_fwd(h, y, g, b, alpha, name):
    T, D = y.shape
    tr = _pick(T, 512)
    row = pl.BlockSpec((tr, D), lambda i: (i, 0))
    vec = pl.BlockSpec((1, D), lambda i: (0, 0))
    col = pl.BlockSpec((tr, 1), lambda i: (i, 0))

    def body(*refs):
        if h is None:
            y_ref, g_ref, b_ref, o_ref, ob_ref, xh_ref, rs_ref = refs
            u = y_ref[...]
        else:
            h_ref, y_ref, g_ref, b_ref, o_ref, ob_ref, xh_ref, rs_ref = refs
            u = alpha * h_ref[...] + y_ref[...]
        mu = jnp.mean(u, axis=-1, keepdims=True)
        d = u - mu
        var = jnp.mean(d * d, axis=-1, keepdims=True)
        rstd = lax.rsqrt(var + LN_EPS)
        xh = d * rstd
        xh_ref[...] = xh
        rs_ref[...] = rstd
        o = xh * g_ref[...] + b_ref[...]
        o_ref[...] = o
        ob_ref[...] = o.astype(bf16)

    ins = ([] if h is None else [h]) + [y, g.reshape(1, D), b.reshape(1, D)]
    specs = ([] if h is None else [row]) + [row, vec, vec]
    return pl.pallas_call(
        body, grid=(T // tr,), in_specs=specs, out_specs=[row, row, row, col],
        out_shape=[SDS((T, D), f32), SDS((T, D), bf16), SDS((T, D), f32), SDS((T, 1), f32)], name=name,
        compiler_params=_params(("parallel",)))(*ins)


def _ln_bwd(dy, xh, rstd, g, name):
    T, D = dy.shape
    tr = _pick(T, 512)
    row = pl.BlockSpec((tr, D), lambda i: (i, 0))
    vec = pl.BlockSpec((1, D), lambda i: (0, 0))
    col = pl.BlockSpec((tr, 1), lambda i: (i, 0))

    def body(dy_ref, xh_ref, rs_ref, g_ref, du_ref, dub_ref, dg_ref, db_ref):
        @pl.when(pl.program_id(0) == 0)
        def _():
            dg_ref[...] = jnp.zeros_like(dg_ref)
            db_ref[...] = jnp.zeros_like(db_ref)

        dyv, xhv = dy_ref[...], xh_ref[...]
        dxh = dyv * g_ref[...]
        m1 = jnp.mean(dxh, axis=-1, keepdims=True)
        m2 = jnp.mean(dxh * xhv, axis=-1, keepdims=True)
        du = rs_ref[...] * (dxh - m1 - xhv * m2)
        du_ref[...] = du
        dub_ref[...] = du.astype(bf16)
        dg_ref[...] += jnp.sum(dyv * xhv, axis=0, keepdims=True)
        db_ref[...] += jnp.sum(dyv, axis=0, keepdims=True)

    return pl.pallas_call(
        body, grid=(T // tr,), in_specs=[row, row, col, vec], out_specs=[row, row, vec, vec],
        out_shape=[SDS((T, D), f32), SDS((T, D), bf16), SDS((1, D), f32), SDS((1, D), f32)], name=name,
        compiler_params=_params(("arbitrary",)))(dy, xh, rstd, g.reshape(1, D))


def _rms_fwd(proj, col_off, R, g, name):
    T = proj.shape[0]
    tr = _pick(T, 512)
    cb = col_off // R

    def body(c_ref, g_ref, o_ref):
        c = c_ref[...]
        r = lax.rsqrt(jnp.mean(c * c, axis=-1, keepdims=True) + RMS_EPS)
        o_ref[...] = (c * r * g_ref[...]).astype(bf16)

    return pl.pallas_call(
        body, grid=(T // tr,), in_specs=[pl.BlockSpec((tr, R), lambda i: (i, cb)), pl.BlockSpec((1, R), lambda i: (0, 0))],
        out_specs=pl.BlockSpec((tr, R), lambda i: (i, 0)), out_shape=SDS((T, R), bf16), name=name,
        compiler_params=_params(("parallel",)))(proj, g.reshape(1, R))


def _rms_bwd(dy, proj, col_off, R, g, name):
    T = proj.shape[0]
    tr = _pick(T, 512)
    cb = col_off // R

    def body(dy_ref, c_ref, g_ref, dc_ref, dg_ref):
        @pl.when(pl.program_id(0) == 0)
        def _():
            dg_ref[...] = jnp.zeros_like(dg_ref)

        c, dyv = c_ref[...], dy_ref[...]
        r = lax.rsqrt(jnp.mean(c * c, axis=-1, keepdims=True) + RMS_EPS)
        t = dyv * g_ref[...]
        mt = jnp.mean(t * c, axis=-1, keepdims=True)
        dc_ref[...] = (r * t - c * (r * r * r) * mt).astype(bf16)
        dg_ref[...] += jnp.sum(dyv * c * r, axis=0, keepdims=True)

    return pl.pallas_call(
        body, grid=(T // tr,),
        in_specs=[pl.BlockSpec((tr, R), lambda i: (i, 0)), pl.BlockSpec((tr, R), lambda i: (i, cb)), pl.BlockSpec((1, R), lambda i: (0, 0))],
        out_specs=[pl.BlockSpec((tr, R), lambda i: (i, 0)), pl.BlockSpec((1, R), lambda i: (0, 0))],
        out_shape=[SDS((T, R), bf16), SDS((1, R), f32)], name=name,
        compiler_params=_params(("arbitrary",)))(dy, proj, g.reshape(1, R))


def _tri(n, lower):
    r = lax.broadcasted_iota(jnp.int32, (n, n), 0)
    c = lax.broadcasted_iota(jnp.int32, (n, n), 1)
    return ((r >= c) if lower else (c >= r)).astype(f32)


def _gate_fwd(proj, bpad, B, S, name):
    ch = _pick(S, 256)
    nch = S // ch
    cb = C_F // LANES

    def body(f_ref, b_ref, o_ref):
        tri = _tri(ch, True)
        carry = jnp.zeros((1, LANES), f32)
        for c in range(nch):
            z = f_ref[c * ch:(c + 1) * ch, :] + b_ref[...]
            lf = jnp.minimum(z, 0.0) - jnp.log1p(jnp.exp(-jnp.abs(z)))
            cs = jnp.dot(tri, lf, precision=lax.Precision.HIGHEST, preferred_element_type=f32) + carry
            o_ref[c * ch:(c + 1) * ch, :] = cs
            carry = cs[ch - 1:ch, :]

    return pl.pallas_call(
        body, grid=(B,), in_specs=[pl.BlockSpec((S, LANES), lambda b: (b, cb)), pl.BlockSpec((1, LANES), lambda b: (0, 0))],
        out_specs=pl.BlockSpec((S, LANES), lambda b: (b, 0)), out_shape=SDS((B * S, LANES), f32), name=name,
        compiler_params=_params(("parallel",)))(proj, bpad)


def _gate_bwd(dcum, proj, bpad, B, S, name):
    ch = _pick(S, 256)
    nch = S // ch
    cb = C_F // LANES

    def body(d_ref, f_ref, b_ref, dz_ref, db_ref):
        @pl.when(pl.program_id(0) == 0)
        def _():
            db_ref[...] = jnp.zeros_like(db_ref)

        tri = _tri(ch, False)
        carry = jnp.zeros((1, LANES), f32)
        dbs = jnp.zeros((1, LANES), f32)
        for c in reversed(range(nch)):
            d = d_ref[c * ch:(c + 1) * ch, :]
            dlf = jnp.dot(tri, d, precision=lax.Precision.HIGHEST, preferred_element_type=f32) + carry
            carry = carry + jnp.sum(d, axis=0, keepdims=True)
            z = f_ref[c * ch:(c + 1) * ch, :] + b_ref[...]
            dz = dlf / (1.0 + jnp.exp(z))
            dz_ref[c * ch:(c + 1) * ch, :] = dz.astype(bf16)
            dbs = dbs + jnp.sum(dz, axis=0, keepdims=True)
        db_ref[...] += dbs

    return pl.pallas_call(
        body, grid=(B,),
        in_specs=[pl.BlockSpec((S, LANES), lambda b: (b, 0)), pl.BlockSpec((S, LANES), lambda b: (b, cb)), pl.BlockSpec((1, LANES), lambda b: (0, 0))],
        out_specs=[pl.BlockSpec((S, LANES), lambda b: (b, 0)), pl.BlockSpec((1, LANES), lambda b: (0, 0))],
        out_shape=[SDS((B * S, LANES), bf16), SDS((1, LANES), f32)], name=name,
        compiler_params=_params(("arbitrary",)))(dcum, proj, bpad)


def _rot(v, c, sa, sb):
    return v * c + pltpu.roll(v, LANES - 16, 1) * sa + pltpu.roll(v, 16, 1) * sb


def _rot_t(v, c, sa, sb):
    return v * c + pltpu.roll(v * sa, 16, 1) + pltpu.roll(v * sb, LANES - 16, 1)


def _rope_q(x, tabs, *, bwd, mul, name):
    T = x.shape[0]
    W = MLA_HEADS * LANES
    tr = _pick(T, 512)
    fn = _rot_t if bwd else _rot

    def body(x_ref, c_ref, sa_ref, sb_ref, o_ref):
        c, sa, sb = c_ref[...], sa_ref[...], sb_ref[...]
        for g in range(MLA_HEADS):
            o_ref[:, g * LANES:(g + 1) * LANES] = (fn(x_ref[:, g * LANES:(g + 1) * LANES], c, sa, sb) * mul).astype(bf16)

    tab = pl.BlockSpec((tr, LANES), lambda i: (i, 0))
    wide = pl.BlockSpec((tr, W), lambda i: (i, 0))
    return pl.pallas_call(body, grid=(T // tr,), in_specs=[wide, tab, tab, tab], out_specs=wide, out_shape=SDS((T, W), bf16),
                          name=name, compiler_params=_params(("parallel",)))(x, *tabs)


def _rope_k(kv, tabs, proj, name):
    T = kv.shape[0]
    W = MLA_HEADS * LANES
    tr = _pick(T, 512)
    kcb = C_KPE // LANES

    def body(x_ref, c_ref, sa_ref, sb_ref, kpe_ref, k_ref, v_ref):
        c, sa, sb, kpe = c_ref[...], sa_ref[...], sb_ref[...], kpe_ref[...]
        low = lax.broadcasted_iota(jnp.int32, (tr, LANES), 1) < NOPE
        halves = []
        for g in range(MLA_HEADS):
            x = x_ref[:, g * LANES:(g + 1) * LANES]
            k_ref[:, g * LANES:(g + 1) * LANES] = _rot(jnp.where(low, x, 0.0) + kpe, c, sa, sb).astype(bf16)
            halves.append(x[:, NOPE:])
            if g % 2 == 1:
                v_ref[:, (g // 2) * LANES:(g // 2 + 1) * LANES] = jnp.concatenate(halves, axis=1).astype(bf16)
                halves = []

    tab = pl.BlockSpec((tr, LANES), lambda i: (i, 0))
    wide = pl.BlockSpec((tr, W), lambda i: (i, 0))
    return pl.pallas_call(
        body, grid=(T // tr,), in_specs=[wide, tab, tab, tab, pl.BlockSpec((tr, LANES), lambda i: (i, kcb))],
        out_specs=[wide, pl.BlockSpec((tr, W // 2), lambda i: (i, 0))], out_shape=[SDS((T, W), bf16), SDS((T, W // 2), bf16)],
        name=name, compiler_params=_params(("parallel",)))(kv, *tabs, proj)


def _rope_k_bwd(dk, dv, tabs, name):
    T = dk.shape[0]
    W = MLA_HEADS * LANES
    tr = _pick(T, 512)

    def body(dk_ref, dv_ref, c_ref, sa_ref, sb_ref, o_ref, s_ref):
        c, sa, sb = c_ref[...], sa_ref[...], sb_ref[...]
        tot = jnp.zeros((tr, LANES), f32)
        for g in range(MLA_HEADS):
            d = _rot_t(dk_ref[:, g * LANES:(g + 1) * LANES], c, sa, sb)
            tot = tot + d
            o_ref[:, g * LANES:(g + 1) * LANES] = jnp.concatenate(
                [d[:, :NOPE].astype(bf16), dv_ref[:, g * MLA_VD:(g + 1) * MLA_VD]], axis=1)
        s_ref[...] = tot.astype(bf16)

    tab = pl.BlockSpec((tr, LANES), lambda i: (i, 0))
    wide = pl.BlockSpec((tr, W), lambda i: (i, 0))
    return pl.pallas_call(
        body, grid=(T // tr,), in_specs=[wide, pl.BlockSpec((tr, W // 2), lambda i: (i, 0)), tab, tab, tab],
        out_specs=[wide, tab], out_shape=[SDS((T, W), bf16), SDS((T, LANES), bf16)], name=name,
        compiler_params=_params(("parallel",)))(dk, dv, *tabs)


def _tile_mask(mode, r0, c0, nr, nc):
    r = r0 + lax.broadcasted_iota(jnp.int32, (nr, nc), 0)
    c = c0 + lax.broadcasted_iota(jnp.int32, (nr, nc), 1)
    if mode == "chunk":
        return (c // CHUNK) <= (r // CHUNK)
    return c <= r


def _diagonal(block, carry, i, tq, tk):
    nfull = (i * tq) // tk
    for d in range(max(1, tq // tk)):
        carry = block((nfull + d) * tk, tk, 0, tq, carry, True)
    return carry


def _attn_tiles(Sq, Sk):
    tk = _pick(Sk, ATT_TK)
    tq = _pick(Sq, ATT_TQ)
    assert (tk % tq == 0 or tq % tk == 0) and tq % CHUNK == 0 and tk % CHUNK == 0
    return tq, tk


def _stage_heads(dst, src_ref, hb, w, mark=None, mul=1.0):
    S = src_ref.shape[0]
    ch = _pick(S, 256)
    assert w <= LANES and (mark is None or w <= mark < LANES)

    def step(c, _):
        rows = pl.ds(pl.multiple_of(c * ch, ch), ch)
        for hh in range(hb):
            x = src_ref[rows, hh * w:(hh + 1) * w]
            x = (x.astype(f32) * mul).astype(bf16) if mul != 1.0 else x.astype(bf16)
            if w < LANES:
                pad = jnp.zeros((ch, LANES - w), bf16)
                if mark is not None:
                    lane = lax.broadcasted_iota(jnp.int32, (ch, LANES - w), 1)
                    pad = jnp.where(lane == mark - w, 1.0, 0.0).astype(bf16)
                x = jnp.concatenate([x, pad], axis=1)
            dst[hh, rows, :] = x
        return 0

    lax.fori_loop(0, S // ch, step, 0)


def _carry(rider, refs, n_in, n_out, n_scr, grid):
    if rider is None:
        return refs
    ri, ro = len(rider.inputs), len(rider.out_shapes)
    own = refs[:n_in] + refs[n_in + ri:n_in + ri + n_out] + refs[n_in + ri + n_out + ro:n_in + ri + n_out + ro + n_scr]
    rider.refs = (refs[n_in:n_in + ri], refs[n_in + ri + n_out:n_in + ri + n_out + ro], refs[n_in + ri + n_out + ro + n_scr:])
    first = functools.reduce(jnp.logical_and, [pl.program_id(a) == 0 for a in range(len(grid))])

    @pl.when(first)
    def _():
        rider.start(*rider.refs)

    return own


def _carry_finish(rider, grid):
    if rider is None:
        return
    last = functools.reduce(jnp.logical_and, [pl.program_id(a) == n - 1 for a, n in enumerate(grid)])

    @pl.when(last)
    def _():
        rider.finish(*rider.refs)


def _carrier_call(body, rider, ins, in_specs, out_shape, out_specs, scratch, grid, name):
    if rider is None:
        return pl.pallas_call(body, grid=grid, in_specs=in_specs, out_specs=out_specs, out_shape=out_shape, scratch_shapes=scratch,
                              name=name, compiler_params=_params(("parallel",) * len(grid)))(*ins)
    any_spec = pl.BlockSpec(memory_space=pl.ANY)
    ri, ro = len(rider.inputs), len(rider.out_shapes)
    res = pl.pallas_call(
        body, grid=grid, in_specs=list(in_specs) + [any_spec] * ri, out_specs=list(out_specs) + [any_spec] * ro,
        out_shape=list(out_shape) + rider.out_shapes, scratch_shapes=list(scratch) + rider.scratch,
        input_output_aliases={len(ins) + i: len(out_shape) + o for i, o in rider.aliases.items()}, name=name,
        compiler_params=_params(("arbitrary",) * len(grid)))(*ins, *rider.inputs)
    return res[:len(out_shape)], res[len(out_shape):]


def _attn_fwd(q_arr, q_cb, k_arr, k_cb, v_arr, v_cb, *, B, Sq, Sk, nblk, hb, dqk, dv, mode, scale, cum_r=None, rider=None, name):
    wq, wv = hb * dqk, hb * dv
    tq, tk = _attn_tiles(Sq, Sk)
    fox = mode == "fox"
    n_in = 4 if fox else 3
    ones = dv < LANES

    def body(*refs):
        refs = _carry(rider, refs, n_in, 2, 3, (B, nblk))
        q_ref, k_ref, v_ref = refs[:3]
        cr_ref = refs[3] if fox else None
        o_ref, lse_ref, qh, kh, vh = refs[n_in:]
        _stage_heads(qh, q_ref, hb, dqk, mul=scale)
        _stage_heads(kh, k_ref, hb, dqk)
        _stage_heads(vh, v_ref, hb, dv, dv if ones else None)

        def qstep(i, _):
            rows = pl.ds(pl.multiple_of(i * tq, tq), tq)
            qs = [qh[hh, rows, :] for hh in range(hb)]

            def block(c0, w, r0, nr, carry, diag):
                cols = pl.ds(pl.multiple_of(c0, w), w)
                out = []
                for hh in range(hb):
                    m, l, acc = carry[hh]
                    s = lax.dot_general(qs[hh][r0:r0 + nr], kh[hh, cols, :], _DIMS["nt"], preferred_element_type=f32)
                    if fox:
                        s = s - cr_ref[0, hh:hh + 1, cols]
                    if diag:
                        s = jnp.where(_tile_mask(mode, i * tq + r0, c0, nr, w), s, NEG_INF)
                    m_new = jnp.maximum(m, jnp.max(s, axis=-1, keepdims=True))
                    a = jnp.exp(m - m_new)
                    p = jnp.exp(s - m_new)
                    if not ones:
                        l = a * l + jnp.sum(p, axis=-1, keepdims=True)
                    acc = a * acc + lax.dot_general(p.astype(bf16), vh[hh, cols, :], _DIMS["nn"], preferred_element_type=f32)
                    out.append((m_new, l, acc))
                return tuple(out)

            def tile(j, carry):
                return block(j * tk, tk, 0, tq, carry, False)

            carry = tuple((jnp.full((tq, 1), -jnp.inf, f32), jnp.zeros((tq, 1), f32), jnp.zeros((tq, LANES), f32)) for _ in range(hb))
            if mode == "none":
                carry = lax.fori_loop(0, Sk // tk, tile, carry)
            else:
                nfull = (i * tq) // tk
                carry = lax.fori_loop(0, nfull, tile, carry)
                carry = _diagonal(block, carry, i, tq, tk)
            for hh in range(hb):
                m, l, acc = carry[hh]
                if ones:
                    l = acc[:, dv:dv + 1]
                o_ref[rows, hh * dv:(hh + 1) * dv] = acc[:, :dv] / l
                lse_ref[0, rows, hh:hh + 1] = m + jnp.log(l)
            return 0

        lax.fori_loop(0, Sq // tq, qstep, 0)
        _carry_finish(rider, (B, nblk))

    in_specs = [pl.BlockSpec((Sq, wq), lambda b, p: (b, q_cb + p)), pl.BlockSpec((Sk, wq), lambda b, p: (b, k_cb + p)),
                pl.BlockSpec((Sk, wv), lambda b, p: (b, v_cb + p))]
    ins = [q_arr, k_arr, v_arr]
    if fox:
        in_specs.append(pl.BlockSpec((1, hb, Sk), lambda b, p: (b * nblk + p, 0, 0)))
        ins.append(cum_r)
    return _carrier_call(
        body, rider, ins, in_specs,
        [SDS((B * Sq, nblk * wv), f32), SDS((B * nblk, Sq, hb), f32)],
        [pl.BlockSpec((Sq, wv), lambda b, p: (b, p)), pl.BlockSpec((1, Sq, hb), lambda b, p: (b * nblk + p, 0, 0))],
        [pltpu.VMEM((hb, Sq, LANES), bf16), pltpu.VMEM((hb, Sk, LANES), bf16), pltpu.VMEM((hb, Sk, LANES), bf16)],
        (B, nblk), name)


def _attn_bwd(q_arr, q_cb, k_arr, k_cb, v_arr, v_cb, o, do, lse, *, B, Sq, Sk, nblk, hb, dqk, dv, mode, scale, cum_r=None,
              dq_dtype=bf16, rider=None, name):
    wq, wv = hb * dqk, hb * dv
    tq, tk = _attn_tiles(Sq, Sk)
    fox = mode == "fox"
    n_in = 7 if fox else 6
    n_out = 4 if fox else 3
    if fox:
        assert dqk + 2 <= LANES

    def body(*refs):
        refs = _carry(rider, refs, n_in, n_out, 7 if fox else 6, (B, nblk))
        q_ref, k_ref, v_ref, o_ref, do_ref, lse_ref = refs[:6]
        cr_ref = refs[6] if fox else None
        dq_ref, dk_ref, dv_ref = refs[n_in:n_in + 3]
        dcum_ref = refs[n_in + 3] if fox else None
        qh, kh, vh, doh, dk_acc, dv_acc = refs[n_in + n_out:n_in + n_out + 6]
        dcq_acc = refs[n_in + n_out + 6] if fox else None
        _stage_heads(qh, q_ref, hb, dqk, dqk + 1 if fox else None, mul=scale)
        _stage_heads(kh, k_ref, hb, dqk, dqk if fox else None)
        _stage_heads(vh, v_ref, hb, dv)
        _stage_heads(doh, do_ref, hb, dv)
        dk_acc[...] = jnp.zeros_like(dk_acc)
        dv_acc[...] = jnp.zeros_like(dv_acc)

        def qstep(i, _):
            rows = pl.ds(pl.multiple_of(i * tq, tq), tq)
            qs = [qh[hh, rows, :] for hh in range(hb)]
            dos = [doh[hh, rows, :] for hh in range(hb)]
            dds = [jnp.sum(do_ref[rows, hh * dv:(hh + 1) * dv] * o_ref[rows, hh * dv:(hh + 1) * dv], axis=-1, keepdims=True)
                   for hh in range(hb)]
            lses = [lse_ref[0, rows, hh:hh + 1] for hh in range(hb)]

            def block(c0, w, r0, nr, carry, diag):
                cols = pl.ds(pl.multiple_of(c0, w), w)
                rs = slice(r0, r0 + nr)
                out = []
                for hh in range(hb):
                    k = kh[hh, cols, :]
                    q, do_ = qs[hh][rs], dos[hh][rs]
                    s = lax.dot_general(q, k, _DIMS["nt"], preferred_element_type=f32)
                    if fox:
                        s = s - cr_ref[0, hh:hh + 1, cols]
                    if diag:
                        s = jnp.where(_tile_mask(mode, i * tq + r0, c0, nr, w), s, NEG_INF)
                    p = jnp.exp(s - lses[hh][rs])
                    dv_acc[hh, cols, :] += lax.dot_general(p.astype(bf16), do_, _DIMS["tn"], preferred_element_type=f32)
                    dp = lax.dot_general(do_, vh[hh, cols, :], _DIMS["nt"], preferred_element_type=f32)
                    ds = p * (dp - dds[hh][rs])
                    dsb = ds.astype(bf16)
                    out.append(carry[hh] + lax.dot_general(dsb, k, _DIMS["nn"], preferred_element_type=f32))
                    dk_acc[hh, cols, :] += lax.dot_general(dsb, q, _DIMS["tn"], preferred_element_type=f32)
                return tuple(out)

            def tile(j, carry):
                return block(j * tk, tk, 0, tq, carry, False)

            carry = tuple(jnp.zeros((tq, LANES), f32) for _ in range(hb))
            if mode == "none":
                carry = lax.fori_loop(0, Sk // tk, tile, carry)
            else:
                nfull = (i * tq) // tk
                carry = lax.fori_loop(0, nfull, tile, carry)
                carry = _diagonal(block, carry, i, tq, tk)
            for hh in range(hb):
                dq_ref[rows, hh * dqk:(hh + 1) * dqk] = (carry[hh][:, :dqk] * scale).astype(dq_dtype)
                if fox:
                    dcq_acc[rows, hh:hh + 1] = carry[hh][:, dqk:dqk + 1]
            return 0

        lax.fori_loop(0, Sq // tq, qstep, 0)
        for hh in range(hb):
            dk_ref[:, hh * dqk:(hh + 1) * dqk] = dk_acc[hh, :, :dqk].astype(dq_dtype)
            dv_ref[:, hh * dv:(hh + 1) * dv] = dv_acc[hh, :, :dv].astype(bf16)
            if fox:
                dcum_ref[0, :, hh:hh + 1] = dcq_acc[:, hh:hh + 1] - dk_acc[hh, :, dqk + 1:dqk + 2]
        _carry_finish(rider, (B, nblk))

    qs_ =pl.BlockSpec((Sq, wq), lambda b, p: (b, q_cb + p))
    ks_ = pl.BlockSpec((Sk, wq), lambda b, p: (b, k_cb + p))
    vs_ = pl.BlockSpec((Sk, wv), lambda b, p: (b, v_cb + p))
    os_ = pl.BlockSpec((Sq, wv), lambda b, p: (b, p))
    st = pl.BlockSpec((1, Sq, hb), lambda b, p: (b * nblk + p, 0, 0))
    rw = pl.BlockSpec((1, hb, Sk), lambda b, p: (b * nblk + p, 0, 0))
    in_specs = [qs_, ks_, vs_, os_, os_, st] + ([rw] if fox else [])
    ins = [q_arr, k_arr, v_arr, o, do, lse] + ([cum_r] if fox else [])
    out_specs = [pl.BlockSpec((Sq, wq), lambda b, p: (b, p)), pl.BlockSpec((Sk, wq), lambda b, p: (b, p)),
                 pl.BlockSpec((Sk, wv), lambda b, p: (b, p))] + ([st] if fox else [])
    out_shape = [SDS((B * Sq, nblk * wq), dq_dtype), SDS((B * Sk, nblk * wq), dq_dtype), SDS((B * Sk, nblk * wv), bf16)]
    scratch = [pltpu.VMEM((hb, Sq, LANES), bf16), pltpu.VMEM((hb, Sk, LANES), bf16), pltpu.VMEM((hb, Sk, LANES), bf16),
               pltpu.VMEM((hb, Sq, LANES), bf16), pltpu.VMEM((hb, Sk, LANES), f32), pltpu.VMEM((hb, Sk, LANES), f32)]
    if fox:
        assert Sq == Sk
        out_shape.append(SDS((B * nblk, Sq, hb), f32))
        scratch.append(pltpu.VMEM((Sq, hb), f32))
    return _carrier_call(body, rider, ins, in_specs, out_shape, out_specs, scratch, (B, nblk), name)


def _merge_fwd(oa, ob, oc, w_br, proj, D, name):
    T = oa.shape[0]
    tr = _pick(T, 256)
    g0 = C_GATE // D
    o_spec = pl.BlockSpec((tr, BW), lambda i: (i, 0))
    w_spec = pl.BlockSpec((3, BW, D), lambda i: (0, 0, 0))

    def body(oa_ref, ob_ref, oc_ref, w_ref, ga_ref, gb_ref, gc_ref, m_ref):
        tot = jnp.zeros((tr, D), f32)
        for n, (o_ref, g_ref) in enumerate(((oa_ref, ga_ref), (ob_ref, gb_ref), (oc_ref, gc_ref))):
            bp = lax.dot_general(o_ref[...].astype(bf16), w_ref[n], _DIMS["nn"], preferred_element_type=f32)
            tot = tot + jax.nn.sigmoid(g_ref[...]) * bp
        m_ref[...] = tot.astype(bf16)

    gspecs = [pl.BlockSpec((tr, D), lambda i, n=n: (i, g0 + n)) for n in range(3)]
    return pl.pallas_call(
        body, grid=(T // tr,), in_specs=[o_spec, o_spec, o_spec, w_spec, *gspecs],
        out_specs=pl.BlockSpec((tr, D), lambda i: (i, 0)), out_shape=SDS((T, D), bf16), name=name,
        compiler_params=_params(("parallel",)))(oa, ob, oc, w_br, proj, proj, proj)


def _merge_bwd(dm, oa, ob, oc, w_br, proj, D, name, rider=None):
    T = oa.shape[0]
    tr = _pick(T, 256)
    g0 = C_GATE // D
    o_spec = pl.BlockSpec((tr, BW), lambda i: (i, 0))
    d_spec = pl.BlockSpec((tr, D), lambda i: (i, 0))
    w_spec = pl.BlockSpec((3, BW, D), lambda i: (0, 0, 0))

    def body(*refs):
        refs = _carry(rider, refs, 8, 9, 0, (T // tr,))
        dm_ref, oa_ref, ob_ref, oc_ref, w_ref, ga_ref, gb_ref, gc_ref = refs[:8]
        outs = refs[8:]
        dmv = dm_ref[...]
        for n, (o_ref, g_ref) in enumerate(((oa_ref, ga_ref), (ob_ref, gb_ref), (oc_ref, gc_ref))):
            bp = lax.dot_general(o_ref[...].astype(bf16), w_ref[n], _DIMS["nn"], preferred_element_type=f32)
            g = jax.nn.sigmoid(g_ref[...])
            dbp = (dmv * g).astype(bf16)
            outs[n][...] = (dmv * bp * g * (1.0 - g)).astype(bf16)
            outs[3 + n][...] = lax.dot_general(dbp, w_ref[n], _DIMS["nt"], preferred_element_type=f32)
            outs[6 + n][...] = dbp
        _carry_finish(rider, (T // tr,))

    gspecs = [pl.BlockSpec((tr, D), lambda i, n=n: (i, g0 + n)) for n in range(3)]
    return _carrier_call(
        body, rider, [dm, oa, ob, oc, w_br, proj, proj, proj], [d_spec, o_spec, o_spec, o_spec, w_spec, *gspecs],
        [SDS((T, D), bf16)] * 3 + [SDS((T, BW), f32)] * 3 + [SDS((T, D), bf16)] * 3,
        [d_spec] * 3 + [o_spec] * 3 + [d_spec] * 3, [], (T // tr,), name)


def _loss(h, target, name):
    T, D = h.shape
    tr = _pick(T, 256)
    nt = T // tr
    row = pl.BlockSpec((tr, D), lambda i: (i, 0))

    def body(h_ref, t_ref, l_ref, d_ref, acc):
        i = pl.program_id(0)

        @pl.when(i == 0)
        def _():
            acc[...] = jnp.zeros_like(acc)

        e = h_ref[...] - t_ref[...]
        d_ref[...] = e * (1.0 / D)
        acc[...] += jnp.sum(e * e, axis=0, keepdims=True)

        @pl.when(i == nt - 1)
        def _():
            l_ref[...] = jnp.sum(acc[...], axis=-1, keepdims=True) * (0.5 / D)

    return pl.pallas_call(
        body, grid=(nt,), in_specs=[row, row], out_specs=[pl.BlockSpec((1, 1), lambda i: (0, 0)), row],
        out_shape=[SDS((1, 1), f32), SDS((T, D), f32)], scratch_shapes=[pltpu.VMEM((1, D), f32)], name=name,
        compiler_params=_params(("arbitrary",)))(h, target)


def _adamw(w, m, v, parts, name):
    shape = w.shape
    C = shape[-1]
    R = int(np.prod(shape[:-1])) if len(shape) > 1 else 1
    n_parts, Cp = parts.shape[0], parts.shape[-1]
    tr = _pick(R, 256)
    spec = pl.BlockSpec((tr, C), lambda i: (i, 0))

    def body(w_ref, m_ref, v_ref, p_ref, g_ref, d_ref, mo_ref, vo_ref):
        gv = p_ref[0].astype(f32)
        for q in range(1, n_parts):
            gv = gv + p_ref[q].astype(f32)
        gv = gv[:, :C]
        mn = ADAM_B1 * m_ref[...] + (1.0 - ADAM_B1) * gv
        vn = ADAM_B2 * v_ref[...] + (1.0 - ADAM_B2) * (gv * gv)
        m_hat = mn / (1.0 - ADAM_B1 ** ADAM_STEP)
        v_hat = vn / (1.0 - ADAM_B2 ** ADAM_STEP)
        g_ref[...] = gv
        d_ref[...] = -ADAM_LR * (m_hat / (jnp.sqrt(v_hat) + ADAM_EPS) + ADAM_WD * w_ref[...])
        mo_ref[...] = mn
        vo_ref[...] = vn

    outs = pl.pallas_call(
        body, grid=(R // tr,), in_specs=[spec] * 3 + [pl.BlockSpec((n_parts, tr, Cp), lambda i: (0, i, 0))], out_specs=[spec] * 4,
        out_shape=[SDS((R, C), f32)] * 4, name=name,
        compiler_params=_params(("parallel",)))(*[a.reshape(R, C) for a in (w, m, v)], parts.reshape(n_parts, R, Cp))
    return [o.reshape(shape) for o in outs]


def _coords():
    return lax.axis_index("x"), lax.axis_index("y"), lax.axis_index("c")


def _window(ref, axis, idx, n):
    start = pl.multiple_of(idx * n, n)
    if axis == 1:
        return ref.at[:, pl.ds(start, n), :]
    return ref.at[:, :, pl.ds(start, n)]


class _Rider:
    def __init__(self, inputs, out_shapes, aliases, scratch, start, finish):
        self.inputs, self.out_shapes, self.aliases, self.scratch = list(inputs), list(out_shapes), dict(aliases), list(scratch)
        self.start, self.finish = start, finish


def _join_riders(riders):
    riders = [r for r in riders if r is not None]
    if len(riders) < 2:
        return riders[0] if riders else None
    ni = np.cumsum([0] + [len(r.inputs) for r in riders])
    no = np.cumsum([0] + [len(r.out_shapes) for r in riders])
    ns = np.cumsum([0] + [len(r.scratch) for r in riders])
    aliases = {int(ni[k]) + i: int(no[k]) + o for k, r in enumerate(riders) for i, o in r.aliases.items()}

    def phase(which):
        def run(ins, outs, scr):
            for k, r in enumerate(riders):
                getattr(r, which)(ins[ni[k]:ni[k + 1]], outs[no[k]:no[k + 1]], scr[ns[k]:ns[k + 1]])
        return run

    return _Rider(sum([r.inputs for r in riders], []), sum([r.out_shapes for r in riders], []), aliases,
                  sum([r.scratch for r in riders], []), phase("start"), phase("finish"))


def _run_rider(r, name):
    ni, no = len(r.inputs), len(r.out_shapes)

    def body(*refs):
        ins, outs, scr = refs[:ni], refs[ni:ni + no], refs[ni + no:]
        r.start(ins, outs, scr)
        r.finish(ins, outs, scr)

    any_spec = pl.BlockSpec(memory_space=pl.ANY)
    return pl.pallas_call(body, out_shape=r.out_shapes, in_specs=[any_spec] * ni, out_specs=[any_spec] * no,
                          input_output_aliases=r.aliases, scratch_shapes=r.scratch, name=name)(*r.inputs)


def _gather_rider(shards, axes):
    nt = len(shards)
    sizes = [s.shape[a] for s, a in zip(shards, axes)]
    out_shapes = [SDS(tuple(d * N_DEV if i == a else d for i, d in enumerate(s.shape)), s.dtype) for s, a in zip(shards, axes)]

    def plan(x_refs, out_refs, scr):
        send_sems, recv_sems, local_sems = scr
        x, y, c = _coords()
        me, sibling = (x, y, c), (x, y, 1 - c)
        chips = [(1 - x, y), (x, 1 - y), (1 - x, 1 - y)]

        def win(t, px, py, pc):
            return _window(out_refs[t], axes[t], 4 * px + 2 * py + pc, sizes[t])

        def copy(t, k, block, to, src=None):
            return pltpu.make_async_remote_copy(
                src_ref=win(t, *block) if src is None else src, dst_ref=win(t, *block),
                send_sem=send_sems.at[7 * t + k], recv_sem=recv_sems.at[7 * t + k], device_id=to, device_id_type=MESH)

        mine = [pltpu.make_async_copy(x_refs[t], win(t, *me), local_sems.at[t]) for t in range(nt)]
        first = []
        for t in range(nt):
            first.append(copy(t, 0, me, sibling, src=x_refs[t]))
            first += [copy(t, 1 + j, me, (*chip, c), src=x_refs[t]) for j, chip in enumerate(chips)]
        return c, me, sibling, chips, copy, mine, first

    def start(x_refs, out_refs, scr):
        _, _, _, _, _, mine, first = plan(x_refs, out_refs, scr)
        for cp in mine + first:
            cp.start()

    def finish(x_refs, out_refs, scr):
        c, me, sibling, chips, copy, mine, first = plan(x_refs, out_refs, scr)
        passed = []
        for j, chip in enumerate(chips):
            for t in range(nt):
                copy(t, 1 + j, (*chip, c), me).wait_recv()
                fwd = copy(t, 4 + j, (*chip, c), sibling)
                fwd.start()
                passed.append(fwd)
        for t in range(nt):
            copy(t, 0, sibling, me).wait_recv()
            for j, chip in enumerate(chips):
                copy(t, 4 + j, (*chip, 1 - c), me).wait_recv()
        for cp in first + passed:
            cp.wait_send()
        for cp in mine:
            cp.wait()

    scratch = [pltpu.SemaphoreType.DMA((7 * nt,)), pltpu.SemaphoreType.DMA((7 * nt,)), pltpu.SemaphoreType.DMA((nt,))]
    return _Rider(shards, out_shapes, {}, scratch, start, finish)


N_CHIP = N_DEV // 2


def _pair_exchange_rider(grads, axes, shards):
    nt = len(grads)
    sizes = [s[a] for s, a in zip(shards, axes)]

    def plan(ins, outs, scr):
        send_sems, recv_sems = scr
        x, y, c = _coords()
        return [pltpu.make_async_remote_copy(
            src_ref=_window(ins[t], axes[t], 2 * q + (1 - c), sizes[t]), dst_ref=outs[t].at[q],
            send_sem=send_sems.at[N_CHIP * t + q], recv_sem=recv_sems.at[N_CHIP * t + q],
            device_id=(x, y, 1 - c), device_id_type=MESH) for t in range(nt) for q in range(N_CHIP)]

    def start(ins, outs, scr):
        for cp in plan(ins, outs, scr):
            cp.start()

    def finish(ins, outs, scr):
        copies = plan(ins, outs, scr)
        for cp in copies:
            cp.wait_recv()
        for cp in copies:
            cp.wait_send()

    scratch = [pltpu.SemaphoreType.DMA((N_CHIP * nt,)), pltpu.SemaphoreType.DMA((N_CHIP * nt,))]
    return _Rider(grads, [SDS((N_CHIP,) + tuple(s), g.dtype) for s, g in zip(shards, grads)], {}, scratch, start, finish)


def _pair_sum(grad, buf, axis, core, name):
    _, G, r, c = buf.shape
    tr = _pick(r, 512)
    nr = r // tr

    def body(core_ref, g_ref, b_ref, o_ref):
        o_ref[0, 0] = (g_ref[0].astype(f32) + b_ref[0, 0].astype(f32)).astype(bf16)

    if axis == 2:
        g_spec = pl.BlockSpec((1, tr, c), lambda q, g, i, core_ref: (g, i, 2 * q + core_ref[0]))
    else:
        g_spec = pl.BlockSpec((1, tr, c), lambda q, g, i, core_ref: (g, (2 * q + core_ref[0]) * nr + i, 0))
    blk = pl.BlockSpec((1, 1, tr, c), lambda q, g, i, core_ref: (q, g, i, 0))
    grid_spec = pltpu.PrefetchScalarGridSpec(num_scalar_prefetch=1, grid=(N_CHIP, G, nr), in_specs=[g_spec, blk], out_specs=blk)
    return pl.pallas_call(body, grid_spec=grid_spec, out_shape=SDS(buf.shape, bf16), name=name,
                          compiler_params=_params(("parallel", "parallel", "parallel")))(core, grad, buf)


def _chip_exchange_rider(chips, slots, layer):
    nt = len(chips)

    def plan(ins, outs, scr):
        send_sems, recv_sems, local_sems = scr
        x, y, c = _coords()
        me = 2 * x + y
        mine = [pltpu.make_async_copy(ins[t].at[me], outs[t].at[me, layer], local_sems.at[t]) for t in range(nt)]
        copies = []
        for k in range(1, N_CHIP):
            px, py = lax.rem(x + (k >> 1), 2), lax.rem(y + (k & 1), 2)
            copies += [pltpu.make_async_remote_copy(
                src_ref=ins[t].at[2 * px + py], dst_ref=outs[t].at[me, layer],
                send_sem=send_sems.at[3 * t + k - 1], recv_sem=recv_sems.at[3 * t + k - 1],
                device_id=(px, py, c), device_id_type=MESH) for t in range(nt)]
        return mine, copies

    def start(ins, outs, scr):
        mine, copies = plan(ins, outs, scr)
        for cp in mine + copies:
            cp.start()

    def finish(ins, outs, scr):
        mine, copies = plan(ins, outs, scr)
        for cp in copies:
            cp.wait_recv()
        for cp in copies:
            cp.wait_send()
        for cp in mine:
            cp.wait()

    scratch = [pltpu.SemaphoreType.DMA((3 * nt,)), pltpu.SemaphoreType.DMA((3 * nt,)), pltpu.SemaphoreType.DMA((nt,))]
    return _Rider(list(chips) + list(slots), [SDS(s.shape, s.dtype) for s in slots], {nt + t: t for t in range(nt)}, scratch,
                  start, finish)


_BIG = (("w_in", 2), ("w_uq", 2), ("w_ukv", 2), ("w_mem_kv", 1), ("w_br", 2), ("w_out", 1), ("w_ff1", 2), ("w_ff2", 1))


def _in_col_maps(D):
    d_in = O_GATE + 3 * D
    sh = d_in // N_DEV
    shp = _round_up(sh, LANES)
    perm_of_orig = np.empty(d_in, np.int64)
    for a, b, dst in ((0, O_F, C_Q), (O_F, O_CQ, C_F), (O_CQ, O_CKV, C_CQ), (O_CKV, O_KR, C_CKV), (O_KR, O_QM, C_KPE + NOPE),
                      (O_QM, O_GATE, C_QM), (O_GATE, d_in, C_GATE)):
        perm_of_orig[a:b] = dst + np.arange(b - a)
    orig = np.arange(d_in)
    gath_of_orig = (orig // sh) * shp + orig % sh
    fwd = -np.ones(C_GATE + 3 * D, np.int64)
    fwd[perm_of_orig] = gath_of_orig
    bwd = -np.ones(N_DEV * shp, np.int64)
    bwd[gath_of_orig] = perm_of_orig
    return fwd, bwd


def _lane_pad(a):
    c = a.shape[-1]
    return jnp.pad(a, [(0, 0)] * (a.ndim - 1) + [(0, _round_up(c, LANES) - c)])


def _as3(a):
    return a.reshape((-1,) + a.shape[-2:]) if a.ndim != 3 else a


def _rope_tables(positions):
    inv_freq = ROPE_BASE ** (-jnp.arange(0, ROPE, 2, dtype=f32) / ROPE)
    ang = positions.astype(f32).reshape(-1, 1) * inv_freq
    cos, sin = jnp.cos(ang), jnp.sin(ang)
    T = ang.shape[0]
    one, zero = jnp.ones((T, NOPE), f32), jnp.zeros((T, 16), f32)
    c = jnp.concatenate([one, cos, cos, jnp.ones((T, 32), f32)], axis=1)
    sa = jnp.concatenate([jnp.zeros((T, NOPE), f32), -sin, zero, jnp.zeros((T, 32), f32)], axis=1)
    sb = jnp.concatenate([jnp.zeros((T, NOPE), f32), zero, sin, jnp.zeros((T, 32), f32)], axis=1)
    return c, sa, sb


def kernel(x, mem, positions, ln_in_g, ln_in_b, w_in, b_forget, w_uq, g_cq, w_ukv, g_ckv, w_mem_kv, w_br, w_out, ln1_g, ln1_b, w_ff1, w_ff2, ln2_g, ln2_b, loss_target, m_ln_in_g, m_ln_in_b, m_w_in, m_b_forget, m_w_uq, m_g_cq, m_w_ukv, m_g_ckv, m_w_mem_kv, m_w_br, m_w_out, m_ln1_g, m_ln1_b, m_w_ff1, m_w_ff2, m_ln2_g, m_ln2_b, v_ln_in_g, v_ln_in_b, v_w_in, v_b_forget, v_w_uq, v_g_cq, v_w_ukv, v_g_ckv, v_w_mem_kv, v_w_br, v_w_out, v_ln1_g, v_ln1_b, v_w_ff1, v_w_ff2, v_ln2_g, v_ln2_b):
    B, S, D = x.shape
    T = B * S
    L = w_in.shape[0]
    NM = mem.shape[1]
    alpha = (2 * L) ** 0.25
    weights = dict(ln_in_g=ln_in_g, ln_in_b=ln_in_b, w_in=w_in, b_forget=b_forget, w_uq=w_uq, g_cq=g_cq, w_ukv=w_ukv, g_ckv=g_ckv,
                   w_mem_kv=w_mem_kv, w_br=w_br, w_out=w_out, ln1_g=ln1_g, ln1_b=ln1_b, w_ff1=w_ff1, w_ff2=w_ff2, ln2_g=ln2_g, ln2_b=ln2_b)
    mom_m = dict(ln_in_g=m_ln_in_g, ln_in_b=m_ln_in_b, w_in=m_w_in, b_forget=m_b_forget, w_uq=m_w_uq, g_cq=m_g_cq, w_ukv=m_w_ukv,
                 g_ckv=m_g_ckv, w_mem_kv=m_w_mem_kv, w_br=m_w_br, w_out=m_w_out, ln1_g=m_ln1_g, ln1_b=m_ln1_b, w_ff1=m_w_ff1,
                 w_ff2=m_w_ff2, ln2_g=m_ln2_g, ln2_b=m_ln2_b)
    mom_v = dict(ln_in_g=v_ln_in_g, ln_in_b=v_ln_in_b, w_in=v_w_in, b_forget=v_b_forget, w_uq=v_w_uq, g_cq=v_g_cq, w_ukv=v_w_ukv,
                 g_ckv=v_g_ckv, w_mem_kv=v_w_mem_kv, w_br=v_w_br, w_out=v_w_out, ln1_g=v_ln1_g, ln1_b=v_ln1_b, w_ff1=v_w_ff1,
                 w_ff2=v_w_ff2, ln2_g=v_ln2_g, ln2_b=v_ln2_b)
    order = list(weights)
    big_axes = [ax for _, ax in _BIG]
    fwd_map, bwd_map = _in_col_maps(D)
    shard3 = {n: _lane_pad(weights[n]) for n, _ in _BIG}

    half = len(_BIG) // 2

    def gather_rider(l, lo=0, hi=len(_BIG)):
        return _gather_rider([_as3(shard3[n][l]).astype(bf16) for n, _ in _BIG[lo:hi]], big_axes[lo:hi])

    def named(full, lo, hi):
        return {n: (f if n == "w_br" else f[0]) for (n, _), f in zip(_BIG[lo:hi], full)}

    tabs = _rope_tables(positions)
    x2d, mem2d, tgt2d = x.reshape(T, D), mem.reshape(B * NM, D), loss_target.reshape(T, D)
    fox = dict(B=B, Sq=S, Sk=S, nblk=FOX_HEADS // 2, hb=2, dqk=FOX_HD, dv=FOX_HD, mode="fox", scale=FOX_HD ** -0.5)
    mla_scale = (NOPE + ROPE) ** -0.5
    mla = dict(B=B, Sq=S, Sk=S, nblk=MLA_HEADS // 2, hb=2, dqk=LANES, dv=MLA_VD, mode="chunk", scale=1.0)
    memat = dict(B=B, Sq=S, Sk=NM, nblk=MEM_HEADS, hb=1, dqk=MEM_HD, dv=MEM_HD, mode="none", scale=MEM_HD ** -0.5)
    nb = FOX_HEADS // 2

    h, hb, xh0, rstd0 = _ln_fwd(None, x2d, ln_in_g, ln_in_b, alpha, "ln_in")
    saved = []
    first = _run_rider(gather_rider(0, 0, half), "weights_all_gather")
    for l in range(L):
        W = named(first, 0, half)
        W["w_in"] = _lane_permute(W["w_in"], fwd_map, bf16, "w_in_reorder")
        bpad = jnp.pad(b_forget[l].reshape(1, FOX_HEADS), ((0, 0), (0, LANES - FOX_HEADS)))
        proj = _mm(hb, W["w_in"], name="proj")
        cum = _gate_fwd(proj, bpad, B, S, "gate_fwd")
        cum_r = cum[:, :FOX_HEADS].reshape(B, S, nb, 2).transpose(0, 2, 3, 1).reshape(B * nb, 2, S)
        more = l + 1 < L
        (o_a, lse_a), second = _attn_fwd(proj, C_Q // LANES, proj, C_K // LANES, proj, C_V // LANES, cum_r=cum_r, name="fox_fwd",
                                         rider=gather_rider(l, half), **fox)
        W.update(named(second, half, len(_BIG)))
        cqn = _rms_fwd(proj, C_CQ, Q_RANK, g_cq[l], "rms_q")
        ckvn = _rms_fwd(proj, C_CKV, KV_RANK, g_ckv[l], "rms_kv")
        qraw = _mm(cqn, W["w_uq"], name="q_up")
        kv = _mm(ckvn, W["w_ukv"], name="kv_up")
        qf = _rope_q(qraw, tabs, bwd=False, mul=mla_scale, name="rope_q")
        kf, v_b = _rope_k(kv, tabs, proj, "rope_k")
        res = _attn_fwd(qf, 0, kf, 0, v_b, 0, name="mla_fwd", rider=gather_rider(l + 1, 0, half) if more else None, **mla)
        (o_b, lse_b), first = res if more else (res, None)
        mkv = _mm(mem2d, W["w_mem_kv"], name="mem_kv")
        o_c, lse_c = _attn_fwd(proj, C_QM // MEM_HD, mkv, 0, mkv, MEM_HEADS, name="mem_fwd", **memat)
        merged = _merge_fwd(o_a, o_b, o_c, W["w_br"], proj, D, "merge_fwd")
        y = _mm(merged, W["w_out"], name="out_proj")
        h1, h1b, xh1, rstd1 = _ln_fwd(h, y, ln1_g[l], ln1_b[l], alpha, "ln1")
        a, r = _mm(h1b, W["w_ff1"], epi="relu2", name="ff1")
        ff = _mm(a, W["w_ff2"], name="ff2")
        h2, h2b, xh2, rstd2 = _ln_fwd(h1, ff, ln2_g[l], ln2_b[l], alpha, "ln2")
        saved.append(dict(W=W, bpad=bpad, h=hb, proj=proj, cum_r=cum_r, o_a=o_a, lse_a=lse_a, cqn=cqn, ckvn=ckvn, qf=qf, kf=kf,
                          v_b=v_b, o_b=o_b, lse_b=lse_b, mkv=mkv, o_c=o_c, lse_c=lse_c, merged=merged, h1=h1b, xh1=xh1,
                          rstd1=rstd1, a=a, r=r, xh2=xh2, rstd2=rstd2))
        h, hb = h2, h2b

    loss_local, dh = _loss(h, tgt2d, "loss")
    loss = lax.psum(loss_local[0, 0], ("x", "y", "c"))

    shard_shapes = [_as3(shard3[n][0]).shape for n, _ in _BIG]
    slots = [lax.empty((N_CHIP, L) + s, bf16) for s in shard_shapes]
    core = lax.axis_index("c").astype(jnp.int32).reshape(1)
    small = {n: [None] * L for n in ("b_forget", "g_cq", "g_ckv", "ln1_g", "ln1_b", "ln2_g", "ln2_b")}
    ax_a, ax_c = big_axes[:half], big_axes[half:]
    sh_a, sh_c = shard_shapes[:half], shard_shapes[half:]

    def pair_sums(gs, bufs, axes):
        return [_pair_sum(g, b, ax, core, "grads_pair_sum") for g, b, ax in zip(gs, bufs, axes)]

    pending = unpaired = None
    for l in reversed(range(L)):
        sv = saved[l]
        W = sv["W"]
        du2, du2b, dg2, db2 = _ln_bwd(dh, sv["xh2"], sv["rstd2"], ln2_g[l], "ln2_bwd")
        dz = _mm(du2b, W["w_ff2"], mode="nt", epi="mul", extra=sv["r"], out_dtype=bf16, name="ff2_dx")
        dw_ff2 = _mm(sv["a"], du2b, mode="tn", out_dtype=bf16, name="ff2_dw")
        dh1 = _mm(dz, W["w_ff1"], mode="nt", epi="add", extra=du2, alpha=alpha, name="ff1_dx")
        dw_ff1 = _mm(sv["h1"], dz, mode="tn", out_dtype=bf16, name="ff1_dw")
        du1, du1b, dg1, db1 = _ln_bwd(dh1, sv["xh1"], sv["rstd1"], ln1_g[l], "ln1_bwd")
        dmerged = _mm(du1b, W["w_out"], mode="nt", name="out_dx")
        dw_out = _mm(sv["merged"], du1b, mode="tn", out_dtype=bf16, name="out_dw")
        mb = _merge_bwd(dmerged, sv["o_a"], sv["o_b"], sv["o_c"], W["w_br"], sv["proj"], D, "merge_bwd",
                        rider=None if unpaired is None else _pair_exchange_rider(unpaired[0], ax_a, sh_a))
        if unpaired is not None:
            mb, bufs = mb
            pending = (pair_sums(unpaired[0], bufs, ax_a), unpaired[1])
        dgl, do, dbp = mb[0:3], mb[3:6], mb[6:9]
        dw_br = jnp.stack([_mm(o_n, dbp_n, mode="tn", out_dtype=bf16, name="br_dw") for o_n, dbp_n in zip((sv["o_a"], sv["o_b"], sv["o_c"]), dbp)])
        late = [_as3(g) for g in (dw_br, dw_out, dw_ff1, dw_ff2)]
        proj = sv["proj"]
        chip_rider = None if pending is None else _chip_exchange_rider(pending[0], slots[:half], pending[1])
        own, got = _attn_bwd(proj, C_Q // LANES, proj, C_K // LANES, proj, C_V // LANES, sv["o_a"], do[0], sv["lse_a"],
                             cum_r=sv["cum_r"], name="fox_bwd",
                             rider=_join_riders([chip_rider, _pair_exchange_rider(late, ax_c, sh_c)]), **fox)
        dq_a, dk_a, dv_a, dcum = own
        if chip_rider is not None:
            slots[:half], got = got[:half], got[half:]
        dcum = dcum.reshape(B, nb, S, 2).transpose(0, 2, 1, 3).reshape(T, FOX_HEADS)
        dcum = jnp.pad(dcum, ((0, 0), (0, LANES - FOX_HEADS)))
        dzf, dbf = _gate_bwd(dcum, proj, sv["bpad"], B, S, "gate_bwd")
        (dqf, dkf, dv_b), slots[half:] = _attn_bwd(
            sv["qf"], 0, sv["kf"], 0, sv["v_b"], 0, sv["o_b"], do[1], sv["lse_b"], dq_dtype=f32, name="mla_bwd",
            rider=_chip_exchange_rider(pair_sums(late, got, ax_c), slots[half:], l), **mla)
        dqraw = _rope_q(dqf, tabs, bwd=True, mul=mla_scale, name="rope_q_bwd")
        dkv, dkpe = _rope_k_bwd(dkf, dv_b, tabs, "rope_k_bwd")
        dw_uq = _mm(sv["cqn"], dqraw, mode="tn", out_dtype=bf16, name="q_up_dw")
        dcqn = _mm(dqraw, W["w_uq"], mode="nt", name="q_up_dx")
        dcq_, dg_cq = _rms_bwd(dcqn, proj, C_CQ, Q_RANK, g_cq[l], "rms_q_bwd")
        dw_ukv = _mm(sv["ckvn"], dkv, mode="tn", out_dtype=bf16, name="kv_up_dw")
        dckvn = _mm(dkv, W["w_ukv"], mode="nt", name="kv_up_dx")
        dckv, dg_ckv = _rms_bwd(dckvn, proj, C_CKV, KV_RANK, g_ckv[l], "rms_kv_bwd")
        dqm, dmk, dmv = _attn_bwd(proj, C_QM // MEM_HD, sv["mkv"], 0, sv["mkv"], MEM_HEADS, sv["o_c"], do[2], sv["lse_c"],
                                  name="mem_bwd", **memat)
        dw_mem = _mm(mem2d, jnp.concatenate([dmk, dmv], axis=1), mode="tn", out_dtype=bf16, name="mem_kv_dw")
        dproj = jnp.concatenate([dq_a, dk_a, dv_a, dzf, dkpe, dckv, dqm, jnp.zeros((T, C_CQ - C_PAD), bf16), dcq_, *dgl], axis=1)
        dw_in = _lane_permute(_mm(sv["h"], dproj, mode="tn", out_dtype=bf16, name="proj_dw"), bwd_map, bf16, "w_in_grad_reorder")
        dh = _mm(dproj, W["w_in"], mode="nt", epi="add", extra=du1, alpha=alpha, name="proj_dx")
        unpaired = ([_as3(g) for g in (dw_in, dw_uq, dw_ukv, dw_mem)], l)
        for n, val in (("b_forget", dbf[:, :FOX_HEADS]), ("g_cq", dg_cq), ("g_ckv", dg_ckv), ("ln1_g", dg1), ("ln1_b", db1),
                       ("ln2_g", dg2), ("ln2_b", db2)):
            small[n][l] = val.reshape(-1)
    bufs = _run_rider(_pair_exchange_rider(unpaired[0], ax_a, sh_a), "grads_pair_exchange")
    slots[:half] = _run_rider(_chip_exchange_rider(pair_sums(unpaired[0], bufs, ax_a), slots[:half], unpaired[1]),
                              "grads_chip_exchange")
    grad_x, _, dg_in, db_in = _ln_bwd(dh, xh0, rstd0, ln_in_g, "ln_in_bwd")

    small_list = [("ln_in_g", dg_in.reshape(-1)), ("ln_in_b", db_in.reshape(-1))] + [(n, jnp.stack(v).reshape(-1)) for n, v in small.items()]
    sflat = jnp.concatenate([v for _, v in small_list])
    n_small = sflat.shape[0]
    rs = _round_up(-(-n_small // LANES), 8)
    spacked = jnp.pad(sflat, (0, rs * LANES - n_small)).reshape(1, rs, LANES)
    sparts = _run_rider(_gather_rider([spacked], [1]), "small_all_gather")[0].reshape(N_DEV, rs * LANES)

    grads, deltas, new_m, new_v = {}, {}, {}, {}
    off = 0
    for n, vv in small_list:
        parts = sparts[:, off:off + vv.shape[0]].reshape((N_DEV,) + weights[n].shape)
        off += vv.shape[0]
        grads[n], deltas[n], new_m[n], new_v[n] = _adamw(weights[n], mom_m[n], mom_v[n], parts, "adamw_" + n)
    for (n, _), sl in zip(_BIG, slots):
        parts = sl.reshape((N_CHIP,) + weights[n].shape[:-1] + (sl.shape[-1],))
        grads[n], deltas[n], new_m[n], new_v[n] = _adamw(weights[n], mom_m[n], mom_v[n], parts, "adamw_" + n)
    return (loss, grad_x.reshape(B, S, D), *[grads[n] for n in order], *[deltas[n] for n in order], *[new_m[n] for n in order],
            *[new_v[n] for n in order])
```

```python
import functools

import numpy as np
import jax
import jax.numpy as jnp
from jax import lax
from jax.experimental import pallas as pl
from jax.experimental.pallas import tpu as pltpu

f32, bf16 = jnp.float32, jnp.bfloat16
SDS = jax.ShapeDtypeStruct

N_DEV = 8
MESH = pl.DeviceIdType.MESH
LANES = 128
VMEM_LIMIT = 56 * 1024 * 1024

FOX_HEADS, FOX_HD = 8, 64
MLA_HEADS, NOPE, ROPE, MLA_VD = 8, 64, 32, 64
Q_RANK, KV_RANK = 384, 256
ROPE_BASE = 10000.0
MEM_HEADS, MEM_HD = 4, 128
BW = 512
CHUNK = 64
LN_EPS, RMS_EPS, NEG_INF = 1e-5, 1e-6, -1e30
ADAM_LR, ADAM_B1, ADAM_B2, ADAM_EPS, ADAM_WD, ADAM_STEP = 0.001, 0.9, 0.999, 1e-08, 0.01, 10

ATT_TQ, ATT_TK = 512, 512

C_Q, C_K, C_V, C_F, C_KPE, C_CKV, C_QM, C_PAD, C_CQ, C_GATE = 0, 512, 1024, 1536, 1664, 1792, 2048, 2560, 2688, 3072
O_F, O_CQ, O_CKV, O_KR, O_QM, O_GATE = 1536, 1544, 1928, 2184, 2216, 2728


def _pick(dim, pref):
    if dim <= pref:
        return dim
    for c in (2048, 1024, 768, 512, 384, 256, 128, 64, 32, 16, 8):
        if c <= pref and dim % c == 0:
            return c
    return dim


def _round_up(n, m):
    return -(-n // m) * m


def _params(sem=None):
    return pltpu.CompilerParams(dimension_semantics=sem, vmem_limit_bytes=VMEM_LIMIT)


_DIMS = {"nn": (((1,), (0,)), ((), ())), "nt": (((1,), (1,)), ((), ())), "tn": (((0,), (0,)), ((), ()))}


def _mm(a, b, *, mode="nn", out_dtype=f32, epi="none", extra=None, alpha=1.0, tm=1024, tn=1024, tk=None, name):
    if mode == "nn":
        (M, K), (_, N) = a.shape, b.shape
    elif mode == "nt":
        (M, K), (N, _) = a.shape, b.shape
    else:
        (K, M), (_, N) = a.shape, b.shape
    tm, tn, tk = _pick(M, tm), _pick(N, tn), _pick(K, tk or 2048)
    nk = K // tk
    a_spec = pl.BlockSpec((tk, tm), lambda i, j, k: (k, i)) if mode == "tn" else pl.BlockSpec((tm, tk), lambda i, j, k: (i, k))
    b_spec = pl.BlockSpec((tn, tk), lambda i, j, k: (j, k)) if mode == "nt" else pl.BlockSpec((tk, tn), lambda i, j, k: (k, j))
    o_spec = pl.BlockSpec((tm, tn), lambda i, j, k: (i, j))
    n_out = 2 if epi == "relu2" else 1
    dims = _DIMS[mode]

    def body(*refs):
        a_ref, b_ref = refs[0], refs[1]
        e_ref = refs[2] if extra is not None else None
        outs = refs[2 + (extra is not None):2 + (extra is not None) + n_out]

        def finish(r):
            if epi == "none":
                outs[0][...] = r.astype(out_dtype)
            elif epi == "relu2":
                p = jnp.maximum(r, 0.0)
                outs[0][...] = (p * p).astype(bf16)
                outs[1][...] = (2.0 * p).astype(bf16)
            elif epi == "mul":
                outs[0][...] = (r * e_ref[...].astype(f32)).astype(out_dtype)
            else:
                outs[0][...] = (r + alpha * e_ref[...].astype(f32)).astype(out_dtype)

        part = lax.dot_general(a_ref[...].astype(bf16), b_ref[...].astype(bf16), dims, preferred_element_type=f32)
        if nk == 1:
            finish(part)
            return
        acc = refs[-1]
        k = pl.program_id(2)

        @pl.when(k == 0)
        def _():
            acc[...] = part

        @pl.when(k > 0)
        def _():
            acc[...] += part

        @pl.when(k == nk - 1)
        def _():
            finish(acc[...])

    ins, in_specs = [a, b], [a_spec, b_spec]
    if extra is not None:
        ins.append(extra)
        in_specs.append(o_spec)
    if epi == "relu2":
        out_shape, out_specs = [SDS((M, N), bf16), SDS((M, N), bf16)], [o_spec, o_spec]
    else:
        out_shape, out_specs = SDS((M, N), out_dtype), o_spec
    return pl.pallas_call(
        body, grid=(M // tm, N // tn, nk), in_specs=in_specs, out_specs=out_specs, out_shape=out_shape,
        scratch_shapes=[pltpu.VMEM((tm, tn), f32)] if nk > 1 else [], name=name,
        compiler_params=_params(("parallel", "parallel", "arbitrary")))(*ins)


def _lane_permute(src, cmap, out_dtype, name):
    R, Cs = src.shape
    cmap = np.asarray(cmap, np.int64)
    nb = cmap.shape[0] // LANES
    srcs = [sorted({int(c) // LANES for c in cmap[b * LANES:(b + 1) * LANES] if c >= 0}) for b in range(nb)]
    exact = src.dtype == f32
    tr = _pick(R, 256 if exact else 512)

    def body(src_ref, cm_ref, o_ref):
        rows = lax.broadcasted_iota(jnp.int32, (LANES, LANES), 0)
        for b in range(nb):
            tbl = cm_ref[:, b * LANES:(b + 1) * LANES]
            acc = jnp.zeros((tr, LANES), f32)
            for sb in srcs[b]:
                sel = (rows + sb * LANES) == tbl
                blk = src_ref[:, sb * LANES:(sb + 1) * LANES]
                if exact:
                    acc = acc + jnp.dot(blk, sel.astype(f32), precision=lax.Precision.HIGHEST, preferred_element_type=f32)
                else:
                    acc = acc + jnp.dot(blk, sel.astype(blk.dtype), preferred_element_type=f32)
            o_ref[:, b * LANES:(b + 1) * LANES] = acc.astype(out_dtype)

    return pl.pallas_call(
        body, grid=(R // tr,), in_specs=[pl.BlockSpec((tr, Cs), lambda i: (i, 0)), pl.BlockSpec((1, nb * LANES), lambda i: (0, 0))],
        out_specs=pl.BlockSpec((tr, nb * LANES), lambda i: (i, 0)), out_shape=SDS((R, nb * LANES), out_dtype), name=name,
        compiler_params=_params(("parallel",)))(src, jnp.asarray(cmap.astype(np.int32)).reshape(1, -1))


def _ln_fwd(h, y, g, b, alpha, name):
    T, D = y.shape
    tr = _pick(T, 1024)
    row = pl.BlockSpec((tr, D), lambda i: (i, 0))
    vec = pl.BlockSpec((1, D), lambda i: (0, 0))
    col = pl.BlockSpec((tr, 1), lambda i: (i, 0))

    def body(*refs):
        if h is None:
            y_ref, g_ref, b_ref, o_ref, ob_ref, xh_ref, rs_ref = refs
            u = y_ref[...]
        else:
            h_ref, y_ref, g_ref, b_ref, o_ref, ob_ref, xh_ref, rs_ref = refs
            u = alpha * h_ref[...] + y_ref[...]
        mu = jnp.mean(u, axis=-1, keepdims=True)
        d = u - mu
        var = jnp.mean(d * d, axis=-1, keepdims=True)
        rstd = lax.rsqrt(var + LN_EPS)
        xh = d * rstd
        xh_ref[...] = xh
        rs_ref[...] = rstd
        o = xh * g_ref[...] + b_ref[...]
        o_ref[...] = o
        ob_ref[...] = o.astype(bf16)

    ins = ([] if h is None else [h]) + [y, g.reshape(1, D), b.reshape(1, D)]
    specs = ([] if h is None else [row]) + [row, vec, vec]
    return pl.pallas_call(
        body, grid=(T // tr,), in_specs=specs, out_specs=[row, row, row, col],
        out_shape=[SDS((T, D), f32), SDS((T, D), bf16), SDS((T, D), f32), SDS((T, 1), f32)], name=name,
        compiler_params=_params(("parallel",)))(*ins)


def _ln_bwd(dy, xh, rstd, g, name):
    T, D = dy.shape
    tr = _pick(T, 1024)
    row = pl.BlockSpec((tr, D), lambda i: (i, 0))
    vec = pl.BlockSpec((1, D), lambda i: (0, 0))
    col = pl.BlockSpec((tr, 1), lambda i: (i, 0))

    def body(dy_ref, xh_ref, rs_ref, g_ref, du_ref, dub_ref, dg_ref, db_ref):
        @pl.when(pl.program_id(0) == 0)
        def _():
            dg_ref[...] = jnp.zeros_like(dg_ref)
            db_ref[...] = jnp.zeros_like(db_ref)

        dyv, xhv = dy_ref[...], xh_ref[...]
        dxh = dyv * g_ref[...]
        m1 = jnp.mean(dxh, axis=-1, keepdims=True)
        m2 = jnp.mean(dxh * xhv, axis=-1, keepdims=True)
        du = rs_ref[...] * (dxh - m1 - xhv * m2)
        du_ref[...] = du
        dub_ref[...] = du.astype(bf16)
        dg_ref[...] += jnp.sum(dyv * xhv, axis=0, keepdims=True)
        db_ref[...] += jnp.sum(dyv, axis=0, keepdims=True)

    return pl.pallas_call(
        body, grid=(T // tr,), in_specs=[row, row, col, vec], out_specs=[row, row, vec, vec],
        out_shape=[SDS((T, D), f32), SDS((T, D), bf16), SDS((1, D), f32), SDS((1, D), f32)], name=name,
        compiler_params=_params(("arbitrary",)))(dy, xh, rstd, g.reshape(1, D))


def _rms_fwd(proj, col_off, R, g, name):
    T = proj.shape[0]
    tr = _pick(T, 512)
    cb = col_off // R

    def body(c_ref, g_ref, o_ref):
        c = c_ref[...]
        r = lax.rsqrt(jnp.mean(c * c, axis=-1, keepdims=True) + RMS_EPS)
        o_ref[...] = (c * r * g_ref[...]).astype(bf16)

    return pl.pallas_call(
        body, grid=(T // tr,), in_specs=[pl.BlockSpec((tr, R), lambda i: (i, cb)), pl.BlockSpec((1, R), lambda i: (0, 0))],
        out_specs=pl.BlockSpec((tr, R), lambda i: (i, 0)), out_shape=SDS((T, R), bf16), name=name,
        compiler_params=_params(("parallel",)))(proj, g.reshape(1, R))


def _rms_bwd(dy, proj, col_off, R, g, name):
    T = proj.shape[0]
    tr = _pick(T, 512)
    cb = col_off // R

    def body(dy_ref, c_ref, g_ref, dc_ref, dg_ref):
        @pl.when(pl.program_id(0) == 0)
        def _():
            dg_ref[...] = jnp.zeros_like(dg_ref)

        c, dyv = c_ref[...], dy_ref[...]
        r = lax.rsqrt(jnp.mean(c * c, axis=-1, keepdims=True) + RMS_EPS)
        t = dyv * g_ref[...]
        mt = jnp.mean(t * c, axis=-1, keepdims=True)
        dc_ref[...] = (r * t - c * (r * r * r) * mt).astype(bf16)
        dg_ref[...] += jnp.sum(dyv * c * r, axis=0, keepdims=True)

    return pl.pallas_call(
        body, grid=(T // tr,),
        in_specs=[pl.BlockSpec((tr, R), lambda i: (i, 0)), pl.BlockSpec((tr, R), lambda i: (i, cb)), pl.BlockSpec((1, R), lambda i: (0, 0))],
        out_specs=[pl.BlockSpec((tr, R), lambda i: (i, 0)), pl.BlockSpec((1, R), lambda i: (0, 0))],
        out_shape=[SDS((T, R), bf16), SDS((1, R), f32)], name=name,
        compiler_params=_params(("arbitrary",)))(dy, proj, g.reshape(1, R))


def _tri(n, lower):
    r = lax.broadcasted_iota(jnp.int32, (n, n), 0)
    c = lax.broadcasted_iota(jnp.int32, (n, n), 1)
    return ((r >= c) if lower else (c >= r)).astype(f32)


def _gate_fwd(proj, bpad, B, S, name):
    ch = _pick(S, 256)
    nch = S // ch
    cb = C_F // LANES

    def body(f_ref, b_ref, o_ref):
        tri = _tri(ch, True)
        carry = jnp.zeros((1, LANES), f32)
        for c in range(nch):
            z = f_ref[c * ch:(c + 1) * ch, :] + b_ref[...]
            lf = jnp.minimum(z, 0.0) - jnp.log1p(jnp.exp(-jnp.abs(z)))
            cs = jnp.dot(tri, lf, precision=lax.Precision.HIGHEST, preferred_element_type=f32) + carry
            o_ref[c * ch:(c + 1) * ch, :] = cs
            carry = cs[ch - 1:ch, :]

    return pl.pallas_call(
        body, grid=(B,), in_specs=[pl.BlockSpec((S, LANES), lambda b: (b, cb)), pl.BlockSpec((1, LANES), lambda b: (0, 0))],
        out_specs=pl.BlockSpec((S, LANES), lambda b: (b, 0)), out_shape=SDS((B * S, LANES), f32), name=name,
        compiler_params=_params(("parallel",)))(proj, bpad)


def _gate_bwd(dcum, proj, bpad, B, S, name):
    ch = _pick(S, 256)
    nch = S // ch
    cb = C_F // LANES

    def body(d_ref, f_ref, b_ref, dz_ref, db_ref):
        @pl.when(pl.program_id(0) == 0)
        def _():
            db_ref[...] = jnp.zeros_like(db_ref)

        tri = _tri(ch, False)
        carry = jnp.zeros((1, LANES), f32)
        dbs = jnp.zeros((1, LANES), f32)
        for c in reversed(range(nch)):
            d = d_ref[c * ch:(c + 1) * ch, :]
            dlf = jnp.dot(tri, d, precision=lax.Precision.HIGHEST, preferred_element_type=f32) + carry
            carry = carry + jnp.sum(d, axis=0, keepdims=True)
            z = f_ref[c * ch:(c + 1) * ch, :] + b_ref[...]
            dz = dlf / (1.0 + jnp.exp(z))
            dz_ref[c * ch:(c + 1) * ch, :] = dz.astype(bf16)
            dbs = dbs + jnp.sum(dz, axis=0, keepdims=True)
        db_ref[...] += dbs

    return pl.pallas_call(
        body, grid=(B,),
        in_specs=[pl.BlockSpec((S, LANES), lambda b: (b, 0)), pl.BlockSpec((S, LANES), lambda b: (b, cb)), pl.BlockSpec((1, LANES), lambda b: (0, 0))],
        out_specs=[pl.BlockSpec((S, LANES), lambda b: (b, 0)), pl.BlockSpec((1, LANES), lambda b: (0, 0))],
        out_shape=[SDS((B * S, LANES), bf16), SDS((1, LANES), f32)], name=name,
        compiler_params=_params(("arbitrary",)))(dcum, proj, bpad)


def _rot(v, c, sa, sb):
    return v * c + pltpu.roll(v, LANES - 16, 1) * sa + pltpu.roll(v, 16, 1) * sb


def _rot_t(v, c, sa, sb):
    return v * c + pltpu.roll(v * sa, 16, 1) + pltpu.roll(v * sb, LANES - 16, 1)


def _rope_q(x, tabs, *, bwd, mul, name):
    T = x.shape[0]
    W = MLA_HEADS * LANES
    tr = _pick(T, 512)
    fn = _rot_t if bwd else _rot

    def body(x_ref, c_ref, sa_ref, sb_ref, o_ref):
        c, sa, sb = c_ref[...], sa_ref[...], sb_ref[...]
        for g in range(MLA_HEADS):
            o_ref[:, g * LANES:(g + 1) * LANES] = (fn(x_ref[:, g * LANES:(g + 1) * LANES], c, sa, sb) * mul).astype(bf16)

    tab = pl.BlockSpec((tr, LANES), lambda i: (i, 0))
    wide = pl.BlockSpec((tr, W), lambda i: (i, 0))
    return pl.pallas_call(body, grid=(T // tr,), in_specs=[wide, tab, tab, tab], out_specs=wide, out_shape=SDS((T, W), bf16),
                          name=name, compiler_params=_params(("parallel",)))(x, *tabs)


def _rope_k(kv, tabs, proj, name):
    T = kv.shape[0]
    W = MLA_HEADS * LANES
    tr = _pick(T, 512)
    kcb = C_KPE // LANES

    def body(x_ref, c_ref, sa_ref, sb_ref, kpe_ref, k_ref, v_ref):
        c, sa, sb, kpe = c_ref[...], sa_ref[...], sb_ref[...], kpe_ref[...]
        low = lax.broadcasted_iota(jnp.int32, (tr, LANES), 1) < NOPE
        halves = []
        for g in range(MLA_HEADS):
            x = x_ref[:, g * LANES:(g + 1) * LANES]
            k_ref[:, g * LANES:(g + 1) * LANES] = _rot(jnp.where(low, x, 0.0) + kpe, c, sa, sb).astype(bf16)
            halves.append(x[:, NOPE:])
            if g % 2 == 1:
                v_ref[:, (g // 2) * LANES:(g // 2 + 1) * LANES] = jnp.concatenate(halves, axis=1).astype(bf16)
                halves = []

    tab = pl.BlockSpec((tr, LANES), lambda i: (i, 0))
    wide = pl.BlockSpec((tr, W), lambda i: (i, 0))
    return pl.pallas_call(
        body, grid=(T // tr,), in_specs=[wide, tab, tab, tab, pl.BlockSpec((tr, LANES), lambda i: (i, kcb))],
        out_specs=[wide, pl.BlockSpec((tr, W // 2), lambda i: (i, 0))], out_shape=[SDS((T, W), bf16), SDS((T, W // 2), bf16)],
        name=name, compiler_params=_params(("parallel",)))(kv, *tabs, proj)


def _rope_k_bwd(dk, dv, tabs, name):
    T = dk.shape[0]
    W = MLA_HEADS * LANES
    tr = _pick(T, 512)

    def body(dk_ref, dv_ref, c_ref, sa_ref, sb_ref, o_ref, s_ref):
        c, sa, sb = c_ref[...], sa_ref[...], sb_ref[...]
        tot = jnp.zeros((tr, LANES), f32)
        for g in range(MLA_HEADS):
            d = _rot_t(dk_ref[:, g * LANES:(g + 1) * LANES], c, sa, sb)
            tot = tot + d
            o_ref[:, g * LANES:(g + 1) * LANES] = jnp.concatenate(
                [d[:, :NOPE].astype(bf16), dv_ref[:, g * MLA_VD:(g + 1) * MLA_VD]], axis=1)
        s_ref[...] = tot.astype(bf16)

    tab = pl.BlockSpec((tr, LANES), lambda i: (i, 0))
    wide = pl.BlockSpec((tr, W), lambda i: (i, 0))
    return pl.pallas_call(
        body, grid=(T // tr,), in_specs=[wide, pl.BlockSpec((tr, W // 2), lambda i: (i, 0)), tab, tab, tab],
        out_specs=[wide, tab], out_shape=[SDS((T, W), bf16), SDS((T, LANES), bf16)], name=name,
        compiler_params=_params(("parallel",)))(dk, dv, *tabs)


def _tile_mask(mode, r0, c0, nr, nc):
    r = r0 + lax.broadcasted_iota(jnp.int32, (nr, nc), 0)
    c = c0 + lax.broadcasted_iota(jnp.int32, (nr, nc), 1)
    if mode == "chunk":
        return (c // CHUNK) <= (r // CHUNK)
    return c <= r


def _diagonal(block, carry, i, tq, tk):
    nfull = (i * tq) // tk
    for d in range(max(1, tq // tk)):
        carry = block((nfull + d) * tk, tk, 0, tq, carry, True)
    return carry


def _attn_tiles(Sq, Sk):
    tk = _pick(Sk, ATT_TK)
    tq = _pick(Sq, ATT_TQ)
    assert (tk % tq == 0 or tq % tk == 0) and tq % CHUNK == 0 and tk % CHUNK == 0
    return tq, tk


def _stage_heads(dst, src_ref, hb, w, mark=None, mul=1.0):
    S = src_ref.shape[0]
    ch = _pick(S, 256)
    assert w <= LANES and (mark is None or w <= mark < LANES)

    def step(c, _):
        rows = pl.ds(pl.multiple_of(c * ch, ch), ch)
        for hh in range(hb):
            x = src_ref[rows, hh * w:(hh + 1) * w]
            x = (x.astype(f32) * mul).astype(bf16) if mul != 1.0 else x.astype(bf16)
            if w < LANES:
                pad = jnp.zeros((ch, LANES - w), bf16)
                if mark is not None:
                    lane = lax.broadcasted_iota(jnp.int32, (ch, LANES - w), 1)
                    pad = jnp.where(lane == mark - w, 1.0, 0.0).astype(bf16)
                x = jnp.concatenate([x, pad], axis=1)
            dst[hh, rows, :] = x
        return 0

    lax.fori_loop(0, S // ch, step, 0)


def _carry(rider, refs, n_in, n_out, n_scr, grid):
    if rider is None:
        return refs
    ri, ro = len(rider.inputs), len(rider.out_shapes)
    own = refs[:n_in] + refs[n_in + ri:n_in + ri + n_out] + refs[n_in + ri + n_out + ro:n_in + ri + n_out + ro + n_scr]
    rider.refs = (refs[n_in:n_in + ri], refs[n_in + ri + n_out:n_in + ri + n_out + ro], refs[n_in + ri + n_out + ro + n_scr:])
    first = functools.reduce(jnp.logical_and, [pl.program_id(a) == 0 for a in range(len(grid))])

    @pl.when(first)
    def _():
        rider.start(*rider.refs)

    return own


def _carry_finish(rider, grid):
    if rider is None:
        return
    last = functools.reduce(jnp.logical_and, [pl.program_id(a) == n - 1 for a, n in enumerate(grid)])

    @pl.when(last)
    def _():
        rider.finish(*rider.refs)


def _carrier_call(body, rider, ins, in_specs, out_shape, out_specs, scratch, grid, name):
    if rider is None:
        return pl.pallas_call(body, grid=grid, in_specs=in_specs, out_specs=out_specs, out_shape=out_shape, scratch_shapes=scratch,
                              name=name, compiler_params=_params(("parallel",) * len(grid)))(*ins)
    any_spec = pl.BlockSpec(memory_space=pl.ANY)
    ri, ro = len(rider.inputs), len(rider.out_shapes)
    res = pl.pallas_call(
        body, grid=grid, in_specs=list(in_specs) + [any_spec] * ri, out_specs=list(out_specs) + [any_spec] * ro,
        out_shape=list(out_shape) + rider.out_shapes, scratch_shapes=list(scratch) + rider.scratch,
        input_output_aliases={len(ins) + i: len(out_shape) + o for i, o in rider.aliases.items()}, name=name,
        compiler_params=_params(("arbitrary",) * len(grid)))(*ins, *rider.inputs)
    return res[:len(out_shape)], res[len(out_shape):]


def _attn_fwd(q_arr, q_cb, k_arr, k_cb, v_arr, v_cb, *, B, Sq, Sk, nblk, hb, dqk, dv, mode, scale, cum_r=None, rider=None, name):
    wq, wv = hb * dqk, hb * dv
    tq, tk = _attn_tiles(Sq, Sk)
    fox = mode == "fox"
    n_in = 4 if fox else 3
    ones = dv < LANES

    def body(*refs):
        refs = _carry(rider, refs, n_in, 2, 3, (B, nblk))
        q_ref, k_ref, v_ref = refs[:3]
        cr_ref = refs[3] if fox else None
        o_ref, lse_ref, qh, kh, vh = refs[n_in:]
        _stage_heads(qh, q_ref, hb, dqk, mul=scale)
        _stage_heads(kh, k_ref, hb, dqk)
        _stage_heads(vh, v_ref, hb, dv, dv if ones else None)

        def qstep(i, _):
            rows = pl.ds(pl.multiple_of(i * tq, tq), tq)
            qs = [qh[hh, rows, :] for hh in range(hb)]

            def block(c0, w, r0, nr, carry, diag):
                cols = pl.ds(pl.multiple_of(c0, w), w)
                out = []
                for hh in range(hb):
                    m, l, acc = carry[hh]
                    s = lax.dot_general(qs[hh][r0:r0 + nr], kh[hh, cols, :], _DIMS["nt"], preferred_element_type=f32)
                    if fox:
                        s = s - cr_ref[0, hh:hh + 1, cols]
                    if diag:
                        s = jnp.where(_tile_mask(mode, i * tq + r0, c0, nr, w), s, NEG_INF)
                    m_new = jnp.maximum(m, jnp.max(s, axis=-1, keepdims=True))
                    a = jnp.exp(m - m_new)
                    p = jnp.exp(s - m_new)
                    if not ones:
                        l = a * l + jnp.sum(p, axis=-1, keepdims=True)
                    acc = a * acc + lax.dot_general(p.astype(bf16), vh[hh, cols, :], _DIMS["nn"], preferred_element_type=f32)
                    out.append((m_new, l, acc))
                return tuple(out)

            def tile(j, carry):
                return block(j * tk, tk, 0, tq, carry, False)

            carry = tuple((jnp.full((tq, 1), -jnp.inf, f32), jnp.zeros((tq, 1), f32), jnp.zeros((tq, LANES), f32)) for _ in range(hb))
            if mode == "none":
                carry = lax.fori_loop(0, Sk // tk, tile, carry)
            else:
                nfull = (i * tq) // tk
                carry = lax.fori_loop(0, nfull, tile, carry)
                carry = _diagonal(block, carry, i, tq, tk)
            for hh in range(hb):
                m, l, acc = carry[hh]
                if ones:
                    l = acc[:, dv:dv + 1]
                o_ref[rows, hh * dv:(hh + 1) * dv] = acc[:, :dv] / l
                lse_ref[0, rows, hh:hh + 1] = m + jnp.log(l)
            return 0

        lax.fori_loop(0, Sq // tq, qstep, 0)
        _carry_finish(rider, (B, nblk))

    in_specs = [pl.BlockSpec((Sq, wq), lambda b, p: (b, q_cb + p)), pl.BlockSpec((Sk, wq), lambda b, p: (b, k_cb + p)),
                pl.BlockSpec((Sk, wv), lambda b, p: (b, v_cb + p))]
    ins = [q_arr, k_arr, v_arr]
    if fox:
        in_specs.append(pl.BlockSpec((1, hb, Sk), lambda b, p: (b * nblk + p, 0, 0)))
        ins.append(cum_r)
    return _carrier_call(
        body, rider, ins, in_specs,
        [SDS((B * Sq, nblk * wv), f32), SDS((B * nblk, Sq, hb), f32)],
        [pl.BlockSpec((Sq, wv), lambda b, p: (b, p)), pl.BlockSpec((1, Sq, hb), lambda b, p: (b * nblk + p, 0, 0))],
        [pltpu.VMEM((hb, Sq, LANES), bf16), pltpu.VMEM((hb, Sk, LANES), bf16), pltpu.VMEM((hb, Sk, LANES), bf16)],
        (B, nblk), name)


def _attn_bwd(q_arr, q_cb, k_arr, k_cb, v_arr, v_cb, o, do, lse, *, B, Sq, Sk, nblk, hb, dqk, dv, mode, scale, cum_r=None,
              dq_dtype=bf16, rider=None, name):
    wq, wv = hb * dqk, hb * dv
    tq, tk = _attn_tiles(Sq, Sk)
    fox = mode == "fox"
    n_in = 7 if fox else 6
    n_out = 4 if fox else 3
    if fox:
        assert dqk + 2 <= LANES

    def body(*refs):
        refs = _carry(rider, refs, n_in, n_out, 7 if fox else 6, (B, nblk))
        q_ref, k_ref, v_ref, o_ref, do_ref, lse_ref = refs[:6]
        cr_ref = refs[6] if fox else None
        dq_ref, dk_ref, dv_ref = refs[n_in:n_in + 3]
        dcum_ref = refs[n_in + 3] if fox else None
        qh, kh, vh, doh, dk_acc, dv_acc = refs[n_in + n_out:n_in + n_out + 6]
        dcq_acc = refs[n_in + n_out + 6] if fox else None
        _stage_heads(qh, q_ref, hb, dqk, dqk + 1 if fox else None, mul=scale)
        _stage_heads(kh, k_ref, hb, dqk, dqk if fox else None)
        _stage_heads(vh, v_ref, hb, dv)
        _stage_heads(doh, do_ref, hb, dv)
        dk_acc[...] = jnp.zeros_like(dk_acc)
        dv_acc[...] = jnp.zeros_like(dv_acc)

        def qstep(i, _):
            rows = pl.ds(pl.multiple_of(i * tq, tq), tq)
            qs = [qh[hh, rows, :] for hh in range(hb)]
            dos = [doh[hh, rows, :] for hh in range(hb)]
            dds = [jnp.sum(do_ref[rows, hh * dv:(hh + 1) * dv] * o_ref[rows, hh * dv:(hh + 1) * dv], axis=-1, keepdims=True)
                   for hh in range(hb)]
            lses = [lse_ref[0, rows, hh:hh + 1] for hh in range(hb)]

            def block(c0, w, r0, nr, carry, diag):
                cols = pl.ds(pl.multiple_of(c0, w), w)
                rs = slice(r0, r0 + nr)
                out = []
                for hh in range(hb):
                    k = kh[hh, cols, :]
                    q, do_ = qs[hh][rs], dos[hh][rs]
                    s = lax.dot_general(q, k, _DIMS["nt"], preferred_element_type=f32)
                    if fox:
                        s = s - cr_ref[0, hh:hh + 1, cols]
                    if diag:
                        s = jnp.where(_tile_mask(mode, i * tq + r0, c0, nr, w), s, NEG_INF)
                    p = jnp.exp(s - lses[hh][rs])
                    dv_acc[hh, cols, :] += lax.dot_general(p.astype(bf16), do_, _DIMS["tn"], preferred_element_type=f32)
                    dp = lax.dot_general(do_, vh[hh, cols, :], _DIMS["nt"], preferred_element_type=f32)
                    ds = p * (dp - dds[hh][rs])
                    dsb = ds.astype(bf16)
                    out.append(carry[hh] + lax.dot_general(dsb, k, _DIMS["nn"], preferred_element_type=f32))
                    dk_acc[hh, cols, :] += lax.dot_general(dsb, q, _DIMS["tn"], preferred_element_type=f32)
                return tuple(out)

            def tile(j, carry):
                return block(j * tk, tk, 0, tq, carry, False)

            carry = tuple(jnp.zeros((tq, LANES), f32) for _ in range(hb))
            if mode == "none":
                carry = lax.fori_loop(0, Sk // tk, tile, carry)
            else:
                nfull = (i * tq) // tk
                carry = lax.fori_loop(0, nfull, tile, carry)
                carry = _diagonal(block, carry, i, tq, tk)
            for hh in range(hb):
                dq_ref[rows, hh * dqk:(hh + 1) * dqk] = (carry[hh][:, :dqk] * scale).astype(dq_dtype)
                if fox:
                    dcq_acc[rows, hh:hh + 1] = carry[hh][:, dqk:dqk + 1]
            return 0

        lax.fori_loop(0, Sq // tq, qstep, 0)
        for hh in range(hb):
            dk_ref[:, hh * dqk:(hh + 1) * dqk] = dk_acc[hh, :, :dqk].astype(dq_dtype)
            dv_ref[:, hh * dv:(hh + 1) * dv] = dv_acc[hh, :, :dv].astype(bf16)
            if fox:
                dcum_ref[0, :, hh:hh + 1] = dcq_acc[:, hh:hh + 1] - dk_acc[hh, :, dqk + 1:dqk + 2]
        _carry_finish(rider, (B, nblk))

    qs_ =pl.BlockSpec((Sq, wq), lambda b, p: (b, q_cb + p))
    ks_ = pl.BlockSpec((Sk, wq), lambda b, p: (b, k_cb + p))
    vs_ = pl.BlockSpec((Sk, wv), lambda b, p: (b, v_cb + p))
    os_ = pl.BlockSpec((Sq, wv), lambda b, p: (b, p))
    st = pl.BlockSpec((1, Sq, hb), lambda b, p: (b * nblk + p, 0, 0))
    rw = pl.BlockSpec((1, hb, Sk), lambda b, p: (b * nblk + p, 0, 0))
    in_specs = [qs_, ks_, vs_, os_, os_, st] + ([rw] if fox else [])
    ins = [q_arr, k_arr, v_arr, o, do, lse] + ([cum_r] if fox else [])
    out_specs = [pl.BlockSpec((Sq, wq), lambda b, p: (b, p)), pl.BlockSpec((Sk, wq), lambda b, p: (b, p)),
                 pl.BlockSpec((Sk, wv), lambda b, p: (b, p))] + ([st] if fox else [])
    out_shape = [SDS((B * Sq, nblk * wq), dq_dtype), SDS((B * Sk, nblk * wq), dq_dtype), SDS((B * Sk, nblk * wv), bf16)]
    scratch = [pltpu.VMEM((hb, Sq, LANES), bf16), pltpu.VMEM((hb, Sk, LANES), bf16), pltpu.VMEM((hb, Sk, LANES), bf16),
               pltpu.VMEM((hb, Sq, LANES), bf16), pltpu.VMEM((hb, Sk, LANES), f32), pltpu.VMEM((hb, Sk, LANES), f32)]
    if fox:
        assert Sq == Sk
        out_shape.append(SDS((B * nblk, Sq, hb), f32))
        scratch.append(pltpu.VMEM((Sq, hb), f32))
    return _carrier_call(body, rider, ins, in_specs, out_shape, out_specs, scratch, (B, nblk), name)


def _merge_fwd(oa, ob, oc, w_br, proj, D, name):
    T = oa.shape[0]
    tr = _pick(T, 256)
    g0 = C_GATE // D
    o_spec = pl.BlockSpec((tr, BW), lambda i: (i, 0))
    w_spec = pl.BlockSpec((3, BW, D), lambda i: (0, 0, 0))

    def body(oa_ref, ob_ref, oc_ref, w_ref, ga_ref, gb_ref, gc_ref, m_ref):
        tot = jnp.zeros((tr, D), f32)
        for n, (o_ref, g_ref) in enumerate(((oa_ref, ga_ref), (ob_ref, gb_ref), (oc_ref, gc_ref))):
            bp = lax.dot_general(o_ref[...].astype(bf16), w_ref[n], _DIMS["nn"], preferred_element_type=f32)
            tot = tot + jax.nn.sigmoid(g_ref[...]) * bp
        m_ref[...] = tot.astype(bf16)

    gspecs = [pl.BlockSpec((tr, D), lambda i, n=n: (i, g0 + n)) for n in range(3)]
    return pl.pallas_call(
        body, grid=(T // tr,), in_specs=[o_spec, o_spec, o_spec, w_spec, *gspecs],
        out_specs=pl.BlockSpec((tr, D), lambda i: (i, 0)), out_shape=SDS((T, D), bf16), name=name,
        compiler_params=_params(("parallel",)))(oa, ob, oc, w_br, proj, proj, proj)


def _merge_bwd(dm, oa, ob, oc, w_br, proj, D, name, rider=None):
    T = oa.shape[0]
    tr = _pick(T, 256)
    g0 = C_GATE // D
    o_spec = pl.BlockSpec((tr, BW), lambda i: (i, 0))
    d_spec = pl.BlockSpec((tr, D), lambda i: (i, 0))
    w_spec = pl.BlockSpec((3, BW, D), lambda i: (0, 0, 0))

    def body(*refs):
        refs = _carry(rider, refs, 8, 9, 0, (T // tr,))
        dm_ref, oa_ref, ob_ref, oc_ref, w_ref, ga_ref, gb_ref, gc_ref = refs[:8]
        outs = refs[8:]
        dmv = dm_ref[...]
        for n, (o_ref, g_ref) in enumerate(((oa_ref, ga_ref), (ob_ref, gb_ref), (oc_ref, gc_ref))):
            bp = lax.dot_general(o_ref[...].astype(bf16), w_ref[n], _DIMS["nn"], preferred_element_type=f32)
            g = jax.nn.sigmoid(g_ref[...])
            dbp = (dmv * g).astype(bf16)
            outs[n][...] = (dmv * bp * g * (1.0 - g)).astype(bf16)
            outs[3 + n][...] = lax.dot_general(dbp, w_ref[n], _DIMS["nt"], preferred_element_type=f32)
            outs[6 + n][...] = dbp
        _carry_finish(rider, (T // tr,))

    gspecs = [pl.BlockSpec((tr, D), lambda i, n=n: (i, g0 + n)) for n in range(3)]
    return _carrier_call(
        body, rider, [dm, oa, ob, oc, w_br, proj, proj, proj], [d_spec, o_spec, o_spec, o_spec, w_spec, *gspecs],
        [SDS((T, D), bf16)] * 3 + [SDS((T, BW), f32)] * 3 + [SDS((T, D), bf16)] * 3,
        [d_spec] * 3 + [o_spec] * 3 + [d_spec] * 3, [], (T // tr,), name)


def _loss(h, target, name):
    T, D = h.shape
    tr = _pick(T, 512)
    nt = T // tr
    row = pl.BlockSpec((tr, D), lambda i: (i, 0))

    def body(h_ref, t_ref, l_ref, d_ref, acc):
        i = pl.program_id(0)

        @pl.when(i == 0)
        def _():
            acc[...] = jnp.zeros_like(acc)

        e = h_ref[...] - t_ref[...]
        d_ref[...] = e * (1.0 / D)
        acc[...] += jnp.sum(e * e, axis=0, keepdims=True)

        @pl.when(i == nt - 1)
        def _():
            l_ref[...] = jnp.sum(acc[...], axis=-1, keepdims=True) * (0.5 / D)

    return pl.pallas_call(
        body, grid=(nt,), in_specs=[row, row], out_specs=[pl.BlockSpec((1, 1), lambda i: (0, 0)), row],
        out_shape=[SDS((1, 1), f32), SDS((T, D), f32)], scratch_shapes=[pltpu.VMEM((1, D), f32)], name=name,
        compiler_params=_params(("arbitrary",)))(h, target)


def _adamw(w, m, v, parts, name):
    shape = w.shape
    C = shape[-1]
    R = int(np.prod(shape[:-1])) if len(shape) > 1 else 1
    n_parts, Cp = parts.shape[0], parts.shape[-1]
    tr = _pick(R, 512)
    spec = pl.BlockSpec((tr, C), lambda i: (i, 0))

    def body(w_ref, m_ref, v_ref, p_ref, g_ref, d_ref, mo_ref, vo_ref):
        gv = p_ref[0].astype(f32)
        for q in range(1, n_parts):
            gv = gv + p_ref[q].astype(f32)
        gv = gv[:, :C]
        mn = ADAM_B1 * m_ref[...] + (1.0 - ADAM_B1) * gv
        vn = ADAM_B2 * v_ref[...] + (1.0 - ADAM_B2) * (gv * gv)
        m_hat = mn / (1.0 - ADAM_B1 ** ADAM_STEP)
        v_hat = vn / (1.0 - ADAM_B2 ** ADAM_STEP)
        g_ref[...] = gv
        d_ref[...] = -ADAM_LR * (m_hat / (jnp.sqrt(v_hat) + ADAM_EPS) + ADAM_WD * w_ref[...])
        mo_ref[...] = mn
        vo_ref[...] = vn

    outs = pl.pallas_call(
        body, grid=(R // tr,), in_specs=[spec] * 3 + [pl.BlockSpec((n_parts, tr, Cp), lambda i: (0, i, 0))], out_specs=[spec] * 4,
        out_shape=[SDS((R, C), f32)] * 4, name=name,
        compiler_params=_params(("parallel",)))(*[a.reshape(R, C) for a in (w, m, v)], parts.reshape(n_parts, R, Cp))
    return [o.reshape(shape) for o in outs]


def _coords():
    return lax.axis_index("x"), lax.axis_index("y"), lax.axis_index("c")


def _window(ref, axis, idx, n):
    start = pl.multiple_of(idx * n, n)
    if axis == 1:
        return ref.at[:, pl.ds(start, n), :]
    return ref.at[:, :, pl.ds(start, n)]


class _Rider:
    def __init__(self, inputs, out_shapes, aliases, scratch, start, finish):
        self.inputs, self.out_shapes, self.aliases, self.scratch = list(inputs), list(out_shapes), dict(aliases), list(scratch)
        self.start, self.finish = start, finish


def _join_riders(riders):
    riders = [r for r in riders if r is not None]
    if len(riders) < 2:
        return riders[0] if riders else None
    ni = np.cumsum([0] + [len(r.inputs) for r in riders])
    no = np.cumsum([0] + [len(r.out_shapes) for r in riders])
    ns = np.cumsum([0] + [len(r.scratch) for r in riders])
    aliases = {int(ni[k]) + i: int(no[k]) + o for k, r in enumerate(riders) for i, o in r.aliases.items()}

    def phase(which):
        def run(ins, outs, scr):
            for k, r in enumerate(riders):
                getattr(r, which)(ins[ni[k]:ni[k + 1]], outs[no[k]:no[k + 1]], scr[ns[k]:ns[k + 1]])
        return run

    return _Rider(sum([r.inputs for r in riders], []), sum([r.out_shapes for r in riders], []), aliases,
                  sum([r.scratch for r in riders], []), phase("start"), phase("finish"))


def _run_rider(r, name):
    ni, no = len(r.inputs), len(r.out_shapes)

    def body(*refs):
        ins, outs, scr = refs[:ni], refs[ni:ni + no], refs[ni + no:]
        r.start(ins, outs, scr)
        r.finish(ins, outs, scr)

    any_spec = pl.BlockSpec(memory_space=pl.ANY)
    return pl.pallas_call(body, out_shape=r.out_shapes, in_specs=[any_spec] * ni, out_specs=[any_spec] * no,
                          input_output_aliases=r.aliases, scratch_shapes=r.scratch, name=name)(*r.inputs)


def _gather_rider(shards, axes):
    nt = len(shards)
    sizes = [s.shape[a] for s, a in zip(shards, axes)]
    out_shapes = [SDS(tuple(d * N_DEV if i == a else d for i, d in enumerate(s.shape)), s.dtype) for s, a in zip(shards, axes)]

    def plan(x_refs, out_refs, scr):
        send_sems, recv_sems, local_sems = scr
        x, y, c = _coords()
        me, sibling = (x, y, c), (x, y, 1 - c)
        chips = [(1 - x, y), (x, 1 - y), (1 - x, 1 - y)]

        def win(t, px, py, pc):
            return _window(out_refs[t], axes[t], 4 * px + 2 * py + pc, sizes[t])

        def copy(t, k, block, to, src=None):
            return pltpu.make_async_remote_copy(
                src_ref=win(t, *block) if src is None else src, dst_ref=win(t, *block),
                send_sem=send_sems.at[7 * t + k], recv_sem=recv_sems.at[7 * t + k], device_id=to, device_id_type=MESH)

        mine = [pltpu.make_async_copy(x_refs[t], win(t, *me), local_sems.at[t]) for t in range(nt)]
        first = []
        for t in range(nt):
            first.append(copy(t, 0, me, sibling, src=x_refs[t]))
            first += [copy(t, 1 + j, me, (*chip, c), src=x_refs[t]) for j, chip in enumerate(chips)]
        return c, me, sibling, chips, copy, mine, first

    def start(x_refs, out_refs, scr):
        _, _, _, _, _, mine, first = plan(x_refs, out_refs, scr)
        for cp in mine + first:
            cp.start()

    def finish(x_refs, out_refs, scr):
        c, me, sibling, chips, copy, mine, first = plan(x_refs, out_refs, scr)
        passed = []
        for j, chip in enumerate(chips):
            for t in range(nt):
                copy(t, 1 + j, (*chip, c), me).wait_recv()
                fwd = copy(t, 4 + j, (*chip, c), sibling)
                fwd.start()
                passed.append(fwd)
        for t in range(nt):
            copy(t, 0, sibling, me).wait_recv()
            for j, chip in enumerate(chips):
                copy(t, 4 + j, (*chip, 1 - c), me).wait_recv()
        for cp in first + passed:
            cp.wait_send()
        for cp in mine:
            cp.wait()

    scratch = [pltpu.SemaphoreType.DMA((7 * nt,)), pltpu.SemaphoreType.DMA((7 * nt,)), pltpu.SemaphoreType.DMA((nt,))]
    return _Rider(shards, out_shapes, {}, scratch, start, finish)


N_CHIP = N_DEV // 2


def _pair_exchange_rider(grads, axes, shards):
    nt = len(grads)
    sizes = [s[a] for s, a in zip(shards, axes)]

    def plan(ins, outs, scr):
        send_sems, recv_sems = scr
        x, y, c = _coords()
        return [pltpu.make_async_remote_copy(
            src_ref=_window(ins[t], axes[t], 2 * q + (1 - c), sizes[t]), dst_ref=outs[t].at[q],
            send_sem=send_sems.at[N_CHIP * t + q], recv_sem=recv_sems.at[N_CHIP * t + q],
            device_id=(x, y, 1 - c), device_id_type=MESH) for t in range(nt) for q in range(N_CHIP)]

    def start(ins, outs, scr):
        for cp in plan(ins, outs, scr):
            cp.start()

    def finish(ins, outs, scr):
        copies = plan(ins, outs, scr)
        for cp in copies:
            cp.wait_recv()
        for cp in copies:
            cp.wait_send()

    scratch = [pltpu.SemaphoreType.DMA((N_CHIP * nt,)), pltpu.SemaphoreType.DMA((N_CHIP * nt,))]
    return _Rider(grads, [SDS((N_CHIP,) + tuple(s), g.dtype) for s, g in zip(shards, grads)], {}, scratch, start, finish)


def _pair_sum(grad, buf, axis, core, name):
    _, G, r, c = buf.shape
    tr = _pick(r, 512)
    nr = r // tr

    def body(core_ref, g_ref, b_ref, o_ref):
        o_ref[0, 0] = (g_ref[0].astype(f32) + b_ref[0, 0].astype(f32)).astype(bf16)

    if axis == 2:
        g_spec = pl.BlockSpec((1, tr, c), lambda q, g, i, core_ref: (g, i, 2 * q + core_ref[0]))
    else:
        g_spec = pl.BlockSpec((1, tr, c), lambda q, g, i, core_ref: (g, (2 * q + core_ref[0]) * nr + i, 0))
    blk = pl.BlockSpec((1, 1, tr, c), lambda q, g, i, core_ref: (q, g, i, 0))
    grid_spec = pltpu.PrefetchScalarGridSpec(num_scalar_prefetch=1, grid=(N_CHIP, G, nr), in_specs=[g_spec, blk], out_specs=blk)
    return pl.pallas_call(body, grid_spec=grid_spec, out_shape=SDS(buf.shape, bf16), name=name,
                          compiler_params=_params(("parallel", "parallel", "parallel")))(core, grad, buf)


def _chip_exchange_rider(chips, slots, layer):
    nt = len(chips)

    def plan(ins, outs, scr):
        send_sems, recv_sems, local_sems = scr
        x, y, c = _coords()
        me = 2 * x + y
        mine = [pltpu.make_async_copy(ins[t].at[me], outs[t].at[me, layer], local_sems.at[t]) for t in range(nt)]
        copies = []
        for k in range(1, N_CHIP):
            px, py = lax.rem(x + (k >> 1), 2), lax.rem(y + (k & 1), 2)
            copies += [pltpu.make_async_remote_copy(
                src_ref=ins[t].at[2 * px + py], dst_ref=outs[t].at[me, layer],
                send_sem=send_sems.at[3 * t + k - 1], recv_sem=recv_sems.at[3 * t + k - 1],
                device_id=(px, py, c), device_id_type=MESH) for t in range(nt)]
        return mine, copies

    def start(ins, outs, scr):
        mine, copies = plan(ins, outs, scr)
        for cp in mine + copies:
            cp.start()

    def finish(ins, outs, scr):
        mine, copies = plan(ins, outs, scr)
        for cp in copies:
            cp.wait_recv()
        for cp in copies:
            cp.wait_send()
        for cp in mine:
            cp.wait()

    scratch = [pltpu.SemaphoreType.DMA((3 * nt,)), pltpu.SemaphoreType.DMA((3 * nt,)), pltpu.SemaphoreType.DMA((nt,))]
    return _Rider(list(chips) + list(slots), [SDS(s.shape, s.dtype) for s in slots], {nt + t: t for t in range(nt)}, scratch,
                  start, finish)


_BIG = (("w_in", 2), ("w_uq", 2), ("w_ukv", 2), ("w_mem_kv", 1), ("w_br", 2), ("w_out", 1), ("w_ff1", 2), ("w_ff2", 1))


def _in_col_maps(D):
    d_in = O_GATE + 3 * D
    sh = d_in // N_DEV
    shp = _round_up(sh, LANES)
    perm_of_orig = np.empty(d_in, np.int64)
    for a, b, dst in ((0, O_F, C_Q), (O_F, O_CQ, C_F), (O_CQ, O_CKV, C_CQ), (O_CKV, O_KR, C_CKV), (O_KR, O_QM, C_KPE + NOPE),
                      (O_QM, O_GATE, C_QM), (O_GATE, d_in, C_GATE)):
        perm_of_orig[a:b] = dst + np.arange(b - a)
    orig = np.arange(d_in)
    gath_of_orig = (orig // sh) * shp + orig % sh
    fwd = -np.ones(C_GATE + 3 * D, np.int64)
    fwd[perm_of_orig] = gath_of_orig
    bwd = -np.ones(N_DEV * shp, np.int64)
    bwd[gath_of_orig] = perm_of_orig
    return fwd, bwd


def _lane_pad(a):
    c = a.shape[-1]
    return jnp.pad(a, [(0, 0)] * (a.ndim - 1) + [(0, _round_up(c, LANES) - c)])


def _as3(a):
    return a.reshape((-1,) + a.shape[-2:]) if a.ndim != 3 else a


def _rope_tables(positions):
    inv_freq = ROPE_BASE ** (-jnp.arange(0, ROPE, 2, dtype=f32) / ROPE)
    ang = positions.astype(f32).reshape(-1, 1) * inv_freq
    cos, sin = jnp.cos(ang), jnp.sin(ang)
    T = ang.shape[0]
    one, zero = jnp.ones((T, NOPE), f32), jnp.zeros((T, 16), f32)
    c = jnp.concatenate([one, cos, cos, jnp.ones((T, 32), f32)], axis=1)
    sa = jnp.concatenate([jnp.zeros((T, NOPE), f32), -sin, zero, jnp.zeros((T, 32), f32)], axis=1)
    sb = jnp.concatenate([jnp.zeros((T, NOPE), f32), zero, sin, jnp.zeros((T, 32), f32)], axis=1)
    return c, sa, sb


def kernel(x, mem, positions, ln_in_g, ln_in_b, w_in, b_forget, w_uq, g_cq, w_ukv, g_ckv, w_mem_kv, w_br, w_out, ln1_g, ln1_b, w_ff1, w_ff2, ln2_g, ln2_b, loss_target, m_ln_in_g, m_ln_in_b, m_w_in, m_b_forget, m_w_uq, m_g_cq, m_w_ukv, m_g_ckv, m_w_mem_kv, m_w_br, m_w_out, m_ln1_g, m_ln1_b, m_w_ff1, m_w_ff2, m_ln2_g, m_ln2_b, v_ln_in_g, v_ln_in_b, v_w_in, v_b_forget, v_w_uq, v_g_cq, v_w_ukv, v_g_ckv, v_w_mem_kv, v_w_br, v_w_out, v_ln1_g, v_ln1_b, v_w_ff1, v_w_ff2, v_ln2_g, v_ln2_b):
    B, S, D = x.shape
    T = B * S
    L = w_in.shape[0]
    NM = mem.shape[1]
    alpha = (2 * L) ** 0.25
    weights = dict(ln_in_g=ln_in_g, ln_in_b=ln_in_b, w_in=w_in, b_forget=b_forget, w_uq=w_uq, g_cq=g_cq, w_ukv=w_ukv, g_ckv=g_ckv,
                   w_mem_kv=w_mem_kv, w_br=w_br, w_out=w_out, ln1_g=ln1_g, ln1_b=ln1_b, w_ff1=w_ff1, w_ff2=w_ff2, ln2_g=ln2_g, ln2_b=ln2_b)
    mom_m = dict(ln_in_g=m_ln_in_g, ln_in_b=m_ln_in_b, w_in=m_w_in, b_forget=m_b_forget, w_uq=m_w_uq, g_cq=m_g_cq, w_ukv=m_w_ukv,
                 g_ckv=m_g_ckv, w_mem_kv=m_w_mem_kv, w_br=m_w_br, w_out=m_w_out, ln1_g=m_ln1_g, ln1_b=m_ln1_b, w_ff1=m_w_ff1,
                 w_ff2=m_w_ff2, ln2_g=m_ln2_g, ln2_b=m_ln2_b)
    mom_v = dict(ln_in_g=v_ln_in_g, ln_in_b=v_ln_in_b, w_in=v_w_in, b_forget=v_b_forget, w_uq=v_w_uq, g_cq=v_g_cq, w_ukv=v_w_ukv,
                 g_ckv=v_g_ckv, w_mem_kv=v_w_mem_kv, w_br=v_w_br, w_out=v_w_out, ln1_g=v_ln1_g, ln1_b=v_ln1_b, w_ff1=v_w_ff1,
                 w_ff2=v_w_ff2, ln2_g=v_ln2_g, ln2_b=v_ln2_b)
    order = list(weights)
    big_axes = [ax for _, ax in _BIG]
    fwd_map, bwd_map = _in_col_maps(D)
    shard3 = {n: _lane_pad(weights[n]) for n, _ in _BIG}

    half = len(_BIG) // 2

    def gather_rider(l, lo=0, hi=len(_BIG)):
        return _gather_rider([_as3(shard3[n][l]).astype(bf16) for n, _ in _BIG[lo:hi]], big_axes[lo:hi])

    def named(full, lo, hi):
        return {n: (f if n == "w_br" else f[0]) for (n, _), f in zip(_BIG[lo:hi], full)}

    tabs = _rope_tables(positions)
    x2d, mem2d, tgt2d = x.reshape(T, D), mem.reshape(B * NM, D), loss_target.reshape(T, D)
    fox = dict(B=B, Sq=S, Sk=S, nblk=FOX_HEADS // 2, hb=2, dqk=FOX_HD, dv=FOX_HD, mode="fox", scale=FOX_HD ** -0.5)
    mla_scale = (NOPE + ROPE) ** -0.5
    mla = dict(B=B, Sq=S, Sk=S, nblk=MLA_HEADS // 2, hb=2, dqk=LANES, dv=MLA_VD, mode="chunk", scale=1.0)
    memat = dict(B=B, Sq=S, Sk=NM, nblk=MEM_HEADS, hb=1, dqk=MEM_HD, dv=MEM_HD, mode="none", scale=MEM_HD ** -0.5)
    nb = FOX_HEADS // 2

    h, hb, xh0, rstd0 = _ln_fwd(None, x2d, ln_in_g, ln_in_b, alpha, "ln_in")
    saved = []
    first = _run_rider(gather_rider(0, 0, half), "weights_all_gather")
    for l in range(L):
        W = named(first, 0, half)
        W["w_in"] = _lane_permute(W["w_in"], fwd_map, bf16, "w_in_reorder")
        bpad = jnp.pad(b_forget[l].reshape(1, FOX_HEADS), ((0, 0), (0, LANES - FOX_HEADS)))
        proj = _mm(hb, W["w_in"], name="proj")
        cum = _gate_fwd(proj, bpad, B, S, "gate_fwd")
        cum_r = cum[:, :FOX_HEADS].reshape(B, S, nb, 2).transpose(0, 2, 3, 1).reshape(B * nb, 2, S)
        more = l + 1 < L
        (o_a, lse_a), second = _attn_fwd(proj, C_Q // LANES, proj, C_K // LANES, proj, C_V // LANES, cum_r=cum_r, name="fox_fwd",
                                         rider=gather_rider(l, half), **fox)
        W.update(named(second, half, len(_BIG)))
        cqn = _rms_fwd(proj, C_CQ, Q_RANK, g_cq[l], "rms_q")
        ckvn = _rms_fwd(proj, C_CKV, KV_RANK, g_ckv[l], "rms_kv")
        qraw = _mm(cqn, W["w_uq"], name="q_up")
        kv = _mm(ckvn, W["w_ukv"], name="kv_up")
        qf = _rope_q(qraw, tabs, bwd=False, mul=mla_scale, name="rope_q")
        kf, v_b = _rope_k(kv, tabs, proj, "rope_k")
        res = _attn_fwd(qf, 0, kf, 0, v_b, 0, name="mla_fwd", rider=gather_rider(l + 1, 0, half) if more else None, **mla)
        (o_b, lse_b), first = res if more else (res, None)
        mkv = _mm(mem2d, W["w_mem_kv"], name="mem_kv")
        o_c, lse_c = _attn_fwd(proj, C_QM // MEM_HD, mkv, 0, mkv, MEM_HEADS, name="mem_fwd", **memat)
        merged = _merge_fwd(o_a, o_b, o_c, W["w_br"], proj, D, "merge_fwd")
        y = _mm(merged, W["w_out"], name="out_proj")
        h1, h1b, xh1, rstd1 = _ln_fwd(h, y, ln1_g[l], ln1_b[l], alpha, "ln1")
        a, r = _mm(h1b, W["w_ff1"], epi="relu2", name="ff1")
        ff = _mm(a, W["w_ff2"], name="ff2")
        h2, h2b, xh2, rstd2 = _ln_fwd(h1, ff, ln2_g[l], ln2_b[l], alpha, "ln2")
        saved.append(dict(W=W, bpad=bpad, h=hb, proj=proj, cum_r=cum_r, o_a=o_a, lse_a=lse_a, cqn=cqn, ckvn=ckvn, qf=qf, kf=kf,
                          v_b=v_b, o_b=o_b, lse_b=lse_b, mkv=mkv, o_c=o_c, lse_c=lse_c, merged=merged, h1=h1b, xh1=xh1,
                          rstd1=rstd1, a=a, r=r, xh2=xh2, rstd2=rstd2))
        h, hb = h2, h2b

    loss_local, dh = _loss(h, tgt2d, "loss")
    loss = lax.psum(loss_local[0, 0], ("x", "y", "c"))

    shard_shapes = [_as3(shard3[n][0]).shape for n, _ in _BIG]
    slots = [lax.empty((N_CHIP, L) + s, bf16) for s in shard_shapes]
    core = lax.axis_index("c").astype(jnp.int32).reshape(1)
    small = {n: [None] * L for n in ("b_forget", "g_cq", "g_ckv", "ln1_g", "ln1_b", "ln2_g", "ln2_b")}
    ax_a, ax_c = big_axes[:half], big_axes[half:]
    sh_a, sh_c = shard_shapes[:half], shard_shapes[half:]

    def pair_sums(gs, bufs, axes):
        return [_pair_sum(g, b, ax, core, "grads_pair_sum") for g, b, ax in zip(gs, bufs, axes)]

    pending = unpaired = None
    for l in reversed(range(L)):
        sv = saved[l]
        W = sv["W"]
        du2, du2b, dg2, db2 = _ln_bwd(dh, sv["xh2"], sv["rstd2"], ln2_g[l], "ln2_bwd")
        dz = _mm(du2b, W["w_ff2"], mode="nt", epi="mul", extra=sv["r"], out_dtype=bf16, name="ff2_dx")
        dw_ff2 = _mm(sv["a"], du2b, mode="tn", out_dtype=bf16, name="ff2_dw")
        dh1 = _mm(dz, W["w_ff1"], mode="nt", epi="add", extra=du2, alpha=alpha, name="ff1_dx")
        dw_ff1 = _mm(sv["h1"], dz, mode="tn", out_dtype=bf16, name="ff1_dw")
        du1, du1b, dg1, db1 = _ln_bwd(dh1, sv["xh1"], sv["rstd1"], ln1_g[l], "ln1_bwd")
        dmerged = _mm(du1b, W["w_out"], mode="nt", name="out_dx")
        dw_out = _mm(sv["merged"], du1b, mode="tn", out_dtype=bf16, name="out_dw")
        mb = _merge_bwd(dmerged, sv["o_a"], sv["o_b"], sv["o_c"], W["w_br"], sv["proj"], D, "merge_bwd",
                        rider=None if unpaired is None else _pair_exchange_rider(unpaired[0], ax_a, sh_a))
        if unpaired is not None:
            mb, bufs = mb
            pending = (pair_sums(unpaired[0], bufs, ax_a), unpaired[1])
        dgl, do, dbp = mb[0:3], mb[3:6], mb[6:9]
        dw_br = jnp.stack([_mm(o_n, dbp_n, mode="tn", out_dtype=bf16, name="br_dw") for o_n, dbp_n in zip((sv["o_a"], sv["o_b"], sv["o_c"]), dbp)])
        late = [_as3(g) for g in (dw_br, dw_out, dw_ff1, dw_ff2)]
        proj = sv["proj"]
        chip_rider = None if pending is None else _chip_exchange_rider(pending[0], slots[:half], pending[1])
        own, got = _attn_bwd(proj, C_Q // LANES, proj, C_K // LANES, proj, C_V // LANES, sv["o_a"], do[0], sv["lse_a"],
                             cum_r=sv["cum_r"], name="fox_bwd",
                             rider=_join_riders([chip_rider, _pair_exchange_rider(late, ax_c, sh_c)]), **fox)
        dq_a, dk_a, dv_a, dcum = own
        if chip_rider is not None:
            slots[:half], got = got[:half], got[half:]
        dcum = dcum.reshape(B, nb, S, 2).transpose(0, 2, 1, 3).reshape(T, FOX_HEADS)
        dcum = jnp.pad(dcum, ((0, 0), (0, LANES - FOX_HEADS)))
        dzf, dbf = _gate_bwd(dcum, proj, sv["bpad"], B, S, "gate_bwd")
        (dqf, dkf, dv_b), slots[half:] = _attn_bwd(
            sv["qf"], 0, sv["kf"], 0, sv["v_b"], 0, sv["o_b"], do[1], sv["lse_b"], dq_dtype=f32, name="mla_bwd",
            rider=_chip_exchange_rider(pair_sums(late, got, ax_c), slots[half:], l), **mla)
        dqraw = _rope_q(dqf, tabs, bwd=True, mul=mla_scale, name="rope_q_bwd")
        dkv, dkpe = _rope_k_bwd(dkf, dv_b, tabs, "rope_k_bwd")
        dw_uq = _mm(sv["cqn"], dqraw, mode="tn", out_dtype=bf16, name="q_up_dw")
        dcqn = _mm(dqraw, W["w_uq"], mode="nt", name="q_up_dx")
        dcq_, dg_cq = _rms_bwd(dcqn, proj, C_CQ, Q_RANK, g_cq[l], "rms_q_bwd")
        dw_ukv = _mm(sv["ckvn"], dkv, mode="tn", out_dtype=bf16, name="kv_up_dw")
        dckvn = _mm(dkv, W["w_ukv"], mode="nt", name="kv_up_dx")
        dckv, dg_ckv = _rms_bwd(dckvn, proj, C_CKV, KV_RANK, g_ckv[l], "rms_kv_bwd")
        dqm, dmk, dmv = _attn_bwd(proj, C_QM // MEM_HD, sv["mkv"], 0, sv["mkv"], MEM_HEADS, sv["o_c"], do[2], sv["lse_c"],
                                  name="mem_bwd", **memat)
        dw_mem = _mm(mem2d, jnp.concatenate([dmk, dmv], axis=1), mode="tn", out_dtype=bf16, name="mem_kv_dw")
        dproj = jnp.concatenate([dq_a, dk_a, dv_a, dzf, dkpe, dckv, dqm, jnp.zeros((T, C_CQ - C_PAD), bf16), dcq_, *dgl], axis=1)
        dw_in = _lane_permute(_mm(sv["h"], dproj, mode="tn", out_dtype=bf16, name="proj_dw"), bwd_map, bf16, "w_in_grad_reorder")
        dh = _mm(dproj, W["w_in"], mode="nt", epi="add", extra=du1, alpha=alpha, name="proj_dx")
        unpaired = ([_as3(g) for g in (dw_in, dw_uq, dw_ukv, dw_mem)], l)
        for n, val in (("b_forget", dbf[:, :FOX_HEADS]), ("g_cq", dg_cq), ("g_ckv", dg_ckv), ("ln1_g", dg1), ("ln1_b", db1),
                       ("ln2_g", dg2), ("ln2_b", db2)):
            small[n][l] = val.reshape(-1)
    bufs = _run_rider(_pair_exchange_rider(unpaired[0], ax_a, sh_a), "grads_pair_exchange")
    slots[:half] = _run_rider(_chip_exchange_rider(pair_sums(unpaired[0], bufs, ax_a), slots[:half], unpaired[1]),
                              "grads_chip_exchange")
    grad_x, _, dg_in, db_in = _ln_bwd(dh, xh0, rstd0, ln_in_g, "ln_in_bwd")

    small_list = [("ln_in_g", dg_in.reshape(-1)), ("ln_in_b", db_in.reshape(-1))] + [(n, jnp.stack(v).reshape(-1)) for n, v in small.items()]
    sflat = jnp.concatenate([v for _, v in small_list])
    n_small = sflat.shape[0]
    rs = _round_up(-(-n_small // LANES), 8)
    spacked = jnp.pad(sflat, (0, rs * LANES - n_small)).reshape(1, rs, LANES)
    sparts = _run_rider(_gather_rider([spacked], [1]), "small_all_gather")[0].reshape(N_DEV, rs * LANES)

    grads, deltas, new_m, new_v = {}, {}, {}, {}
    off = 0
    for n, vv in small_list:
        parts = sparts[:, off:off + vv.shape[0]].reshape((N_DEV,) + weights[n].shape)
        off += vv.shape[0]
        grads[n], deltas[n], new_m[n], new_v[n] = _adamw(weights[n], mom_m[n], mom_v[n], parts, "adamw_" + n)
    for (n, _), sl in zip(_BIG, slots):
        parts = sl.reshape((N_CHIP,) + weights[n].shape[:-1] + (sl.shape[-1],))
        grads[n], deltas[n], new_m[n], new_v[n] = _adamw(weights[n], mom_m[n], mom_v[n], parts, "adamw_" + n)
    return (loss, grad_x.reshape(B, S, D), *[grads[n] for n in order], *[deltas[n] for n in order], *[new_m[n] for n in order],
            *[new_v[n] for n in order])
```
